```python
import jax, jax.numpy as jnp
from jax import lax
import numpy as np

D_MODEL = 1024
BATCH = 8
SEQ = 2048
DEPTH = 1
DEC_BATCH = 128
DEC_SEQ = 1
PAST_LEN = 2048
PAGE_SIZE = 128

HEAD_DIM = 64
N_HEADS = 8
N_KV = 2
GQA = N_HEADS // N_KV
WIDTH_A = N_HEADS * HEAD_DIM
KV_WIDTH = N_KV * HEAD_DIM
L_CMP = 32
L_SLC = 64
N_SEL = 8
WINDOW = 512
Q_BLOCK = 128
FORCE_BONUS = 1.0e4
ROPE_THETA = 10000.0
CHUNK = 128
N_GROUPS_B = 4
WIDTH_B = 512
GROUP_W_B = WIDTH_B // N_GROUPS_B
D_IN = WIDTH_A + 6 * KV_WIDTH + 3 * N_HEADS + WIDTH_A + 3 * WIDTH_B + 2 * D_MODEL
EPS = 1e-6
NEG = -1e30

kernel_name = 'nsa_gmlp_hybrid_step'


def rms_norm(x, g):
    xf = x.astype(jnp.float32)
    y = xf * lax.rsqrt(jnp.mean(xf * xf, axis=-1, keepdims=True) + EPS)
    return (y * g.astype(jnp.float32)).astype(x.dtype)


def layer_norm(x, g, b):
    xf = x.astype(jnp.float32)
    mu = jnp.mean(xf, axis=-1, keepdims=True)
    var = jnp.mean(jnp.square(xf - mu), axis=-1, keepdims=True)
    y = (xf - mu) * lax.rsqrt(var + EPS) * g.astype(jnp.float32) + b.astype(jnp.float32)
    return y.astype(x.dtype)


def rope(x, pos):
    half = HEAD_DIM // 2
    inv = ROPE_THETA ** (-jnp.arange(half, dtype=jnp.float32) * 2.0 / HEAD_DIM)
    ang = pos.astype(jnp.float32)[:, None] * inv[None, :]
    shape = (pos.shape[0],) + (1,) * (x.ndim - 3) + (half,)
    cos, sin = jnp.cos(ang).reshape(shape), jnp.sin(ang).reshape(shape)
    xf = x.astype(jnp.float32)
    x1, x2 = xf[..., :half], xf[..., half:]
    return jnp.concatenate([x1 * cos - x2 * sin, x2 * cos + x1 * sin], axis=-1).astype(x.dtype)


def compress(rows, pos_emb, w):
    B, T = rows.shape[:2]
    blocks = rows.reshape(B, T // L_CMP, L_CMP, N_KV, HEAD_DIM)
    pooled = jnp.mean(blocks + pos_emb[:, None, :], axis=2)
    return pooled @ w


def nsa_core(q, qpos, kc, vc, ks, vs, kw, vw, kwpos):
    B, Tq = q.shape[:2]
    dt = q.dtype
    scale = HEAD_DIM ** -0.5
    qg = q.reshape(B, Tq, N_KV, GQA, HEAD_DIM)
    NC = kc.shape[1]
    cend = (jnp.arange(NC, dtype=jnp.int32) + 1) * L_CMP - 1
    mc = (cend[None, :] <= qpos[:, None])[None, :, None, None, :]
    s_c = jnp.einsum('bqgrd,bcgd->bqgrc', qg, kc).astype(jnp.float32) * scale
    p_c = jax.nn.softmax(jnp.where(mc, s_c, NEG), axis=-1) * mc
    o_c = jnp.einsum('bqgrc,bcgd->bqgrd', p_c.astype(dt), vc)
    NS = ks.shape[1] // L_SLC
    ratio = L_SLC // L_CMP
    imp = p_c.sum(axis=3).reshape(B, Tq, N_KV, NS, ratio).sum(axis=-1)
    blk = jnp.arange(NS, dtype=jnp.int32)
    qblk = qpos // L_SLC
    forced = ((blk[None, :] == 0) | (blk[None, :] == qblk[:, None])).astype(jnp.float32)
    causal_blk = blk[None, :] <= qblk[:, None]
    imp = jnp.where(causal_blk[None, :, None, :], imp + FORCE_BONUS * forced[None, :, None, :], NEG)
    n_sel = min(N_SEL, NS)
    _, idx = lax.top_k(imp, n_sel)
    ksb = ks.reshape(B, NS, L_SLC, N_KV, HEAD_DIM).transpose(0, 3, 1, 2, 4)
    vsb = vs.reshape(B, NS, L_SLC, N_KV, HEAD_DIM).transpose(0, 3, 1, 2, 4)
    idx_t = idx.transpose(0, 2, 1, 3)
    gather = jax.vmap(jax.vmap(lambda blocks, ids: blocks[ids]))
    k_sel = gather(ksb, idx_t)
    v_sel = gather(vsb, idx_t)
    s_s = jnp.einsum('bqgrd,bgqnld->bqgrnl', qg, k_sel).astype(jnp.float32) * scale
    kpos_s = idx_t[..., None] * L_SLC + jnp.arange(L_SLC, dtype=jnp.int32)
    m_s = (kpos_s <= qpos[None, None, :, None, None]).transpose(0, 2, 1, 3, 4)[:, :, :, None]
    s_s = jnp.where(m_s, s_s, NEG).reshape(B, Tq, N_KV, GQA, n_sel * L_SLC)
    p_s = jax.nn.softmax(s_s, axis=-1).reshape(B, Tq, N_KV, GQA, n_sel, L_SLC)
    o_s = jnp.einsum('bqgrnl,bgqnld->bqgrd', p_s.astype(dt), v_sel)
    m_w = ((kwpos[None, :] <= qpos[:, None]) & (kwpos[None, :] > qpos[:, None] - WINDOW)
           & (kwpos[None, :] >= 0))[None, :, None, None, :]
    s_w = jnp.einsum('bqgrd,bkgd->bqgrk', qg, kw).astype(jnp.float32) * scale
    p_w = jax.nn.softmax(jnp.where(m_w, s_w, NEG), axis=-1)
    o_w = jnp.einsum('bqgrk,bkgd->bqgrd', p_w.astype(dt), vw)
    shp = (B, Tq, N_HEADS, HEAD_DIM)
    return (o_c.reshape(shp), o_s.reshape(shp), o_w.reshape(shp))


def spatial_mix(vn, w_s, b_s):
    B, T, _ = vn.shape
    Lc = min(CHUNK, T)
    vc = vn.reshape(B, T // Lc, Lc, N_GROUPS_B, GROUP_W_B)
    ws = jnp.tril(w_s[:, :Lc, :Lc])
    s = jnp.einsum('gij,bcjgd->bcigd', ws, vc) + b_s[:, :Lc].T[None, None, :, :, None]
    return s.reshape(B, T, WIDTH_B)


def mixer_inputs(x, c, pos, w_ada, b_ada, norm_g, w_in, q_norm_g, k_norm_g, vnorm_g, vnorm_b):
    B, T, _ = x.shape
    mod = c @ w_ada + b_ada
    shift, scale, gate = jnp.split(mod, 3, axis=-1)
    h = rms_norm(x, norm_g) * (1.0 + scale[:, None]) + shift[:, None]
    proj = h @ w_in
    sizes = (WIDTH_A, 6 * KV_WIDTH, 3 * N_HEADS, WIDTH_A, WIDTH_B, WIDTH_B, WIDTH_B, D_MODEL, D_MODEL)
    cuts = [int(v) for v in np.cumsum(sizes)[:-1]]
    q, kv, nsa_g, z_a, u, v, z_b, g_a, g_b = jnp.split(proj, cuts, axis=-1)
    q = rope(rms_norm(q.reshape(B, T, N_HEADS, HEAD_DIM), q_norm_g), pos)
    kv = kv.reshape(B, T, 2, 3, N_KV, HEAD_DIM)
    keys = rope(rms_norm(kv[:, :, 0], k_norm_g), pos)
    vals = kv[:, :, 1]
    vn = layer_norm(v, vnorm_g, vnorm_b)
    return gate, q, keys, vals, nsa_g, z_a, u, vn, z_b, g_a, g_b


def mixer_output(x, gate, o_c, o_s, o_w, nsa_g, z_a, u, s_b, z_b, g_a, g_b, w_br_a, w_br_b, w_out):
    B, T, _ = x.shape
    gw = jax.nn.sigmoid(nsa_g).reshape(B, T, N_HEADS, 3, 1)
    o_a = gw[:, :, :, 0] * o_c + gw[:, :, :, 1] * o_s + gw[:, :, :, 2] * o_w
    a = (o_a.reshape(B, T, WIDTH_A) * jax.nn.silu(z_a)) @ w_br_a
    b = (u * s_b * jax.nn.silu(z_b)) @ w_br_b
    m = jax.nn.sigmoid(g_a) * a + jax.nn.sigmoid(g_b) * b
    return x + gate[:, None] * (m @ w_out)


def prompt_layer(x, c, lw):
    (w_ada, b_ada, norm_g, w_in, q_norm_g, k_norm_g, pe_k, pe_v, w_ck, w_cv,
     vnorm_g, vnorm_b, w_s, b_s, w_br_a, w_br_b, w_out) = lw
    B, S, _ = x.shape
    pos = jnp.arange(S, dtype=jnp.int32)
    gate, q, keys, vals, nsa_g, z_a, u, vn, z_b, g_a, g_b = mixer_inputs(
        x, c, pos, w_ada, b_ada, norm_g, w_in, q_norm_g, k_norm_g, vnorm_g, vnorm_b)
    k_cmp, k_slc, k_win = keys[:, :, 0], keys[:, :, 1], keys[:, :, 2]
    v_cmp, v_slc, v_win = vals[:, :, 0], vals[:, :, 1], vals[:, :, 2]
    kc = compress(k_cmp, pe_k, w_ck)
    vc = compress(v_cmp, pe_v, w_cv)
    pad = ((0, 0), (WINDOW, 0), (0, 0), (0, 0))
    kw_pad, vw_pad = jnp.pad(k_win, pad), jnp.pad(v_win, pad)
    band = WINDOW + Q_BLOCK

    def block(i):
        start = i * Q_BLOCK
        qb = lax.dynamic_slice_in_dim(q, start, Q_BLOCK, axis=1)
        qpos = start + jnp.arange(Q_BLOCK, dtype=jnp.int32)
        kwb = lax.dynamic_slice_in_dim(kw_pad, start, band, axis=1)
        vwb = lax.dynamic_slice_in_dim(vw_pad, start, band, axis=1)
        kwpos = start - WINDOW + jnp.arange(band, dtype=jnp.int32)
        return nsa_core(qb, qpos, kc, vc, k_slc, v_slc, kwb, vwb, kwpos)

    outs = lax.map(block, jnp.arange(S // Q_BLOCK, dtype=jnp.int32))
    o_c, o_s, o_w = [o.transpose(1, 0, 2, 3, 4).reshape(B, S, N_HEADS, HEAD_DIM) for o in outs]
    s_b = spatial_mix(vn, w_s, b_s)
    y = mixer_output(x, gate, o_c, o_s, o_w, nsa_g, z_a, u, s_b, z_b, g_a, g_b, w_br_a, w_br_b, w_out)
    wb = min(WINDOW, S)
    return y, (k_cmp, v_cmp, k_slc, v_slc, k_win[:, S - wb:], v_win[:, S - wb:])


def sample_layer(x, c, caches, page_table, lw):
    (w_ada, b_ada, norm_g, w_in, q_norm_g, k_norm_g, pe_k, pe_v, w_ck, w_cv,
     vnorm_g, vnorm_b, w_s, b_s, w_br_a, w_br_b, w_out) = lw
    cache_k_cmp, cache_v_cmp, cache_k_slc, cache_v_slc, cache_k_win, cache_v_win = caches
    B, T_new, _ = x.shape
    past = page_table.shape[1] * PAGE_SIZE
    pos = past + jnp.arange(T_new, dtype=jnp.int32)
    gate, q, keys, vals, nsa_g, z_a, u, vn, z_b, g_a, g_b = mixer_inputs(
        x, c, pos, w_ada, b_ada, norm_g, w_in, q_norm_g, k_norm_g, vnorm_g, vnorm_b)
    k_cmp, k_slc, k_win = keys[:, :, 0], keys[:, :, 1], keys[:, :, 2]
    v_cmp, v_slc, v_win = vals[:, :, 0], vals[:, :, 1], vals[:, :, 2]
    total = past + T_new
    t_pad = -(-total // L_SLC) * L_SLC

    def full_rows(pool, new):
        rows = pool[page_table].reshape(B, past, N_KV, HEAD_DIM).astype(new.dtype)
        tail = jnp.zeros((B, t_pad - total, N_KV, HEAD_DIM), new.dtype)
        return jnp.concatenate([rows, new, tail], axis=1)

    kc = compress(full_rows(cache_k_cmp, k_cmp), pe_k, w_ck)
    vc = compress(full_rows(cache_v_cmp, v_cmp), pe_v, w_cv)
    ks_full = full_rows(cache_k_slc, k_slc)
    vs_full = full_rows(cache_v_slc, v_slc)
    wb = cache_k_win.shape[1]
    kw = jnp.concatenate([cache_k_win.astype(k_win.dtype), k_win], axis=1)
    vw = jnp.concatenate([cache_v_win.astype(v_win.dtype), v_win], axis=1)
    kwpos = past - wb + jnp.arange(wb + T_new, dtype=jnp.int32)
    o_c, o_s, o_w = nsa_core(q, pos, kc, vc, ks_full, vs_full, kw, vw, kwpos)
    s_b = spatial_mix(vn, w_s, b_s)
    y = mixer_output(x, gate, o_c, o_s, o_w, nsa_g, z_a, u, s_b, z_b, g_a, g_b, w_br_a, w_br_b, w_out)
    return y, (k_cmp, v_cmp, k_slc, v_slc, kw[:, T_new:], vw[:, T_new:], vn)


def setup_inputs(seed: int = 0) -> dict:
    key = jax.random.key(seed)
    ks = jax.random.split(key, 28)
    n_pages = PAST_LEN // PAGE_SIZE
    n_pool = (DEC_BATCH * n_pages * 5) // 4
    wb = min(WINDOW, PAST_LEN)

    def nrm(k, shape, s=1.0):
        return jax.random.normal(k, shape, jnp.float32) * s

    page_table = jax.random.permutation(ks[0], n_pool)[: DEC_BATCH * n_pages]
    page_table = page_table.reshape(DEC_BATCH, n_pages).astype(jnp.int32)
    paged = (DEPTH, n_pool, PAGE_SIZE, N_KV, HEAD_DIM)
    win = (DEPTH, DEC_BATCH, wb, N_KV, HEAD_DIM)
    return {
        'x_prompt': nrm(ks[1], (BATCH, SEQ, D_MODEL)),
        'x_sample': nrm(ks[2], (DEC_BATCH, DEC_SEQ, D_MODEL)),
        'cache_k_cmp': nrm(ks[3], paged),
        'cache_v_cmp': nrm(ks[4], paged),
        'cache_k_slc': nrm(ks[5], paged),
        'cache_v_slc': nrm(ks[6], paged),
        'cache_k_win': nrm(ks[7], win),
        'cache_v_win': nrm(ks[8], win),
        'page_table': page_table,
        'c_prompt': nrm(ks[9], (BATCH, D_MODEL)),
        'c_sample': nrm(ks[10], (DEC_BATCH, D_MODEL)),
        'w_ada': nrm(ks[11], (DEPTH, D_MODEL, 3 * D_MODEL), 0.5 * D_MODEL ** -0.5),
        'b_ada': nrm(ks[12], (DEPTH, 3 * D_MODEL), 0.02),
        'norm_g': 1.0 + nrm(ks[13], (DEPTH, D_MODEL), 0.02),
        'w_in': nrm(ks[14], (DEPTH, D_MODEL, D_IN), D_MODEL ** -0.5),
        'q_norm_g': 1.0 + nrm(ks[15], (DEPTH, HEAD_DIM), 0.02),
        'k_norm_g': 1.0 + nrm(ks[16], (DEPTH, HEAD_DIM), 0.02),
        'cmp_pos_k': nrm(ks[17], (DEPTH, L_CMP, HEAD_DIM), 0.1),
        'cmp_pos_v': nrm(ks[18], (DEPTH, L_CMP, HEAD_DIM), 0.1),
        'w_cmp_k': nrm(ks[19], (DEPTH, HEAD_DIM, HEAD_DIM), HEAD_DIM ** -0.5),
        'w_cmp_v': nrm(ks[20], (DEPTH, HEAD_DIM, HEAD_DIM), HEAD_DIM ** -0.5),
        'vnorm_g': 1.0 + nrm(ks[21], (DEPTH, WIDTH_B), 0.02),
        'vnorm_b': nrm(ks[22], (DEPTH, WIDTH_B), 0.02),
        'w_s': nrm(ks[23], (DEPTH, N_GROUPS_B, CHUNK, CHUNK), CHUNK ** -0.5),
        'b_s': 1.0 + nrm(ks[24], (DEPTH, N_GROUPS_B, CHUNK), 0.02),
        'w_br_a': nrm(ks[25], (DEPTH, WIDTH_A, D_MODEL), WIDTH_A ** -0.5),
        'w_br_b': nrm(ks[26], (DEPTH, WIDTH_B, D_MODEL), WIDTH_B ** -0.5),
        'w_out': nrm(ks[27], (DEPTH, D_MODEL, D_MODEL), D_MODEL ** -0.5),
    }


def reference(x_prompt, x_sample, cache_k_cmp, cache_v_cmp, cache_k_slc, cache_v_slc,
              cache_k_win, cache_v_win, page_table, c_prompt, c_sample,
              w_ada, b_ada, norm_g, w_in, q_norm_g, k_norm_g, cmp_pos_k, cmp_pos_v,
              w_cmp_k, w_cmp_v, vnorm_g, vnorm_b, w_s, b_s, w_br_a, w_br_b, w_out):
    xp, xs = x_prompt, x_sample
    p_states, s_states = [], []
    for l in range(DEPTH):
        lw = (w_ada[l], b_ada[l], norm_g[l], w_in[l], q_norm_g[l], k_norm_g[l], cmp_pos_k[l],
              cmp_pos_v[l], w_cmp_k[l], w_cmp_v[l], vnorm_g[l], vnorm_b[l], w_s[l], b_s[l],
              w_br_a[l], w_br_b[l], w_out[l])
        xp, st_p = prompt_layer(xp, c_prompt, lw)
        caches = (cache_k_cmp[l], cache_v_cmp[l], cache_k_slc[l], cache_v_slc[l],
                  cache_k_win[l], cache_v_win[l])
        xs, st_s = sample_layer(xs, c_sample, caches, page_table, lw)
        p_states.append(st_p)
        s_states.append(st_s)
    p_k_cmp, p_v_cmp, p_k_slc, p_v_slc, p_k_win, p_v_win = [jnp.stack(z) for z in zip(*p_states)]
    s_k_cmp, s_v_cmp, s_k_slc, s_v_slc, s_k_win, s_v_win, s_v_chunk = [jnp.stack(z) for z in zip(*s_states)]
    return (xp, xs, p_k_cmp, p_v_cmp, p_k_slc, p_v_slc, p_k_win, p_v_win,
            s_k_cmp, s_v_cmp, s_k_slc, s_v_slc, s_k_win, s_v_win, s_v_chunk)
```

```python
import functools

import numpy as np
import jax
import jax.numpy as jnp
from jax import lax
from jax.experimental import pallas as pl
from jax.experimental.pallas import tpu as pltpu

F32 = jnp.float32
BF16 = jnp.bfloat16

D_MODEL = 1024
HEAD_DIM = 64
N_HEADS = 8
N_KV = 2
GQA = N_HEADS // N_KV
WIDTH_A = N_HEADS * HEAD_DIM
KV_WIDTH = N_KV * HEAD_DIM
L_CMP = 32
L_SLC = 64
N_SEL = 8
WINDOW = 512
Q_BLOCK = 128
FORCE_BONUS = 1.0e4
ROPE_THETA = 10000.0
CHUNK = 128
N_GROUPS_B = 4
WIDTH_B = 512
GROUP_W_B = WIDTH_B // N_GROUPS_B
PAGE_SIZE = 128
EPS = 1e-6
NEG = -1e30
SM_SCALE = HEAD_DIM ** -0.5

LANES = 128
SUBLANES = 8
VMEM_LIMIT = 56 * 1024 * 1024

C_Q = 0
C_K = C_Q + WIDTH_A
C_V = C_K + 3 * KV_WIDTH
C_G = C_V + 3 * KV_WIDTH
C_ZA = C_G + LANES
C_U = C_ZA + WIDTH_A
C_VB = C_U + WIDTH_B
C_ZB = C_VB + WIDTH_B
C_GA = C_ZB + WIDTH_B
C_GB = C_GA + D_MODEL
C_END = C_GB + D_MODEL

PROMPT_ROWS = 256


def _dot(a, b):
    return jnp.dot(a, b, preferred_element_type=F32)


def _dot_nt(a, b):
    return lax.dot_general(a, b, (((1,), (1,)), ((), ())), preferred_element_type=F32)


def _iota(shape, dim):
    return lax.broadcasted_iota(jnp.int32, shape, dim)


def _split_bf16(x):
    hi = x.astype(BF16)
    lo = (x - hi.astype(F32)).astype(BF16)
    return hi, lo


def _head_mean_sq(x):
    w = x.shape[1]
    ones_bd = jnp.where(_iota((LANES, LANES), 0) // HEAD_DIM == _iota((LANES, LANES), 1) // HEAD_DIM,
                        1.0, 0.0).astype(BF16)
    hi, lo = _split_bf16(x * x)
    cols = []
    for c in range(w // LANES):
        sl = slice(c * LANES, (c + 1) * LANES)
        cols.append(_dot(hi[:, sl], ones_bd) + _dot(lo[:, sl], ones_bd))
    return jnp.concatenate(cols, axis=1) * (1.0 / HEAD_DIM)


def _tile_lanes(t, width):
    return jnp.concatenate([t] * (width // t.shape[1]), axis=1)


def _norm_rope(x, g, cos, sin):
    w = x.shape[1]
    y = x * lax.rsqrt(_head_mean_sq(x) + EPS) * _tile_lanes(g, w)
    first_half = (_iota(y.shape, 1) % HEAD_DIM) < (HEAD_DIM // 2)
    rot = jnp.where(first_half, pltpu.roll(y, w - HEAD_DIM // 2, 1), pltpu.roll(y, HEAD_DIM // 2, 1))
    return y * _tile_lanes(cos, w) + rot * _tile_lanes(sin, w)


def _silu(z):
    return z * jax.nn.sigmoid(z)


def _project(x, shift, scale, norm_g, w_ref):
    ms = jnp.mean(x * x, axis=-1, keepdims=True)
    h = (x * lax.rsqrt(ms + EPS) * norm_g) * (1.0 + scale) + shift
    hb = h.astype(BF16)
    return lambda lo, hi: _dot(hb, w_ref[:, lo:hi])


def _layer_norm(v, g, b):
    mu = jnp.mean(v, axis=-1, keepdims=True)
    d = v - mu
    var = jnp.mean(d * d, axis=-1, keepdims=True)
    return d * lax.rsqrt(var + EPS) * g + b


def _ada_kernel(c_ref, w_ref, b_ref, o_ref):
    o_ref[...] = _dot(c_ref[...].astype(BF16), w_ref[...].astype(BF16)) + b_ref[...]


def _ada_call(c_all, w_ada, b_ada):
    rows = c_all.shape[0]
    n = w_ada.shape[1]
    return pl.pallas_call(
        _ada_kernel,
        grid=(n // D_MODEL,),
        in_specs=[pl.BlockSpec((rows, D_MODEL), lambda j: (0, 0)),
                  pl.BlockSpec((D_MODEL, D_MODEL), lambda j: (0, j)),
                  pl.BlockSpec((1, D_MODEL), lambda j: (0, j))],
        out_specs=pl.BlockSpec((rows, D_MODEL), lambda j: (0, j)),
        out_shape=jax.ShapeDtypeStruct((rows, n), F32),
        compiler_params=pltpu.CompilerParams(vmem_limit_bytes=VMEM_LIMIT),
        name="ada",
    )(c_all, w_ada, b_ada)


def _compress_rows(rows, pe, w_bd):
    t = rows.shape[0]
    pooled = jnp.sum(rows.reshape(t // L_CMP, L_CMP, KV_WIDTH) + pe[None], axis=1) * (1.0 / L_CMP)
    return _dot(pooled.astype(BF16), w_bd)


def _p_proj_kernel(x_ref, shift_ref, scale_ref, ng_ref, w_ref, cos_ref, sin_ref, qg_ref, kg_ref,
                   pek_ref, pev_ref, wck_ref, wcv_ref, vng_ref, vnb_ref, ws_ref, bst_ref, wbrb_ref,
                   q_out, kc_t, ks_t, kw_t, vc_t, vs_t, vw_t, kcmp_out, vcmp_out,
                   gates_out, sza_out, sga_out, mb_out):
    tm = x_ref.shape[1]
    seg = _project(x_ref[0], shift_ref[0], scale_ref[0], ng_ref[...], w_ref)
    cos, sin = cos_ref[...], sin_ref[...]

    q_out[0] = _norm_rope(seg(C_Q, C_K), qg_ref[...], cos, sin).astype(BF16)
    k = _norm_rope(seg(C_K, C_V), kg_ref[...], cos, sin)
    v = seg(C_V, C_G)
    for br, (k_t, v_t) in enumerate(((kc_t, vc_t), (ks_t, vs_t), (kw_t, vw_t))):
        sl = slice(br * KV_WIDTH, (br + 1) * KV_WIDTH)
        k_t[0] = k[:, sl].T
        v_t[0] = v[:, sl].T
    kcmp_out[0] = _compress_rows(k[:, 0:KV_WIDTH], pek_ref[...], wck_ref[...])
    vcmp_out[0] = _compress_rows(v[:, 0:KV_WIDTH], pev_ref[...], wcv_ref[...])

    gates_out[0] = jax.nn.sigmoid(seg(C_G, C_ZA))
    sza_out[0] = _silu(seg(C_ZA, C_U))
    sga_out[0] = jax.nn.sigmoid(seg(C_GA, C_GB))

    vn = _layer_norm(seg(C_VB, C_ZB), vng_ref[...], vnb_ref[...]).astype(BF16)
    causal = _iota((CHUNK, CHUNK), 0) >= _iota((CHUNK, CHUNK), 1)
    chunks = []
    for c in range(tm // CHUNK):
        groups = []
        for g in range(N_GROUPS_B):
            wsg = jnp.where(causal, ws_ref[g], 0.0).astype(BF16)
            vg = vn[c * CHUNK:(c + 1) * CHUNK, g * GROUP_W_B:(g + 1) * GROUP_W_B]
            groups.append(_dot(wsg, vg) + bst_ref[:, g:g + 1])
        chunks.append(jnp.concatenate(groups, axis=1))
    s_b = jnp.concatenate(chunks, axis=0)
    t = seg(C_U, C_VB) * s_b * _silu(seg(C_ZB, C_GA))
    mb_out[0] = jax.nn.sigmoid(seg(C_GB, C_END)) * _dot(t.astype(BF16), wbrb_ref[...])


def _full(shape):
    nd = len(shape)
    return pl.BlockSpec(shape, lambda *_: (0,) * nd)


def _p_proj_call(x, shift, scale, norm_g, w_all, cos_t, sin_t, qg, kg, pek, pev, wck, wcv, vng, vnb,
                 w_s, bs_t, w_br_b):
    b, s, _ = x.shape
    tm = PROMPT_ROWS
    row = lambda w: pl.BlockSpec((1, tm, w), lambda bi, i: (bi, i, 0))
    tok_minor = pl.BlockSpec((1, KV_WIDTH, tm), lambda bi, i: (bi, 0, i))
    per_batch = pl.BlockSpec((1, 1, D_MODEL), lambda bi, i: (bi, 0, 0))
    cmp_spec = pl.BlockSpec((1, tm // L_CMP, KV_WIDTH), lambda bi, i: (bi, i, 0))
    tab = pl.BlockSpec((tm, LANES), lambda bi, i: (i, 0))
    in_specs = [row(D_MODEL), per_batch, per_batch, _full(norm_g.shape), _full(w_all.shape), tab, tab,
                _full(qg.shape), _full(kg.shape), _full(pek.shape), _full(pev.shape), _full(wck.shape),
                _full(wcv.shape), _full(vng.shape), _full(vnb.shape), _full(w_s.shape), _full(bs_t.shape),
                _full(w_br_b.shape)]
    out_specs = [row(WIDTH_A)] + [tok_minor] * 6 + [cmp_spec, cmp_spec,
                                                     row(LANES), row(WIDTH_A), row(D_MODEL), row(D_MODEL)]
    tm_shape = jax.ShapeDtypeStruct((b, KV_WIDTH, s), F32)
    cmp_shape = jax.ShapeDtypeStruct((b, s // L_CMP, KV_WIDTH), F32)
    out_shape = [jax.ShapeDtypeStruct((b, s, WIDTH_A), BF16)] + [tm_shape] * 6 + [cmp_shape, cmp_shape] + [
        jax.ShapeDtypeStruct((b, s, LANES), F32), jax.ShapeDtypeStruct((b, s, WIDTH_A), F32),
        jax.ShapeDtypeStruct((b, s, D_MODEL), F32), jax.ShapeDtypeStruct((b, s, D_MODEL), F32)]
    return pl.pallas_call(
        _p_proj_kernel,
        grid=(b, s // tm),
        in_specs=in_specs,
        out_specs=out_specs,
        out_shape=out_shape,
        compiler_params=pltpu.CompilerParams(dimension_semantics=("arbitrary", "arbitrary"),
                                             vmem_limit_bytes=VMEM_LIMIT),
        name="p_proj",
    )(x, shift, scale, norm_g, w_all, cos_t, sin_t, qg, kg, pek, pev, wck, wcv, vng, vnb, w_s, bs_t, w_br_b)


def _select_blocks(imp_c, qblk, n_blocks):
    ratio = L_SLC // L_CMP
    assert ratio == 2
    lane = _iota(imp_c.shape, 1)
    imp = imp_c + pltpu.roll(imp_c, LANES - 1, 1)
    blk = lane // ratio
    forced = jnp.where((blk == 0) | (blk == qblk), 1.0, 0.0)
    score = jnp.where(blk <= qblk, imp + FORCE_BONUS * forced, NEG)
    rank = jnp.zeros(imp_c.shape, F32)
    for j in range(n_blocks):
        vj = score[:, ratio * j:ratio * j + 1]
        earlier = jnp.where(lane > ratio * j, 1.0, 0.0)
        rank = rank + jnp.where(vj > score, 1.0, jnp.where(vj == score, earlier, 0.0))
    cand = (lane % ratio == 0) & (lane < ratio * n_blocks)
    return jnp.where(cand & (rank < float(min(N_SEL, n_blocks))), 1.0, 0.0)


def _softmax_rows(s, valid):
    sm = jnp.where(valid, s, NEG)
    e = jnp.exp(sm - jnp.max(sm, axis=-1, keepdims=True))
    return e, 1.0 / jnp.sum(e, axis=-1, keepdims=True)


def _merge_and_project(o_a, sza, sga, mb, x, gate, wbra_ref, wout_ref):
    a = _dot((o_a * sza).astype(BF16), wbra_ref[...])
    m = sga * a + mb
    return x + gate * _dot(m.astype(BF16), wout_ref[...])


def _p_attn_kernel(q_ref, kst_ref, vst_ref, kwt_ref, vwt_ref, kc_ref, vc_ref, gates_ref, sza_ref, sga_ref,
                   mb_ref, x_ref, gate_ref, wbra_ref, wout_ref, y_ref):
    i = pl.program_id(1)
    tq = Q_BLOCK
    n_cmp = kc_ref.shape[1]
    n_slc = kst_ref.shape[2] // L_SLC
    q = q_ref[0]
    qpos1 = i * tq + _iota((tq, 1), 0)
    qpos = jnp.concatenate([qpos1] * GQA, axis=0)
    gates = gates_ref[0]
    band = WINDOW + tq
    ws = jnp.maximum(i * tq - WINDOW, 0)
    pad = jnp.zeros((LANES - n_cmp, HEAD_DIM), BF16)
    heads = []
    for g in range(N_KV):
        gl = slice(g * HEAD_DIM, (g + 1) * HEAD_DIM)
        qg = jnp.concatenate([q[:, h * HEAD_DIM:(h + 1) * HEAD_DIM] for h in range(g * GQA, (g + 1) * GQA)], axis=0)

        kc = jnp.concatenate([kc_ref[0][:, gl].astype(BF16), pad], axis=0)
        vc = jnp.concatenate([vc_ref[0][:, gl].astype(BF16), pad], axis=0)
        s_c = _dot_nt(qg, kc) * SM_SCALE
        c = _iota(s_c.shape, 1)
        mc = ((c + 1) * L_CMP - 1 <= qpos) & (c < n_cmp)
        e_c, r_c = _softmax_rows(s_c, mc)
        p_c = jnp.where(mc, e_c * r_c, 0.0)
        o_c = _dot(p_c.astype(BF16), vc)
        imp = p_c[0:tq]
        for r in range(1, GQA):
            imp = imp + p_c[r * tq:(r + 1) * tq]
        sel = _select_blocks(imp, qpos1 // L_SLC, n_slc).astype(BF16)

        def body(kc_i, carry):
            m, l, acc = carry
            off = pl.multiple_of(kc_i * LANES, LANES)
            kt = kst_ref[0, gl, pl.ds(off, LANES)].astype(BF16)
            vt = vst_ref[0, gl, pl.ds(off, LANES)].astype(BF16)
            expand = jnp.where(_iota((LANES, LANES), 0) ==
                               2 * ((off + _iota((LANES, LANES), 1)) // L_SLC), 1.0, 0.0).astype(BF16)
            picked = _dot(sel, expand)
            picked = jnp.concatenate([picked] * GQA, axis=0)
            kpos = off + _iota((GQA * tq, LANES), 1)
            s = _dot(qg, kt) * SM_SCALE
            s = jnp.where(picked > 0.5, jnp.where(kpos <= qpos, s, NEG), NEG)
            m_new = jnp.maximum(m, jnp.max(s, axis=-1, keepdims=True))
            alpha = jnp.exp(m - m_new)
            p = jnp.exp(s - m_new)
            l = alpha * l + jnp.sum(p, axis=-1, keepdims=True)
            acc = alpha * acc + _dot_nt(p.astype(BF16), vt)
            return m_new, l, acc

        init = (jnp.full((GQA * tq, 1), NEG, F32), jnp.zeros((GQA * tq, 1), F32),
                jnp.zeros((GQA * tq, HEAD_DIM), F32))
        _, l_s, acc_s = lax.fori_loop(0, (i + 1) * (tq // LANES), body, init)
        o_s = acc_s / l_s

        off_w = pl.multiple_of(ws, LANES)
        kw = kwt_ref[0, gl, pl.ds(off_w, band)].astype(BF16)
        vw = vwt_ref[0, gl, pl.ds(off_w, band)].astype(BF16)
        s_w = _dot(qg, kw) * SM_SCALE
        kwpos = ws + _iota(s_w.shape, 1)
        m_w = (kwpos <= qpos) & (kwpos > qpos - WINDOW)
        e_w, r_w = _softmax_rows(s_w, m_w)
        o_w = _dot_nt(e_w.astype(BF16), vw) * r_w

        for r in range(GQA):
            h = g * GQA + r
            rows = slice(r * tq, (r + 1) * tq)
            heads.append(gates[:, 3 * h:3 * h + 1] * o_c[rows] + gates[:, 3 * h + 1:3 * h + 2] * o_s[rows]
                         + gates[:, 3 * h + 2:3 * h + 3] * o_w[rows])
    o_a = jnp.concatenate(heads, axis=1)
    y_ref[0] = _merge_and_project(o_a, sza_ref[0], sga_ref[0], mb_ref[0], x_ref[0], gate_ref[0],
                                  wbra_ref, wout_ref)


def _p_attn_call(q, ks_t, vs_t, kw_t, vw_t, kc, vc, gates, sza, sga, mb, x, gate, w_br_a, w_out):
    b, s, _ = x.shape
    tq = Q_BLOCK
    row = lambda w: pl.BlockSpec((1, tq, w), lambda bi, i: (bi, i, 0))
    seq = lambda a: pl.BlockSpec((1,) + a.shape[1:], lambda bi, i: (bi, 0, 0))
    in_specs = [row(WIDTH_A), seq(ks_t), seq(vs_t), seq(kw_t), seq(vw_t), seq(kc), seq(vc),
                row(LANES), row(WIDTH_A), row(D_MODEL), row(D_MODEL), row(D_MODEL), seq(gate),
                _full(w_br_a.shape), _full(w_out.shape)]
    return pl.pallas_call(
        _p_attn_kernel,
        grid=(b, s // tq),
        in_specs=in_specs,
        out_specs=row(D_MODEL),
        out_shape=jax.ShapeDtypeStruct((b, s, D_MODEL), F32),
        compiler_params=pltpu.CompilerParams(dimension_semantics=("arbitrary", "arbitrary"),
                                             vmem_limit_bytes=VMEM_LIMIT),
        name="p_attn",
    )(q, ks_t, vs_t, kw_t, vw_t, kc, vc, gates, sza, sga, mb, x, gate, w_br_a, w_out)


def _s_proj_kernel(x_ref, shift_ref, scale_ref, ng_ref, w_ref, cos_ref, sin_ref, qg_ref, kg_ref,
                   vng_ref, vnb_ref, ws0_ref, bs0_ref, wbrb_ref,
                   q_out, k_out, v_out, kt_out, vt_out, gates_out, sza_out, sga_out, mb_out, vn_out):
    seg = _project(x_ref[...], shift_ref[...], scale_ref[...], ng_ref[...], w_ref)
    cos, sin = cos_ref[...], sin_ref[...]
    q_out[...] = _norm_rope(seg(C_Q, C_K), qg_ref[...], cos, sin)
    k = _norm_rope(seg(C_K, C_V), kg_ref[...], cos, sin)
    v = seg(C_V, C_G)
    k_out[...] = k
    v_out[...] = v
    for br in range(3):
        sl = slice(br * KV_WIDTH, (br + 1) * KV_WIDTH)
        kt_out[br] = k[:, sl].T
        vt_out[br] = v[:, sl].T
    gates_out[...] = jax.nn.sigmoid(seg(C_G, C_ZA))
    sza_out[...] = _silu(seg(C_ZA, C_U))
    sga_out[...] = jax.nn.sigmoid(seg(C_GA, C_GB))
    vn = _layer_norm(seg(C_VB, C_ZB), vng_ref[...], vnb_ref[...])
    vn_out[...] = vn
    s_b = ws0_ref[...] * vn + bs0_ref[...]
    t = seg(C_U, C_VB) * s_b * _silu(seg(C_ZB, C_GA))
    mb_out[...] = jax.nn.sigmoid(seg(C_GB, C_END)) * _dot(t.astype(BF16), wbrb_ref[...])


def _s_proj_call(x, shift, scale, norm_g, w_all, cos1, sin1, qg, kg, vng, vnb, ws0, bs0, w_br_b):
    n = x.shape[0]
    args = (x, shift, scale, norm_g, w_all, cos1, sin1, qg, kg, vng, vnb, ws0, bs0, w_br_b)
    sds = lambda *shape: jax.ShapeDtypeStruct(shape, F32)
    out_shape = [sds(n, WIDTH_A), sds(n, 3 * KV_WIDTH), sds(n, 3 * KV_WIDTH), sds(3, KV_WIDTH, n),
                 sds(3, KV_WIDTH, n), sds(n, LANES), sds(n, WIDTH_A), sds(n, D_MODEL), sds(n, D_MODEL),
                 sds(n, WIDTH_B)]
    return pl.pallas_call(
        _s_proj_kernel,
        grid=(1,),
        in_specs=[_full(a.shape) for a in args],
        out_specs=[_full(o.shape) for o in out_shape],
        out_shape=out_shape,
        compiler_params=pltpu.CompilerParams(vmem_limit_bytes=VMEM_LIMIT),
        name="s_proj",
    )(*args)


def _s_attn_kernel(pt_ref, q_ref, gates_ref, knew_ref, vnew_ref, knewt_ref, vnewt_ref, kwin_ref, vwin_ref,
                   pek_ref, pev_ref, wck_ref, wcv_ref, pool_ref, expand_ref,
                   kc_hbm, vc_hbm, ks_hbm, vs_hbm,
                   oa_ref, okw_ref, ovw_ref, buf, sem):
    b = pl.program_id(0)
    nb = pl.num_programs(0)
    slot = lax.rem(b, 2)
    n_pages = pt_ref.shape[1]
    past = n_pages * PAGE_SIZE
    qpos = past
    t_pad = buf.shape[3]
    caches = (kc_hbm, vc_hbm, ks_hbm, vs_hbm)

    def page_copies(seq, slot_):
        return [pltpu.make_async_copy(hbm.at[pt_ref[seq, p]],
                                      buf.at[slot_, c, :, pl.ds(p * PAGE_SIZE, PAGE_SIZE)],
                                      sem.at[slot_, c])
                for c, hbm in enumerate(caches) for p in range(n_pages)]

    @pl.when(b == 0)
    def _():
        for cp in page_copies(0, 0):
            cp.start()

    @pl.when(b + 1 < nb)
    def _():
        for cp in page_copies(b + 1, 1 - slot):
            cp.start()

    lane = _iota((KV_WIDTH, knewt_ref.shape[2]), 1)
    col = lambda ref, br: jnp.sum(jnp.where(lane == b, ref[br], 0.0), axis=1, keepdims=True)
    knew = knew_ref[pl.ds(b, 1), :]
    vnew = vnew_ref[pl.ds(b, 1), :]

    hrow = _iota((N_HEADS, 1), 0)
    first_group = hrow < GQA
    by_group = lambda f: jnp.where(first_group, f(0), f(1))
    gl = lambda g: slice(g * HEAD_DIM, (g + 1) * HEAD_DIM)
    qb = q_ref[0].astype(BF16)

    for cp in page_copies(b, slot):
        cp.wait()

    tail_lane = _iota((KV_WIDTH, PAGE_SIZE), 1)
    buf[slot, 2, :, pl.ds(past, PAGE_SIZE)] = jnp.where(tail_lane == 0, col(knewt_ref, 1), 0.0)
    buf[slot, 3, :, pl.ds(past, PAGE_SIZE)] = jnp.where(tail_lane == 0, col(vnewt_ref, 1), 0.0)

    n_cmp = -(-(past + 1) // L_SLC) * L_SLC // L_CMP
    n_slc = n_cmp * L_CMP // L_SLC

    def summaries(c, new_row, pe_ref, w_ref):
        rows_t = buf[slot, c, :, pl.ds(0, past)]
        hi, lo = _split_bf16(rows_t)
        pe_sum = jnp.sum(pe_ref[...], axis=0, keepdims=True)
        pooled = (_dot_nt(pool_ref[...], hi) + _dot_nt(pool_ref[...], lo) + pe_sum) * (1.0 / L_CMP)
        r = _iota((SUBLANES, KV_WIDTH), 0)
        tail = jnp.where(r == 0, new_row + pe_sum, jnp.where(r == 1, pe_sum, 0.0)) * (1.0 / L_CMP)
        zeros = jnp.zeros((LANES - pooled.shape[0] - SUBLANES, KV_WIDTH), F32)
        return _dot(jnp.concatenate([pooled, tail, zeros], axis=0).astype(BF16), w_ref[...])

    kc = summaries(0, knew[:, 0:KV_WIDTH], pek_ref, wck_ref).astype(BF16)
    vc = summaries(1, vnew[:, 0:KV_WIDTH], pev_ref, wcv_ref).astype(BF16)
    s_c = by_group(lambda g: _dot_nt(qb, kc[:, gl(g)])) * SM_SCALE
    c = _iota(s_c.shape, 1)
    mc = ((c + 1) * L_CMP - 1 <= qpos) & (c < n_cmp)
    e_c, r_c = _softmax_rows(s_c, mc)
    p_c = jnp.where(mc, e_c * r_c, 0.0)
    o_c = by_group(lambda g: _dot(p_c.astype(BF16), vc[:, gl(g)]))
    imp = by_group(lambda g: jnp.sum(p_c[g * GQA:(g + 1) * GQA], axis=0, keepdims=True))
    imp = jnp.broadcast_to(imp, p_c.shape)
    sel = _select_blocks(imp, jnp.full((N_HEADS, 1), qpos // L_SLC, jnp.int32), n_slc)

    picked = _dot(sel.astype(BF16), expand_ref[...])
    s_s = by_group(lambda g: _dot(qb, buf[slot, 2, gl(g), :].astype(BF16))) * SM_SCALE
    kpos = _iota(s_s.shape, 1)
    e_s, r_s = _softmax_rows(s_s, (picked > 0.5) & (kpos <= qpos))
    o_s = by_group(lambda g: _dot_nt(e_s.astype(BF16), buf[slot, 3, gl(g), :].astype(BF16))) * r_s

    wb = kwin_ref.shape[2]
    wlane = _iota((KV_WIDTH, wb), 1)
    kw = jnp.where(wlane == wb - 1, col(knewt_ref, 2), pltpu.roll(kwin_ref[0], wb - 1, 1))
    vw = jnp.where(wlane == wb - 1, col(vnewt_ref, 2), pltpu.roll(vwin_ref[0], wb - 1, 1))
    okw_ref[0] = kw
    ovw_ref[0] = vw
    s_w = by_group(lambda g: _dot(qb, kw[gl(g)].astype(BF16))) * SM_SCALE
    kwpos = past - wb + 1 + _iota(s_w.shape, 1)
    e_w, r_w = _softmax_rows(s_w, (kwpos <= qpos) & (kwpos > qpos - WINDOW) & (kwpos >= 0))
    o_w = by_group(lambda g: _dot_nt(e_w.astype(BF16), vw[gl(g)].astype(BF16))) * r_w

    gates = gates_ref[0]
    oa_ref[0] = gates[:, 0:1] * o_c + gates[:, 1:2] * o_s + gates[:, 2:3] * o_w


def _s_attn_call(page_table, q3, gates3, knew, vnew, knew_t, vnew_t, kwin_t, vwin_t, pek, pev, wck, wcv,
                 kc_pool, vc_pool, ks_pool, vs_pool):
    n, n_pages = page_table.shape
    past = n_pages * PAGE_SIZE
    t_pad = past + PAGE_SIZE
    wb = kwin_t.shape[2]
    tok = np.arange(past)
    pool = jnp.asarray((tok[None, :] // L_CMP == np.arange(past // L_CMP)[:, None]), BF16)
    expand = jnp.asarray(np.arange(LANES)[:, None] == 2 * (np.arange(t_pad)[None, :] // L_SLC), BF16)
    per_seq = lambda a: pl.BlockSpec((1,) + a.shape[1:], lambda bi, pt: (bi,) + (0,) * (a.ndim - 1))
    full = lambda a: pl.BlockSpec(a.shape, lambda bi, pt: (0,) * a.ndim)
    hbm = pl.BlockSpec(memory_space=pl.ANY)
    resident = (knew, vnew, knew_t, vnew_t)
    consts = (pek, pev, wck, wcv, pool, expand)
    grid_spec = pltpu.PrefetchScalarGridSpec(
        num_scalar_prefetch=1,
        grid=(n,),
        in_specs=[per_seq(q3), per_seq(gates3)] + [full(a) for a in resident]
                 + [per_seq(kwin_t), per_seq(vwin_t)] + [full(a) for a in consts] + [hbm] * 4,
        out_specs=[per_seq(q3), per_seq(kwin_t), per_seq(vwin_t)],
        scratch_shapes=[pltpu.VMEM((2, 4, KV_WIDTH, t_pad), F32), pltpu.SemaphoreType.DMA((2, 4))],
    )
    return pl.pallas_call(
        _s_attn_kernel,
        grid_spec=grid_spec,
        out_shape=[jax.ShapeDtypeStruct(q3.shape, F32), jax.ShapeDtypeStruct(kwin_t.shape, F32),
                   jax.ShapeDtypeStruct(vwin_t.shape, F32)],
        compiler_params=pltpu.CompilerParams(dimension_semantics=("arbitrary",),
                                             vmem_limit_bytes=VMEM_LIMIT),
        name="s_attn",
    )(page_table, q3, gates3, knew, vnew, knew_t, vnew_t, kwin_t, vwin_t, pek, pev, wck, wcv, pool, expand,
      kc_pool, vc_pool, ks_pool, vs_pool)


def _s_out_kernel(oa_ref, sza_ref, sga_ref, mb_ref, x_ref, gate_ref, wbra_ref, wout_ref, y_ref):
    y_ref[...] = _merge_and_project(oa_ref[...], sza_ref[...], sga_ref[...], mb_ref[...], x_ref[...],
                                    gate_ref[...], wbra_ref, wout_ref)


def _s_out_call(o_a, sza, sga, mb, x, gate, w_br_a, w_out):
    args = (o_a, sza, sga, mb, x, gate, w_br_a, w_out)
    return pl.pallas_call(
        _s_out_kernel,
        grid=(1,),
        in_specs=[_full(a.shape) for a in args],
        out_specs=_full(x.shape),
        out_shape=jax.ShapeDtypeStruct(x.shape, F32),
        compiler_params=pltpu.CompilerParams(vmem_limit_bytes=VMEM_LIMIT),
        name="s_out",
    )(*args)


def _rope_tables(pos):
    half = HEAD_DIM // 2
    inv = ROPE_THETA ** (-jnp.arange(half, dtype=F32) * 2.0 / HEAD_DIM)
    ang = pos.astype(F32)[:, None] * inv[None, :]
    cos, sin = jnp.cos(ang), jnp.sin(ang)
    cos_t = jnp.concatenate([cos, cos] * (LANES // HEAD_DIM), axis=1)
    sin_t = jnp.concatenate([-sin, sin] * (LANES // HEAD_DIM), axis=1)
    return cos_t, sin_t


def _token_minor(a):
    b, t = a.shape[:2]
    return jnp.transpose(a, (0, 2, 3, 1)).reshape(b, KV_WIDTH, t)


def _token_major(a_t):
    b, _, t = a_t.shape
    return jnp.transpose(a_t.reshape(b, N_KV, HEAD_DIM, t), (0, 3, 1, 2))


def kernel(x_prompt, x_sample, cache_k_cmp, cache_v_cmp, cache_k_slc, cache_v_slc, cache_k_win, cache_v_win, page_table, c_prompt, c_sample, w_ada, b_ada, norm_g, w_in, q_norm_g, k_norm_g, cmp_pos_k, cmp_pos_v, w_cmp_k, w_cmp_v, vnorm_g, vnorm_b, w_s, b_s, w_br_a, w_br_b, w_out):
    assert w_ada.shape[0] == 1, "single layer"
    b, s, _ = x_prompt.shape
    n = x_sample.shape[0]
    assert x_sample.shape[1] == 1
    n_pages = page_table.shape[1]
    past = n_pages * PAGE_SIZE

    w = w_in[0]
    n_gate = 3 * N_HEADS
    w_all = jnp.concatenate([w[:, :C_G], w[:, C_G:C_G + n_gate], jnp.zeros((D_MODEL, LANES - n_gate), F32),
                             w[:, C_G + n_gate:]], axis=1).astype(BF16)
    assert w_all.shape[1] == C_END
    eye = jnp.eye(N_KV, dtype=F32)
    wck = jnp.kron(eye, w_cmp_k[0]).astype(BF16)
    wcv = jnp.kron(eye, w_cmp_v[0]).astype(BF16)
    pek = jnp.tile(cmp_pos_k[0], (1, N_KV))
    pev = jnp.tile(cmp_pos_v[0], (1, N_KV))
    qg = jnp.tile(q_norm_g, (1, LANES // HEAD_DIM))
    kg = jnp.tile(k_norm_g, (1, LANES // HEAD_DIM))
    w_br_a_b, w_br_b_b, w_out_b = w_br_a[0].astype(BF16), w_br_b[0].astype(BF16), w_out[0].astype(BF16)

    mod = _ada_call(jnp.concatenate([c_prompt, c_sample], axis=0), w_ada[0], b_ada)
    shift, scale, gate = mod[:, :D_MODEL], mod[:, D_MODEL:2 * D_MODEL], mod[:, 2 * D_MODEL:]

    cos_p, sin_p = _rope_tables(jnp.arange(s, dtype=jnp.int32))
    (q, kc_t, ks_t, kw_t, vc_t, vs_t, vw_t, kcmp, vcmp, gates, sza, sga, mb) = _p_proj_call(
        x_prompt, shift[:b, None], scale[:b, None], norm_g, w_all, cos_p, sin_p, qg, kg, pek, pev, wck, wcv,
        vnorm_g, vnorm_b, w_s[0], b_s[0].T, w_br_b_b)
    y_prompt = _p_attn_call(q, ks_t, vs_t, kw_t, vw_t, kcmp, vcmp, gates, sza, sga, mb, x_prompt,
                            gate[:b, None], w_br_a_b, w_out_b)
    wb_p = min(WINDOW, s)
    p_states = [_token_major(a)[None] for a in (kc_t, vc_t, ks_t, vs_t, kw_t[:, :, s - wb_p:], vw_t[:, :, s - wb_p:])]

    xs = x_sample.reshape(n, D_MODEL)
    cos_s, sin_s = _rope_tables(jnp.full((1,), past, jnp.int32))
    ws0 = jnp.repeat(w_s[0, :, 0, 0], GROUP_W_B)[None]
    bs0 = jnp.repeat(b_s[0, :, 0], GROUP_W_B)[None]
    (q_s, k_s, v_s, kt_s, vt_s, gates_s, sza_s, sga_s, mb_s, vn_s) = _s_proj_call(
        xs, shift[b:], scale[b:], norm_g, w_all, cos_s, sin_s, qg, kg, vnorm_g, vnorm_b, ws0, bs0, w_br_b_b)
    pools = [_token_minor(c[0]) for c in (cache_k_cmp, cache_v_cmp, cache_k_slc, cache_v_slc)]
    o_a, kwin_new, vwin_new = _s_attn_call(
        page_table, q_s.reshape(n, N_HEADS, HEAD_DIM), gates_s[:, :3 * N_HEADS].reshape(n, N_HEADS, 3),
        k_s, v_s, kt_s, vt_s, _token_minor(cache_k_win[0]), _token_minor(cache_v_win[0]),
        pek, pev, wck, wcv, *pools)
    y_sample = _s_out_call(o_a.reshape(n, WIDTH_A), sza_s, sga_s, mb_s, xs, gate[b:], w_br_a_b, w_out_b)

    new_rows = lambda t, br: jnp.transpose(t[br].reshape(N_KV, HEAD_DIM, n), (2, 0, 1))[None, :, None]
    s_states = [new_rows(kt_s, 0), new_rows(vt_s, 0), new_rows(kt_s, 1), new_rows(vt_s, 1),
                _token_major(kwin_new)[None], _token_major(vwin_new)[None], vn_s[None, :, None]]
    return (y_prompt, y_sample.reshape(n, 1, D_MODEL), *p_states, *s_states)
```

```python
import functools

import numpy as np
import jax
import jax.numpy as jnp
from jax import lax
from jax.experimental import pallas as pl
from jax.experimental.pallas import tpu as pltpu

F32 = jnp.float32
BF16 = jnp.bfloat16

D_MODEL = 1024
HEAD_DIM = 64
N_HEADS = 8
N_KV = 2
GQA = N_HEADS // N_KV
WIDTH_A = N_HEADS * HEAD_DIM
KV_WIDTH = N_KV * HEAD_DIM
L_CMP = 32
L_SLC = 64
N_SEL = 8
WINDOW = 512
Q_BLOCK = 128
FORCE_BONUS = 1.0e4
ROPE_THETA = 10000.0
CHUNK = 128
N_GROUPS_B = 4
WIDTH_B = 512
GROUP_W_B = WIDTH_B // N_GROUPS_B
PAGE_SIZE = 128
EPS = 1e-6
NEG = -1e30
SM_SCALE = HEAD_DIM ** -0.5

LANES = 128
SUBLANES = 8
VMEM_LIMIT = 56 * 1024 * 1024

C_Q = 0
C_K = C_Q + WIDTH_A
C_V = C_K + 3 * KV_WIDTH
C_G = C_V + 3 * KV_WIDTH
C_ZA = C_G + LANES
C_U = C_ZA + WIDTH_A
C_VB = C_U + WIDTH_B
C_ZB = C_VB + WIDTH_B
C_GA = C_ZB + WIDTH_B
C_GB = C_GA + D_MODEL
C_END = C_GB + D_MODEL

PROMPT_ROWS = 256
SEL_CHUNK = 256


def _dot(a, b):
    return jnp.dot(a, b, preferred_element_type=F32)


def _dot_nt(a, b):
    return lax.dot_general(a, b, (((1,), (1,)), ((), ())), preferred_element_type=F32)


def _iota(shape, dim):
    return lax.broadcasted_iota(jnp.int32, shape, dim)


def _split_bf16(x):
    hi = x.astype(BF16)
    lo = (x - hi.astype(F32)).astype(BF16)
    return hi, lo


def _head_mean_sq(x):
    w = x.shape[1]
    ones_bd = jnp.where(_iota((LANES, LANES), 0) // HEAD_DIM == _iota((LANES, LANES), 1) // HEAD_DIM,
                        1.0, 0.0).astype(BF16)
    hi, lo = _split_bf16(x * x)
    cols = []
    for c in range(w // LANES):
        sl = slice(c * LANES, (c + 1) * LANES)
        cols.append(_dot(hi[:, sl], ones_bd) + _dot(lo[:, sl], ones_bd))
    return jnp.concatenate(cols, axis=1) * (1.0 / HEAD_DIM)


def _tile_lanes(t, width):
    return jnp.concatenate([t] * (width // t.shape[1]), axis=1)


def _norm_rope(x, g, cos, sin):
    w = x.shape[1]
    y = x * lax.rsqrt(_head_mean_sq(x) + EPS) * _tile_lanes(g, w)
    first_half = (_iota(y.shape, 1) % HEAD_DIM) < (HEAD_DIM // 2)
    rot = jnp.where(first_half, pltpu.roll(y, w - HEAD_DIM // 2, 1), pltpu.roll(y, HEAD_DIM // 2, 1))
    return y * _tile_lanes(cos, w) + rot * _tile_lanes(sin, w)


def _silu(z):
    return z * jax.nn.sigmoid(z)


def _project(x, shift, scale, norm_g, w_ref):
    ms = jnp.mean(x * x, axis=-1, keepdims=True)
    h = (x * lax.rsqrt(ms + EPS) * norm_g) * (1.0 + scale) + shift
    hb = h.astype(BF16)
    return lambda lo, hi: _dot(hb, w_ref[:, lo:hi])


def _layer_norm(v, g, b):
    mu = jnp.mean(v, axis=-1, keepdims=True)
    d = v - mu
    var = jnp.mean(d * d, axis=-1, keepdims=True)
    return d * lax.rsqrt(var + EPS) * g + b


def _ada_kernel(c_ref, w_ref, b_ref, o_ref):
    o_ref[...] = _dot(c_ref[...].astype(BF16), w_ref[...].astype(BF16)) + b_ref[...]


def _ada_call(c_all, w_ada, b_ada):
    rows = c_all.shape[0]
    n = w_ada.shape[1]
    return pl.pallas_call(
        _ada_kernel,
        grid=(n // D_MODEL,),
        in_specs=[pl.BlockSpec((rows, D_MODEL), lambda j: (0, 0)),
                  pl.BlockSpec((D_MODEL, D_MODEL), lambda j: (0, j)),
                  pl.BlockSpec((1, D_MODEL), lambda j: (0, j))],
        out_specs=pl.BlockSpec((rows, D_MODEL), lambda j: (0, j)),
        out_shape=jax.ShapeDtypeStruct((rows, n), F32),
        compiler_params=pltpu.CompilerParams(vmem_limit_bytes=VMEM_LIMIT),
        name="ada",
    )(c_all, w_ada, b_ada)


def _compress_rows(rows, pe, w_bd):
    t = rows.shape[0]
    pooled = jnp.sum(rows.reshape(t // L_CMP, L_CMP, KV_WIDTH) + pe[None], axis=1) * (1.0 / L_CMP)
    return _dot(pooled.astype(BF16), w_bd)


def _p_proj_kernel(x_ref, shift_ref, scale_ref, ng_ref, w_ref, cos_ref, sin_ref, qg_ref, kg_ref,
                   pek_ref, pev_ref, wck_ref, wcv_ref, vng_ref, vnb_ref, ws_ref, bst_ref, wbrb_ref,
                   qt_out, kc_t, ks_t, kw_t, vc_t, vs_t, vw_t, ks_rows, kw_rows, kcmp_out, vcmp_out,
                   gates_t, sza_out, sga_out, mb_out):
    tm = x_ref.shape[1]
    seg = _project(x_ref[0], shift_ref[0], scale_ref[0], ng_ref[...], w_ref)
    cos, sin = cos_ref[...], sin_ref[...]

    q = _norm_rope(seg(C_Q, C_K), qg_ref[...], cos, sin) * SM_SCALE
    for c in range(WIDTH_A // LANES):
        qt_out[0, c * LANES:(c + 1) * LANES, :] = q[:, c * LANES:(c + 1) * LANES].T.astype(BF16)
    k = _norm_rope(seg(C_K, C_V), kg_ref[...], cos, sin)
    v = seg(C_V, C_G)
    for br, (k_t, v_t) in enumerate(((kc_t, vc_t), (ks_t, vs_t), (kw_t, vw_t))):
        sl = slice(br * KV_WIDTH, (br + 1) * KV_WIDTH)
        k_t[0] = k[:, sl].T
        v_t[0] = v[:, sl].T
    ks_rows[0] = k[:, KV_WIDTH:2 * KV_WIDTH].astype(BF16)
    kw_rows[0] = k[:, 2 * KV_WIDTH:3 * KV_WIDTH].astype(BF16)
    kcmp_out[0] = _compress_rows(k[:, 0:KV_WIDTH], pek_ref[...], wck_ref[...])
    vcmp_out[0] = _compress_rows(v[:, 0:KV_WIDTH], pev_ref[...], wcv_ref[...])

    gates_t[0] = jax.nn.sigmoid(seg(C_G, C_ZA)).T
    sza_out[0] = _silu(seg(C_ZA, C_U))
    sga_out[0] = jax.nn.sigmoid(seg(C_GA, C_GB))

    vn = _layer_norm(seg(C_VB, C_ZB), vng_ref[...], vnb_ref[...]).astype(BF16)
    causal = _iota((CHUNK, CHUNK), 0) >= _iota((CHUNK, CHUNK), 1)
    chunks = []
    for c in range(tm // CHUNK):
        groups = []
        for g in range(N_GROUPS_B):
            wsg = jnp.where(causal, ws_ref[g], 0.0).astype(BF16)
            vg = vn[c * CHUNK:(c + 1) * CHUNK, g * GROUP_W_B:(g + 1) * GROUP_W_B]
            groups.append(_dot(wsg, vg) + bst_ref[:, g:g + 1])
        chunks.append(jnp.concatenate(groups, axis=1))
    s_b = jnp.concatenate(chunks, axis=0)
    t = seg(C_U, C_VB) * s_b * _silu(seg(C_ZB, C_GA))
    mb_out[0] = jax.nn.sigmoid(seg(C_GB, C_END)) * _dot(t.astype(BF16), wbrb_ref[...])


def _full(shape):
    nd = len(shape)
    return pl.BlockSpec(shape, lambda *_: (0,) * nd)


def _p_proj_call(x, shift, scale, norm_g, w_all, cos_t, sin_t, qg, kg, pek, pev, wck, wcv, vng, vnb,
                 w_s, bs_t, w_br_b):
    b, s, _ = x.shape
    tm = PROMPT_ROWS
    row = lambda w: pl.BlockSpec((1, tm, w), lambda bi, i: (bi, i, 0))
    tok_minor = pl.BlockSpec((1, KV_WIDTH, tm), lambda bi, i: (bi, 0, i))
    per_batch = pl.BlockSpec((1, 1, D_MODEL), lambda bi, i: (bi, 0, 0))
    cmp_spec = pl.BlockSpec((1, tm // L_CMP, KV_WIDTH), lambda bi, i: (bi, i, 0))
    tab = pl.BlockSpec((tm, LANES), lambda bi, i: (i, 0))
    in_specs = [row(D_MODEL), per_batch, per_batch, _full(norm_g.shape), _full(w_all.shape), tab, tab,
                _full(qg.shape), _full(kg.shape), _full(pek.shape), _full(pev.shape), _full(wck.shape),
                _full(wcv.shape), _full(vng.shape), _full(vnb.shape), _full(w_s.shape), _full(bs_t.shape),
                _full(w_br_b.shape)]
    qt_spec = pl.BlockSpec((1, WIDTH_A, tm), lambda bi, i: (bi, 0, i))
    out_specs = [qt_spec] + [tok_minor] * 6 + [row(KV_WIDTH), row(KV_WIDTH), cmp_spec, cmp_spec,
                                                tok_minor, row(WIDTH_A), row(D_MODEL), row(D_MODEL)]
    tm_shape = jax.ShapeDtypeStruct((b, KV_WIDTH, s), F32)
    rows_shape = jax.ShapeDtypeStruct((b, s, KV_WIDTH), BF16)
    cmp_shape = jax.ShapeDtypeStruct((b, s // L_CMP, KV_WIDTH), F32)
    out_shape = [jax.ShapeDtypeStruct((b, WIDTH_A, s), BF16)] + [tm_shape] * 6 + [
        rows_shape, rows_shape, cmp_shape, cmp_shape,
        jax.ShapeDtypeStruct((b, LANES, s), F32), jax.ShapeDtypeStruct((b, s, WIDTH_A), F32),
        jax.ShapeDtypeStruct((b, s, D_MODEL), F32), jax.ShapeDtypeStruct((b, s, D_MODEL), F32)]
    return pl.pallas_call(
        _p_proj_kernel,
        grid=(b, s // tm),
        in_specs=in_specs,
        out_specs=out_specs,
        out_shape=out_shape,
        compiler_params=pltpu.CompilerParams(dimension_semantics=("arbitrary", "arbitrary"),
                                             vmem_limit_bytes=VMEM_LIMIT),
        name="p_proj",
    )(x, shift, scale, norm_g, w_all, cos_t, sin_t, qg, kg, pek, pev, wck, wcv, vng, vnb, w_s, bs_t, w_br_b)


def _select_blocks(imp_c, qblk, n_blocks):
    ratio = L_SLC // L_CMP
    assert ratio == 2
    lane = _iota(imp_c.shape, 1)
    imp = imp_c + pltpu.roll(imp_c, LANES - 1, 1)
    blk = lane // ratio
    forced = jnp.where((blk == 0) | (blk == qblk), 1.0, 0.0)
    score = jnp.where(blk <= qblk, imp + FORCE_BONUS * forced, NEG)
    rank = jnp.zeros(imp_c.shape, F32)
    for j in range(n_blocks):
        vj = score[:, ratio * j:ratio * j + 1]
        earlier = jnp.where(lane > ratio * j, 1.0, 0.0)
        rank = rank + jnp.where(vj > score, 1.0, jnp.where(vj == score, earlier, 0.0))
    cand = (lane % ratio == 0) & (lane < ratio * n_blocks)
    return jnp.where(cand & (rank < float(min(N_SEL, n_blocks))), 1.0, 0.0)


def _softmax_rows(s, valid):
    sm = jnp.where(valid, s, NEG)
    e = jnp.exp(sm - jnp.max(sm, axis=-1, keepdims=True))
    return e, 1.0 / jnp.sum(e, axis=-1, keepdims=True)


def _merge_and_project(o_a, sza, sga, mb, x, gate, wbra_ref, wout_ref):
    a = _dot((o_a * sza).astype(BF16), wbra_ref[...])
    m = sga * a + mb
    return x + gate * _dot(m.astype(BF16), wout_ref[...])


def _select_blocks_t(imp, qblk):
    n_blocks = imp.shape[0]
    blk = _iota(imp.shape, 0)
    forced = jnp.where((blk == 0) | (blk == qblk), 1.0, 0.0)
    score = jnp.where(blk <= qblk, imp + FORCE_BONUS * forced, NEG)
    rank = jnp.zeros(imp.shape, F32)
    for j in range(n_blocks):
        vj = score[j:j + 1, :]
        earlier = jnp.where(blk > j, 1.0, 0.0)
        rank = rank + jnp.where(vj > score, 1.0, jnp.where(vj == score, earlier, 0.0))
    return jnp.where(rank < float(min(N_SEL, n_blocks)), 1.0, 0.0)


def _softmax_cols(s):
    e = jnp.exp(s - jnp.max(s, axis=0, keepdims=True))
    return e, 1.0 / jnp.sum(e, axis=0, keepdims=True)


def _p_attn_kernel(qt_ref, ks_ref, vst_ref, kw_ref, vwt_ref, kc_ref, vc_ref, gt_ref, sza_ref, sga_ref,
                   mb_ref, x_ref, gate_ref, wbra_ref, wout_ref, y_ref, qt_scr, sel_scr, m_scr, l_scr, acc_scr):
    i = pl.program_id(1)
    tq = Q_BLOCK
    n_cmp = kc_ref.shape[1]
    half = n_cmp // 2
    assert L_SLC == 2 * L_CMP and n_cmp <= LANES
    cols = GQA * tq
    qpos1 = i * tq + _iota((1, tq), 1)
    qpos = jnp.concatenate([qpos1] * GQA, axis=1)
    band = WINDOW + tq
    ws = pl.multiple_of(jnp.maximum(i * tq - WINDOW, 0), LANES)
    zeros_q = jnp.zeros((HEAD_DIM, cols), BF16)
    perm = lambda ref: jnp.concatenate([ref[0, pl.ds(0, half, stride=2), :], ref[0, pl.ds(1, half, stride=2), :],
                                        jnp.zeros((LANES - n_cmp, KV_WIDTH), F32)], axis=0)
    kc = perm(kc_ref).astype(BF16)
    vc_t = perm(vc_ref).T.astype(BF16)
    crow = _iota((LANES, cols), 0)
    cblk = 2 * (crow % half) + crow // half
    mc = ((cblk + 1) * L_CMP - 1 <= qpos) & (crow < n_cmp)
    gates_t = gt_ref[0]
    groups = [slice(g * HEAD_DIM, (g + 1) * HEAD_DIM) for g in range(N_KV)]

    o_c = []
    for g in range(N_KV):
        qt_g = jnp.concatenate([qt_ref[0, h * HEAD_DIM:(h + 1) * HEAD_DIM, :]
                                for h in range(g * GQA, (g + 1) * GQA)], axis=1)
        qt_scr[g] = jnp.concatenate([qt_g, zeros_q] if g == 0 else [zeros_q, qt_g], axis=0)
        s_c = jnp.where(mc, _dot(kc, qt_scr[g]), NEG)
        e_c, r_c = _softmax_cols(s_c)
        p_c = jnp.where(mc, e_c * r_c, 0.0)
        o_c.append(_dot(vc_t[groups[g]], p_c.astype(BF16)))
        imp = p_c[:, 0:tq]
        for r in range(1, GQA):
            imp = imp + p_c[:, r * tq:(r + 1) * tq]
        sel_scr[g] = _select_blocks_t(imp[0:half] + imp[half:2 * half], qpos1 // L_SLC)

    m_scr[...] = jnp.full(m_scr.shape, NEG, F32)
    l_scr[...] = jnp.zeros(l_scr.shape, F32)
    acc_scr[...] = jnp.zeros(acc_scr.shape, F32)
    per_chunk = SEL_CHUNK // L_SLC

    def body(kc_i, carry):
        off = pl.multiple_of(kc_i * SEL_CHUNK, SEL_CHUNK)
        causal = off + _iota((SEL_CHUNK, tq), 0) <= qpos1
        k_rows = ks_ref[0, pl.ds(off, SEL_CHUNK), :]
        for g in range(N_KV):
            picked = jnp.concatenate(
                [jnp.broadcast_to(sel_scr[g, pl.ds(kc_i * per_chunk + t, 1), :], (L_SLC, tq))
                 for t in range(per_chunk)], axis=0)
            bias = jnp.where((picked > 0.5) & causal, 0.0, NEG)
            s = _dot(k_rows, qt_scr[g]) + jnp.concatenate([bias] * GQA, axis=1)
            m_old = m_scr[g]
            m_new = jnp.maximum(m_old, jnp.max(s, axis=0, keepdims=True))
            alpha = jnp.exp(m_old - m_new)
            p = jnp.exp(s - m_new)
            m_scr[g] = m_new
            l_scr[g] = alpha * l_scr[g] + jnp.sum(p, axis=0, keepdims=True)
            acc_scr[g] = alpha * acc_scr[g] + _dot(vst_ref[0, groups[g], pl.ds(off, SEL_CHUNK)].astype(BF16),
                                                   p.astype(BF16))
        return carry

    lax.fori_loop(0, ((i + 1) * tq + SEL_CHUNK - 1) // SEL_CHUNK, body, 0)

    kwpos = ws + _iota((band, tq), 0)
    bias_w = jnp.where((kwpos <= qpos1) & (kwpos > qpos1 - WINDOW), 0.0, NEG)
    bias_w = jnp.concatenate([bias_w] * GQA, axis=1)
    pair = []
    for g in range(N_KV):
        s_w = _dot(kw_ref[0, pl.ds(ws, band), :], qt_scr[g]) + bias_w
        e_w, r_w = _softmax_cols(s_w)
        o_w = _dot(vwt_ref[0, groups[g], pl.ds(ws, band)].astype(BF16), e_w.astype(BF16)) * r_w
        o_s = acc_scr[g] * (1.0 / l_scr[g])
        for r in range(GQA):
            h = g * GQA + r
            cs = slice(r * tq, (r + 1) * tq)
            pair.append(gates_t[3 * h:3 * h + 1] * o_c[g][:, cs] + gates_t[3 * h + 1:3 * h + 2] * o_s[:, cs]
                        + gates_t[3 * h + 2:3 * h + 3] * o_w[:, cs])
    per_lane = LANES // HEAD_DIM
    o_a = jnp.concatenate([jnp.concatenate(pair[j:j + per_lane], axis=0).T
                           for j in range(0, N_HEADS, per_lane)], axis=1)
    y_ref[0] = _merge_and_project(o_a, sza_ref[0], sga_ref[0], mb_ref[0], x_ref[0], gate_ref[0],
                                  wbra_ref, wout_ref)


def _p_attn_call(q_t, ks_rows, vs_t, kw_rows, vw_t, kc, vc, gates_t, sza, sga, mb, x, gate, w_br_a, w_out):
    b, s, _ = x.shape
    tq = Q_BLOCK
    row = lambda w: pl.BlockSpec((1, tq, w), lambda bi, i: (bi, i, 0))
    col = lambda a: pl.BlockSpec((1, a.shape[1], tq), lambda bi, i: (bi, 0, i))
    seq = lambda a: pl.BlockSpec((1,) + a.shape[1:], lambda bi, i: (bi, 0, 0))
    in_specs = [col(q_t), seq(ks_rows), seq(vs_t), seq(kw_rows), seq(vw_t), seq(kc), seq(vc),
                col(gates_t), row(WIDTH_A), row(D_MODEL), row(D_MODEL), row(D_MODEL), seq(gate),
                _full(w_br_a.shape), _full(w_out.shape)]
    return pl.pallas_call(
        _p_attn_kernel,
        grid=(b, s // tq),
        in_specs=in_specs,
        out_specs=row(D_MODEL),
        out_shape=jax.ShapeDtypeStruct((b, s, D_MODEL), F32),
        scratch_shapes=[pltpu.VMEM((N_KV, KV_WIDTH, GQA * tq), BF16),
                        pltpu.VMEM((N_KV, s // L_SLC, tq), F32),
                        pltpu.VMEM((N_KV, 1, GQA * tq), F32), pltpu.VMEM((N_KV, 1, GQA * tq), F32),
                        pltpu.VMEM((N_KV, HEAD_DIM, GQA * tq), F32)],
        compiler_params=pltpu.CompilerParams(dimension_semantics=("arbitrary", "arbitrary"),
                                             vmem_limit_bytes=VMEM_LIMIT),
        name="p_attn",
    )(q_t, ks_rows, vs_t, kw_rows, vw_t, kc, vc, gates_t, sza, sga, mb, x, gate, w_br_a, w_out)


def _s_proj_kernel(x_ref, shift_ref, scale_ref, ng_ref, w_ref, cos_ref, sin_ref, qg_ref, kg_ref,
                   vng_ref, vnb_ref, ws0_ref, bs0_ref, wbrb_ref,
                   q_out, k_out, v_out, kt_out, vt_out, gates_out, sza_out, sga_out, mb_out, vn_out):
    seg = _project(x_ref[...], shift_ref[...], scale_ref[...], ng_ref[...], w_ref)
    cos, sin = cos_ref[...], sin_ref[...]
    q_out[...] = _norm_rope(seg(C_Q, C_K), qg_ref[...], cos, sin)
    k = _norm_rope(seg(C_K, C_V), kg_ref[...], cos, sin)
    v = seg(C_V, C_G)
    k_out[...] = k
    v_out[...] = v
    for br in range(3):
        sl = slice(br * KV_WIDTH, (br + 1) * KV_WIDTH)
        kt_out[br] = k[:, sl].T
        vt_out[br] = v[:, sl].T
    gates_out[...] = jax.nn.sigmoid(seg(C_G, C_ZA))
    sza_out[...] = _silu(seg(C_ZA, C_U))
    sga_out[...] = jax.nn.sigmoid(seg(C_GA, C_GB))
    vn = _layer_norm(seg(C_VB, C_ZB), vng_ref[...], vnb_ref[...])
    vn_out[...] = vn
    s_b = ws0_ref[...] * vn + bs0_ref[...]
    t = seg(C_U, C_VB) * s_b * _silu(seg(C_ZB, C_GA))
    mb_out[...] = jax.nn.sigmoid(seg(C_GB, C_END)) * _dot(t.astype(BF16), wbrb_ref[...])


def _s_proj_call(x, shift, scale, norm_g, w_all, cos1, sin1, qg, kg, vng, vnb, ws0, bs0, w_br_b):
    n = x.shape[0]
    args = (x, shift, scale, norm_g, w_all, cos1, sin1, qg, kg, vng, vnb, ws0, bs0, w_br_b)
    sds = lambda *shape: jax.ShapeDtypeStruct(shape, F32)
    out_shape = [sds(n, WIDTH_A), sds(n, 3 * KV_WIDTH), sds(n, 3 * KV_WIDTH), sds(3, KV_WIDTH, n),
                 sds(3, KV_WIDTH, n), sds(n, LANES), sds(n, WIDTH_A), sds(n, D_MODEL), sds(n, D_MODEL),
                 sds(n, WIDTH_B)]
    return pl.pallas_call(
        _s_proj_kernel,
        grid=(1,),
        in_specs=[_full(a.shape) for a in args],
        out_specs=[_full(o.shape) for o in out_shape],
        out_shape=out_shape,
        compiler_params=pltpu.CompilerParams(vmem_limit_bytes=VMEM_LIMIT),
        name="s_proj",
    )(*args)


def _s_attn_kernel(pt_ref, q_ref, gates_ref, knew_ref, vnew_ref, knewt_ref, vnewt_ref, kwin_ref, vwin_ref,
                   pek_ref, pev_ref, wck_ref, wcv_ref, pool_ref, expand_ref,
                   kc_hbm, vc_hbm, ks_hbm, vs_hbm,
                   oa_ref, okw_ref, ovw_ref, buf, sem):
    b = pl.program_id(0)
    nb = pl.num_programs(0)
    slot = lax.rem(b, 2)
    n_pages = pt_ref.shape[1]
    past = n_pages * PAGE_SIZE
    qpos = past
    t_pad = buf.shape[3]
    caches = (kc_hbm, vc_hbm, ks_hbm, vs_hbm)

    def page_copies(seq, slot_):
        return [pltpu.make_async_copy(hbm.at[pt_ref[seq, p]],
                                      buf.at[slot_, c, :, pl.ds(p * PAGE_SIZE, PAGE_SIZE)],
                                      sem.at[slot_, c])
                for c, hbm in enumerate(caches) for p in range(n_pages)]

    @pl.when(b == 0)
    def _():
        for cp in page_copies(0, 0):
            cp.start()

    @pl.when(b + 1 < nb)
    def _():
        for cp in page_copies(b + 1, 1 - slot):
            cp.start()

    lane = _iota((KV_WIDTH, knewt_ref.shape[2]), 1)
    col = lambda ref, br: jnp.sum(jnp.where(lane == b, ref[br], 0.0), axis=1, keepdims=True)
    knew = knew_ref[pl.ds(b, 1), :]
    vnew = vnew_ref[pl.ds(b, 1), :]

    hrow = _iota((N_HEADS, 1), 0)
    first_group = hrow < GQA
    by_group = lambda f: jnp.where(first_group, f(0), f(1))
    gl = lambda g: slice(g * HEAD_DIM, (g + 1) * HEAD_DIM)
    qb = q_ref[0].astype(BF16)

    for cp in page_copies(b, slot):
        cp.wait()

    tail_lane = _iota((KV_WIDTH, PAGE_SIZE), 1)
    buf[slot, 2, :, pl.ds(past, PAGE_SIZE)] = jnp.where(tail_lane == 0, col(knewt_ref, 1), 0.0)
    buf[slot, 3, :, pl.ds(past, PAGE_SIZE)] = jnp.where(tail_lane == 0, col(vnewt_ref, 1), 0.0)

    n_cmp = -(-(past + 1) // L_SLC) * L_SLC // L_CMP
    n_slc = n_cmp * L_CMP // L_SLC

    def summaries(c, new_row, pe_ref, w_ref):
        rows_t = buf[slot, c, :, pl.ds(0, past)]
        hi, lo = _split_bf16(rows_t)
        pe_sum = jnp.sum(pe_ref[...], axis=0, keepdims=True)
        pooled = (_dot_nt(pool_ref[...], hi) + _dot_nt(pool_ref[...], lo) + pe_sum) * (1.0 / L_CMP)
        r = _iota((SUBLANES, KV_WIDTH), 0)
        tail = jnp.where(r == 0, new_row + pe_sum, jnp.where(r == 1, pe_sum, 0.0)) * (1.0 / L_CMP)
        zeros = jnp.zeros((LANES - pooled.shape[0] - SUBLANES, KV_WIDTH), F32)
        return _dot(jnp.concatenate([pooled, tail, zeros], axis=0).astype(BF16), w_ref[...])

    kc = summaries(0, knew[:, 0:KV_WIDTH], pek_ref, wck_ref).astype(BF16)
    vc = summaries(1, vnew[:, 0:KV_WIDTH], pev_ref, wcv_ref).astype(BF16)
    s_c = by_group(lambda g: _dot_nt(qb, kc[:, gl(g)])) * SM_SCALE
    c = _iota(s_c.shape, 1)
    mc = ((c + 1) * L_CMP - 1 <= qpos) & (c < n_cmp)
    e_c, r_c = _softmax_rows(s_c, mc)
    p_c = jnp.where(mc, e_c * r_c, 0.0)
    o_c = by_group(lambda g: _dot(p_c.astype(BF16), vc[:, gl(g)]))
    imp = by_group(lambda g: jnp.sum(p_c[g * GQA:(g + 1) * GQA], axis=0, keepdims=True))
    imp = jnp.broadcast_to(imp, p_c.shape)
    sel = _select_blocks(imp, jnp.full((N_HEADS, 1), qpos // L_SLC, jnp.int32), n_slc)

    picked = _dot(sel.astype(BF16), expand_ref[...])
    s_s = by_group(lambda g: _dot(qb, buf[slot, 2, gl(g), :].astype(BF16))) * SM_SCALE
    kpos = _iota(s_s.shape, 1)
    e_s, r_s = _softmax_rows(s_s, (picked > 0.5) & (kpos <= qpos))
    o_s = by_group(lambda g: _dot_nt(e_s.astype(BF16), buf[slot, 3, gl(g), :].astype(BF16))) * r_s

    wb = kwin_ref.shape[2]
    wlane = _iota((KV_WIDTH, wb), 1)
    kw = jnp.where(wlane == wb - 1, col(knewt_ref, 2), pltpu.roll(kwin_ref[0], wb - 1, 1))
    vw = jnp.where(wlane == wb - 1, col(vnewt_ref, 2), pltpu.roll(vwin_ref[0], wb - 1, 1))
    okw_ref[0] = kw
    ovw_ref[0] = vw
    s_w = by_group(lambda g: _dot(qb, kw[gl(g)].astype(BF16))) * SM_SCALE
    kwpos = past - wb + 1 + _iota(s_w.shape, 1)
    e_w, r_w = _softmax_rows(s_w, (kwpos <= qpos) & (kwpos > qpos - WINDOW) & (kwpos >= 0))
    o_w = by_group(lambda g: _dot_nt(e_w.astype(BF16), vw[gl(g)].astype(BF16))) * r_w

    gates = gates_ref[0]
    oa_ref[0] = gates[:, 0:1] * o_c + gates[:, 1:2] * o_s + gates[:, 2:3] * o_w


def _s_attn_call(page_table, q3, gates3, knew, vnew, knew_t, vnew_t, kwin_t, vwin_t, pek, pev, wck, wcv,
                 kc_pool, vc_pool, ks_pool, vs_pool):
    n, n_pages = page_table.shape
    past = n_pages * PAGE_SIZE
    t_pad = past + PAGE_SIZE
    wb = kwin_t.shape[2]
    tok = np.arange(past)
    pool = jnp.asarray((tok[None, :] // L_CMP == np.arange(past // L_CMP)[:, None]), BF16)
    expand = jnp.asarray(np.arange(LANES)[:, None] == 2 * (np.arange(t_pad)[None, :] // L_SLC), BF16)
    per_seq = lambda a: pl.BlockSpec((1,) + a.shape[1:], lambda bi, pt: (bi,) + (0,) * (a.ndim - 1))
    full = lambda a: pl.BlockSpec(a.shape, lambda bi, pt: (0,) * a.ndim)
    hbm = pl.BlockSpec(memory_space=pl.ANY)
    resident = (knew, vnew, knew_t, vnew_t)
    consts = (pek, pev, wck, wcv, pool, expand)
    grid_spec = pltpu.PrefetchScalarGridSpec(
        num_scalar_prefetch=1,
        grid=(n,),
        in_specs=[per_seq(q3), per_seq(gates3)] + [full(a) for a in resident]
                 + [per_seq(kwin_t), per_seq(vwin_t)] + [full(a) for a in consts] + [hbm] * 4,
        out_specs=[per_seq(q3), per_seq(kwin_t), per_seq(vwin_t)],
        scratch_shapes=[pltpu.VMEM((2, 4, KV_WIDTH, t_pad), F32), pltpu.SemaphoreType.DMA((2, 4))],
    )
    return pl.pallas_call(
        _s_attn_kernel,
        grid_spec=grid_spec,
        out_shape=[jax.ShapeDtypeStruct(q3.shape, F32), jax.ShapeDtypeStruct(kwin_t.shape, F32),
                   jax.ShapeDtypeStruct(vwin_t.shape, F32)],
        compiler_params=pltpu.CompilerParams(dimension_semantics=("arbitrary",),
                                             vmem_limit_bytes=VMEM_LIMIT),
        name="s_attn",
    )(page_table, q3, gates3, knew, vnew, knew_t, vnew_t, kwin_t, vwin_t, pek, pev, wck, wcv, pool, expand,
      kc_pool, vc_pool, ks_pool, vs_pool)


def _s_out_kernel(oa_ref, sza_ref, sga_ref, mb_ref, x_ref, gate_ref, wbra_ref, wout_ref, y_ref):
    y_ref[...] = _merge_and_project(oa_ref[...], sza_ref[...], sga_ref[...], mb_ref[...], x_ref[...],
                                    gate_ref[...], wbra_ref, wout_ref)


def _s_out_call(o_a, sza, sga, mb, x, gate, w_br_a, w_out):
    args = (o_a, sza, sga, mb, x, gate, w_br_a, w_out)
    return pl.pallas_call(
        _s_out_kernel,
        grid=(1,),
        in_specs=[_full(a.shape) for a in args],
        out_specs=_full(x.shape),
        out_shape=jax.ShapeDtypeStruct(x.shape, F32),
        compiler_params=pltpu.CompilerParams(vmem_limit_bytes=VMEM_LIMIT),
        name="s_out",
    )(*args)


def _rope_tables(pos):
    half = HEAD_DIM // 2
    inv = ROPE_THETA ** (-jnp.arange(half, dtype=F32) * 2.0 / HEAD_DIM)
    ang = pos.astype(F32)[:, None] * inv[None, :]
    cos, sin = jnp.cos(ang), jnp.sin(ang)
    cos_t = jnp.concatenate([cos, cos] * (LANES // HEAD_DIM), axis=1)
    sin_t = jnp.concatenate([-sin, sin] * (LANES // HEAD_DIM), axis=1)
    return cos_t, sin_t


def _token_minor(a):
    b, t = a.shape[:2]
    return jnp.transpose(a, (0, 2, 3, 1)).reshape(b, KV_WIDTH, t)


def _token_major(a_t):
    b, _, t = a_t.shape
    return jnp.transpose(a_t.reshape(b, N_KV, HEAD_DIM, t), (0, 3, 1, 2))


def kernel(x_prompt, x_sample, cache_k_cmp, cache_v_cmp, cache_k_slc, cache_v_slc, cache_k_win, cache_v_win, page_table, c_prompt, c_sample, w_ada, b_ada, norm_g, w_in, q_norm_g, k_norm_g, cmp_pos_k, cmp_pos_v, w_cmp_k, w_cmp_v, vnorm_g, vnorm_b, w_s, b_s, w_br_a, w_br_b, w_out):
    assert w_ada.shape[0] == 1, "single layer"
    b, s, _ = x_prompt.shape
    n = x_sample.shape[0]
    assert x_sample.shape[1] == 1
    n_pages = page_table.shape[1]
    past = n_pages * PAGE_SIZE

    w = w_in[0]
    n_gate = 3 * N_HEADS
    w_all = jnp.concatenate([w[:, :C_G], w[:, C_G:C_G + n_gate], jnp.zeros((D_MODEL, LANES - n_gate), F32),
                             w[:, C_G + n_gate:]], axis=1).astype(BF16)
    assert w_all.shape[1] == C_END
    eye = jnp.eye(N_KV, dtype=F32)
    wck = jnp.kron(eye, w_cmp_k[0]).astype(BF16)
    wcv = jnp.kron(eye, w_cmp_v[0]).astype(BF16)
    pek = jnp.tile(cmp_pos_k[0], (1, N_KV))
    pev = jnp.tile(cmp_pos_v[0], (1, N_KV))
    qg = jnp.tile(q_norm_g, (1, LANES // HEAD_DIM))
    kg = jnp.tile(k_norm_g, (1, LANES // HEAD_DIM))
    w_br_a_b, w_br_b_b, w_out_b = w_br_a[0].astype(BF16), w_br_b[0].astype(BF16), w_out[0].astype(BF16)

    mod = _ada_call(jnp.concatenate([c_prompt, c_sample], axis=0), w_ada[0], b_ada)
    shift, scale, gate = mod[:, :D_MODEL], mod[:, D_MODEL:2 * D_MODEL], mod[:, 2 * D_MODEL:]

    cos_p, sin_p = _rope_tables(jnp.arange(s, dtype=jnp.int32))
    (q_t, kc_t, ks_t, kw_t, vc_t, vs_t, vw_t, ks_rows, kw_rows, kcmp, vcmp, gates_t, sza, sga, mb) = _p_proj_call(
        x_prompt, shift[:b, None], scale[:b, None], norm_g, w_all, cos_p, sin_p, qg, kg, pek, pev, wck, wcv,
        vnorm_g, vnorm_b, w_s[0], b_s[0].T, w_br_b_b)
    y_prompt = _p_attn_call(q_t, ks_rows, vs_t, kw_rows, vw_t, kcmp, vcmp, gates_t, sza, sga, mb, x_prompt,
                            gate[:b, None], w_br_a_b, w_out_b)
    wb_p = min(WINDOW, s)
    p_states = [_token_major(a)[None] for a in (kc_t, vc_t, ks_t, vs_t, kw_t[:, :, s - wb_p:], vw_t[:, :, s - wb_p:])]

    xs = x_sample.reshape(n, D_MODEL)
    cos_s, sin_s = _rope_tables(jnp.full((1,), past, jnp.int32))
    ws0 = jnp.repeat(w_s[0, :, 0, 0], GROUP_W_B)[None]
    bs0 = jnp.repeat(b_s[0, :, 0], GROUP_W_B)[None]
    (q_s, k_s, v_s, kt_s, vt_s, gates_s, sza_s, sga_s, mb_s, vn_s) = _s_proj_call(
        xs, shift[b:], scale[b:], norm_g, w_all, cos_s, sin_s, qg, kg, vnorm_g, vnorm_b, ws0, bs0, w_br_b_b)
    pools = [_token_minor(c[0]) for c in (cache_k_cmp, cache_v_cmp, cache_k_slc, cache_v_slc)]
    o_a, kwin_new, vwin_new = _s_attn_call(
        page_table, q_s.reshape(n, N_HEADS, HEAD_DIM), gates_s[:, :3 * N_HEADS].reshape(n, N_HEADS, 3),
        k_s, v_s, kt_s, vt_s, _token_minor(cache_k_win[0]), _token_minor(cache_v_win[0]),
        pek, pev, wck, wcv, *pools)
    y_sample = _s_out_call(o_a.reshape(n, WIDTH_A), sza_s, sga_s, mb_s, xs, gate[b:], w_br_a_b, w_out_b)

    new_rows = lambda t, br: jnp.transpose(t[br].reshape(N_KV, HEAD_DIM, n), (2, 0, 1))[None, :, None]
    s_states = [new_rows(kt_s, 0), new_rows(vt_s, 0), new_rows(kt_s, 1), new_rows(vt_s, 1),
                _token_major(kwin_new)[None], _token_major(vwin_new)[None], vn_s[None, :, None]]
    return (y_prompt, y_sample.reshape(n, 1, D_MODEL), *p_states, *s_states)
```

```python
import functools

import numpy as np
import jax
import jax.numpy as jnp
from jax import lax
from jax.experimental import pallas as pl
from jax.experimental.pallas import tpu as pltpu

F32 = jnp.float32
BF16 = jnp.bfloat16

D_MODEL = 1024
HEAD_DIM = 64
N_HEADS = 8
N_KV = 2
GQA = N_HEADS // N_KV
WIDTH_A = N_HEADS * HEAD_DIM
KV_WIDTH = N_KV * HEAD_DIM
L_CMP = 32
L_SLC = 64
N_SEL = 8
WINDOW = 512
Q_BLOCK = 256
FORCE_BONUS = 1.0e4
ROPE_THETA = 10000.0
CHUNK = 128
N_GROUPS_B = 4
WIDTH_B = 512
GROUP_W_B = WIDTH_B // N_GROUPS_B
PAGE_SIZE = 128
EPS = 1e-6
NEG = -1e30
SM_SCALE = HEAD_DIM ** -0.5

LANES = 128
SUBLANES = 8
VMEM_LIMIT = 56 * 1024 * 1024

C_Q = 0
C_K = C_Q + WIDTH_A
C_V = C_K + 3 * KV_WIDTH
C_G = C_V + 3 * KV_WIDTH
C_ZA = C_G + LANES
C_U = C_ZA + WIDTH_A
C_VB = C_U + WIDTH_B
C_ZB = C_VB + WIDTH_B
C_GA = C_ZB + WIDTH_B
C_GB = C_GA + D_MODEL
C_END = C_GB + D_MODEL

PROMPT_ROWS = 256
SEL_CHUNK = 256


def _dot(a, b):
    return jnp.dot(a, b, preferred_element_type=F32)


def _dot_nt(a, b):
    return lax.dot_general(a, b, (((1,), (1,)), ((), ())), preferred_element_type=F32)


def _iota(shape, dim):
    return lax.broadcasted_iota(jnp.int32, shape, dim)


def _split_bf16(x):
    hi = x.astype(BF16)
    lo = (x - hi.astype(F32)).astype(BF16)
    return hi, lo


def _head_mean_sq(x):
    w = x.shape[1]
    ones_bd = jnp.where(_iota((LANES, LANES), 0) // HEAD_DIM == _iota((LANES, LANES), 1) // HEAD_DIM,
                        1.0, 0.0).astype(BF16)
    hi, lo = _split_bf16(x * x)
    cols = []
    for c in range(w // LANES):
        sl = slice(c * LANES, (c + 1) * LANES)
        cols.append(_dot(hi[:, sl], ones_bd) + _dot(lo[:, sl], ones_bd))
    return jnp.concatenate(cols, axis=1) * (1.0 / HEAD_DIM)


def _tile_lanes(t, width):
    return jnp.concatenate([t] * (width // t.shape[1]), axis=1)


def _norm_rope(x, g, cos, sin):
    w = x.shape[1]
    y = x * lax.rsqrt(_head_mean_sq(x) + EPS) * _tile_lanes(g, w)
    first_half = (_iota(y.shape, 1) % HEAD_DIM) < (HEAD_DIM // 2)
    rot = jnp.where(first_half, pltpu.roll(y, w - HEAD_DIM // 2, 1), pltpu.roll(y, HEAD_DIM // 2, 1))
    return y * _tile_lanes(cos, w) + rot * _tile_lanes(sin, w)


def _silu(z):
    return z * jax.nn.sigmoid(z)


def _project(x, shift, scale, norm_g, w_ref):
    ms = jnp.mean(x * x, axis=-1, keepdims=True)
    h = (x * lax.rsqrt(ms + EPS) * norm_g) * (1.0 + scale) + shift
    hb = h.astype(BF16)
    return lambda lo, hi: _dot(hb, w_ref[:, lo:hi])


def _layer_norm(v, g, b):
    mu = jnp.mean(v, axis=-1, keepdims=True)
    d = v - mu
    var = jnp.mean(d * d, axis=-1, keepdims=True)
    return d * lax.rsqrt(var + EPS) * g + b


def _ada_kernel(c_ref, w_ref, b_ref, o_ref):
    o_ref[...] = _dot(c_ref[...].astype(BF16), w_ref[...].astype(BF16)) + b_ref[...]


def _ada_call(c_all, w_ada, b_ada):
    rows = c_all.shape[0]
    n = w_ada.shape[1]
    return pl.pallas_call(
        _ada_kernel,
        grid=(n // D_MODEL,),
        in_specs=[pl.BlockSpec((rows, D_MODEL), lambda j: (0, 0)),
                  pl.BlockSpec((D_MODEL, D_MODEL), lambda j: (0, j)),
                  pl.BlockSpec((1, D_MODEL), lambda j: (0, j))],
        out_specs=pl.BlockSpec((rows, D_MODEL), lambda j: (0, j)),
        out_shape=jax.ShapeDtypeStruct((rows, n), F32),
        compiler_params=pltpu.CompilerParams(vmem_limit_bytes=VMEM_LIMIT),
        name="ada",
    )(c_all, w_ada, b_ada)


def _compress_rows(rows, pe, w_bd):
    t = rows.shape[0]
    pooled = jnp.sum(rows.reshape(t // L_CMP, L_CMP, KV_WIDTH) + pe[None], axis=1) * (1.0 / L_CMP)
    return _dot(pooled.astype(BF16), w_bd)


def _p_proj_kernel(x_ref, shift_ref, scale_ref, ng_ref, w_ref, cos_ref, sin_ref, qg_ref, kg_ref,
                   pek_ref, pev_ref, wck_ref, wcv_ref, vng_ref, vnb_ref, ws_ref, bst_ref, wbrb_ref,
                   qt_out, kc_t, ks_t, kw_t, vc_t, vs_t, vw_t, ks_rows, kw_rows, kcmp_out, vcmp_out,
                   gates_t, sza_out, sga_out, mb_out):
    tm = x_ref.shape[1]
    seg = _project(x_ref[0], shift_ref[0], scale_ref[0], ng_ref[...], w_ref)
    cos, sin = cos_ref[...], sin_ref[...]

    q = _norm_rope(seg(C_Q, C_K), qg_ref[...], cos, sin) * SM_SCALE
    for c in range(WIDTH_A // LANES):
        qt_out[0, c * LANES:(c + 1) * LANES, :] = q[:, c * LANES:(c + 1) * LANES].T.astype(BF16)
    k = _norm_rope(seg(C_K, C_V), kg_ref[...], cos, sin)
    v = seg(C_V, C_G)
    for br, (k_t, v_t) in enumerate(((kc_t, vc_t), (ks_t, vs_t), (kw_t, vw_t))):
        sl = slice(br * KV_WIDTH, (br + 1) * KV_WIDTH)
        k_t[0] = k[:, sl].T
        v_t[0] = v[:, sl].T
    ks_rows[0] = k[:, KV_WIDTH:2 * KV_WIDTH].astype(BF16)
    kw_rows[0] = k[:, 2 * KV_WIDTH:3 * KV_WIDTH].astype(BF16)
    kcmp_out[0] = _compress_rows(k[:, 0:KV_WIDTH], pek_ref[...], wck_ref[...])
    vcmp_out[0] = _compress_rows(v[:, 0:KV_WIDTH], pev_ref[...], wcv_ref[...])

    gates_t[0] = jax.nn.sigmoid(seg(C_G, C_ZA)).T
    sza_out[0] = _silu(seg(C_ZA, C_U))
    sga_out[0] = jax.nn.sigmoid(seg(C_GA, C_GB))

    vn = _layer_norm(seg(C_VB, C_ZB), vng_ref[...], vnb_ref[...]).astype(BF16)
    causal = _iota((CHUNK, CHUNK), 0) >= _iota((CHUNK, CHUNK), 1)
    chunks = []
    for c in range(tm // CHUNK):
        groups = []
        for g in range(N_GROUPS_B):
            wsg = jnp.where(causal, ws_ref[g], 0.0).astype(BF16)
            vg = vn[c * CHUNK:(c + 1) * CHUNK, g * GROUP_W_B:(g + 1) * GROUP_W_B]
            groups.append(_dot(wsg, vg) + bst_ref[:, g:g + 1])
        chunks.append(jnp.concatenate(groups, axis=1))
    s_b = jnp.concatenate(chunks, axis=0)
    t = seg(C_U, C_VB) * s_b * _silu(seg(C_ZB, C_GA))
    mb_out[0] = jax.nn.sigmoid(seg(C_GB, C_END)) * _dot(t.astype(BF16), wbrb_ref[...])


def _full(shape):
    nd = len(shape)
    return pl.BlockSpec(shape, lambda *_: (0,) * nd)


def _p_proj_call(x, shift, scale, norm_g, w_all, cos_t, sin_t, qg, kg, pek, pev, wck, wcv, vng, vnb,
                 w_s, bs_t, w_br_b):
    b, s, _ = x.shape
    tm = PROMPT_ROWS
    row = lambda w: pl.BlockSpec((1, tm, w), lambda bi, i: (bi, i, 0))
    tok_minor = pl.BlockSpec((1, KV_WIDTH, tm), lambda bi, i: (bi, 0, i))
    per_batch = pl.BlockSpec((1, 1, D_MODEL), lambda bi, i: (bi, 0, 0))
    cmp_spec = pl.BlockSpec((1, tm // L_CMP, KV_WIDTH), lambda bi, i: (bi, i, 0))
    tab = pl.BlockSpec((tm, LANES), lambda bi, i: (i, 0))
    in_specs = [row(D_MODEL), per_batch, per_batch, _full(norm_g.shape), _full(w_all.shape), tab, tab,
                _full(qg.shape), _full(kg.shape), _full(pek.shape), _full(pev.shape), _full(wck.shape),
                _full(wcv.shape), _full(vng.shape), _full(vnb.shape), _full(w_s.shape), _full(bs_t.shape),
                _full(w_br_b.shape)]
    qt_spec = pl.BlockSpec((1, WIDTH_A, tm), lambda bi, i: (bi, 0, i))
    out_specs = [qt_spec] + [tok_minor] * 6 + [row(KV_WIDTH), row(KV_WIDTH), cmp_spec, cmp_spec,
                                                tok_minor, row(WIDTH_A), row(D_MODEL), row(D_MODEL)]
    tm_shape = jax.ShapeDtypeStruct((b, KV_WIDTH, s), F32)
    rows_shape = jax.ShapeDtypeStruct((b, s, KV_WIDTH), BF16)
    cmp_shape = jax.ShapeDtypeStruct((b, s // L_CMP, KV_WIDTH), F32)
    out_shape = [jax.ShapeDtypeStruct((b, WIDTH_A, s), BF16)] + [tm_shape] * 6 + [
        rows_shape, rows_shape, cmp_shape, cmp_shape,
        jax.ShapeDtypeStruct((b, LANES, s), F32), jax.ShapeDtypeStruct((b, s, WIDTH_A), F32),
        jax.ShapeDtypeStruct((b, s, D_MODEL), F32), jax.ShapeDtypeStruct((b, s, D_MODEL), F32)]
    return pl.pallas_call(
        _p_proj_kernel,
        grid=(b, s // tm),
        in_specs=in_specs,
        out_specs=out_specs,
        out_shape=out_shape,
        compiler_params=pltpu.CompilerParams(dimension_semantics=("arbitrary", "arbitrary"),
                                             vmem_limit_bytes=VMEM_LIMIT),
        name="p_proj",
    )(x, shift, scale, norm_g, w_all, cos_t, sin_t, qg, kg, pek, pev, wck, wcv, vng, vnb, w_s, bs_t, w_br_b)


def _select_blocks(imp_c, qblk, n_blocks):
    ratio = L_SLC // L_CMP
    assert ratio == 2
    lane = _iota(imp_c.shape, 1)
    imp = imp_c + pltpu.roll(imp_c, LANES - 1, 1)
    blk = lane // ratio
    forced = jnp.where((blk == 0) | (blk == qblk), 1.0, 0.0)
    score = jnp.where(blk <= qblk, imp + FORCE_BONUS * forced, NEG)
    rank = jnp.zeros(imp_c.shape, F32)
    for j in range(n_blocks):
        vj = score[:, ratio * j:ratio * j + 1]
        earlier = jnp.where(lane > ratio * j, 1.0, 0.0)
        rank = rank + jnp.where(vj > score, 1.0, jnp.where(vj == score, earlier, 0.0))
    cand = (lane % ratio == 0) & (lane < ratio * n_blocks)
    return jnp.where(cand & (rank < float(min(N_SEL, n_blocks))), 1.0, 0.0)


def _softmax_rows(s, valid):
    sm = jnp.where(valid, s, NEG)
    e = jnp.exp(sm - jnp.max(sm, axis=-1, keepdims=True))
    return e, 1.0 / jnp.sum(e, axis=-1, keepdims=True)


def _merge_and_project(o_a, sza, sga, mb, x, gate, wbra_ref, wout_ref):
    a = _dot((o_a * sza).astype(BF16), wbra_ref[...])
    m = sga * a + mb
    return x + gate * _dot(m.astype(BF16), wout_ref[...])


def _select_blocks_t(imp, qblk):
    n_blocks = imp.shape[0]
    blk = _iota(imp.shape, 0)
    forced = jnp.where((blk == 0) | (blk == qblk), 1.0, 0.0)
    score = jnp.where(blk <= qblk, imp + FORCE_BONUS * forced, NEG)
    rank = jnp.zeros(imp.shape, F32)
    for j in range(n_blocks):
        vj = score[j:j + 1, :]
        earlier = jnp.where(blk > j, 1.0, 0.0)
        rank = rank + jnp.where(vj > score, 1.0, jnp.where(vj == score, earlier, 0.0))
    return jnp.where(rank < float(min(N_SEL, n_blocks)), 1.0, 0.0)


def _softmax_cols(s):
    e = jnp.exp(s - jnp.max(s, axis=0, keepdims=True))
    return e, 1.0 / jnp.sum(e, axis=0, keepdims=True)


def _p_attn_kernel(qt_ref, ks_ref, vst_ref, kw_ref, vwt_ref, kc_ref, vc_ref, gt_ref, sza_ref, sga_ref,
                   mb_ref, x_ref, gate_ref, wbra_ref, wout_ref, y_ref, qt_scr, sel_scr, m_scr, l_scr, acc_scr):
    i = pl.program_id(1)
    tq = Q_BLOCK
    n_cmp = kc_ref.shape[1]
    half = n_cmp // 2
    assert L_SLC == 2 * L_CMP and n_cmp <= LANES
    cols = GQA * tq
    qpos1 = i * tq + _iota((1, tq), 1)
    qpos = jnp.concatenate([qpos1] * GQA, axis=1)
    band = WINDOW + tq
    ws = pl.multiple_of(jnp.maximum(i * tq - WINDOW, 0), LANES)
    zeros_q = jnp.zeros((HEAD_DIM, cols), BF16)
    perm = lambda ref: jnp.concatenate([ref[0, pl.ds(0, half, stride=2), :], ref[0, pl.ds(1, half, stride=2), :],
                                        jnp.zeros((LANES - n_cmp, KV_WIDTH), F32)], axis=0)
    kc = perm(kc_ref).astype(BF16)
    vc_t = perm(vc_ref).T.astype(BF16)
    crow = _iota((LANES, cols), 0)
    cblk = 2 * (crow % half) + crow // half
    mc = ((cblk + 1) * L_CMP - 1 <= qpos) & (crow < n_cmp)
    gates_t = gt_ref[0]
    groups = [slice(g * HEAD_DIM, (g + 1) * HEAD_DIM) for g in range(N_KV)]

    o_c = []
    for g in range(N_KV):
        qt_g = jnp.concatenate([qt_ref[0, h * HEAD_DIM:(h + 1) * HEAD_DIM, :]
                                for h in range(g * GQA, (g + 1) * GQA)], axis=1)
        qt_scr[g] = jnp.concatenate([qt_g, zeros_q] if g == 0 else [zeros_q, qt_g], axis=0)
        s_c = jnp.where(mc, _dot(kc, qt_scr[g]), NEG)
        e_c, r_c = _softmax_cols(s_c)
        p_c = jnp.where(mc, e_c * r_c, 0.0)
        o_c.append(_dot(vc_t[groups[g]], p_c.astype(BF16)))
        imp = p_c[:, 0:tq]
        for r in range(1, GQA):
            imp = imp + p_c[:, r * tq:(r + 1) * tq]
        sel_scr[g] = _select_blocks_t(imp[0:half] + imp[half:2 * half], qpos1 // L_SLC)

    m_scr[...] = jnp.full(m_scr.shape, NEG, F32)
    l_scr[...] = jnp.zeros(l_scr.shape, F32)
    acc_scr[...] = jnp.zeros(acc_scr.shape, F32)
    per_chunk = SEL_CHUNK // L_SLC

    def body(kc_i, carry):
        off = pl.multiple_of(kc_i * SEL_CHUNK, SEL_CHUNK)
        causal = off + _iota((SEL_CHUNK, tq), 0) <= qpos1
        k_rows = ks_ref[0, pl.ds(off, SEL_CHUNK), :]
        for g in range(N_KV):
            picked = jnp.concatenate(
                [jnp.broadcast_to(sel_scr[g, pl.ds(kc_i * per_chunk + t, 1), :], (L_SLC, tq))
                 for t in range(per_chunk)], axis=0)
            bias = jnp.where((picked > 0.5) & causal, 0.0, NEG)
            s = _dot(k_rows, qt_scr[g]) + jnp.concatenate([bias] * GQA, axis=1)
            m_old = m_scr[g]
            m_new = jnp.maximum(m_old, jnp.max(s, axis=0, keepdims=True))
            alpha = jnp.exp(m_old - m_new)
            p = jnp.exp(s - m_new)
            m_scr[g] = m_new
            l_scr[g] = alpha * l_scr[g] + jnp.sum(p, axis=0, keepdims=True)
            acc_scr[g] = alpha * acc_scr[g] + _dot(vst_ref[0, groups[g], pl.ds(off, SEL_CHUNK)].astype(BF16),
                                                   p.astype(BF16))
        return carry

    lax.fori_loop(0, ((i + 1) * tq + SEL_CHUNK - 1) // SEL_CHUNK, body, 0)

    kwpos = ws + _iota((band, tq), 0)
    bias_w = jnp.where((kwpos <= qpos1) & (kwpos > qpos1 - WINDOW), 0.0, NEG)
    bias_w = jnp.concatenate([bias_w] * GQA, axis=1)
    pair = []
    for g in range(N_KV):
        s_w = _dot(kw_ref[0, pl.ds(ws, band), :], qt_scr[g]) + bias_w
        e_w, r_w = _softmax_cols(s_w)
        o_w = _dot(vwt_ref[0, groups[g], pl.ds(ws, band)].astype(BF16), e_w.astype(BF16)) * r_w
        o_s = acc_scr[g] * (1.0 / l_scr[g])
        for r in range(GQA):
            h = g * GQA + r
            cs = slice(r * tq, (r + 1) * tq)
            pair.append(gates_t[3 * h:3 * h + 1] * o_c[g][:, cs] + gates_t[3 * h + 1:3 * h + 2] * o_s[:, cs]
                        + gates_t[3 * h + 2:3 * h + 3] * o_w[:, cs])
    per_lane = LANES // HEAD_DIM
    o_a = jnp.concatenate([jnp.concatenate(pair[j:j + per_lane], axis=0).T
                           for j in range(0, N_HEADS, per_lane)], axis=1)
    y_ref[0] = _merge_and_project(o_a, sza_ref[0], sga_ref[0], mb_ref[0], x_ref[0], gate_ref[0],
                                  wbra_ref, wout_ref)


def _p_attn_call(q_t, ks_rows, vs_t, kw_rows, vw_t, kc, vc, gates_t, sza, sga, mb, x, gate, w_br_a, w_out):
    b, s, _ = x.shape
    tq = Q_BLOCK
    row = lambda w: pl.BlockSpec((1, tq, w), lambda bi, i: (bi, i, 0))
    col = lambda a: pl.BlockSpec((1, a.shape[1], tq), lambda bi, i: (bi, 0, i))
    seq = lambda a: pl.BlockSpec((1,) + a.shape[1:], lambda bi, i: (bi, 0, 0))
    in_specs = [col(q_t), seq(ks_rows), seq(vs_t), seq(kw_rows), seq(vw_t), seq(kc), seq(vc),
                col(gates_t), row(WIDTH_A), row(D_MODEL), row(D_MODEL), row(D_MODEL), seq(gate),
                _full(w_br_a.shape), _full(w_out.shape)]
    return pl.pallas_call(
        _p_attn_kernel,
        grid=(b, s // tq),
        in_specs=in_specs,
        out_specs=row(D_MODEL),
        out_shape=jax.ShapeDtypeStruct((b, s, D_MODEL), F32),
        scratch_shapes=[pltpu.VMEM((N_KV, KV_WIDTH, GQA * tq), BF16),
                        pltpu.VMEM((N_KV, s // L_SLC, tq), F32),
                        pltpu.VMEM((N_KV, 1, GQA * tq), F32), pltpu.VMEM((N_KV, 1, GQA * tq), F32),
                        pltpu.VMEM((N_KV, HEAD_DIM, GQA * tq), F32)],
        compiler_params=pltpu.CompilerParams(dimension_semantics=("arbitrary", "arbitrary"),
                                             vmem_limit_bytes=VMEM_LIMIT),
        name="p_attn",
    )(q_t, ks_rows, vs_t, kw_rows, vw_t, kc, vc, gates_t, sza, sga, mb, x, gate, w_br_a, w_out)


def _s_proj_kernel(x_ref, shift_ref, scale_ref, ng_ref, w_ref, cos_ref, sin_ref, qg_ref, kg_ref,
                   vng_ref, vnb_ref, ws0_ref, bs0_ref, wbrb_ref,
                   q_out, k_out, v_out, kt_out, vt_out, gates_out, sza_out, sga_out, mb_out, vn_out):
    seg = _project(x_ref[...], shift_ref[...], scale_ref[...], ng_ref[...], w_ref)
    cos, sin = cos_ref[...], sin_ref[...]
    q_out[...] = _norm_rope(seg(C_Q, C_K), qg_ref[...], cos, sin)
    k = _norm_rope(seg(C_K, C_V), kg_ref[...], cos, sin)
    v = seg(C_V, C_G)
    k_out[...] = k
    v_out[...] = v
    for br in range(3):
        sl = slice(br * KV_WIDTH, (br + 1) * KV_WIDTH)
        kt_out[br] = k[:, sl].T
        vt_out[br] = v[:, sl].T
    gates_out[...] = jax.nn.sigmoid(seg(C_G, C_ZA))
    sza_out[...] = _silu(seg(C_ZA, C_U))
    sga_out[...] = jax.nn.sigmoid(seg(C_GA, C_GB))
    vn = _layer_norm(seg(C_VB, C_ZB), vng_ref[...], vnb_ref[...])
    vn_out[...] = vn
    s_b = ws0_ref[...] * vn + bs0_ref[...]
    t = seg(C_U, C_VB) * s_b * _silu(seg(C_ZB, C_GA))
    mb_out[...] = jax.nn.sigmoid(seg(C_GB, C_END)) * _dot(t.astype(BF16), wbrb_ref[...])


def _s_proj_call(x, shift, scale, norm_g, w_all, cos1, sin1, qg, kg, vng, vnb, ws0, bs0, w_br_b):
    n = x.shape[0]
    args = (x, shift, scale, norm_g, w_all, cos1, sin1, qg, kg, vng, vnb, ws0, bs0, w_br_b)
    sds = lambda *shape: jax.ShapeDtypeStruct(shape, F32)
    out_shape = [sds(n, WIDTH_A), sds(n, 3 * KV_WIDTH), sds(n, 3 * KV_WIDTH), sds(3, KV_WIDTH, n),
                 sds(3, KV_WIDTH, n), sds(n, LANES), sds(n, WIDTH_A), sds(n, D_MODEL), sds(n, D_MODEL),
                 sds(n, WIDTH_B)]
    return pl.pallas_call(
        _s_proj_kernel,
        grid=(1,),
        in_specs=[_full(a.shape) for a in args],
        out_specs=[_full(o.shape) for o in out_shape],
        out_shape=out_shape,
        compiler_params=pltpu.CompilerParams(vmem_limit_bytes=VMEM_LIMIT),
        name="s_proj",
    )(*args)


def _s_attn_kernel(pt_ref, q_ref, gates_ref, knew_ref, vnew_ref, knewt_ref, vnewt_ref, kwin_ref, vwin_ref,
                   pek_ref, pev_ref, wck_ref, wcv_ref, pool_ref, expand_ref,
                   kc_hbm, vc_hbm, ks_hbm, vs_hbm,
                   oa_ref, okw_ref, ovw_ref, buf, sem):
    b = pl.program_id(0)
    nb = pl.num_programs(0)
    slot = lax.rem(b, 2)
    n_pages = pt_ref.shape[1]
    past = n_pages * PAGE_SIZE
    qpos = past
    t_pad = buf.shape[3]
    caches = (kc_hbm, vc_hbm, ks_hbm, vs_hbm)

    def page_copies(seq, slot_):
        return [pltpu.make_async_copy(hbm.at[pt_ref[seq, p]],
                                      buf.at[slot_, c, :, pl.ds(p * PAGE_SIZE, PAGE_SIZE)],
                                      sem.at[slot_, c])
                for c, hbm in enumerate(caches) for p in range(n_pages)]

    @pl.when(b == 0)
    def _():
        for cp in page_copies(0, 0):
            cp.start()

    @pl.when(b + 1 < nb)
    def _():
        for cp in page_copies(b + 1, 1 - slot):
            cp.start()

    lane = _iota((KV_WIDTH, knewt_ref.shape[2]), 1)
    col = lambda ref, br: jnp.sum(jnp.where(lane == b, ref[br], 0.0), axis=1, keepdims=True)
    knew = knew_ref[pl.ds(b, 1), :]
    vnew = vnew_ref[pl.ds(b, 1), :]

    hrow = _iota((N_HEADS, 1), 0)
    first_group = hrow < GQA
    by_group = lambda f: jnp.where(first_group, f(0), f(1))
    gl = lambda g: slice(g * HEAD_DIM, (g + 1) * HEAD_DIM)
    qb = q_ref[0].astype(BF16)

    for cp in page_copies(b, slot):
        cp.wait()

    tail_lane = _iota((KV_WIDTH, PAGE_SIZE), 1)
    buf[slot, 2, :, pl.ds(past, PAGE_SIZE)] = jnp.where(tail_lane == 0, col(knewt_ref, 1), 0.0)
    buf[slot, 3, :, pl.ds(past, PAGE_SIZE)] = jnp.where(tail_lane == 0, col(vnewt_ref, 1), 0.0)

    n_cmp = -(-(past + 1) // L_SLC) * L_SLC // L_CMP
    n_slc = n_cmp * L_CMP // L_SLC

    def summaries(c, new_row, pe_ref, w_ref):
        rows_t = buf[slot, c, :, pl.ds(0, past)]
        hi, lo = _split_bf16(rows_t)
        pe_sum = jnp.sum(pe_ref[...], axis=0, keepdims=True)
        pooled = (_dot_nt(pool_ref[...], hi) + _dot_nt(pool_ref[...], lo) + pe_sum) * (1.0 / L_CMP)
        r = _iota((SUBLANES, KV_WIDTH), 0)
        tail = jnp.where(r == 0, new_row + pe_sum, jnp.where(r == 1, pe_sum, 0.0)) * (1.0 / L_CMP)
        zeros = jnp.zeros((LANES - pooled.shape[0] - SUBLANES, KV_WIDTH), F32)
        return _dot(jnp.concatenate([pooled, tail, zeros], axis=0).astype(BF16), w_ref[...])

    kc = summaries(0, knew[:, 0:KV_WIDTH], pek_ref, wck_ref).astype(BF16)
    vc = summaries(1, vnew[:, 0:KV_WIDTH], pev_ref, wcv_ref).astype(BF16)
    s_c = by_group(lambda g: _dot_nt(qb, kc[:, gl(g)])) * SM_SCALE
    c = _iota(s_c.shape, 1)
    mc = ((c + 1) * L_CMP - 1 <= qpos) & (c < n_cmp)
    e_c, r_c = _softmax_rows(s_c, mc)
    p_c = jnp.where(mc, e_c * r_c, 0.0)
    o_c = by_group(lambda g: _dot(p_c.astype(BF16), vc[:, gl(g)]))
    imp = by_group(lambda g: jnp.sum(p_c[g * GQA:(g + 1) * GQA], axis=0, keepdims=True))
    imp = jnp.broadcast_to(imp, p_c.shape)
    sel = _select_blocks(imp, jnp.full((N_HEADS, 1), qpos // L_SLC, jnp.int32), n_slc)

    picked = _dot(sel.astype(BF16), expand_ref[...])
    s_s = by_group(lambda g: _dot(qb, buf[slot, 2, gl(g), :].astype(BF16))) * SM_SCALE
    kpos = _iota(s_s.shape, 1)
    e_s, r_s = _softmax_rows(s_s, (picked > 0.5) & (kpos <= qpos))
    o_s = by_group(lambda g: _dot_nt(e_s.astype(BF16), buf[slot, 3, gl(g), :].astype(BF16))) * r_s

    wb = kwin_ref.shape[2]
    wlane = _iota((KV_WIDTH, wb), 1)
    kw = jnp.where(wlane == wb - 1, col(knewt_ref, 2), pltpu.roll(kwin_ref[0], wb - 1, 1))
    vw = jnp.where(wlane == wb - 1, col(vnewt_ref, 2), pltpu.roll(vwin_ref[0], wb - 1, 1))
    okw_ref[0] = kw
    ovw_ref[0] = vw
    s_w = by_group(lambda g: _dot(qb, kw[gl(g)].astype(BF16))) * SM_SCALE
    kwpos = past - wb + 1 + _iota(s_w.shape, 1)
    e_w, r_w = _softmax_rows(s_w, (kwpos <= qpos) & (kwpos > qpos - WINDOW) & (kwpos >= 0))
    o_w = by_group(lambda g: _dot_nt(e_w.astype(BF16), vw[gl(g)].astype(BF16))) * r_w

    gates = gates_ref[0]
    oa_ref[0] = gates[:, 0:1] * o_c + gates[:, 1:2] * o_s + gates[:, 2:3] * o_w


def _s_attn_call(page_table, q3, gates3, knew, vnew, knew_t, vnew_t, kwin_t, vwin_t, pek, pev, wck, wcv,
                 kc_pool, vc_pool, ks_pool, vs_pool):
    n, n_pages = page_table.shape
    past = n_pages * PAGE_SIZE
    t_pad = past + PAGE_SIZE
    wb = kwin_t.shape[2]
    tok = np.arange(past)
    pool = jnp.asarray((tok[None, :] // L_CMP == np.arange(past // L_CMP)[:, None]), BF16)
    expand = jnp.asarray(np.arange(LANES)[:, None] == 2 * (np.arange(t_pad)[None, :] // L_SLC), BF16)
    per_seq = lambda a: pl.BlockSpec((1,) + a.shape[1:], lambda bi, pt: (bi,) + (0,) * (a.ndim - 1))
    full = lambda a: pl.BlockSpec(a.shape, lambda bi, pt: (0,) * a.ndim)
    hbm = pl.BlockSpec(memory_space=pl.ANY)
    resident = (knew, vnew, knew_t, vnew_t)
    consts = (pek, pev, wck, wcv, pool, expand)
    grid_spec = pltpu.PrefetchScalarGridSpec(
        num_scalar_prefetch=1,
        grid=(n,),
        in_specs=[per_seq(q3), per_seq(gates3)] + [full(a) for a in resident]
                 + [per_seq(kwin_t), per_seq(vwin_t)] + [full(a) for a in consts] + [hbm] * 4,
        out_specs=[per_seq(q3), per_seq(kwin_t), per_seq(vwin_t)],
        scratch_shapes=[pltpu.VMEM((2, 4, KV_WIDTH, t_pad), F32), pltpu.SemaphoreType.DMA((2, 4))],
    )
    return pl.pallas_call(
        _s_attn_kernel,
        grid_spec=grid_spec,
        out_shape=[jax.ShapeDtypeStruct(q3.shape, F32), jax.ShapeDtypeStruct(kwin_t.shape, F32),
                   jax.ShapeDtypeStruct(vwin_t.shape, F32)],
        compiler_params=pltpu.CompilerParams(dimension_semantics=("arbitrary",),
                                             vmem_limit_bytes=VMEM_LIMIT),
        name="s_attn",
    )(page_table, q3, gates3, knew, vnew, knew_t, vnew_t, kwin_t, vwin_t, pek, pev, wck, wcv, pool, expand,
      kc_pool, vc_pool, ks_pool, vs_pool)


def _s_out_kernel(oa_ref, sza_ref, sga_ref, mb_ref, x_ref, gate_ref, wbra_ref, wout_ref, y_ref):
    y_ref[...] = _merge_and_project(oa_ref[...], sza_ref[...], sga_ref[...], mb_ref[...], x_ref[...],
                                    gate_ref[...], wbra_ref, wout_ref)


def _s_out_call(o_a, sza, sga, mb, x, gate, w_br_a, w_out):
    args = (o_a, sza, sga, mb, x, gate, w_br_a, w_out)
    return pl.pallas_call(
        _s_out_kernel,
        grid=(1,),
        in_specs=[_full(a.shape) for a in args],
        out_specs=_full(x.shape),
        out_shape=jax.ShapeDtypeStruct(x.shape, F32),
        compiler_params=pltpu.CompilerParams(vmem_limit_bytes=VMEM_LIMIT),
        name="s_out",
    )(*args)


def _rope_tables(pos):
    half = HEAD_DIM // 2
    inv = ROPE_THETA ** (-jnp.arange(half, dtype=F32) * 2.0 / HEAD_DIM)
    ang = pos.astype(F32)[:, None] * inv[None, :]
    cos, sin = jnp.cos(ang), jnp.sin(ang)
    cos_t = jnp.concatenate([cos, cos] * (LANES // HEAD_DIM), axis=1)
    sin_t = jnp.concatenate([-sin, sin] * (LANES // HEAD_DIM), axis=1)
    return cos_t, sin_t


def _token_minor(a):
    b, t = a.shape[:2]
    return jnp.transpose(a, (0, 2, 3, 1)).reshape(b, KV_WIDTH, t)


def _token_major(a_t):
    b, _, t = a_t.shape
    return jnp.transpose(a_t.reshape(b, N_KV, HEAD_DIM, t), (0, 3, 1, 2))


def kernel(x_prompt, x_sample, cache_k_cmp, cache_v_cmp, cache_k_slc, cache_v_slc, cache_k_win, cache_v_win, page_table, c_prompt, c_sample, w_ada, b_ada, norm_g, w_in, q_norm_g, k_norm_g, cmp_pos_k, cmp_pos_v, w_cmp_k, w_cmp_v, vnorm_g, vnorm_b, w_s, b_s, w_br_a, w_br_b, w_out):
    assert w_ada.shape[0] == 1, "single layer"
    b, s, _ = x_prompt.shape
    n = x_sample.shape[0]
    assert x_sample.shape[1] == 1
    n_pages = page_table.shape[1]
    past = n_pages * PAGE_SIZE

    w = w_in[0]
    n_gate = 3 * N_HEADS
    w_all = jnp.concatenate([w[:, :C_G], w[:, C_G:C_G + n_gate], jnp.zeros((D_MODEL, LANES - n_gate), F32),
                             w[:, C_G + n_gate:]], axis=1).astype(BF16)
    assert w_all.shape[1] == C_END
    eye = jnp.eye(N_KV, dtype=F32)
    wck = jnp.kron(eye, w_cmp_k[0]).astype(BF16)
    wcv = jnp.kron(eye, w_cmp_v[0]).astype(BF16)
    pek = jnp.tile(cmp_pos_k[0], (1, N_KV))
    pev = jnp.tile(cmp_pos_v[0], (1, N_KV))
    qg = jnp.tile(q_norm_g, (1, LANES // HEAD_DIM))
    kg = jnp.tile(k_norm_g, (1, LANES // HEAD_DIM))
    w_br_a_b, w_br_b_b, w_out_b = w_br_a[0].astype(BF16), w_br_b[0].astype(BF16), w_out[0].astype(BF16)

    mod = _ada_call(jnp.concatenate([c_prompt, c_sample], axis=0), w_ada[0], b_ada)
    shift, scale, gate = mod[:, :D_MODEL], mod[:, D_MODEL:2 * D_MODEL], mod[:, 2 * D_MODEL:]

    cos_p, sin_p = _rope_tables(jnp.arange(s, dtype=jnp.int32))
    (q_t, kc_t, ks_t, kw_t, vc_t, vs_t, vw_t, ks_rows, kw_rows, kcmp, vcmp, gates_t, sza, sga, mb) = _p_proj_call(
        x_prompt, shift[:b, None], scale[:b, None], norm_g, w_all, cos_p, sin_p, qg, kg, pek, pev, wck, wcv,
        vnorm_g, vnorm_b, w_s[0], b_s[0].T, w_br_b_b)
    y_prompt = _p_attn_call(q_t, ks_rows, vs_t, kw_rows, vw_t, kcmp, vcmp, gates_t, sza, sga, mb, x_prompt,
                            gate[:b, None], w_br_a_b, w_out_b)
    wb_p = min(WINDOW, s)
    p_states = [_token_major(a)[None] for a in (kc_t, vc_t, ks_t, vs_t, kw_t[:, :, s - wb_p:], vw_t[:, :, s - wb_p:])]

    xs = x_sample.reshape(n, D_MODEL)
    cos_s, sin_s = _rope_tables(jnp.full((1,), past, jnp.int32))
    ws0 = jnp.repeat(w_s[0, :, 0, 0], GROUP_W_B)[None]
    bs0 = jnp.repeat(b_s[0, :, 0], GROUP_W_B)[None]
    (q_s, k_s, v_s, kt_s, vt_s, gates_s, sza_s, sga_s, mb_s, vn_s) = _s_proj_call(
        xs, shift[b:], scale[b:], norm_g, w_all, cos_s, sin_s, qg, kg, vnorm_g, vnorm_b, ws0, bs0, w_br_b_b)
    pools = [_token_minor(c[0]) for c in (cache_k_cmp, cache_v_cmp, cache_k_slc, cache_v_slc)]
    o_a, kwin_new, vwin_new = _s_attn_call(
        page_table, q_s.reshape(n, N_HEADS, HEAD_DIM), gates_s[:, :3 * N_HEADS].reshape(n, N_HEADS, 3),
        k_s, v_s, kt_s, vt_s, _token_minor(cache_k_win[0]), _token_minor(cache_v_win[0]),
        pek, pev, wck, wcv, *pools)
    y_sample = _s_out_call(o_a.reshape(n, WIDTH_A), sza_s, sga_s, mb_s, xs, gate[b:], w_br_a_b, w_out_b)

    new_rows = lambda t, br: jnp.transpose(t[br].reshape(N_KV, HEAD_DIM, n), (2, 0, 1))[None, :, None]
    s_states = [new_rows(kt_s, 0), new_rows(vt_s, 0), new_rows(kt_s, 1), new_rows(vt_s, 1),
                _token_major(kwin_new)[None], _token_major(vwin_new)[None], vn_s[None, :, None]]
    return (y_prompt, y_sample.reshape(n, 1, D_MODEL), *p_states, *s_states)
```

```python
import itertools

import numpy as np
import jax
import jax.numpy as jnp
from jax import lax
from jax.experimental import pallas as pl
from jax.experimental.pallas import tpu as pltpu

F32 = jnp.float32
BF16 = jnp.bfloat16

D_MODEL = 1024
HEAD_DIM = 64
N_HEADS = 8
N_KV = 2
GQA = N_HEADS // N_KV
WIDTH_A = N_HEADS * HEAD_DIM
KV_WIDTH = N_KV * HEAD_DIM
L_CMP = 32
L_SLC = 64
N_SEL = 8
WINDOW = 512
Q_BLOCK = 256
FORCE_BONUS = 1.0e4
ROPE_THETA = 10000.0
CHUNK = 128
N_GROUPS_B = 4
WIDTH_B = 512
GROUP_W_B = WIDTH_B // N_GROUPS_B
PAGE_SIZE = 128
EPS = 1e-6
NEG = -1e30
SM_SCALE = HEAD_DIM ** -0.5

LANES = 128
SUBLANES = 8
VMEM_LIMIT = 56 * 1024 * 1024

C_Q = 0
C_K = C_Q + WIDTH_A
C_V = C_K + 3 * KV_WIDTH
C_G = C_V + 3 * KV_WIDTH
C_ZA = C_G + LANES
C_U = C_ZA + WIDTH_A
C_VB = C_U + WIDTH_B
C_ZB = C_VB + WIDTH_B
C_GA = C_ZB + WIDTH_B
C_GB = C_GA + D_MODEL
C_END = C_GB + D_MODEL

PROMPT_ROWS = 256
SEL_CHUNK = 256
ONES_ROWS = 16
SAMPLE_SEQS_PER_STEP = 2


def _dot(a, b):
    return jnp.dot(a, b, preferred_element_type=F32)


def _dot_nt(a, b):
    return lax.dot_general(a, b, (((1,), (1,)), ((), ())), preferred_element_type=F32)


def _iota(shape, dim):
    return lax.broadcasted_iota(jnp.int32, shape, dim)


def _split_bf16(x):
    hi = x.astype(BF16)
    lo = (x - hi.astype(F32)).astype(BF16)
    return hi, lo


def _head_mean_sq(x):
    w = x.shape[1]
    ones_bd = jnp.where(_iota((LANES, LANES), 0) // HEAD_DIM == _iota((LANES, LANES), 1) // HEAD_DIM,
                        1.0, 0.0).astype(BF16)
    hi, lo = _split_bf16(x * x)
    cols = []
    for c in range(w // LANES):
        sl = slice(c * LANES, (c + 1) * LANES)
        cols.append(_dot(hi[:, sl], ones_bd) + _dot(lo[:, sl], ones_bd))
    return jnp.concatenate(cols, axis=1) * (1.0 / HEAD_DIM)


def _tile_lanes(t, width):
    return jnp.concatenate([t] * (width // t.shape[1]), axis=1)


def _norm_rope(x, g, cos, sin):
    w = x.shape[1]
    y = x * lax.rsqrt(_head_mean_sq(x) + EPS) * _tile_lanes(g, w)
    first_half = (_iota(y.shape, 1) % HEAD_DIM) < (HEAD_DIM // 2)
    rot = jnp.where(first_half, pltpu.roll(y, w - HEAD_DIM // 2, 1), pltpu.roll(y, HEAD_DIM // 2, 1))
    return y * _tile_lanes(cos, w) + rot * _tile_lanes(sin, w)


def _silu(z):
    return z * jax.nn.sigmoid(z)


def _project(x, shift, scale, norm_g, w_ref):
    ms = jnp.mean(x * x, axis=-1, keepdims=True)
    h = (x * lax.rsqrt(ms + EPS) * norm_g) * (1.0 + scale) + shift
    hb = h.astype(BF16)
    return lambda lo, hi: _dot(hb, w_ref[:, lo:hi])


def _layer_norm(v, g, b):
    mu = jnp.mean(v, axis=-1, keepdims=True)
    d = v - mu
    var = jnp.mean(d * d, axis=-1, keepdims=True)
    return d * lax.rsqrt(var + EPS) * g + b


def _ada_kernel(c_ref, w_ref, b_ref, o_ref):
    o_ref[...] = _dot(c_ref[...].astype(BF16), w_ref[...].astype(BF16)) + b_ref[...]


def _ada_call(c_all, w_ada, b_ada):
    rows = c_all.shape[0]
    n = w_ada.shape[1]
    return pl.pallas_call(
        _ada_kernel,
        grid=(n // D_MODEL,),
        in_specs=[pl.BlockSpec((rows, D_MODEL), lambda j: (0, 0)),
                  pl.BlockSpec((D_MODEL, D_MODEL), lambda j: (0, j)),
                  pl.BlockSpec((1, D_MODEL), lambda j: (0, j))],
        out_specs=pl.BlockSpec((rows, D_MODEL), lambda j: (0, j)),
        out_shape=jax.ShapeDtypeStruct((rows, n), F32),
        compiler_params=pltpu.CompilerParams(vmem_limit_bytes=VMEM_LIMIT),
        name="ada",
    )(c_all, w_ada, b_ada)


def _compress_rows(rows, pe, w_bd):
    t = rows.shape[0]
    pooled = jnp.sum(rows.reshape(t // L_CMP, L_CMP, KV_WIDTH) + pe[None], axis=1) * (1.0 / L_CMP)
    return _dot(pooled.astype(BF16), w_bd)


def _p_proj_kernel(x_ref, shift_ref, scale_ref, ng_ref, w_ref, cos_ref, sin_ref, qg_ref, kg_ref,
                   pek_ref, pev_ref, wck_ref, wcv_ref, vng_ref, vnb_ref, ws_ref, bst_ref, wbrb_ref,
                   qt_out, kc_t, ks_t, kw_t, vc_t, vs_t, vw_t, ks_rows, kw_rows, kcmp_out, vcmp_out,
                   gates_t, sza_out, sga_out, mb_out):
    tm = x_ref.shape[1]
    seg = _project(x_ref[0], shift_ref[0], scale_ref[0], ng_ref[...], w_ref)
    cos, sin = cos_ref[...], sin_ref[...]

    q = _norm_rope(seg(C_Q, C_K), qg_ref[...], cos, sin) * SM_SCALE
    for c in range(WIDTH_A // LANES):
        qt_out[0, c * LANES:(c + 1) * LANES, :] = q[:, c * LANES:(c + 1) * LANES].T.astype(BF16)
    k = _norm_rope(seg(C_K, C_V), kg_ref[...], cos, sin)
    v = seg(C_V, C_G)
    for br, (k_t, v_t) in enumerate(((kc_t, vc_t), (ks_t, vs_t), (kw_t, vw_t))):
        sl = slice(br * KV_WIDTH, (br + 1) * KV_WIDTH)
        k_t[0] = k[:, sl].T
        v_t[0] = v[:, sl].T
    ks_rows[0] = k[:, KV_WIDTH:2 * KV_WIDTH].astype(BF16)
    kw_rows[0] = k[:, 2 * KV_WIDTH:3 * KV_WIDTH].astype(BF16)
    kcmp_out[0] = _compress_rows(k[:, 0:KV_WIDTH], pek_ref[...], wck_ref[...])
    vcmp_out[0] = _compress_rows(v[:, 0:KV_WIDTH], pev_ref[...], wcv_ref[...])

    gates_t[0] = jax.nn.sigmoid(seg(C_G, C_ZA)).T
    sza_out[0] = _silu(seg(C_ZA, C_U))
    sga_out[0] = jax.nn.sigmoid(seg(C_GA, C_GB))

    vn = _layer_norm(seg(C_VB, C_ZB), vng_ref[...], vnb_ref[...]).astype(BF16)
    causal = _iota((CHUNK, CHUNK), 0) >= _iota((CHUNK, CHUNK), 1)
    chunks = []
    for c in range(tm // CHUNK):
        groups = []
        for g in range(N_GROUPS_B):
            wsg = jnp.where(causal, ws_ref[g], 0.0).astype(BF16)
            vg = vn[c * CHUNK:(c + 1) * CHUNK, g * GROUP_W_B:(g + 1) * GROUP_W_B]
            groups.append(_dot(wsg, vg) + bst_ref[:, g:g + 1])
        chunks.append(jnp.concatenate(groups, axis=1))
    s_b = jnp.concatenate(chunks, axis=0)
    t = seg(C_U, C_VB) * s_b * _silu(seg(C_ZB, C_GA))
    mb_out[0] = jax.nn.sigmoid(seg(C_GB, C_END)) * _dot(t.astype(BF16), wbrb_ref[...])


def _full(shape):
    nd = len(shape)
    return pl.BlockSpec(shape, lambda *_: (0,) * nd)


def _p_proj_call(x, shift, scale, norm_g, w_all, cos_t, sin_t, qg, kg, pek, pev, wck, wcv, vng, vnb,
                 w_s, bs_t, w_br_b):
    b, s, _ = x.shape
    tm = PROMPT_ROWS
    row = lambda w: pl.BlockSpec((1, tm, w), lambda bi, i: (bi, i, 0))
    tok_minor = pl.BlockSpec((1, KV_WIDTH, tm), lambda bi, i: (bi, 0, i))
    per_batch = pl.BlockSpec((1, 1, D_MODEL), lambda bi, i: (bi, 0, 0))
    cmp_spec = pl.BlockSpec((1, tm // L_CMP, KV_WIDTH), lambda bi, i: (bi, i, 0))
    tab = pl.BlockSpec((tm, LANES), lambda bi, i: (i, 0))
    in_specs = [row(D_MODEL), per_batch, per_batch, _full(norm_g.shape), _full(w_all.shape), tab, tab,
                _full(qg.shape), _full(kg.shape), _full(pek.shape), _full(pev.shape), _full(wck.shape),
                _full(wcv.shape), _full(vng.shape), _full(vnb.shape), _full(w_s.shape), _full(bs_t.shape),
                _full(w_br_b.shape)]
    qt_spec = pl.BlockSpec((1, WIDTH_A, tm), lambda bi, i: (bi, 0, i))
    out_specs = [qt_spec] + [tok_minor] * 6 + [row(KV_WIDTH), row(KV_WIDTH), cmp_spec, cmp_spec,
                                                tok_minor, row(WIDTH_A), row(D_MODEL), row(D_MODEL)]
    tm_shape = jax.ShapeDtypeStruct((b, KV_WIDTH, s), F32)
    rows_shape = jax.ShapeDtypeStruct((b, s, KV_WIDTH), BF16)
    cmp_shape = jax.ShapeDtypeStruct((b, s // L_CMP, KV_WIDTH), F32)
    out_shape = [jax.ShapeDtypeStruct((b, WIDTH_A, s), BF16)] + [tm_shape] * 6 + [
        rows_shape, rows_shape, cmp_shape, cmp_shape,
        jax.ShapeDtypeStruct((b, LANES, s), F32), jax.ShapeDtypeStruct((b, s, WIDTH_A), F32),
        jax.ShapeDtypeStruct((b, s, D_MODEL), F32), jax.ShapeDtypeStruct((b, s, D_MODEL), F32)]
    return pl.pallas_call(
        _p_proj_kernel,
        grid=(b, s // tm),
        in_specs=in_specs,
        out_specs=out_specs,
        out_shape=out_shape,
        compiler_params=pltpu.CompilerParams(dimension_semantics=("arbitrary", "arbitrary"),
                                             vmem_limit_bytes=VMEM_LIMIT),
        name="p_proj",
    )(x, shift, scale, norm_g, w_all, cos_t, sin_t, qg, kg, pek, pev, wck, wcv, vng, vnb, w_s, bs_t, w_br_b)


def _select_blocks(imp_c, qblk, n_blocks):
    ratio = L_SLC // L_CMP
    assert ratio == 2
    lane = _iota(imp_c.shape, 1)
    imp = imp_c + pltpu.roll(imp_c, LANES - 1, 1)
    blk = lane // ratio
    forced = jnp.where((blk == 0) | (blk == qblk), 1.0, 0.0)
    score = jnp.where(blk <= qblk, imp + FORCE_BONUS * forced, NEG)
    rank = jnp.zeros(imp_c.shape, F32)
    for j in range(n_blocks):
        vj = score[:, ratio * j:ratio * j + 1]
        earlier = jnp.where(lane > ratio * j, 1.0, 0.0)
        rank = rank + jnp.where(vj > score, 1.0, jnp.where(vj == score, earlier, 0.0))
    cand = (lane % ratio == 0) & (lane < ratio * n_blocks)
    return jnp.where(cand & (rank < float(min(N_SEL, n_blocks))), 1.0, 0.0)


def _softmax_rows(s, valid):
    sm = jnp.where(valid, s, NEG)
    e = jnp.exp(sm - jnp.max(sm, axis=-1, keepdims=True))
    return e, 1.0 / jnp.sum(e, axis=-1, keepdims=True)


def _merge_and_project(o_a, sza, sga, mb, x, gate, wbra_ref, wout_ref):
    a = _dot((o_a * sza).astype(BF16), wbra_ref[...])
    m = sga * a + mb
    return x + gate * _dot(m.astype(BF16), wout_ref[...])


def _select_blocks_t(imp, qblk):
    n_blocks = imp.shape[0]
    blk = _iota(imp.shape, 0)
    forced = jnp.where((blk == 0) | (blk == qblk), 1.0, 0.0)
    score = jnp.where(blk <= qblk, imp + FORCE_BONUS * forced, NEG)
    rank = jnp.zeros(imp.shape, F32)
    for j in range(n_blocks):
        vj = score[j:j + 1, :]
        earlier = jnp.where(blk > j, 1.0, 0.0)
        rank = rank + jnp.where(vj > score, 1.0, jnp.where(vj == score, earlier, 0.0))
    return jnp.where(rank < float(min(N_SEL, n_blocks)), 1.0, 0.0)


def _softmax_cols(s):
    e = jnp.exp(s - jnp.max(s, axis=0, keepdims=True))
    return e, 1.0 / jnp.sum(e, axis=0, keepdims=True)


def _p_attn_kernel(qt_ref, ks_ref, vst_ref, kw_ref, vwt_ref, kc_ref, vc_ref, gt_ref, sza_ref, sga_ref,
                   mb_ref, x_ref, gate_ref, wbra_ref, wout_ref, y_ref, qt_scr, sel_scr, m_scr, acc_scr):
    i = pl.program_id(1)
    tq = Q_BLOCK
    n_cmp = kc_ref.shape[1]
    half = n_cmp // 2
    assert L_SLC == 2 * L_CMP and n_cmp <= LANES
    cols = GQA * tq
    qpos1 = i * tq + _iota((1, tq), 1)
    qpos = jnp.concatenate([qpos1] * GQA, axis=1)
    band = WINDOW + tq
    ws = pl.multiple_of(jnp.maximum(i * tq - WINDOW, 0), LANES)
    zeros_q = jnp.zeros((HEAD_DIM, cols), BF16)
    perm = lambda ref: jnp.concatenate([ref[0, pl.ds(0, half, stride=2), :], ref[0, pl.ds(1, half, stride=2), :],
                                        jnp.zeros((LANES - n_cmp, KV_WIDTH), F32)], axis=0)
    kc = perm(kc_ref).astype(BF16)
    vc_t = perm(vc_ref).T.astype(BF16)
    crow = _iota((LANES, cols), 0)
    cblk = 2 * (crow % half) + crow // half
    mc = ((cblk + 1) * L_CMP - 1 <= qpos) & (crow < n_cmp)
    gates_t = gt_ref[0]
    groups = [slice(g * HEAD_DIM, (g + 1) * HEAD_DIM) for g in range(N_KV)]

    o_c = []
    for g in range(N_KV):
        qt_g = jnp.concatenate([qt_ref[0, h * HEAD_DIM:(h + 1) * HEAD_DIM, :]
                                for h in range(g * GQA, (g + 1) * GQA)], axis=1)
        qt_scr[g] = jnp.concatenate([qt_g, zeros_q] if g == 0 else [zeros_q, qt_g], axis=0)
        s_c = jnp.where(mc, _dot(kc, qt_scr[g]), NEG)
        e_c, r_c = _softmax_cols(s_c)
        p_c = jnp.where(mc, e_c * r_c, 0.0)
        o_c.append(_dot(vc_t[groups[g]], p_c.astype(BF16)))
        imp = p_c[:, 0:tq]
        for r in range(1, GQA):
            imp = imp + p_c[:, r * tq:(r + 1) * tq]
        sel_scr[g] = _select_blocks_t(imp[0:half] + imp[half:2 * half], qpos1 // L_SLC)

    m_scr[...] = jnp.full(m_scr.shape, NEG, F32)
    acc_scr[...] = jnp.zeros(acc_scr.shape, F32)
    per_chunk = SEL_CHUNK // L_SLC

    def with_ones(v_t):
        return jnp.concatenate([v_t.astype(BF16), jnp.ones((ONES_ROWS, v_t.shape[1]), BF16)], axis=0)

    def body(kc_i, carry):
        off = pl.multiple_of(kc_i * SEL_CHUNK, SEL_CHUNK)
        causal = off + _iota((SEL_CHUNK, tq), 0) <= qpos1
        k_rows = ks_ref[0, pl.ds(off, SEL_CHUNK), :]
        scores = [_dot(k_rows, qt_scr[g]) for g in range(N_KV)]
        for g in range(N_KV):
            picked = jnp.concatenate(
                [jnp.broadcast_to(sel_scr[g, pl.ds(kc_i * per_chunk + t, 1), :], (L_SLC, tq))
                 for t in range(per_chunk)], axis=0)
            bias = jnp.where((picked > 0.5) & causal, 0.0, NEG)
            s = scores[g] + jnp.concatenate([bias] * GQA, axis=1)
            m_old = m_scr[g]
            m_new = jnp.maximum(m_old, jnp.max(s, axis=0, keepdims=True))
            p = jnp.exp(s - m_new).astype(BF16)
            m_scr[g] = m_new
            acc_scr[g] = jnp.exp(m_old - m_new) * acc_scr[g] + _dot(
                with_ones(vst_ref[0, groups[g], pl.ds(off, SEL_CHUNK)]), p)
        return carry

    lax.fori_loop(0, ((i + 1) * tq + SEL_CHUNK - 1) // SEL_CHUNK, body, 0)

    kwpos = ws + _iota((band, tq), 0)
    bias_w = jnp.where((kwpos <= qpos1) & (kwpos > qpos1 - WINDOW), 0.0, NEG)
    bias_w = jnp.concatenate([bias_w] * GQA, axis=1)
    k_band = kw_ref[0, pl.ds(ws, band), :]
    scores_w = [_dot(k_band, qt_scr[g]) for g in range(N_KV)]
    pair = []
    for g in range(N_KV):
        s_w = scores_w[g] + bias_w
        e_w = jnp.exp(s_w - jnp.max(s_w, axis=0, keepdims=True)).astype(BF16)
        acc_w = _dot(with_ones(vwt_ref[0, groups[g], pl.ds(ws, band)]), e_w)
        o_w = acc_w[0:HEAD_DIM] * (1.0 / acc_w[HEAD_DIM:HEAD_DIM + 1])
        acc_s = acc_scr[g]
        o_s = acc_s[0:HEAD_DIM] * (1.0 / acc_s[HEAD_DIM:HEAD_DIM + 1])
        for r in range(GQA):
            h = g * GQA + r
            cs = slice(r * tq, (r + 1) * tq)
            pair.append(gates_t[3 * h:3 * h + 1] * o_c[g][:, cs] + gates_t[3 * h + 1:3 * h + 2] * o_s[:, cs]
                        + gates_t[3 * h + 2:3 * h + 3] * o_w[:, cs])
    per_lane = LANES // HEAD_DIM
    o_a = jnp.concatenate([jnp.concatenate(pair[j:j + per_lane], axis=0).T
                           for j in range(0, N_HEADS, per_lane)], axis=1)
    y_ref[0] = _merge_and_project(o_a, sza_ref[0], sga_ref[0], mb_ref[0], x_ref[0], gate_ref[0],
                                  wbra_ref, wout_ref)


def _p_attn_call(q_t, ks_rows, vs_t, kw_rows, vw_t, kc, vc, gates_t, sza, sga, mb, x, gate, w_br_a, w_out):
    b, s, _ = x.shape
    tq = Q_BLOCK
    row = lambda w: pl.BlockSpec((1, tq, w), lambda bi, i: (bi, i, 0))
    col = lambda a: pl.BlockSpec((1, a.shape[1], tq), lambda bi, i: (bi, 0, i))
    seq = lambda a: pl.BlockSpec((1,) + a.shape[1:], lambda bi, i: (bi, 0, 0))
    in_specs = [col(q_t), seq(ks_rows), seq(vs_t), seq(kw_rows), seq(vw_t), seq(kc), seq(vc),
                col(gates_t), row(WIDTH_A), row(D_MODEL), row(D_MODEL), row(D_MODEL), seq(gate),
                _full(w_br_a.shape), _full(w_out.shape)]
    return pl.pallas_call(
        _p_attn_kernel,
        grid=(b, s // tq),
        in_specs=in_specs,
        out_specs=row(D_MODEL),
        out_shape=jax.ShapeDtypeStruct((b, s, D_MODEL), F32),
        scratch_shapes=[pltpu.VMEM((N_KV, KV_WIDTH, GQA * tq), BF16),
                        pltpu.VMEM((N_KV, s // L_SLC, tq), F32),
                        pltpu.VMEM((N_KV, 1, GQA * tq), F32),
                        pltpu.VMEM((N_KV, HEAD_DIM + ONES_ROWS, GQA * tq), F32)],
        compiler_params=pltpu.CompilerParams(dimension_semantics=("arbitrary", "arbitrary"),
                                             vmem_limit_bytes=VMEM_LIMIT),
        name="p_attn",
    )(q_t, ks_rows, vs_t, kw_rows, vw_t, kc, vc, gates_t, sza, sga, mb, x, gate, w_br_a, w_out)


def _s_proj_kernel(x_ref, shift_ref, scale_ref, ng_ref, w_ref, cos_ref, sin_ref, qg_ref, kg_ref,
                   vng_ref, vnb_ref, ws0_ref, bs0_ref, wbrb_ref,
                   q_out, k_out, v_out, kt_out, vt_out, gates_out, sza_out, sga_out, mb_out, vn_out):
    seg = _project(x_ref[...], shift_ref[...], scale_ref[...], ng_ref[...], w_ref)
    cos, sin = cos_ref[...], sin_ref[...]
    q_out[...] = _norm_rope(seg(C_Q, C_K), qg_ref[...], cos, sin)
    k = _norm_rope(seg(C_K, C_V), kg_ref[...], cos, sin)
    v = seg(C_V, C_G)
    k_out[...] = k
    v_out[...] = v
    for br in range(3):
        sl = slice(br * KV_WIDTH, (br + 1) * KV_WIDTH)
        kt_out[br] = k[:, sl].T
        vt_out[br] = v[:, sl].T
    gates_out[...] = jax.nn.sigmoid(seg(C_G, C_ZA))
    sza_out[...] = _silu(seg(C_ZA, C_U))
    sga_out[...] = jax.nn.sigmoid(seg(C_GA, C_GB))
    vn = _layer_norm(seg(C_VB, C_ZB), vng_ref[...], vnb_ref[...])
    vn_out[...] = vn
    s_b = ws0_ref[...] * vn + bs0_ref[...]
    t = seg(C_U, C_VB) * s_b * _silu(seg(C_ZB, C_GA))
    mb_out[...] = jax.nn.sigmoid(seg(C_GB, C_END)) * _dot(t.astype(BF16), wbrb_ref[...])


def _s_proj_call(x, shift, scale, norm_g, w_all, cos1, sin1, qg, kg, vng, vnb, ws0, bs0, w_br_b):
    n = x.shape[0]
    args = (x, shift, scale, norm_g, w_all, cos1, sin1, qg, kg, vng, vnb, ws0, bs0, w_br_b)
    sds = lambda *shape: jax.ShapeDtypeStruct(shape, F32)
    out_shape = [sds(n, WIDTH_A), sds(n, 3 * KV_WIDTH), sds(n, 3 * KV_WIDTH), sds(3, KV_WIDTH, n),
                 sds(3, KV_WIDTH, n), sds(n, LANES), sds(n, WIDTH_A), sds(n, D_MODEL), sds(n, D_MODEL),
                 sds(n, WIDTH_B)]
    return pl.pallas_call(
        _s_proj_kernel,
        grid=(1,),
        in_specs=[_full(a.shape) for a in args],
        out_specs=[_full(o.shape) for o in out_shape],
        out_shape=out_shape,
        compiler_params=pltpu.CompilerParams(vmem_limit_bytes=VMEM_LIMIT),
        name="s_proj",
    )(*args)


def _s_attn_kernel(pt_ref, q_ref, gates_ref, knew_ref, vnew_ref, knewt_ref, vnewt_ref, kwin_ref, vwin_ref,
                   pek_ref, pev_ref, wck_ref, wcv_ref, pool_ref, expand_ref,
                   kc_hbm, vc_hbm, ks_hbm, vs_hbm,
                   oa_ref, okw_ref, ovw_ref, buf, sem):
    t = pl.program_id(0)
    n_groups = pl.num_programs(0) - 1
    per_step = q_ref.shape[0]
    n_seqs = pt_ref.shape[0]
    n_pages = pt_ref.shape[1]
    past = n_pages * PAGE_SIZE
    caches = (kc_hbm, vc_hbm, ks_hbm, vs_hbm)

    def page_copies(group, slot_):
        return [pltpu.make_async_copy(hbm.at[pt_ref[jnp.minimum(group * per_step + j, n_seqs - 1), p]],
                                      buf.at[slot_, j, c, :, pl.ds(p * PAGE_SIZE, PAGE_SIZE)],
                                      sem.at[slot_, c])
                for j in range(per_step) for c, hbm in enumerate(caches) for p in range(n_pages)]

    @pl.when(t < n_groups)
    def _():
        for cp in page_copies(t, lax.rem(t, 2)):
            cp.start()

    @pl.when(t > 0)
    def _():
        group = t - 1
        slot = lax.rem(group, 2)
        for cp in page_copies(group, slot):
            cp.wait()
        chains = [_s_attn_one(group * per_step + j, j, buf.at[slot, j], q_ref, gates_ref, knew_ref, vnew_ref,
                              knewt_ref, vnewt_ref, kwin_ref, vwin_ref, pek_ref, pev_ref, wck_ref, wcv_ref,
                              pool_ref, expand_ref, oa_ref, okw_ref, ovw_ref, past) for j in range(per_step)]
        for _ in itertools.zip_longest(*chains):
            pass


def _s_attn_one(b, j, buf, q_ref, gates_ref, knew_ref, vnew_ref, knewt_ref, vnewt_ref, kwin_ref, vwin_ref,
                pek_ref, pev_ref, wck_ref, wcv_ref, pool_ref, expand_ref, oa_ref, okw_ref, ovw_ref, past):
    qpos = past

    lane = _iota((KV_WIDTH, knewt_ref.shape[2]), 1)
    col = lambda ref, br: jnp.sum(jnp.where(lane == b, ref[br], 0.0), axis=1, keepdims=True)
    knew = knew_ref[pl.ds(b, 1), :]
    vnew = vnew_ref[pl.ds(b, 1), :]

    hrow = _iota((N_HEADS, 1), 0)
    first_group = hrow < GQA
    by_group = lambda f: jnp.where(first_group, f(0), f(1))
    gl = lambda g: slice(g * HEAD_DIM, (g + 1) * HEAD_DIM)
    qb = q_ref[j].astype(BF16)

    n_cmp = -(-(past + 1) // L_SLC) * L_SLC // L_CMP
    n_slc = n_cmp * L_CMP // L_SLC

    def summaries(c, new_row, pe_ref, w_ref):
        rows_t = buf[c, :, pl.ds(0, past)]
        hi, lo = _split_bf16(rows_t)
        pe_sum = jnp.sum(pe_ref[...], axis=0, keepdims=True)
        pooled = (_dot_nt(pool_ref[...], hi) + _dot_nt(pool_ref[...], lo) + pe_sum) * (1.0 / L_CMP)
        r = _iota((SUBLANES, KV_WIDTH), 0)
        tail = jnp.where(r == 0, new_row + pe_sum, jnp.where(r == 1, pe_sum, 0.0)) * (1.0 / L_CMP)
        zeros = jnp.zeros((LANES - pooled.shape[0] - SUBLANES, KV_WIDTH), F32)
        return _dot(jnp.concatenate([pooled, tail, zeros], axis=0).astype(BF16), w_ref[...])

    kc = summaries(0, knew[:, 0:KV_WIDTH], pek_ref, wck_ref).astype(BF16)
    yield
    vc = summaries(1, vnew[:, 0:KV_WIDTH], pev_ref, wcv_ref).astype(BF16)
    yield
    s_c = by_group(lambda g: _dot_nt(qb, kc[:, gl(g)])) * SM_SCALE
    c = _iota(s_c.shape, 1)
    mc = ((c + 1) * L_CMP - 1 <= qpos) & (c < n_cmp)
    e_c, r_c = _softmax_rows(s_c, mc)
    p_c = jnp.where(mc, e_c * r_c, 0.0)
    o_c = by_group(lambda g: _dot(p_c.astype(BF16), vc[:, gl(g)]))
    imp = by_group(lambda g: jnp.sum(p_c[g * GQA:(g + 1) * GQA], axis=0, keepdims=True))
    imp = jnp.broadcast_to(imp, p_c.shape)
    yield
    sel = _select_blocks(imp, jnp.full((N_HEADS, 1), qpos // L_SLC, jnp.int32), n_slc)
    yield

    picked = _dot(sel.astype(BF16), expand_ref[...])
    s_s = by_group(lambda g: _dot(qb, buf[2, gl(g), :].astype(BF16))) * SM_SCALE
    kpos = _iota(s_s.shape, 1)
    s_s = jnp.where((picked > 0.5) & (kpos <= qpos), s_s, NEG)
    yield
    own =_iota((N_HEADS, KV_WIDTH), 1) // HEAD_DIM == hrow // GQA
    rounded = lambda a: a.astype(BF16).astype(F32)
    q_pair = jnp.concatenate([qb.astype(F32)] * N_KV, axis=1)
    s_new = jnp.sum(jnp.where(own, q_pair * rounded(knew[:, KV_WIDTH:2 * KV_WIDTH]), 0.0),
                    axis=1, keepdims=True) * SM_SCALE
    new_lane = (L_SLC // L_CMP) * (past // L_SLC)
    s_new = jnp.where((sel[:, new_lane:new_lane + 1] > 0.5) & (past <= qpos), s_new, NEG)
    m_s = jnp.maximum(jnp.max(s_s, axis=-1, keepdims=True), s_new)
    e_s, e_new = jnp.exp(s_s - m_s), jnp.exp(s_new - m_s)
    r_s = 1.0 / (jnp.sum(e_s, axis=-1, keepdims=True) + e_new)
    v_new = by_group(lambda g: rounded(vnew[:, KV_WIDTH + g * HEAD_DIM:KV_WIDTH + (g + 1) * HEAD_DIM]))
    o_s = (by_group(lambda g: _dot_nt(e_s.astype(BF16), buf[3, gl(g), :].astype(BF16)))
           + rounded(e_new) * v_new) * r_s
    yield

    wb = kwin_ref.shape[2]
    wlane = _iota((KV_WIDTH, wb), 1)
    kw = jnp.where(wlane == wb - 1, col(knewt_ref, 2), pltpu.roll(kwin_ref[j], wb - 1, 1))
    vw = jnp.where(wlane == wb - 1, col(vnewt_ref, 2), pltpu.roll(vwin_ref[j], wb - 1, 1))
    okw_ref[j] = kw
    ovw_ref[j] = vw
    yield
    s_w = by_group(lambda g: _dot(qb, kw[gl(g)].astype(BF16))) * SM_SCALE
    kwpos = past - wb + 1 + _iota(s_w.shape, 1)
    e_w, r_w = _softmax_rows(s_w, (kwpos <= qpos) & (kwpos > qpos - WINDOW) & (kwpos >= 0))
    o_w = by_group(lambda g: _dot_nt(e_w.astype(BF16), vw[gl(g)].astype(BF16))) * r_w

    gates = gates_ref[j]
    oa_ref[j] = gates[:, 0:1] * o_c + gates[:, 1:2] * o_s + gates[:, 2:3] * o_w


def _s_attn_call(page_table, q3, gates3, knew, vnew, knew_t, vnew_t, kwin_t, vwin_t, pek, pev, wck, wcv,
                 kc_pool, vc_pool, ks_pool, vs_pool):
    n, n_pages = page_table.shape
    past = n_pages * PAGE_SIZE
    wb = kwin_t.shape[2]
    tok = np.arange(past)
    pool = jnp.asarray((tok[None, :] // L_CMP == np.arange(past // L_CMP)[:, None]), BF16)
    expand = jnp.asarray(np.arange(LANES)[:, None] == 2 * (np.arange(past)[None, :] // L_SLC), BF16)
    k = SAMPLE_SEQS_PER_STEP
    assert n % k == 0
    per_seq = lambda a: pl.BlockSpec((k,) + a.shape[1:],
                                     lambda t, pt: (jnp.maximum(t - 1, 0),) + (0,) * (a.ndim - 1))
    full = lambda a: pl.BlockSpec(a.shape, lambda t, pt: (0,) * a.ndim)
    hbm = pl.BlockSpec(memory_space=pl.ANY)
    resident = (knew, vnew, knew_t, vnew_t)
    consts = (pek, pev, wck, wcv, pool, expand)
    grid_spec = pltpu.PrefetchScalarGridSpec(
        num_scalar_prefetch=1,
        grid=(n // k + 1,),
        in_specs=[per_seq(q3), per_seq(gates3)] + [full(a) for a in resident]
                 + [per_seq(kwin_t), per_seq(vwin_t)] + [full(a) for a in consts] + [hbm] * 4,
        out_specs=[per_seq(q3), per_seq(kwin_t), per_seq(vwin_t)],
        scratch_shapes=[pltpu.VMEM((2, k, 4, KV_WIDTH, past), F32), pltpu.SemaphoreType.DMA((2, 4))],
    )
    return pl.pallas_call(
        _s_attn_kernel,
        grid_spec=grid_spec,
        out_shape=[jax.ShapeDtypeStruct(q3.shape, F32), jax.ShapeDtypeStruct(kwin_t.shape, F32),
                   jax.ShapeDtypeStruct(vwin_t.shape, F32)],
        compiler_params=pltpu.CompilerParams(dimension_semantics=("arbitrary",),
                                             vmem_limit_bytes=VMEM_LIMIT),
        name="s_attn",
    )(page_table, q3, gates3, knew, vnew, knew_t, vnew_t, kwin_t, vwin_t, pek, pev, wck, wcv, pool, expand,
      kc_pool, vc_pool, ks_pool, vs_pool)


def _s_out_kernel(oa_ref, sza_ref, sga_ref, mb_ref, x_ref, gate_ref, wbra_ref, wout_ref, y_ref):
    y_ref[...] = _merge_and_project(oa_ref[...], sza_ref[...], sga_ref[...], mb_ref[...], x_ref[...],
                                    gate_ref[...], wbra_ref, wout_ref)


def _s_out_call(o_a, sza, sga, mb, x, gate, w_br_a, w_out):
    args = (o_a, sza, sga, mb, x, gate, w_br_a, w_out)
    return pl.pallas_call(
        _s_out_kernel,
        grid=(1,),
        in_specs=[_full(a.shape) for a in args],
        out_specs=_full(x.shape),
        out_shape=jax.ShapeDtypeStruct(x.shape, F32),
        compiler_params=pltpu.CompilerParams(vmem_limit_bytes=VMEM_LIMIT),
        name="s_out",
    )(*args)


def _rope_tables(pos):
    half = HEAD_DIM // 2
    inv = ROPE_THETA ** (-jnp.arange(half, dtype=F32) * 2.0 / HEAD_DIM)
    ang = pos.astype(F32)[:, None] * inv[None, :]
    cos, sin = jnp.cos(ang), jnp.sin(ang)
    cos_t = jnp.concatenate([cos, cos] * (LANES // HEAD_DIM), axis=1)
    sin_t = jnp.concatenate([-sin, sin] * (LANES // HEAD_DIM), axis=1)
    return cos_t, sin_t


def _token_minor(a):
    b, t = a.shape[:2]
    return jnp.transpose(a, (0, 2, 3, 1)).reshape(b, KV_WIDTH, t)


def _token_major(a_t):
    b, _, t = a_t.shape
    return jnp.transpose(a_t.reshape(b, N_KV, HEAD_DIM, t), (0, 3, 1, 2))


def kernel(x_prompt, x_sample, cache_k_cmp, cache_v_cmp, cache_k_slc, cache_v_slc, cache_k_win, cache_v_win, page_table, c_prompt, c_sample, w_ada, b_ada, norm_g, w_in, q_norm_g, k_norm_g, cmp_pos_k, cmp_pos_v, w_cmp_k, w_cmp_v, vnorm_g, vnorm_b, w_s, b_s, w_br_a, w_br_b, w_out):
    assert w_ada.shape[0] == 1, "single layer"
    b, s, _ = x_prompt.shape
    n = x_sample.shape[0]
    assert x_sample.shape[1] == 1
    n_pages = page_table.shape[1]
    past = n_pages * PAGE_SIZE

    w = w_in[0]
    n_gate = 3 * N_HEADS
    w_all = jnp.concatenate([w[:, :C_G], w[:, C_G:C_G + n_gate], jnp.zeros((D_MODEL, LANES - n_gate), F32),
                             w[:, C_G + n_gate:]], axis=1).astype(BF16)
    assert w_all.shape[1] == C_END
    eye = jnp.eye(N_KV, dtype=F32)
    wck = jnp.kron(eye, w_cmp_k[0]).astype(BF16)
    wcv = jnp.kron(eye, w_cmp_v[0]).astype(BF16)
    pek = jnp.tile(cmp_pos_k[0], (1, N_KV))
    pev = jnp.tile(cmp_pos_v[0], (1, N_KV))
    qg = jnp.tile(q_norm_g, (1, LANES // HEAD_DIM))
    kg = jnp.tile(k_norm_g, (1, LANES // HEAD_DIM))
    w_br_a_b, w_br_b_b, w_out_b = w_br_a[0].astype(BF16), w_br_b[0].astype(BF16), w_out[0].astype(BF16)

    mod = _ada_call(jnp.concatenate([c_prompt, c_sample], axis=0), w_ada[0], b_ada)
    shift, scale, gate = mod[:, :D_MODEL], mod[:, D_MODEL:2 * D_MODEL], mod[:, 2 * D_MODEL:]

    cos_p, sin_p = _rope_tables(jnp.arange(s, dtype=jnp.int32))
    (q_t, kc_t, ks_t, kw_t, vc_t, vs_t, vw_t, ks_rows, kw_rows, kcmp, vcmp, gates_t, sza, sga, mb) = _p_proj_call(
        x_prompt, shift[:b, None], scale[:b, None], norm_g, w_all, cos_p, sin_p, qg, kg, pek, pev, wck, wcv,
        vnorm_g, vnorm_b, w_s[0], b_s[0].T, w_br_b_b)
    y_prompt = _p_attn_call(q_t, ks_rows, vs_t, kw_rows, vw_t, kcmp, vcmp, gates_t, sza, sga, mb, x_prompt,
                            gate[:b, None], w_br_a_b, w_out_b)
    wb_p = min(WINDOW, s)
    p_states = [_token_major(a)[None] for a in (kc_t, vc_t, ks_t, vs_t, kw_t[:, :, s - wb_p:], vw_t[:, :, s - wb_p:])]

    xs = x_sample.reshape(n, D_MODEL)
    cos_s, sin_s = _rope_tables(jnp.full((1,), past, jnp.int32))
    ws0 = jnp.repeat(w_s[0, :, 0, 0], GROUP_W_B)[None]
    bs0 = jnp.repeat(b_s[0, :, 0], GROUP_W_B)[None]
    (q_s, k_s, v_s, kt_s, vt_s, gates_s, sza_s, sga_s, mb_s, vn_s) = _s_proj_call(
        xs, shift[b:], scale[b:], norm_g, w_all, cos_s, sin_s, qg, kg, vnorm_g, vnorm_b, ws0, bs0, w_br_b_b)
    pools = [_token_minor(c[0]) for c in (cache_k_cmp, cache_v_cmp, cache_k_slc, cache_v_slc)]
    o_a, kwin_new, vwin_new = _s_attn_call(
        page_table, q_s.reshape(n, N_HEADS, HEAD_DIM), gates_s[:, :3 * N_HEADS].reshape(n, N_HEADS, 3),
        k_s, v_s, kt_s, vt_s, _token_minor(cache_k_win[0]), _token_minor(cache_v_win[0]),
        pek, pev, wck, wcv, *pools)
    y_sample = _s_out_call(o_a.reshape(n, WIDTH_A), sza_s, sga_s, mb_s, xs, gate[b:], w_br_a_b, w_out_b)

    new_rows = lambda t, br: jnp.transpose(t[br].reshape(N_KV, HEAD_DIM, n), (2, 0, 1))[None, :, None]
    s_states = [new_rows(kt_s, 0), new_rows(vt_s, 0), new_rows(kt_s, 1), new_rows(vt_s, 1),
                _token_major(kwin_new)[None], _token_major(vwin_new)[None], vn_s[None, :, None]]
    return (y_prompt, y_sample.reshape(n, 1, D_MODEL), *p_states, *s_states)
```

```python
import itertools

import numpy as np
import jax
import jax.numpy as jnp
from jax import lax
from jax.experimental import pallas as pl
from jax.experimental.pallas import tpu as pltpu

F32 = jnp.float32
BF16 = jnp.bfloat16

D_MODEL = 1024
HEAD_DIM = 64
N_HEADS = 8
N_KV = 2
GQA = N_HEADS // N_KV
WIDTH_A = N_HEADS * HEAD_DIM
KV_WIDTH = N_KV * HEAD_DIM
L_CMP = 32
L_SLC = 64
N_SEL = 8
WINDOW = 512
Q_BLOCK = 256
FORCE_BONUS = 1.0e4
ROPE_THETA = 10000.0
CHUNK = 128
N_GROUPS_B = 4
WIDTH_B = 512
GROUP_W_B = WIDTH_B // N_GROUPS_B
PAGE_SIZE = 128
EPS = 1e-6
NEG = -1e30
SM_SCALE = HEAD_DIM ** -0.5
LOG2_E = 1.4426950408889634

LANES = 128
SUBLANES = 8
VMEM_LIMIT = 56 * 1024 * 1024

C_Q = 0
C_K = C_Q + WIDTH_A
C_V = C_K + 3 * KV_WIDTH
C_G = C_V + 3 * KV_WIDTH
C_ZA = C_G + LANES
C_U = C_ZA + WIDTH_A
C_VB = C_U + WIDTH_B
C_ZB = C_VB + WIDTH_B
C_GA = C_ZB + WIDTH_B
C_GB = C_GA + D_MODEL
C_END = C_GB + D_MODEL

PROMPT_ROWS = 256
SEL_CHUNK = 256
ONES_ROWS = 16
SAMPLE_SEQS_PER_STEP = 2


def _dot(a, b):
    return jnp.dot(a, b, preferred_element_type=F32)


def _dot_nt(a, b):
    return lax.dot_general(a, b, (((1,), (1,)), ((), ())), preferred_element_type=F32)


def _iota(shape, dim):
    return lax.broadcasted_iota(jnp.int32, shape, dim)


def _split_bf16(x):
    hi = x.astype(BF16)
    lo = (x - hi.astype(F32)).astype(BF16)
    return hi, lo


def _head_mean_sq(x):
    w = x.shape[1]
    ones_bd = jnp.where(_iota((LANES, LANES), 0) // HEAD_DIM == _iota((LANES, LANES), 1) // HEAD_DIM,
                        1.0, 0.0).astype(BF16)
    hi, lo = _split_bf16(x * x)
    cols = []
    for c in range(w // LANES):
        sl = slice(c * LANES, (c + 1) * LANES)
        cols.append(_dot(hi[:, sl], ones_bd) + _dot(lo[:, sl], ones_bd))
    return jnp.concatenate(cols, axis=1) * (1.0 / HEAD_DIM)


def _tile_lanes(t, width):
    return jnp.concatenate([t] * (width // t.shape[1]), axis=1)


def _norm_rope(x, g, cos, sin):
    w = x.shape[1]
    y = x * lax.rsqrt(_head_mean_sq(x) + EPS) * _tile_lanes(g, w)
    first_half = (_iota(y.shape, 1) % HEAD_DIM) < (HEAD_DIM // 2)
    rot = jnp.where(first_half, pltpu.roll(y, w - HEAD_DIM // 2, 1), pltpu.roll(y, HEAD_DIM // 2, 1))
    return y * _tile_lanes(cos, w) + rot * _tile_lanes(sin, w)


def _norm_rope_t(x_t, gain_t, cos_t, sin_t):
    half = HEAD_DIM // 2
    out = []
    for h in range(x_t.shape[0] // HEAD_DIM):
        x = x_t[h * HEAD_DIM:(h + 1) * HEAD_DIM]
        y = x * lax.rsqrt(jnp.mean(x * x, axis=0, keepdims=True) + EPS) * gain_t
        y1, y2 = y[:half], y[half:]
        out += [y1 * cos_t - y2 * sin_t, y2 * cos_t + y1 * sin_t]
    return jnp.concatenate(out, axis=0)


def _silu(z):
    return z * jax.nn.sigmoid(z)


def _project(x, shift, scale, norm_g, w_ref):
    ms = jnp.mean(x * x, axis=-1, keepdims=True)
    h = (x * lax.rsqrt(ms + EPS) * norm_g) * (1.0 + scale) + shift
    hb = h.astype(BF16)
    return lambda lo, hi: _dot(hb, w_ref[:, lo:hi])


def _layer_norm(v, g, b):
    mu = jnp.mean(v, axis=-1, keepdims=True)
    d = v - mu
    var = jnp.mean(d * d, axis=-1, keepdims=True)
    return d * lax.rsqrt(var + EPS) * g + b


def _ada_kernel(c_ref, w_ref, b_ref, o_ref):
    o_ref[...] = _dot(c_ref[...].astype(BF16), w_ref[...].astype(BF16)) + b_ref[...]


def _ada_call(c_all, w_ada, b_ada):
    rows = c_all.shape[0]
    n = w_ada.shape[1]
    return pl.pallas_call(
        _ada_kernel,
        grid=(n // D_MODEL,),
        in_specs=[pl.BlockSpec((rows, D_MODEL), lambda j: (0, 0)),
                  pl.BlockSpec((D_MODEL, D_MODEL), lambda j: (0, j)),
                  pl.BlockSpec((1, D_MODEL), lambda j: (0, j))],
        out_specs=pl.BlockSpec((rows, D_MODEL), lambda j: (0, j)),
        out_shape=jax.ShapeDtypeStruct((rows, n), F32),
        compiler_params=pltpu.CompilerParams(vmem_limit_bytes=VMEM_LIMIT),
        name="ada",
    )(c_all, w_ada, b_ada)


def _compress_rows(rows, pe, w_bd):
    t = rows.shape[0]
    pooled = jnp.sum(rows.reshape(t // L_CMP, L_CMP, KV_WIDTH) + pe[None], axis=1) * (1.0 / L_CMP)
    return _dot(pooled.astype(BF16), w_bd)


def _p_proj_kernel(x_ref, shift_ref, scale_ref, ng_ref, w_ref, cos_ref, sin_ref, qg_ref, kg_ref,
                   pek_ref, pev_ref, wck_ref, wcv_ref, vng_ref, vnb_ref, ws_ref, bst_ref, wbrb_ref,
                   qt_out, kc_t, ks_t, kw_t, vc_t, vs_t, vw_t, ks_rows, kw_rows, kcmp_out, vcmp_out,
                   gates_t, sza_out, sga_out, mb_out):
    tm = x_ref.shape[1]
    seg = _project(x_ref[0], shift_ref[0], scale_ref[0], ng_ref[...], w_ref)
    cos, sin = cos_ref[...], sin_ref[...]
    to_token_minor = lambda a: jnp.concatenate(
        [a[:, c * LANES:(c + 1) * LANES].T for c in range(a.shape[1] // LANES)], axis=0)

    qt_out[0] = (_norm_rope_t(to_token_minor(seg(C_Q, C_K)), qg_ref[...], cos, sin)
                 * (SM_SCALE * LOG2_E)).astype(BF16)
    k_t = _norm_rope_t(to_token_minor(seg(C_K, C_V)), kg_ref[...], cos, sin)
    v = seg(C_V, C_G)
    for br, (k_out, v_out) in enumerate(((kc_t, vc_t), (ks_t, vs_t), (kw_t, vw_t))):
        sl = slice(br * KV_WIDTH, (br + 1) * KV_WIDTH)
        k_out[0] = k_t[sl]
        v_out[0] = v[:, sl].T
    ks_rows[0] = k_t[KV_WIDTH:2 * KV_WIDTH].T.astype(BF16)
    kw_rows[0] = k_t[2 * KV_WIDTH:3 * KV_WIDTH].T.astype(BF16)
    kcmp_out[0] = _compress_rows(k_t[0:KV_WIDTH].T, pek_ref[...], wck_ref[...])
    vcmp_out[0] = _compress_rows(v[:, 0:KV_WIDTH], pev_ref[...], wcv_ref[...])

    gates_t[0] = jax.nn.sigmoid(seg(C_G, C_ZA)).T
    sza_out[0] = _silu(seg(C_ZA, C_U))
    sga_out[0] = jax.nn.sigmoid(seg(C_GA, C_GB))

    vn = _layer_norm(seg(C_VB, C_ZB), vng_ref[...], vnb_ref[...]).astype(BF16)
    causal = _iota((CHUNK, CHUNK), 0) >= _iota((CHUNK, CHUNK), 1)
    chunks = []
    for c in range(tm // CHUNK):
        groups = []
        for g in range(N_GROUPS_B):
            wsg = jnp.where(causal, ws_ref[g], 0.0).astype(BF16)
            vg = vn[c * CHUNK:(c + 1) * CHUNK, g * GROUP_W_B:(g + 1) * GROUP_W_B]
            groups.append(_dot(wsg, vg) + bst_ref[:, g:g + 1])
        chunks.append(jnp.concatenate(groups, axis=1))
    s_b = jnp.concatenate(chunks, axis=0)
    t = seg(C_U, C_VB) * s_b * _silu(seg(C_ZB, C_GA))
    mb_out[0] = jax.nn.sigmoid(seg(C_GB, C_END)) * _dot(t.astype(BF16), wbrb_ref[...])


def _full(shape):
    nd = len(shape)
    return pl.BlockSpec(shape, lambda *_: (0,) * nd)


def _p_proj_call(x, shift, scale, norm_g, w_all, cos_t, sin_t, qg, kg, pek, pev, wck, wcv, vng, vnb,
                 w_s, bs_t, w_br_b):
    b, s, _ = x.shape
    tm = PROMPT_ROWS
    row = lambda w: pl.BlockSpec((1, tm, w), lambda bi, i: (bi, i, 0))
    tok_minor = pl.BlockSpec((1, KV_WIDTH, tm), lambda bi, i: (bi, 0, i))
    per_batch = pl.BlockSpec((1, 1, D_MODEL), lambda bi, i: (bi, 0, 0))
    cmp_spec = pl.BlockSpec((1, tm // L_CMP, KV_WIDTH), lambda bi, i: (bi, i, 0))
    tab = pl.BlockSpec((HEAD_DIM // 2, tm), lambda bi, i: (0, i))
    in_specs = [row(D_MODEL), per_batch, per_batch, _full(norm_g.shape), _full(w_all.shape), tab, tab,
                _full(qg.shape), _full(kg.shape), _full(pek.shape), _full(pev.shape), _full(wck.shape),
                _full(wcv.shape), _full(vng.shape), _full(vnb.shape), _full(w_s.shape), _full(bs_t.shape),
                _full(w_br_b.shape)]
    qt_spec = pl.BlockSpec((1, WIDTH_A, tm), lambda bi, i: (bi, 0, i))
    out_specs = [qt_spec] + [tok_minor] * 6 + [row(KV_WIDTH), row(KV_WIDTH), cmp_spec, cmp_spec,
                                                tok_minor, row(WIDTH_A), row(D_MODEL), row(D_MODEL)]
    tm_shape = jax.ShapeDtypeStruct((b, KV_WIDTH, s), F32)
    rows_shape = jax.ShapeDtypeStruct((b, s, KV_WIDTH), BF16)
    cmp_shape = jax.ShapeDtypeStruct((b, s // L_CMP, KV_WIDTH), F32)
    out_shape = [jax.ShapeDtypeStruct((b, WIDTH_A, s), BF16)] + [tm_shape] * 6 + [
        rows_shape, rows_shape, cmp_shape, cmp_shape,
        jax.ShapeDtypeStruct((b, LANES, s), F32), jax.ShapeDtypeStruct((b, s, WIDTH_A), F32),
        jax.ShapeDtypeStruct((b, s, D_MODEL), F32), jax.ShapeDtypeStruct((b, s, D_MODEL), F32)]
    return pl.pallas_call(
        _p_proj_kernel,
        grid=(b, s // tm),
        in_specs=in_specs,
        out_specs=out_specs,
        out_shape=out_shape,
        compiler_params=pltpu.CompilerParams(dimension_semantics=("arbitrary", "arbitrary"),
                                             vmem_limit_bytes=VMEM_LIMIT),
        name="p_proj",
    )(x, shift, scale, norm_g, w_all, cos_t, sin_t, qg, kg, pek, pev, wck, wcv, vng, vnb, w_s, bs_t, w_br_b)


def _select_blocks(imp_c, qblk, n_blocks):
    ratio = L_SLC // L_CMP
    assert ratio == 2
    lane = _iota(imp_c.shape, 1)
    imp = imp_c + pltpu.roll(imp_c, LANES - 1, 1)
    blk = lane // ratio
    forced = jnp.where((blk == 0) | (blk == qblk), 1.0, 0.0)
    score = jnp.where(blk <= qblk, imp + FORCE_BONUS * forced, NEG)
    rank = jnp.zeros(imp_c.shape, F32)
    for j in range(n_blocks):
        vj = score[:, ratio * j:ratio * j + 1]
        earlier = jnp.where(lane > ratio * j, 1.0, 0.0)
        rank = rank + jnp.where(vj > score, 1.0, jnp.where(vj == score, earlier, 0.0))
    cand = (lane % ratio == 0) & (lane < ratio * n_blocks)
    return jnp.where(cand & (rank < float(min(N_SEL, n_blocks))), 1.0, 0.0)


def _softmax_rows(s, valid):
    sm = jnp.where(valid, s, NEG)
    e = jnp.exp(sm - jnp.max(sm, axis=-1, keepdims=True))
    return e, 1.0 / jnp.sum(e, axis=-1, keepdims=True)


def _merge_and_project(o_a, sza, sga, mb, x, gate, wbra_ref, wout_ref):
    a = _dot((o_a * sza).astype(BF16), wbra_ref[...])
    m = sga * a + mb
    return x + gate * _dot(m.astype(BF16), wout_ref[...])


def _select_blocks_t(imp, qblk):
    n_blocks = imp.shape[0]
    blk = _iota(imp.shape, 0)
    forced = jnp.where((blk == 0) | (blk == qblk), 1.0, 0.0)
    score = jnp.where(blk <= qblk, imp + FORCE_BONUS * forced, NEG)
    rank = jnp.zeros(imp.shape, F32)
    for j in range(n_blocks):
        vj = score[j:j + 1, :]
        earlier = jnp.where(blk > j, 1.0, 0.0)
        rank = rank + jnp.where(vj > score, 1.0, jnp.where(vj == score, earlier, 0.0))
    return jnp.where(rank < float(min(N_SEL, n_blocks)), 1.0, 0.0)


def _softmax_cols(s):
    e = jnp.exp2(s - jnp.max(s, axis=0, keepdims=True))
    return e, 1.0 / jnp.sum(e, axis=0, keepdims=True)


def _p_attn_kernel(qt_ref, ks_ref, vst_ref, kw_ref, vwt_ref, kc_ref, vc_ref, gt_ref, sza_ref, sga_ref,
                   mb_ref, x_ref, gate_ref, wbra_ref, wout_ref, y_ref, qt_scr, sel_scr, m_scr, acc_scr, s_even, s_odd):
    i = pl.program_id(1)
    tq = Q_BLOCK
    n_cmp = kc_ref.shape[1]
    half = n_cmp // 2
    assert L_SLC == 2 * L_CMP and n_cmp <= LANES
    cols = GQA * tq
    qpos1 = i * tq + _iota((1, tq), 1)
    qpos = jnp.concatenate([qpos1] * GQA, axis=1)
    band = WINDOW + tq
    ws = pl.multiple_of(jnp.maximum(i * tq - WINDOW, 0), LANES)
    zeros_q = jnp.zeros((HEAD_DIM, cols), BF16)
    perm = lambda ref: jnp.concatenate([ref[0, pl.ds(0, half, stride=2), :], ref[0, pl.ds(1, half, stride=2), :],
                                        jnp.zeros((LANES - n_cmp, KV_WIDTH), F32)], axis=0)
    kc = perm(kc_ref).astype(BF16)
    vc_t = perm(vc_ref).T.astype(BF16)
    crow = _iota((LANES, cols), 0)
    cblk = 2 * (crow % half) + crow // half
    mc = ((cblk + 1) * L_CMP - 1 <= qpos) & (crow < n_cmp)
    gates_t = gt_ref[0]
    groups = [slice(g * HEAD_DIM, (g + 1) * HEAD_DIM) for g in range(N_KV)]

    o_c = []
    for g in range(N_KV):
        qt_g = jnp.concatenate([qt_ref[0, h * HEAD_DIM:(h + 1) * HEAD_DIM, :]
                                for h in range(g * GQA, (g + 1) * GQA)], axis=1)
        qt_scr[g] = jnp.concatenate([qt_g, zeros_q] if g == 0 else [zeros_q, qt_g], axis=0)
        s_c = jnp.where(mc, _dot(kc, qt_scr[g]), NEG)
        e_c, r_c = _softmax_cols(s_c)
        p_c = jnp.where(mc, e_c * r_c, 0.0)
        o_c.append(_dot(vc_t[groups[g]], p_c.astype(BF16)))
        imp = p_c[:, 0:tq]
        for r in range(1, GQA):
            imp = imp + p_c[:, r * tq:(r + 1) * tq]
        sel_scr[g] = _select_blocks_t(imp[0:half] + imp[half:2 * half], qpos1 // L_SLC)

    m_scr[...] = jnp.full(m_scr.shape, NEG, F32)
    acc_scr[...] = jnp.zeros(acc_scr.shape, F32)
    per_chunk = SEL_CHUNK // L_SLC

    def with_ones(v_t):
        return jnp.concatenate([v_t.astype(BF16), jnp.ones((ONES_ROWS, v_t.shape[1]), BF16)], axis=0)

    n_trips = ((i + 1) * tq + SEL_CHUNK - 1) // SEL_CHUNK

    def scores_into(dst, chunk):
        k_rows = ks_ref[0, pl.ds(pl.multiple_of(chunk * SEL_CHUNK, SEL_CHUNK), SEL_CHUNK), :]
        for g in range(N_KV):
            dst[g] = _dot(k_rows, qt_scr[g])

    def trip(kc_i, src, dst):
        off = pl.multiple_of(kc_i * SEL_CHUNK, SEL_CHUNK)
        causal = off + _iota((SEL_CHUNK, tq), 0) <= qpos1
        k_next = ks_ref[0, pl.ds(pl.multiple_of(jnp.minimum(kc_i + 1, n_trips - 1) * SEL_CHUNK, SEL_CHUNK),
                                 SEL_CHUNK), :]
        for g in range(N_KV):
            dst[g] = _dot(k_next, qt_scr[g])
            picked = jnp.concatenate(
                [jnp.broadcast_to(sel_scr[g, pl.ds(kc_i * per_chunk + t, 1), :], (L_SLC, tq))
                 for t in range(per_chunk)], axis=0)
            bias = jnp.where((picked > 0.5) & causal, 0.0, NEG)
            s = src[g] + jnp.concatenate([bias] * GQA, axis=1)
            m_old = m_scr[g]
            m_new = jnp.maximum(m_old, jnp.max(s, axis=0, keepdims=True))
            p = jnp.exp2(s - m_new).astype(BF16)
            m_scr[g] = m_new
            acc_scr[g] = jnp.exp2(m_old - m_new) * acc_scr[g] + _dot(
                with_ones(vst_ref[0, groups[g], pl.ds(off, SEL_CHUNK)]), p)

    scores_into(s_even, 0)

    def body(kc_i, carry):
        @pl.when(lax.rem(kc_i, 2) == 0)
        def _():
            trip(kc_i, s_even, s_odd)

        @pl.when(lax.rem(kc_i, 2) == 1)
        def _():
            trip(kc_i, s_odd, s_even)
        return carry

    lax.fori_loop(0, n_trips, body, 0)

    kwpos = ws + _iota((band, tq), 0)
    bias_w = jnp.where((kwpos <= qpos1) & (kwpos > qpos1 - WINDOW), 0.0, NEG)
    bias_w = jnp.concatenate([bias_w] * GQA, axis=1)
    k_band = kw_ref[0, pl.ds(ws, band), :]
    scores_w = [_dot(k_band, qt_scr[g]) for g in range(N_KV)]
    pair = []
    for g in range(N_KV):
        s_w = scores_w[g] + bias_w
        e_w = jnp.exp2(s_w - jnp.max(s_w, axis=0, keepdims=True)).astype(BF16)
        acc_w = _dot(with_ones(vwt_ref[0, groups[g], pl.ds(ws, band)]), e_w)
        o_w = acc_w[0:HEAD_DIM] * (1.0 / acc_w[HEAD_DIM:HEAD_DIM + 1])
        acc_s = acc_scr[g]
        o_s = acc_s[0:HEAD_DIM] * (1.0 / acc_s[HEAD_DIM:HEAD_DIM + 1])
        for r in range(GQA):
            h = g * GQA + r
            cs = slice(r * tq, (r + 1) * tq)
            pair.append(gates_t[3 * h:3 * h + 1] * o_c[g][:, cs] + gates_t[3 * h + 1:3 * h + 2] * o_s[:, cs]
                        + gates_t[3 * h + 2:3 * h + 3] * o_w[:, cs])
    per_lane = LANES // HEAD_DIM
    o_a = jnp.concatenate([jnp.concatenate(pair[j:j + per_lane], axis=0).T
                           for j in range(0, N_HEADS, per_lane)], axis=1)
    y_ref[0] = _merge_and_project(o_a, sza_ref[0], sga_ref[0], mb_ref[0], x_ref[0], gate_ref[0],
                                  wbra_ref, wout_ref)


def _p_attn_call(q_t, ks_rows, vs_t, kw_rows, vw_t, kc, vc, gates_t, sza, sga, mb, x, gate, w_br_a, w_out):
    b, s, _ = x.shape
    tq = Q_BLOCK
    row = lambda w: pl.BlockSpec((1, tq, w), lambda bi, i: (bi, i, 0))
    col = lambda a: pl.BlockSpec((1, a.shape[1], tq), lambda bi, i: (bi, 0, i))
    seq = lambda a: pl.BlockSpec((1,) + a.shape[1:], lambda bi, i: (bi, 0, 0))
    in_specs = [col(q_t), seq(ks_rows), seq(vs_t), seq(kw_rows), seq(vw_t), seq(kc), seq(vc),
                col(gates_t), row(WIDTH_A), row(D_MODEL), row(D_MODEL), row(D_MODEL), seq(gate),
                _full(w_br_a.shape), _full(w_out.shape)]
    return pl.pallas_call(
        _p_attn_kernel,
        grid=(b, s // tq),
        in_specs=in_specs,
        out_specs=row(D_MODEL),
        out_shape=jax.ShapeDtypeStruct((b, s, D_MODEL), F32),
        scratch_shapes=[pltpu.VMEM((N_KV, KV_WIDTH, GQA * tq), BF16),
                        pltpu.VMEM((N_KV, s // L_SLC, tq), F32),
                        pltpu.VMEM((N_KV, 1, GQA * tq), F32),
                        pltpu.VMEM((N_KV, HEAD_DIM + ONES_ROWS, GQA * tq), F32),
                        pltpu.VMEM((N_KV, SEL_CHUNK, GQA * tq), F32),
                        pltpu.VMEM((N_KV, SEL_CHUNK, GQA * tq), F32)],
        compiler_params=pltpu.CompilerParams(dimension_semantics=("arbitrary", "arbitrary"),
                                             vmem_limit_bytes=VMEM_LIMIT),
        name="p_attn",
    )(q_t, ks_rows, vs_t, kw_rows, vw_t, kc, vc, gates_t, sza, sga, mb, x, gate, w_br_a, w_out)


def _s_proj_kernel(x_ref, shift_ref, scale_ref, ng_ref, w_ref, cos_ref, sin_ref, qg_ref, kg_ref,
                   vng_ref, vnb_ref, ws0_ref, bs0_ref, wbrb_ref,
                   q_out, k_out, v_out, kt_out, vt_out, gates_out, sza_out, sga_out, mb_out, vn_out):
    seg = _project(x_ref[...], shift_ref[...], scale_ref[...], ng_ref[...], w_ref)
    cos, sin = cos_ref[...], sin_ref[...]
    q_out[...] = _norm_rope(seg(C_Q, C_K), qg_ref[...], cos, sin)
    k = _norm_rope(seg(C_K, C_V), kg_ref[...], cos, sin)
    v = seg(C_V, C_G)
    k_out[...] = k
    v_out[...] = v
    for br in range(3):
        sl = slice(br * KV_WIDTH, (br + 1) * KV_WIDTH)
        kt_out[br] = k[:, sl].T
        vt_out[br] = v[:, sl].T
    gates_out[...] = jax.nn.sigmoid(seg(C_G, C_ZA))
    sza_out[...] = _silu(seg(C_ZA, C_U))
    sga_out[...] = jax.nn.sigmoid(seg(C_GA, C_GB))
    vn = _layer_norm(seg(C_VB, C_ZB), vng_ref[...], vnb_ref[...])
    vn_out[...] = vn
    s_b = ws0_ref[...] * vn + bs0_ref[...]
    t = seg(C_U, C_VB) * s_b * _silu(seg(C_ZB, C_GA))
    mb_out[...] = jax.nn.sigmoid(seg(C_GB, C_END)) * _dot(t.astype(BF16), wbrb_ref[...])


def _s_proj_call(x, shift, scale, norm_g, w_all, cos1, sin1, qg, kg, vng, vnb, ws0, bs0, w_br_b):
    n = x.shape[0]
    args = (x, shift, scale, norm_g, w_all, cos1, sin1, qg, kg, vng, vnb, ws0, bs0, w_br_b)
    sds = lambda *shape: jax.ShapeDtypeStruct(shape, F32)
    out_shape = [sds(n, WIDTH_A), sds(n, 3 * KV_WIDTH), sds(n, 3 * KV_WIDTH), sds(3, KV_WIDTH, n),
                 sds(3, KV_WIDTH, n), sds(n, LANES), sds(n, WIDTH_A), sds(n, D_MODEL), sds(n, D_MODEL),
                 sds(n, WIDTH_B)]
    return pl.pallas_call(
        _s_proj_kernel,
        grid=(1,),
        in_specs=[_full(a.shape) for a in args],
        out_specs=[_full(o.shape) for o in out_shape],
        out_shape=out_shape,
        compiler_params=pltpu.CompilerParams(vmem_limit_bytes=VMEM_LIMIT),
        name="s_proj",
    )(*args)


def _s_attn_kernel(pt_ref, q_ref, gates_ref, knew_ref, vnew_ref, knewt_ref, vnewt_ref, kwin_ref, vwin_ref,
                   pek_ref, pev_ref, wck_ref, wcv_ref, pool_ref, expand_ref,
                   kc_hbm, vc_hbm, ks_hbm, vs_hbm,
                   oa_ref, okw_ref, ovw_ref, buf, sem):
    t = pl.program_id(0)
    n_groups = pl.num_programs(0) - 1
    per_step = q_ref.shape[0]
    n_seqs = pt_ref.shape[0]
    n_pages = pt_ref.shape[1]
    past = n_pages * PAGE_SIZE
    caches = (kc_hbm, vc_hbm, ks_hbm, vs_hbm)

    def page_copies(group, slot_):
        return [pltpu.make_async_copy(hbm.at[pt_ref[jnp.minimum(group * per_step + j, n_seqs - 1), p]],
                                      buf.at[slot_, j, c, :, pl.ds(p * PAGE_SIZE, PAGE_SIZE)],
                                      sem.at[slot_, c])
                for j in range(per_step) for c, hbm in enumerate(caches) for p in range(n_pages)]

    @pl.when(t < n_groups)
    def _():
        for cp in page_copies(t, lax.rem(t, 2)):
            cp.start()

    @pl.when(t > 0)
    def _():
        group = t - 1
        slot = lax.rem(group, 2)
        for cp in page_copies(group, slot):
            cp.wait()
        chains = [_s_attn_one(group * per_step + j, j, buf.at[slot, j], q_ref, gates_ref, knew_ref, vnew_ref,
                              knewt_ref, vnewt_ref, kwin_ref, vwin_ref, pek_ref, pev_ref, wck_ref, wcv_ref,
                              pool_ref, expand_ref, oa_ref, okw_ref, ovw_ref, past) for j in range(per_step)]
        for _ in itertools.zip_longest(*chains):
            pass


def _s_attn_one(b, j, buf, q_ref, gates_ref, knew_ref, vnew_ref, knewt_ref, vnewt_ref, kwin_ref, vwin_ref,
                pek_ref, pev_ref, wck_ref, wcv_ref, pool_ref, expand_ref, oa_ref, okw_ref, ovw_ref, past):
    qpos = past

    lane = _iota((KV_WIDTH, knewt_ref.shape[2]), 1)
    col = lambda ref, br: jnp.sum(jnp.where(lane == b, ref[br], 0.0), axis=1, keepdims=True)
    knew = knew_ref[pl.ds(b, 1), :]
    vnew = vnew_ref[pl.ds(b, 1), :]

    hrow = _iota((N_HEADS, 1), 0)
    first_group = hrow < GQA
    by_group = lambda f: jnp.where(first_group, f(0), f(1))
    gl = lambda g: slice(g * HEAD_DIM, (g + 1) * HEAD_DIM)
    qb = q_ref[j].astype(BF16)

    n_cmp = -(-(past + 1) // L_SLC) * L_SLC // L_CMP
    n_slc = n_cmp * L_CMP // L_SLC

    def summaries(c, new_row, pe_ref, w_ref):
        rows_t = buf[c, :, pl.ds(0, past)]
        hi, lo = _split_bf16(rows_t)
        pe_sum = jnp.sum(pe_ref[...], axis=0, keepdims=True)
        pooled = (_dot_nt(pool_ref[...], hi) + _dot_nt(pool_ref[...], lo) + pe_sum) * (1.0 / L_CMP)
        r = _iota((SUBLANES, KV_WIDTH), 0)
        tail = jnp.where(r == 0, new_row + pe_sum, jnp.where(r == 1, pe_sum, 0.0)) * (1.0 / L_CMP)
        zeros = jnp.zeros((LANES - pooled.shape[0] - SUBLANES, KV_WIDTH), F32)
        return _dot(jnp.concatenate([pooled, tail, zeros], axis=0).astype(BF16), w_ref[...])

    kc = summaries(0, knew[:, 0:KV_WIDTH], pek_ref, wck_ref).astype(BF16)
    yield
    vc = summaries(1, vnew[:, 0:KV_WIDTH], pev_ref, wcv_ref).astype(BF16)
    yield
    s_c = by_group(lambda g: _dot_nt(qb, kc[:, gl(g)])) * SM_SCALE
    c = _iota(s_c.shape, 1)
    mc = ((c + 1) * L_CMP - 1 <= qpos) & (c < n_cmp)
    e_c, r_c = _softmax_rows(s_c, mc)
    p_c = jnp.where(mc, e_c * r_c, 0.0)
    o_c = by_group(lambda g: _dot(p_c.astype(BF16), vc[:, gl(g)]))
    imp = by_group(lambda g: jnp.sum(p_c[g * GQA:(g + 1) * GQA], axis=0, keepdims=True))
    imp = jnp.broadcast_to(imp, p_c.shape)
    yield
    sel = _select_blocks(imp, jnp.full((N_HEADS, 1), qpos // L_SLC, jnp.int32), n_slc)
    yield

    picked = _dot(sel.astype(BF16), expand_ref[...])
    s_s = by_group(lambda g: _dot(qb, buf[2, gl(g), :].astype(BF16))) * SM_SCALE
    kpos = _iota(s_s.shape, 1)
    s_s = jnp.where((picked > 0.5) & (kpos <= qpos), s_s, NEG)
    yield
    own =_iota((N_HEADS, KV_WIDTH), 1) // HEAD_DIM == hrow // GQA
    rounded = lambda a: a.astype(BF16).astype(F32)
    q_pair = jnp.concatenate([qb.astype(F32)] * N_KV, axis=1)
    s_new = jnp.sum(jnp.where(own, q_pair * rounded(knew[:, KV_WIDTH:2 * KV_WIDTH]), 0.0),
                    axis=1, keepdims=True) * SM_SCALE
    new_lane = (L_SLC // L_CMP) * (past // L_SLC)
    s_new = jnp.where((sel[:, new_lane:new_lane + 1] > 0.5) & (past <= qpos), s_new, NEG)
    m_s = jnp.maximum(jnp.max(s_s, axis=-1, keepdims=True), s_new)
    e_s, e_new = jnp.exp(s_s - m_s), jnp.exp(s_new - m_s)
    r_s = 1.0 / (jnp.sum(e_s, axis=-1, keepdims=True) + e_new)
    v_new = by_group(lambda g: rounded(vnew[:, KV_WIDTH + g * HEAD_DIM:KV_WIDTH + (g + 1) * HEAD_DIM]))
    o_s = (by_group(lambda g: _dot_nt(e_s.astype(BF16), buf[3, gl(g), :].astype(BF16)))
           + rounded(e_new) * v_new) * r_s
    yield

    wb = kwin_ref.shape[2]
    wlane = _iota((KV_WIDTH, wb), 1)
    kw = jnp.where(wlane == wb - 1, col(knewt_ref, 2), pltpu.roll(kwin_ref[j], wb - 1, 1))
    vw = jnp.where(wlane == wb - 1, col(vnewt_ref, 2), pltpu.roll(vwin_ref[j], wb - 1, 1))
    okw_ref[j] = kw
    ovw_ref[j] = vw
    yield
    s_w = by_group(lambda g: _dot(qb, kw[gl(g)].astype(BF16))) * SM_SCALE
    kwpos = past - wb + 1 + _iota(s_w.shape, 1)
    e_w, r_w = _softmax_rows(s_w, (kwpos <= qpos) & (kwpos > qpos - WINDOW) & (kwpos >= 0))
    o_w = by_group(lambda g: _dot_nt(e_w.astype(BF16), vw[gl(g)].astype(BF16))) * r_w

    gates = gates_ref[j]
    oa_ref[j] = gates[:, 0:1] * o_c + gates[:, 1:2] * o_s + gates[:, 2:3] * o_w


def _s_attn_call(page_table, q3, gates3, knew, vnew, knew_t, vnew_t, kwin_t, vwin_t, pek, pev, wck, wcv,
                 kc_pool, vc_pool, ks_pool, vs_pool):
    n, n_pages = page_table.shape
    past = n_pages * PAGE_SIZE
    wb = kwin_t.shape[2]
    tok = np.arange(past)
    pool = jnp.asarray((tok[None, :] // L_CMP == np.arange(past // L_CMP)[:, None]), BF16)
    expand = jnp.asarray(np.arange(LANES)[:, None] == 2 * (np.arange(past)[None, :] // L_SLC), BF16)
    k = SAMPLE_SEQS_PER_STEP
    assert n % k == 0
    per_seq = lambda a: pl.BlockSpec((k,) + a.shape[1:],
                                     lambda t, pt: (jnp.maximum(t - 1, 0),) + (0,) * (a.ndim - 1))
    full = lambda a: pl.BlockSpec(a.shape, lambda t, pt: (0,) * a.ndim)
    hbm = pl.BlockSpec(memory_space=pl.ANY)
    resident = (knew, vnew, knew_t, vnew_t)
    consts = (pek, pev, wck, wcv, pool, expand)
    grid_spec = pltpu.PrefetchScalarGridSpec(
        num_scalar_prefetch=1,
        grid=(n // k + 1,),
        in_specs=[per_seq(q3), per_seq(gates3)] + [full(a) for a in resident]
                 + [per_seq(kwin_t), per_seq(vwin_t)] + [full(a) for a in consts] + [hbm] * 4,
        out_specs=[per_seq(q3), per_seq(kwin_t), per_seq(vwin_t)],
        scratch_shapes=[pltpu.VMEM((2, k, 4, KV_WIDTH, past), F32), pltpu.SemaphoreType.DMA((2, 4))],
    )
    return pl.pallas_call(
        _s_attn_kernel,
        grid_spec=grid_spec,
        out_shape=[jax.ShapeDtypeStruct(q3.shape, F32), jax.ShapeDtypeStruct(kwin_t.shape, F32),
                   jax.ShapeDtypeStruct(vwin_t.shape, F32)],
        compiler_params=pltpu.CompilerParams(dimension_semantics=("arbitrary",),
                                             vmem_limit_bytes=VMEM_LIMIT),
        name="s_attn",
    )(page_table, q3, gates3, knew, vnew, knew_t, vnew_t, kwin_t, vwin_t, pek, pev, wck, wcv, pool, expand,
      kc_pool, vc_pool, ks_pool, vs_pool)


def _s_out_kernel(oa_ref, sza_ref, sga_ref, mb_ref, x_ref, gate_ref, wbra_ref, wout_ref, y_ref):
    y_ref[...] = _merge_and_project(oa_ref[...], sza_ref[...], sga_ref[...], mb_ref[...], x_ref[...],
                                    gate_ref[...], wbra_ref, wout_ref)


def _s_out_call(o_a, sza, sga, mb, x, gate, w_br_a, w_out):
    args = (o_a, sza, sga, mb, x, gate, w_br_a, w_out)
    return pl.pallas_call(
        _s_out_kernel,
        grid=(1,),
        in_specs=[_full(a.shape) for a in args],
        out_specs=_full(x.shape),
        out_shape=jax.ShapeDtypeStruct(x.shape, F32),
        compiler_params=pltpu.CompilerParams(vmem_limit_bytes=VMEM_LIMIT),
        name="s_out",
    )(*args)


def _rope_angles(pos):
    half = HEAD_DIM // 2
    inv = ROPE_THETA ** (-jnp.arange(half, dtype=F32) * 2.0 / HEAD_DIM)
    return pos.astype(F32)[:, None] * inv[None, :]


def _rope_tables(pos):
    ang = _rope_angles(pos)
    cos, sin = jnp.cos(ang), jnp.sin(ang)
    cos_t = jnp.concatenate([cos, cos] * (LANES // HEAD_DIM), axis=1)
    sin_t = jnp.concatenate([-sin, sin] * (LANES // HEAD_DIM), axis=1)
    return cos_t, sin_t


def _token_minor(a):
    b, t = a.shape[:2]
    return jnp.transpose(a, (0, 2, 3, 1)).reshape(b, KV_WIDTH, t)


def _token_major(a_t):
    b, _, t = a_t.shape
    return jnp.transpose(a_t.reshape(b, N_KV, HEAD_DIM, t), (0, 3, 1, 2))


def kernel(x_prompt, x_sample, cache_k_cmp, cache_v_cmp, cache_k_slc, cache_v_slc, cache_k_win, cache_v_win, page_table, c_prompt, c_sample, w_ada, b_ada, norm_g, w_in, q_norm_g, k_norm_g, cmp_pos_k, cmp_pos_v, w_cmp_k, w_cmp_v, vnorm_g, vnorm_b, w_s, b_s, w_br_a, w_br_b, w_out):
    assert w_ada.shape[0] == 1, "single layer"
    b, s, _ = x_prompt.shape
    n = x_sample.shape[0]
    assert x_sample.shape[1] == 1
    n_pages = page_table.shape[1]
    past = n_pages * PAGE_SIZE

    w = w_in[0]
    n_gate = 3 * N_HEADS
    w_all = jnp.concatenate([w[:, :C_G], w[:, C_G:C_G + n_gate], jnp.zeros((D_MODEL, LANES - n_gate), F32),
                             w[:, C_G + n_gate:]], axis=1).astype(BF16)
    assert w_all.shape[1] == C_END
    eye = jnp.eye(N_KV, dtype=F32)
    wck = jnp.kron(eye, w_cmp_k[0]).astype(BF16)
    wcv = jnp.kron(eye, w_cmp_v[0]).astype(BF16)
    pek = jnp.tile(cmp_pos_k[0], (1, N_KV))
    pev = jnp.tile(cmp_pos_v[0], (1, N_KV))
    qg = jnp.tile(q_norm_g, (1, LANES // HEAD_DIM))
    kg = jnp.tile(k_norm_g, (1, LANES // HEAD_DIM))
    w_br_a_b, w_br_b_b, w_out_b = w_br_a[0].astype(BF16), w_br_b[0].astype(BF16), w_out[0].astype(BF16)

    mod = _ada_call(jnp.concatenate([c_prompt, c_sample], axis=0), w_ada[0], b_ada)
    shift, scale, gate = mod[:, :D_MODEL], mod[:, D_MODEL:2 * D_MODEL], mod[:, 2 * D_MODEL:]

    ang_p = _rope_angles(jnp.arange(s, dtype=jnp.int32)).T
    gain_cols = lambda g: jnp.broadcast_to(g[0][:, None], (HEAD_DIM, PROMPT_ROWS))
    (q_t, kc_t, ks_t, kw_t, vc_t, vs_t, vw_t, ks_rows, kw_rows, kcmp, vcmp, gates_t, sza, sga, mb) = _p_proj_call(
        x_prompt, shift[:b, None], scale[:b, None], norm_g, w_all, jnp.cos(ang_p), jnp.sin(ang_p),
        gain_cols(q_norm_g), gain_cols(k_norm_g), pek, pev, wck, wcv,
        vnorm_g, vnorm_b, w_s[0], b_s[0].T, w_br_b_b)
    y_prompt = _p_attn_call(q_t, ks_rows, vs_t, kw_rows, vw_t, kcmp, vcmp, gates_t, sza, sga, mb, x_prompt,
                            gate[:b, None], w_br_a_b, w_out_b)
    wb_p = min(WINDOW, s)
    p_states = [_token_major(a)[None] for a in (kc_t, vc_t, ks_t, vs_t, kw_t[:, :, s - wb_p:], vw_t[:, :, s - wb_p:])]

    xs = x_sample.reshape(n, D_MODEL)
    cos_s, sin_s = _rope_tables(jnp.full((1,), past, jnp.int32))
    ws0 = jnp.repeat(w_s[0, :, 0, 0], GROUP_W_B)[None]
    bs0 = jnp.repeat(b_s[0, :, 0], GROUP_W_B)[None]
    (q_s, k_s, v_s, kt_s, vt_s, gates_s, sza_s, sga_s, mb_s, vn_s) = _s_proj_call(
        xs, shift[b:], scale[b:], norm_g, w_all, cos_s, sin_s, qg, kg, vnorm_g, vnorm_b, ws0, bs0, w_br_b_b)
    pools = [_token_minor(c[0]) for c in (cache_k_cmp, cache_v_cmp, cache_k_slc, cache_v_slc)]
    o_a, kwin_new, vwin_new = _s_attn_call(
        page_table, q_s.reshape(n, N_HEADS, HEAD_DIM), gates_s[:, :3 * N_HEADS].reshape(n, N_HEADS, 3),
        k_s, v_s, kt_s, vt_s, _token_minor(cache_k_win[0]), _token_minor(cache_v_win[0]),
        pek, pev, wck, wcv, *pools)
    y_sample = _s_out_call(o_a.reshape(n, WIDTH_A), sza_s, sga_s, mb_s, xs, gate[b:], w_br_a_b, w_out_b)

    new_rows = lambda t, br: jnp.transpose(t[br].reshape(N_KV, HEAD_DIM, n), (2, 0, 1))[None, :, None]
    s_states = [new_rows(kt_s, 0), new_rows(vt_s, 0), new_rows(kt_s, 1), new_rows(vt_s, 1),
                _token_major(kwin_new)[None], _token_major(vwin_new)[None], vn_s[None, :, None]]
    return (y_prompt, y_sample.reshape(n, 1, D_MODEL), *p_states, *s_states)
```

```python
import itertools

import numpy as np
import jax
import jax.numpy as jnp
from jax import lax
from jax.experimental import pallas as pl
from jax.experimental.pallas import tpu as pltpu

F32 = jnp.float32
BF16 = jnp.bfloat16

D_MODEL = 1024
HEAD_DIM = 64
N_HEADS = 8
N_KV = 2
GQA = N_HEADS // N_KV
WIDTH_A = N_HEADS * HEAD_DIM
KV_WIDTH = N_KV * HEAD_DIM
L_CMP = 32
L_SLC = 64
N_SEL = 8
WINDOW = 512
Q_BLOCK = 256
FORCE_BONUS = 1.0e4
ROPE_THETA = 10000.0
CHUNK = 128
N_GROUPS_B = 4
WIDTH_B = 512
GROUP_W_B = WIDTH_B // N_GROUPS_B
PAGE_SIZE = 128
EPS = 1e-6
NEG = -1e30
SM_SCALE = HEAD_DIM ** -0.5
LOG2_E = 1.4426950408889634

LANES = 128
SUBLANES = 8
VMEM_LIMIT = 56 * 1024 * 1024

C_Q = 0
C_K = C_Q + WIDTH_A
C_V = C_K + 3 * KV_WIDTH
C_G = C_V + 3 * KV_WIDTH
C_ZA = C_G + LANES
C_U = C_ZA + WIDTH_A
C_VB = C_U + WIDTH_B
C_ZB = C_VB + WIDTH_B
C_GA = C_ZB + WIDTH_B
C_GB = C_GA + D_MODEL
C_END = C_GB + D_MODEL

PROMPT_ROWS = 512
PROMPT_SUB_ROWS = 256
SEL_CHUNK = 256
ONES_ROWS = 16
SAMPLE_SEQS_PER_STEP = 2


def _dot(a, b):
    return jnp.dot(a, b, preferred_element_type=F32)


def _dot_nt(a, b):
    return lax.dot_general(a, b, (((1,), (1,)), ((), ())), preferred_element_type=F32)


def _iota(shape, dim):
    return lax.broadcasted_iota(jnp.int32, shape, dim)


def _split_bf16(x):
    hi = x.astype(BF16)
    lo = (x - hi.astype(F32)).astype(BF16)
    return hi, lo


def _head_mean_sq(x):
    w = x.shape[1]
    ones_bd = jnp.where(_iota((LANES, LANES), 0) // HEAD_DIM == _iota((LANES, LANES), 1) // HEAD_DIM,
                        1.0, 0.0).astype(BF16)
    hi, lo = _split_bf16(x * x)
    cols = []
    for c in range(w // LANES):
        sl = slice(c * LANES, (c + 1) * LANES)
        cols.append(_dot(hi[:, sl], ones_bd) + _dot(lo[:, sl], ones_bd))
    return jnp.concatenate(cols, axis=1) * (1.0 / HEAD_DIM)


def _tile_lanes(t, width):
    return jnp.concatenate([t] * (width // t.shape[1]), axis=1)


def _norm_rope(x, g, cos, sin):
    w = x.shape[1]
    y = x * lax.rsqrt(_head_mean_sq(x) + EPS) * _tile_lanes(g, w)
    first_half = (_iota(y.shape, 1) % HEAD_DIM) < (HEAD_DIM // 2)
    rot = jnp.where(first_half, pltpu.roll(y, w - HEAD_DIM // 2, 1), pltpu.roll(y, HEAD_DIM // 2, 1))
    return y * _tile_lanes(cos, w) + rot * _tile_lanes(sin, w)


def _norm_rope_t(x_t, gain_t, cos_t, sin_t):
    half = HEAD_DIM // 2
    out = []
    for h in range(x_t.shape[0] // HEAD_DIM):
        x = x_t[h * HEAD_DIM:(h + 1) * HEAD_DIM]
        y = x * lax.rsqrt(jnp.mean(x * x, axis=0, keepdims=True) + EPS) * gain_t
        y1, y2 = y[:half], y[half:]
        out += [y1 * cos_t - y2 * sin_t, y2 * cos_t + y1 * sin_t]
    return jnp.concatenate(out, axis=0)


def _silu(z):
    return z * jax.nn.sigmoid(z)


def _project(x, shift, scale, norm_g, w_ref):
    ms = jnp.mean(x * x, axis=-1, keepdims=True)
    h = (x * lax.rsqrt(ms + EPS) * norm_g) * (1.0 + scale) + shift
    hb = h.astype(BF16)
    return lambda lo, hi: _dot(hb, w_ref[:, lo:hi])


def _layer_norm(v, g, b):
    mu = jnp.mean(v, axis=-1, keepdims=True)
    d = v - mu
    var = jnp.mean(d * d, axis=-1, keepdims=True)
    return d * lax.rsqrt(var + EPS) * g + b


def _ada_kernel(c_ref, w_ref, b_ref, o_ref):
    o_ref[...] = _dot(c_ref[...].astype(BF16), w_ref[...].astype(BF16)) + b_ref[...]


def _ada_call(c_all, w_ada, b_ada):
    rows = c_all.shape[0]
    n = w_ada.shape[1]
    return pl.pallas_call(
        _ada_kernel,
        grid=(n // D_MODEL,),
        in_specs=[pl.BlockSpec((rows, D_MODEL), lambda j: (0, 0)),
                  pl.BlockSpec((D_MODEL, D_MODEL), lambda j: (0, j)),
                  pl.BlockSpec((1, D_MODEL), lambda j: (0, j))],
        out_specs=pl.BlockSpec((rows, D_MODEL), lambda j: (0, j)),
        out_shape=jax.ShapeDtypeStruct((rows, n), F32),
        compiler_params=pltpu.CompilerParams(vmem_limit_bytes=VMEM_LIMIT),
        name="ada",
    )(c_all, w_ada, b_ada)


def _compress_rows(rows, pe, w_bd):
    t = rows.shape[0]
    pooled = jnp.sum(rows.reshape(t // L_CMP, L_CMP, KV_WIDTH) + pe[None], axis=1) * (1.0 / L_CMP)
    return _dot(pooled.astype(BF16), w_bd)


def _p_proj_kernel(x_ref, shift_ref, scale_ref, ng_ref, w_ref, cos_ref, sin_ref, qg_ref, kg_ref,
                   pek_ref, pev_ref, wck_ref, wcv_ref, vng_ref, vnb_ref, ws_ref, bst_ref, wbrb_ref,
                   *outs):
    tm = x_ref.shape[1]
    chains = [_p_proj_rows(r0, x_ref, shift_ref, scale_ref, ng_ref, w_ref, cos_ref, sin_ref, qg_ref, kg_ref,
                           pek_ref, pev_ref, wck_ref, wcv_ref, vng_ref, vnb_ref, ws_ref, bst_ref, wbrb_ref, *outs)
              for r0 in range(0, tm, PROMPT_SUB_ROWS)]
    for _ in itertools.zip_longest(*chains):
        pass


def _p_proj_rows(r0, x_ref, shift_ref, scale_ref, ng_ref, w_ref, cos_ref, sin_ref, qg_ref, kg_ref,
                 pek_ref, pev_ref, wck_ref, wcv_ref, vng_ref, vnb_ref, ws_ref, bst_ref, wbrb_ref,
                 qt_out, kc_t, ks_t, kw_t, vc_t, vs_t, vw_t, ks_rows, kw_rows, kcmp_out, vcmp_out,
                 gates_t, sza_out, sga_out, mb_out):
    sub = PROMPT_SUB_ROWS
    rs = slice(r0, r0 + sub)
    cmp_rows = slice(r0 // L_CMP, (r0 + sub) // L_CMP)
    seg = _project(x_ref[0, rs, :], shift_ref[0], scale_ref[0], ng_ref[...], w_ref)
    cos, sin = cos_ref[:, rs], sin_ref[:, rs]
    to_token_minor = lambda a: jnp.concatenate(
        [a[:, c * LANES:(c + 1) * LANES].T for c in range(a.shape[1] // LANES)], axis=0)
    yield

    qt_out[0, :, rs] = (_norm_rope_t(to_token_minor(seg(C_Q, C_K)), qg_ref[...], cos, sin)
                        * (SM_SCALE * LOG2_E)).astype(BF16)
    yield
    k_t = _norm_rope_t(to_token_minor(seg(C_K, C_V)), kg_ref[...], cos, sin)
    yield
    v = seg(C_V, C_G)
    for br, (k_out, v_out) in enumerate(((kc_t, vc_t), (ks_t, vs_t), (kw_t, vw_t))):
        sl = slice(br * KV_WIDTH, (br + 1) * KV_WIDTH)
        k_out[0, :, rs] = k_t[sl]
        v_out[0, :, rs] = v[:, sl].T
    ks_rows[0, rs, :] = k_t[KV_WIDTH:2 * KV_WIDTH].T.astype(BF16)
    kw_rows[0, rs, :] = k_t[2 * KV_WIDTH:3 * KV_WIDTH].T.astype(BF16)
    yield
    kcmp_out[0, cmp_rows, :] = _compress_rows(k_t[0:KV_WIDTH].T, pek_ref[...], wck_ref[...])
    vcmp_out[0, cmp_rows, :] = _compress_rows(v[:, 0:KV_WIDTH], pev_ref[...], wcv_ref[...])
    yield

    gates_t[0, :, rs] = jax.nn.sigmoid(seg(C_G, C_ZA)).T
    sza_out[0, rs, :] = _silu(seg(C_ZA, C_U))
    yield
    sga_out[0, rs, :] = jax.nn.sigmoid(seg(C_GA, C_GB))
    yield

    vn = _layer_norm(seg(C_VB, C_ZB), vng_ref[...], vnb_ref[...]).astype(BF16)
    yield
    causal = _iota((CHUNK, CHUNK), 0) >= _iota((CHUNK, CHUNK), 1)
    chunks = []
    for c in range(sub // CHUNK):
        groups = []
        for g in range(N_GROUPS_B):
            wsg = jnp.where(causal, ws_ref[g], 0.0).astype(BF16)
            vg = vn[c * CHUNK:(c + 1) * CHUNK, g * GROUP_W_B:(g + 1) * GROUP_W_B]
            groups.append(_dot(wsg, vg) + bst_ref[:, g:g + 1])
        chunks.append(jnp.concatenate(groups, axis=1))
    s_b = jnp.concatenate(chunks, axis=0)
    yield
    t = seg(C_U, C_VB) * s_b * _silu(seg(C_ZB, C_GA))
    yield
    mb_out[0, rs, :] = jax.nn.sigmoid(seg(C_GB, C_END)) * _dot(t.astype(BF16), wbrb_ref[...])


def _full(shape):
    nd = len(shape)
    return pl.BlockSpec(shape, lambda *_: (0,) * nd)


def _p_proj_call(x, shift, scale, norm_g, w_all, cos_t, sin_t, qg, kg, pek, pev, wck, wcv, vng, vnb,
                 w_s, bs_t, w_br_b):
    b, s, _ = x.shape
    tm = PROMPT_ROWS
    row = lambda w: pl.BlockSpec((1, tm, w), lambda bi, i: (bi, i, 0))
    tok_minor = pl.BlockSpec((1, KV_WIDTH, tm), lambda bi, i: (bi, 0, i))
    per_batch = pl.BlockSpec((1, 1, D_MODEL), lambda bi, i: (bi, 0, 0))
    cmp_spec = pl.BlockSpec((1, tm // L_CMP, KV_WIDTH), lambda bi, i: (bi, i, 0))
    tab = pl.BlockSpec((HEAD_DIM // 2, tm), lambda bi, i: (0, i))
    in_specs = [row(D_MODEL), per_batch, per_batch, _full(norm_g.shape), _full(w_all.shape), tab, tab,
                _full(qg.shape), _full(kg.shape), _full(pek.shape), _full(pev.shape), _full(wck.shape),
                _full(wcv.shape), _full(vng.shape), _full(vnb.shape), _full(w_s.shape), _full(bs_t.shape),
                _full(w_br_b.shape)]
    qt_spec = pl.BlockSpec((1, WIDTH_A, tm), lambda bi, i: (bi, 0, i))
    out_specs = [qt_spec] + [tok_minor] * 6 + [row(KV_WIDTH), row(KV_WIDTH), cmp_spec, cmp_spec,
                                                tok_minor, row(WIDTH_A), row(D_MODEL), row(D_MODEL)]
    tm_shape = jax.ShapeDtypeStruct((b, KV_WIDTH, s), F32)
    rows_shape = jax.ShapeDtypeStruct((b, s, KV_WIDTH), BF16)
    cmp_shape = jax.ShapeDtypeStruct((b, s // L_CMP, KV_WIDTH), F32)
    out_shape = [jax.ShapeDtypeStruct((b, WIDTH_A, s), BF16)] + [tm_shape] * 6 + [
        rows_shape, rows_shape, cmp_shape, cmp_shape,
        jax.ShapeDtypeStruct((b, LANES, s), F32), jax.ShapeDtypeStruct((b, s, WIDTH_A), F32),
        jax.ShapeDtypeStruct((b, s, D_MODEL), F32), jax.ShapeDtypeStruct((b, s, D_MODEL), F32)]
    return pl.pallas_call(
        _p_proj_kernel,
        grid=(b, s // tm),
        in_specs=in_specs,
        out_specs=out_specs,
        out_shape=out_shape,
        compiler_params=pltpu.CompilerParams(dimension_semantics=("arbitrary", "arbitrary"),
                                             vmem_limit_bytes=VMEM_LIMIT),
        name="p_proj",
    )(x, shift, scale, norm_g, w_all, cos_t, sin_t, qg, kg, pek, pev, wck, wcv, vng, vnb, w_s, bs_t, w_br_b)


def _select_blocks(imp_c, qblk, n_blocks):
    ratio = L_SLC // L_CMP
    assert ratio == 2
    lane = _iota(imp_c.shape, 1)
    imp = imp_c + pltpu.roll(imp_c, LANES - 1, 1)
    blk = lane // ratio
    forced = jnp.where((blk == 0) | (blk == qblk), 1.0, 0.0)
    score = jnp.where(blk <= qblk, imp + FORCE_BONUS * forced, NEG)
    rank = jnp.zeros(imp_c.shape, F32)
    for j in range(n_blocks):
        vj = score[:, ratio * j:ratio * j + 1]
        earlier = jnp.where(lane > ratio * j, 1.0, 0.0)
        rank = rank + jnp.where(vj > score, 1.0, jnp.where(vj == score, earlier, 0.0))
    cand = (lane % ratio == 0) & (lane < ratio * n_blocks)
    return jnp.where(cand & (rank < float(min(N_SEL, n_blocks))), 1.0, 0.0)


def _softmax_rows(s, valid):
    sm = jnp.where(valid, s, NEG)
    e = jnp.exp(sm - jnp.max(sm, axis=-1, keepdims=True))
    return e, 1.0 / jnp.sum(e, axis=-1, keepdims=True)


def _merge_and_project(o_a, sza, sga, mb, x, gate, wbra_ref, wout_ref):
    a = _dot((o_a * sza).astype(BF16), wbra_ref[...])
    m = sga * a + mb
    return x + gate * _dot(m.astype(BF16), wout_ref[...])


def _select_blocks_t(imp, qblk):
    n_blocks = imp.shape[0]
    blk = _iota(imp.shape, 0)
    forced = jnp.where((blk == 0) | (blk == qblk), 1.0, 0.0)
    score = jnp.where(blk <= qblk, imp + FORCE_BONUS * forced, NEG)
    rank = jnp.zeros(imp.shape, F32)
    for j in range(n_blocks):
        vj = score[j:j + 1, :]
        earlier = jnp.where(blk > j, 1.0, 0.0)
        rank = rank + jnp.where(vj > score, 1.0, jnp.where(vj == score, earlier, 0.0))
    return jnp.where(rank < float(min(N_SEL, n_blocks)), 1.0, 0.0)


def _softmax_cols(s):
    e = jnp.exp2(s - jnp.max(s, axis=0, keepdims=True))
    return e, 1.0 / jnp.sum(e, axis=0, keepdims=True)


def _p_attn_kernel(qt_ref, ks_ref, vst_ref, kw_ref, vwt_ref, kc_ref, vc_ref, gt_ref, sza_ref, sga_ref,
                   mb_ref, x_ref, gate_ref, wbra_ref, wout_ref, y_ref, qt_scr, sel_scr, m_scr, acc_scr, s_even, s_odd, sw_scr, mw_scr):
    i = pl.program_id(1)
    tq = Q_BLOCK
    n_cmp = kc_ref.shape[1]
    half = n_cmp // 2
    assert L_SLC == 2 * L_CMP and n_cmp <= LANES
    cols = GQA * tq
    qpos1 = i * tq + _iota((1, tq), 1)
    qpos = jnp.concatenate([qpos1] * GQA, axis=1)
    band = WINDOW + tq
    ws = pl.multiple_of(jnp.maximum(i * tq - WINDOW, 0), LANES)
    zeros_q = jnp.zeros((HEAD_DIM, cols), BF16)
    perm = lambda ref: jnp.concatenate([ref[0, pl.ds(0, half, stride=2), :], ref[0, pl.ds(1, half, stride=2), :],
                                        jnp.zeros((LANES - n_cmp, KV_WIDTH), F32)], axis=0)
    kc = perm(kc_ref).astype(BF16)
    vc_t = perm(vc_ref).T.astype(BF16)
    crow = _iota((LANES, cols), 0)
    cblk = 2 * (crow % half) + crow // half
    mc = ((cblk + 1) * L_CMP - 1 <= qpos) & (crow < n_cmp)
    gates_t = gt_ref[0]
    groups = [slice(g * HEAD_DIM, (g + 1) * HEAD_DIM) for g in range(N_KV)]

    for g in range(N_KV):
        qt_g = jnp.concatenate([qt_ref[0, h * HEAD_DIM:(h + 1) * HEAD_DIM, :]
                                for h in range(g * GQA, (g + 1) * GQA)], axis=1)
        qt_scr[g] = jnp.concatenate([qt_g, zeros_q] if g == 0 else [zeros_q, qt_g], axis=0)

    n_trips = ((i + 1) * tq + SEL_CHUNK - 1) // SEL_CHUNK
    assert band % SEL_CHUNK == 0
    w_chunks = [pl.ds(pl.multiple_of(ws + c * SEL_CHUNK, LANES), SEL_CHUNK) for c in range(band // SEL_CHUNK)]

    s_cmp = [_dot(kc, qt_scr[g]) for g in range(N_KV)]
    k_rows0 = ks_ref[0, pl.ds(0, SEL_CHUNK), :]
    for g in range(N_KV):
        s_even[g] = _dot(k_rows0, qt_scr[g])
    for c, kd in enumerate(w_chunks):
        k_rows = kw_ref[0, kd, :]
        for g in range(N_KV):
            sw_scr[g, c] = _dot(k_rows, qt_scr[g])

    o_c = []
    for g in range(N_KV):
        s_c = jnp.where(mc, s_cmp[g], NEG)
        e_c, r_c = _softmax_cols(s_c)
        p_c = jnp.where(mc, e_c * r_c, 0.0)
        o_c.append(_dot(vc_t[groups[g]], p_c.astype(BF16)))
        imp = p_c[:, 0:tq]
        for r in range(1, GQA):
            imp = imp + p_c[:, r * tq:(r + 1) * tq]
        sel_scr[g] = _select_blocks_t(imp[0:half] + imp[half:2 * half], qpos1 // L_SLC)

    for c in range(len(w_chunks)):
        kwpos = ws + c * SEL_CHUNK + _iota((SEL_CHUNK, tq), 0)
        bias_w = jnp.where((kwpos <= qpos1) & (kwpos > qpos1 - WINDOW), 0.0, NEG)
        bias_w = jnp.concatenate([bias_w] * GQA, axis=1)
        for g in range(N_KV):
            s_w = sw_scr[g, c] + bias_w
            sw_scr[g, c] = s_w
            m_c = jnp.max(s_w, axis=0, keepdims=True)
            mw_scr[g] = m_c if c == 0 else jnp.maximum(mw_scr[g], m_c)

    m_scr[...] = jnp.full(m_scr.shape, NEG, F32)
    acc_scr[...] = jnp.zeros(acc_scr.shape, F32)
    per_chunk = SEL_CHUNK // L_SLC

    def with_ones(v_t):
        return jnp.concatenate([v_t.astype(BF16), jnp.ones((ONES_ROWS, v_t.shape[1]), BF16)], axis=0)

    def trip(kc_i, src, dst):
        off = pl.multiple_of(kc_i * SEL_CHUNK, SEL_CHUNK)
        causal = off + _iota((SEL_CHUNK, tq), 0) <= qpos1
        k_next = ks_ref[0, pl.ds(pl.multiple_of(jnp.minimum(kc_i + 1, n_trips - 1) * SEL_CHUNK, SEL_CHUNK),
                                 SEL_CHUNK), :]
        for g in range(N_KV):
            dst[g] = _dot(k_next, qt_scr[g])
            picked = jnp.concatenate(
                [jnp.broadcast_to(sel_scr[g, pl.ds(kc_i * per_chunk + t, 1), :], (L_SLC, tq))
                 for t in range(per_chunk)], axis=0)
            bias = jnp.where((picked > 0.5) & causal, 0.0, NEG)
            s = src[g] + jnp.concatenate([bias] * GQA, axis=1)
            m_old = m_scr[g]
            m_new = jnp.maximum(m_old, jnp.max(s, axis=0, keepdims=True))
            p = jnp.exp2(s - m_new).astype(BF16)
            m_scr[g] = m_new
            acc_scr[g] = jnp.exp2(m_old - m_new) * acc_scr[g] + _dot(
                with_ones(vst_ref[0, groups[g], pl.ds(off, SEL_CHUNK)]), p)

    def body(kc_i, carry):
        @pl.when(lax.rem(kc_i, 2) == 0)
        def _():
            trip(kc_i, s_even, s_odd)

        @pl.when(lax.rem(kc_i, 2) == 1)
        def _():
            trip(kc_i, s_odd, s_even)
        return carry

    lax.fori_loop(0, n_trips, body, 0)

    acc_w = [None] * N_KV
    for c, kd in enumerate(w_chunks):
        for g in range(N_KV):
            pv = _dot(with_ones(vwt_ref[0, groups[g], kd]), jnp.exp2(sw_scr[g, c] - mw_scr[g]).astype(BF16))
            acc_w[g] = pv if c == 0 else acc_w[g] + pv
    pair = []
    for g in range(N_KV):
        o_w = acc_w[g][0:HEAD_DIM] * (1.0 / acc_w[g][HEAD_DIM:HEAD_DIM + 1])
        acc_s = acc_scr[g]
        o_s = acc_s[0:HEAD_DIM] * (1.0 / acc_s[HEAD_DIM:HEAD_DIM + 1])
        for r in range(GQA):
            h = g * GQA + r
            cs = slice(r * tq, (r + 1) * tq)
            pair.append(gates_t[3 * h:3 * h + 1] * o_c[g][:, cs] + gates_t[3 * h + 1:3 * h + 2] * o_s[:, cs]
                        + gates_t[3 * h + 2:3 * h + 3] * o_w[:, cs])
    per_lane = LANES // HEAD_DIM
    o_a = jnp.concatenate([jnp.concatenate(pair[j:j + per_lane], axis=0).T
                           for j in range(0, N_HEADS, per_lane)], axis=1)
    y_ref[0] = _merge_and_project(o_a, sza_ref[0], sga_ref[0], mb_ref[0], x_ref[0], gate_ref[0],
                                  wbra_ref, wout_ref)


def _p_attn_call(q_t, ks_rows, vs_t, kw_rows, vw_t, kc, vc, gates_t, sza, sga, mb, x, gate, w_br_a, w_out):
    b, s, _ = x.shape
    tq = Q_BLOCK
    row = lambda w: pl.BlockSpec((1, tq, w), lambda bi, i: (bi, i, 0))
    col = lambda a: pl.BlockSpec((1, a.shape[1], tq), lambda bi, i: (bi, 0, i))
    seq = lambda a: pl.BlockSpec((1,) + a.shape[1:], lambda bi, i: (bi, 0, 0))
    in_specs = [col(q_t), seq(ks_rows), seq(vs_t), seq(kw_rows), seq(vw_t), seq(kc), seq(vc),
                col(gates_t), row(WIDTH_A), row(D_MODEL), row(D_MODEL), row(D_MODEL), seq(gate),
                _full(w_br_a.shape), _full(w_out.shape)]
    return pl.pallas_call(
        _p_attn_kernel,
        grid=(b, s // tq),
        in_specs=in_specs,
        out_specs=row(D_MODEL),
        out_shape=jax.ShapeDtypeStruct((b, s, D_MODEL), F32),
        scratch_shapes=[pltpu.VMEM((N_KV, KV_WIDTH, GQA * tq), BF16),
                        pltpu.VMEM((N_KV, s // L_SLC, tq), F32),
                        pltpu.VMEM((N_KV, 1, GQA * tq), F32),
                        pltpu.VMEM((N_KV, HEAD_DIM + ONES_ROWS, GQA * tq), F32),
                        pltpu.VMEM((N_KV, SEL_CHUNK, GQA * tq), F32),
                        pltpu.VMEM((N_KV, SEL_CHUNK, GQA * tq), F32),
                        pltpu.VMEM((N_KV, (WINDOW + tq) // SEL_CHUNK, SEL_CHUNK, GQA * tq), F32),
                        pltpu.VMEM((N_KV, 1, GQA * tq), F32)],
        compiler_params=pltpu.CompilerParams(dimension_semantics=("arbitrary", "arbitrary"),
                                             vmem_limit_bytes=VMEM_LIMIT),
        name="p_attn",
    )(q_t, ks_rows, vs_t, kw_rows, vw_t, kc, vc, gates_t, sza, sga, mb, x, gate, w_br_a, w_out)


def _s_proj_kernel(x_ref, shift_ref, scale_ref, ng_ref, w_ref, cos_ref, sin_ref, qg_ref, kg_ref,
                   vng_ref, vnb_ref, ws0_ref, bs0_ref, wbrb_ref,
                   q_out, k_out, v_out, kt_out, vt_out, gates_out, sza_out, sga_out, mb_out, vn_out):
    seg = _project(x_ref[...], shift_ref[...], scale_ref[...], ng_ref[...], w_ref)
    cos, sin = cos_ref[...], sin_ref[...]
    q_out[...] = _norm_rope(seg(C_Q, C_K), qg_ref[...], cos, sin)
    k = _norm_rope(seg(C_K, C_V), kg_ref[...], cos, sin)
    v = seg(C_V, C_G)
    k_out[...] = k
    v_out[...] = v
    for br in range(3):
        sl = slice(br * KV_WIDTH, (br + 1) * KV_WIDTH)
        kt_out[br] = k[:, sl].T
        vt_out[br] = v[:, sl].T
    gates_out[...] = jax.nn.sigmoid(seg(C_G, C_ZA))
    sza_out[...] = _silu(seg(C_ZA, C_U))
    sga_out[...] = jax.nn.sigmoid(seg(C_GA, C_GB))
    vn = _layer_norm(seg(C_VB, C_ZB), vng_ref[...], vnb_ref[...])
    vn_out[...] = vn
    s_b = ws0_ref[...] * vn + bs0_ref[...]
    t = seg(C_U, C_VB) * s_b * _silu(seg(C_ZB, C_GA))
    mb_out[...] = jax.nn.sigmoid(seg(C_GB, C_END)) * _dot(t.astype(BF16), wbrb_ref[...])


def _s_proj_call(x, shift, scale, norm_g, w_all, cos1, sin1, qg, kg, vng, vnb, ws0, bs0, w_br_b):
    n = x.shape[0]
    args = (x, shift, scale, norm_g, w_all, cos1, sin1, qg, kg, vng, vnb, ws0, bs0, w_br_b)
    sds = lambda *shape: jax.ShapeDtypeStruct(shape, F32)
    out_shape = [sds(n, WIDTH_A), sds(n, 3 * KV_WIDTH), sds(n, 3 * KV_WIDTH), sds(3, KV_WIDTH, n),
                 sds(3, KV_WIDTH, n), sds(n, LANES), sds(n, WIDTH_A), sds(n, D_MODEL), sds(n, D_MODEL),
                 sds(n, WIDTH_B)]
    return pl.pallas_call(
        _s_proj_kernel,
        grid=(1,),
        in_specs=[_full(a.shape) for a in args],
        out_specs=[_full(o.shape) for o in out_shape],
        out_shape=out_shape,
        compiler_params=pltpu.CompilerParams(vmem_limit_bytes=VMEM_LIMIT),
        name="s_proj",
    )(*args)


def _s_attn_kernel(pt_ref, q_ref, gates_ref, knew_ref, vnew_ref, knewt_ref, vnewt_ref, kwin_ref, vwin_ref,
                   pek_ref, pev_ref, wck_ref, wcv_ref, pool_ref, expand_ref,
                   kc_hbm, vc_hbm, ks_hbm, vs_hbm,
                   oa_ref, okw_ref, ovw_ref, buf, sem):
    t = pl.program_id(0)
    n_groups = pl.num_programs(0) - 1
    per_step = q_ref.shape[0]
    n_seqs = pt_ref.shape[0]
    n_pages = pt_ref.shape[1]
    past = n_pages * PAGE_SIZE
    caches = (kc_hbm, vc_hbm, ks_hbm, vs_hbm)

    def page_copies(group, slot_):
        return [pltpu.make_async_copy(hbm.at[pt_ref[jnp.minimum(group * per_step + j, n_seqs - 1), p]],
                                      buf.at[slot_, j, c, :, pl.ds(p * PAGE_SIZE, PAGE_SIZE)],
                                      sem.at[slot_, c])
                for j in range(per_step) for c, hbm in enumerate(caches) for p in range(n_pages)]

    @pl.when(t < n_groups)
    def _():
        for cp in page_copies(t, lax.rem(t, 2)):
            cp.start()

    @pl.when(t > 0)
    def _():
        group = t - 1
        slot = lax.rem(group, 2)
        for cp in page_copies(group, slot):
            cp.wait()
        chains = [_s_attn_one(group * per_step + j, j, buf.at[slot, j], q_ref, gates_ref, knew_ref, vnew_ref,
                              knewt_ref, vnewt_ref, kwin_ref, vwin_ref, pek_ref, pev_ref, wck_ref, wcv_ref,
                              pool_ref, expand_ref, oa_ref, okw_ref, ovw_ref, past) for j in range(per_step)]
        for _ in itertools.zip_longest(*chains):
            pass


def _s_attn_one(b, j, buf, q_ref, gates_ref, knew_ref, vnew_ref, knewt_ref, vnewt_ref, kwin_ref, vwin_ref,
                pek_ref, pev_ref, wck_ref, wcv_ref, pool_ref, expand_ref, oa_ref, okw_ref, ovw_ref, past):
    qpos = past

    lane = _iota((KV_WIDTH, knewt_ref.shape[2]), 1)
    col = lambda ref, br: jnp.sum(jnp.where(lane == b, ref[br], 0.0), axis=1, keepdims=True)
    knew = knew_ref[pl.ds(b, 1), :]
    vnew = vnew_ref[pl.ds(b, 1), :]

    hrow = _iota((N_HEADS, 1), 0)
    first_group = hrow < GQA
    by_group = lambda f: jnp.where(first_group, f(0), f(1))
    gl = lambda g: slice(g * HEAD_DIM, (g + 1) * HEAD_DIM)
    qb = q_ref[j].astype(BF16)

    n_cmp = -(-(past + 1) // L_SLC) * L_SLC // L_CMP
    n_slc = n_cmp * L_CMP // L_SLC

    def summaries(c, new_row, pe_ref, w_ref):
        rows_t = buf[c, :, pl.ds(0, past)]
        hi, lo = _split_bf16(rows_t)
        pe_sum = jnp.sum(pe_ref[...], axis=0, keepdims=True)
        pooled = (_dot_nt(pool_ref[...], hi) + _dot_nt(pool_ref[...], lo) + pe_sum) * (1.0 / L_CMP)
        r = _iota((SUBLANES, KV_WIDTH), 0)
        tail = jnp.where(r == 0, new_row + pe_sum, jnp.where(r == 1, pe_sum, 0.0)) * (1.0 / L_CMP)
        zeros = jnp.zeros((LANES - pooled.shape[0] - SUBLANES, KV_WIDTH), F32)
        return _dot(jnp.concatenate([pooled, tail, zeros], axis=0).astype(BF16), w_ref[...])

    kc = summaries(0, knew[:, 0:KV_WIDTH], pek_ref, wck_ref).astype(BF16)
    yield
    vc = summaries(1, vnew[:, 0:KV_WIDTH], pev_ref, wcv_ref).astype(BF16)
    yield
    s_c = by_group(lambda g: _dot_nt(qb, kc[:, gl(g)])) * SM_SCALE
    c = _iota(s_c.shape, 1)
    mc = ((c + 1) * L_CMP - 1 <= qpos) & (c < n_cmp)
    e_c, r_c = _softmax_rows(s_c, mc)
    p_c = jnp.where(mc, e_c * r_c, 0.0)
    o_c = by_group(lambda g: _dot(p_c.astype(BF16), vc[:, gl(g)]))
    imp = by_group(lambda g: jnp.sum(p_c[g * GQA:(g + 1) * GQA], axis=0, keepdims=True))
    imp = jnp.broadcast_to(imp, p_c.shape)
    yield
    sel = _select_blocks(imp, jnp.full((N_HEADS, 1), qpos // L_SLC, jnp.int32), n_slc)
    yield

    picked = _dot(sel.astype(BF16), expand_ref[...])
    s_s = by_group(lambda g: _dot(qb, buf[2, gl(g), :].astype(BF16))) * SM_SCALE
    kpos = _iota(s_s.shape, 1)
    s_s = jnp.where((picked > 0.5) & (kpos <= qpos), s_s, NEG)
    yield
    own =_iota((N_HEADS, KV_WIDTH), 1) // HEAD_DIM == hrow // GQA
    rounded = lambda a: a.astype(BF16).astype(F32)
    q_pair = jnp.concatenate([qb.astype(F32)] * N_KV, axis=1)
    s_new = jnp.sum(jnp.where(own, q_pair * rounded(knew[:, KV_WIDTH:2 * KV_WIDTH]), 0.0),
                    axis=1, keepdims=True) * SM_SCALE
    new_lane = (L_SLC // L_CMP) * (past // L_SLC)
    s_new = jnp.where((sel[:, new_lane:new_lane + 1] > 0.5) & (past <= qpos), s_new, NEG)
    m_s = jnp.maximum(jnp.max(s_s, axis=-1, keepdims=True), s_new)
    e_s, e_new = jnp.exp(s_s - m_s), jnp.exp(s_new - m_s)
    r_s = 1.0 / (jnp.sum(e_s, axis=-1, keepdims=True) + e_new)
    v_new = by_group(lambda g: rounded(vnew[:, KV_WIDTH + g * HEAD_DIM:KV_WIDTH + (g + 1) * HEAD_DIM]))
    o_s = (by_group(lambda g: _dot_nt(e_s.astype(BF16), buf[3, gl(g), :].astype(BF16)))
           + rounded(e_new) * v_new) * r_s
    yield

    wb = kwin_ref.shape[2]
    wlane = _iota((KV_WIDTH, wb), 1)
    kw = jnp.where(wlane == wb - 1, col(knewt_ref, 2), pltpu.roll(kwin_ref[j], wb - 1, 1))
    vw = jnp.where(wlane == wb - 1, col(vnewt_ref, 2), pltpu.roll(vwin_ref[j], wb - 1, 1))
    okw_ref[j] = kw
    ovw_ref[j] = vw
    yield
    s_w = by_group(lambda g: _dot(qb, kw[gl(g)].astype(BF16))) * SM_SCALE
    kwpos = past - wb + 1 + _iota(s_w.shape, 1)
    e_w, r_w = _softmax_rows(s_w, (kwpos <= qpos) & (kwpos > qpos - WINDOW) & (kwpos >= 0))
    o_w = by_group(lambda g: _dot_nt(e_w.astype(BF16), vw[gl(g)].astype(BF16))) * r_w

    gates = gates_ref[j]
    oa_ref[j] = gates[:, 0:1] * o_c + gates[:, 1:2] * o_s + gates[:, 2:3] * o_w


def _s_attn_call(page_table, q3, gates3, knew, vnew, knew_t, vnew_t, kwin_t, vwin_t, pek, pev, wck, wcv,
                 kc_pool, vc_pool, ks_pool, vs_pool):
    n, n_pages = page_table.shape
    past = n_pages * PAGE_SIZE
    wb = kwin_t.shape[2]
    tok = np.arange(past)
    pool = jnp.asarray((tok[None, :] // L_CMP == np.arange(past // L_CMP)[:, None]), BF16)
    expand = jnp.asarray(np.arange(LANES)[:, None] == 2 * (np.arange(past)[None, :] // L_SLC), BF16)
    k = SAMPLE_SEQS_PER_STEP
    assert n % k == 0
    per_seq = lambda a: pl.BlockSpec((k,) + a.shape[1:],
                                     lambda t, pt: (jnp.maximum(t - 1, 0),) + (0,) * (a.ndim - 1))
    full = lambda a: pl.BlockSpec(a.shape, lambda t, pt: (0,) * a.ndim)
    hbm = pl.BlockSpec(memory_space=pl.ANY)
    resident = (knew, vnew, knew_t, vnew_t)
    consts = (pek, pev, wck, wcv, pool, expand)
    grid_spec = pltpu.PrefetchScalarGridSpec(
        num_scalar_prefetch=1,
        grid=(n // k + 1,),
        in_specs=[per_seq(q3), per_seq(gates3)] + [full(a) for a in resident]
                 + [per_seq(kwin_t), per_seq(vwin_t)] + [full(a) for a in consts] + [hbm] * 4,
        out_specs=[per_seq(q3), per_seq(kwin_t), per_seq(vwin_t)],
        scratch_shapes=[pltpu.VMEM((2, k, 4, KV_WIDTH, past), F32), pltpu.SemaphoreType.DMA((2, 4))],
    )
    return pl.pallas_call(
        _s_attn_kernel,
        grid_spec=grid_spec,
        out_shape=[jax.ShapeDtypeStruct(q3.shape, F32), jax.ShapeDtypeStruct(kwin_t.shape, F32),
                   jax.ShapeDtypeStruct(vwin_t.shape, F32)],
        compiler_params=pltpu.CompilerParams(dimension_semantics=("arbitrary",),
                                             vmem_limit_bytes=VMEM_LIMIT),
        name="s_attn",
    )(page_table, q3, gates3, knew, vnew, knew_t, vnew_t, kwin_t, vwin_t, pek, pev, wck, wcv, pool, expand,
      kc_pool, vc_pool, ks_pool, vs_pool)


def _s_out_kernel(oa_ref, sza_ref, sga_ref, mb_ref, x_ref, gate_ref, wbra_ref, wout_ref, y_ref):
    y_ref[...] = _merge_and_project(oa_ref[...], sza_ref[...], sga_ref[...], mb_ref[...], x_ref[...],
                                    gate_ref[...], wbra_ref, wout_ref)


def _s_out_call(o_a, sza, sga, mb, x, gate, w_br_a, w_out):
    args = (o_a, sza, sga, mb, x, gate, w_br_a, w_out)
    return pl.pallas_call(
        _s_out_kernel,
        grid=(1,),
        in_specs=[_full(a.shape) for a in args],
        out_specs=_full(x.shape),
        out_shape=jax.ShapeDtypeStruct(x.shape, F32),
        compiler_params=pltpu.CompilerParams(vmem_limit_bytes=VMEM_LIMIT),
        name="s_out",
    )(*args)


def _rope_angles(pos):
    half = HEAD_DIM // 2
    inv = ROPE_THETA ** (-jnp.arange(half, dtype=F32) * 2.0 / HEAD_DIM)
    return pos.astype(F32)[:, None] * inv[None, :]


def _rope_tables(pos):
    ang = _rope_angles(pos)
    cos, sin = jnp.cos(ang), jnp.sin(ang)
    cos_t = jnp.concatenate([cos, cos] * (LANES // HEAD_DIM), axis=1)
    sin_t = jnp.concatenate([-sin, sin] * (LANES // HEAD_DIM), axis=1)
    return cos_t, sin_t


def _token_minor(a):
    b, t = a.shape[:2]
    return jnp.transpose(a, (0, 2, 3, 1)).reshape(b, KV_WIDTH, t)


def _token_major(a_t):
    b, _, t = a_t.shape
    return jnp.transpose(a_t.reshape(b, N_KV, HEAD_DIM, t), (0, 3, 1, 2))


def kernel(x_prompt, x_sample, cache_k_cmp, cache_v_cmp, cache_k_slc, cache_v_slc, cache_k_win, cache_v_win, page_table, c_prompt, c_sample, w_ada, b_ada, norm_g, w_in, q_norm_g, k_norm_g, cmp_pos_k, cmp_pos_v, w_cmp_k, w_cmp_v, vnorm_g, vnorm_b, w_s, b_s, w_br_a, w_br_b, w_out):
    assert w_ada.shape[0] == 1, "single layer"
    b, s, _ = x_prompt.shape
    n = x_sample.shape[0]
    assert x_sample.shape[1] == 1
    n_pages = page_table.shape[1]
    past = n_pages * PAGE_SIZE

    w_t = w_in[0].T
    n_gate = 3 * N_HEADS
    w_all = jnp.concatenate([w_t[:C_G + n_gate], jnp.zeros((LANES - n_gate, D_MODEL), F32),
                             w_t[C_G + n_gate:]], axis=0).astype(BF16).T
    assert w_all.shape[1] == C_END
    eye = jnp.eye(N_KV, dtype=F32)
    wck = jnp.kron(eye, w_cmp_k[0]).astype(BF16)
    wcv = jnp.kron(eye, w_cmp_v[0]).astype(BF16)
    pek = jnp.tile(cmp_pos_k[0], (1, N_KV))
    pev = jnp.tile(cmp_pos_v[0], (1, N_KV))
    qg = jnp.tile(q_norm_g, (1, LANES // HEAD_DIM))
    kg = jnp.tile(k_norm_g, (1, LANES // HEAD_DIM))
    w_br_a_b, w_br_b_b, w_out_b = w_br_a[0].astype(BF16), w_br_b[0].astype(BF16), w_out[0].astype(BF16)

    mod = _ada_call(jnp.concatenate([c_prompt, c_sample], axis=0), w_ada[0], b_ada)
    shift, scale, gate = mod[:, :D_MODEL], mod[:, D_MODEL:2 * D_MODEL], mod[:, 2 * D_MODEL:]

    ang_p = _rope_angles(jnp.arange(s, dtype=jnp.int32)).T
    gain_cols = lambda g: jnp.broadcast_to(g[0][:, None], (HEAD_DIM, PROMPT_SUB_ROWS))
    (q_t, kc_t, ks_t, kw_t, vc_t, vs_t, vw_t, ks_rows, kw_rows, kcmp, vcmp, gates_t, sza, sga, mb) = _p_proj_call(
        x_prompt, shift[:b, None], scale[:b, None], norm_g, w_all, jnp.cos(ang_p), jnp.sin(ang_p),
        gain_cols(q_norm_g), gain_cols(k_norm_g), pek, pev, wck, wcv,
        vnorm_g, vnorm_b, w_s[0], b_s[0].T, w_br_b_b)
    y_prompt = _p_attn_call(q_t, ks_rows, vs_t, kw_rows, vw_t, kcmp, vcmp, gates_t, sza, sga, mb, x_prompt,
                            gate[:b, None], w_br_a_b, w_out_b)
    wb_p = min(WINDOW, s)
    p_states = [_token_major(a)[None] for a in (kc_t, vc_t, ks_t, vs_t, kw_t[:, :, s - wb_p:], vw_t[:, :, s - wb_p:])]

    xs = x_sample.reshape(n, D_MODEL)
    cos_s, sin_s = _rope_tables(jnp.full((1,), past, jnp.int32))
    ws0 = jnp.repeat(w_s[0, :, 0, 0], GROUP_W_B)[None]
    bs0 = jnp.repeat(b_s[0, :, 0], GROUP_W_B)[None]
    (q_s, k_s, v_s, kt_s, vt_s, gates_s, sza_s, sga_s, mb_s, vn_s) = _s_proj_call(
        xs, shift[b:], scale[b:], norm_g, w_all, cos_s, sin_s, qg, kg, vnorm_g, vnorm_b, ws0, bs0, w_br_b_b)
    pools = [_token_minor(c[0]) for c in (cache_k_cmp, cache_v_cmp, cache_k_slc, cache_v_slc)]
    o_a, kwin_new, vwin_new = _s_attn_call(
        page_table, q_s.reshape(n, N_HEADS, HEAD_DIM), gates_s[:, :3 * N_HEADS].reshape(n, N_HEADS, 3),
        k_s, v_s, kt_s, vt_s, _token_minor(cache_k_win[0]), _token_minor(cache_v_win[0]),
        pek, pev, wck, wcv, *pools)
    y_sample = _s_out_call(o_a.reshape(n, WIDTH_A), sza_s, sga_s, mb_s, xs, gate[b:], w_br_a_b, w_out_b)

    new_rows = lambda t, br: jnp.transpose(t[br].reshape(N_KV, HEAD_DIM, n), (2, 0, 1))[None, :, None]
    s_states = [new_rows(kt_s, 0), new_rows(vt_s, 0), new_rows(kt_s, 1), new_rows(vt_s, 1),
                _token_major(kwin_new)[None], _token_major(vwin_new)[None], vn_s[None, :, None]]
    return (y_prompt, y_sample.reshape(n, 1, D_MODEL), *p_states, *s_states)
```

```python
import itertools

import numpy as np
import jax
import jax.numpy as jnp
from jax import lax
from jax.experimental import pallas as pl
from jax.experimental.pallas import tpu as pltpu

F32 = jnp.float32
BF16 = jnp.bfloat16

D_MODEL = 1024
HEAD_DIM = 64
N_HEADS = 8
N_KV = 2
GQA = N_HEADS // N_KV
WIDTH_A = N_HEADS * HEAD_DIM
KV_WIDTH = N_KV * HEAD_DIM
L_CMP = 32
L_SLC = 64
N_SEL = 8
WINDOW = 512
Q_BLOCK = 256
FORCE_BONUS = 1.0e4
ROPE_THETA = 10000.0
CHUNK = 128
N_GROUPS_B = 4
WIDTH_B = 512
GROUP_W_B = WIDTH_B // N_GROUPS_B
PAGE_SIZE = 128
EPS = 1e-6
NEG = -1e30
SM_SCALE = HEAD_DIM ** -0.5
LOG2_E = 1.4426950408889634

LANES = 128
SUBLANES = 8
VMEM_LIMIT = 56 * 1024 * 1024

C_Q = 0
C_K = C_Q + WIDTH_A
C_V = C_K + 3 * KV_WIDTH
C_G = C_V + 3 * KV_WIDTH
C_ZA = C_G + LANES
C_U = C_ZA + WIDTH_A
C_VB = C_U + WIDTH_B
C_ZB = C_VB + WIDTH_B
C_GA = C_ZB + WIDTH_B
C_GB = C_GA + D_MODEL
C_END = C_GB + D_MODEL

PROMPT_ROWS = 512
PROMPT_SUB_ROWS = 256
SEL_CHUNK = 256
PEN_ROWS = 16
ONES_ROWS = 16
SUMMARY_CHUNK = 512
SAMPLE_SEQS_PER_STEP = 2


def _dot(a, b):
    return jnp.dot(a, b, preferred_element_type=F32)


def _dot_nt(a, b):
    return lax.dot_general(a, b, (((1,), (1,)), ((), ())), preferred_element_type=F32)


def _iota(shape, dim):
    return lax.broadcasted_iota(jnp.int32, shape, dim)


def _split_bf16(x):
    hi = x.astype(BF16)
    lo = (x - hi.astype(F32)).astype(BF16)
    return hi, lo


def _head_mean_sq(x):
    w = x.shape[1]
    ones_bd = jnp.where(_iota((LANES, LANES), 0) // HEAD_DIM == _iota((LANES, LANES), 1) // HEAD_DIM,
                        1.0, 0.0).astype(BF16)
    hi, lo = _split_bf16(x * x)
    cols = []
    for c in range(w // LANES):
        sl = slice(c * LANES, (c + 1) * LANES)
        cols.append(_dot(hi[:, sl], ones_bd) + _dot(lo[:, sl], ones_bd))
    return jnp.concatenate(cols, axis=1) * (1.0 / HEAD_DIM)


def _tile_lanes(t, width):
    return jnp.concatenate([t] * (width // t.shape[1]), axis=1)


def _norm_rope(x, g, cos, sin):
    w = x.shape[1]
    y = x * lax.rsqrt(_head_mean_sq(x) + EPS) * _tile_lanes(g, w)
    first_half = (_iota(y.shape, 1) % HEAD_DIM) < (HEAD_DIM // 2)
    rot = jnp.where(first_half, pltpu.roll(y, w - HEAD_DIM // 2, 1), pltpu.roll(y, HEAD_DIM // 2, 1))
    return y * _tile_lanes(cos, w) + rot * _tile_lanes(sin, w)


def _norm_rope_t(x_t, gain_t, cos_t, sin_t):
    half = HEAD_DIM // 2
    out = []
    for h in range(x_t.shape[0] // HEAD_DIM):
        x = x_t[h * HEAD_DIM:(h + 1) * HEAD_DIM]
        y = x * lax.rsqrt(jnp.mean(x * x, axis=0, keepdims=True) + EPS) * gain_t
        y1, y2 = y[:half], y[half:]
        out += [y1 * cos_t - y2 * sin_t, y2 * cos_t + y1 * sin_t]
    return jnp.concatenate(out, axis=0)


def _silu(z):
    return z * jax.nn.sigmoid(z)


def _project(x, shift, scale, norm_g, w_ref):
    ms = jnp.mean(x * x, axis=-1, keepdims=True)
    h = (x * lax.rsqrt(ms + EPS) * norm_g) * (1.0 + scale) + shift
    hb = h.astype(BF16)
    return lambda lo, hi: _dot(hb, w_ref[:, lo:hi])


def _layer_norm(v, g, b):
    mu = jnp.mean(v, axis=-1, keepdims=True)
    d = v - mu
    var = jnp.mean(d * d, axis=-1, keepdims=True)
    return d * lax.rsqrt(var + EPS) * g + b


def _ada_kernel(c_ref, w_ref, b_ref, o_ref):
    o_ref[...] = _dot(c_ref[...].astype(BF16), w_ref[...].astype(BF16)) + b_ref[...]


def _ada_call(c_all, w_ada, b_ada):
    rows = c_all.shape[0]
    n = w_ada.shape[1]
    return pl.pallas_call(
        _ada_kernel,
        grid=(n // D_MODEL,),
        in_specs=[pl.BlockSpec((rows, D_MODEL), lambda j: (0, 0)),
                  pl.BlockSpec((D_MODEL, D_MODEL), lambda j: (0, j)),
                  pl.BlockSpec((1, D_MODEL), lambda j: (0, j))],
        out_specs=pl.BlockSpec((rows, D_MODEL), lambda j: (0, j)),
        out_shape=jax.ShapeDtypeStruct((rows, n), F32),
        compiler_params=pltpu.CompilerParams(vmem_limit_bytes=VMEM_LIMIT),
        name="ada",
    )(c_all, w_ada, b_ada)


def _compress_rows(rows, pe, w_bd):
    t = rows.shape[0]
    pooled = jnp.sum(rows.reshape(t // L_CMP, L_CMP, KV_WIDTH) + pe[None], axis=1) * (1.0 / L_CMP)
    return _dot(pooled.astype(BF16), w_bd)


def _p_proj_kernel(x_ref, shift_ref, scale_ref, ng_ref, w_ref, cos_ref, sin_ref, qg_ref, kg_ref,
                   pek_ref, pev_ref, wck_ref, wcv_ref, vng_ref, vnb_ref, ws_ref, bst_ref, wbrb_ref,
                   *outs):
    tm = x_ref.shape[1]
    chains = [_p_proj_rows(r0, x_ref, shift_ref, scale_ref, ng_ref, w_ref, cos_ref, sin_ref, qg_ref, kg_ref,
                           pek_ref, pev_ref, wck_ref, wcv_ref, vng_ref, vnb_ref, ws_ref, bst_ref, wbrb_ref, *outs)
              for r0 in range(0, tm, PROMPT_SUB_ROWS)]
    for _ in itertools.zip_longest(*chains):
        pass


def _p_proj_rows(r0, x_ref, shift_ref, scale_ref, ng_ref, w_ref, cos_ref, sin_ref, qg_ref, kg_ref,
                 pek_ref, pev_ref, wck_ref, wcv_ref, vng_ref, vnb_ref, ws_ref, bst_ref, wbrb_ref,
                 qt_out, kc_t, ks_t, kw_t, vc_t, vs_t, vw_t, ks_rows, kw_rows, kcmp_out, vcmp_out,
                 gates_t, sza_out, sga_out, mb_out):
    sub = PROMPT_SUB_ROWS
    rs = slice(r0, r0 + sub)
    cmp_rows = slice(r0 // L_CMP, (r0 + sub) // L_CMP)
    seg = _project(x_ref[0, rs, :], shift_ref[0], scale_ref[0], ng_ref[...], w_ref)
    cos, sin = cos_ref[:, rs], sin_ref[:, rs]
    to_token_minor = lambda a: jnp.concatenate(
        [a[:, c * LANES:(c + 1) * LANES].T for c in range(a.shape[1] // LANES)], axis=0)
    yield

    qt_out[0, :, rs] = (_norm_rope_t(to_token_minor(seg(C_Q, C_K)), qg_ref[...], cos, sin)
                        * (SM_SCALE * LOG2_E)).astype(BF16)
    yield
    k_t = _norm_rope_t(to_token_minor(seg(C_K, C_V)), kg_ref[...], cos, sin)
    yield
    v = seg(C_V, C_G)
    for br, (k_out, v_out) in enumerate(((kc_t, vc_t), (ks_t, vs_t), (kw_t, vw_t))):
        sl = slice(br * KV_WIDTH, (br + 1) * KV_WIDTH)
        k_out[0, :, rs] = k_t[sl]
        v_out[0, :, rs] = v[:, sl].T
    blk_id = jnp.where((_iota((HEAD_DIM, sub), 1) // L_SLC) % (SEL_CHUNK // L_SLC) == _iota((HEAD_DIM, sub), 0),
                       1.0, 0.0)
    ks_g = [k_t[KV_WIDTH + g * HEAD_DIM:KV_WIDTH + (g + 1) * HEAD_DIM] for g in range(N_KV)]
    ks_rows[0, 0, rs, :] = jnp.concatenate([ks_g[0], blk_id], axis=0).T.astype(BF16)
    ks_rows[0, 1, rs, :] = jnp.concatenate([blk_id, ks_g[1]], axis=0).T.astype(BF16)
    kw_rows[0, rs, :] = k_t[2 * KV_WIDTH:3 * KV_WIDTH].T.astype(BF16)
    yield
    kcmp_out[0, cmp_rows, :] = _compress_rows(k_t[0:KV_WIDTH].T, pek_ref[...], wck_ref[...])
    vcmp_out[0, cmp_rows, :] = _compress_rows(v[:, 0:KV_WIDTH], pev_ref[...], wcv_ref[...])
    yield

    gates_t[0, :, rs] = jax.nn.sigmoid(seg(C_G, C_ZA)).T
    sza_out[0, rs, :] = _silu(seg(C_ZA, C_U))
    yield
    sga_out[0, rs, :] = jax.nn.sigmoid(seg(C_GA, C_GB))
    yield

    vn = _layer_norm(seg(C_VB, C_ZB), vng_ref[...], vnb_ref[...]).astype(BF16)
    yield
    causal = _iota((CHUNK, CHUNK), 0) >= _iota((CHUNK, CHUNK), 1)
    chunks = []
    for c in range(sub // CHUNK):
        groups = []
        for g in range(N_GROUPS_B):
            wsg = jnp.where(causal, ws_ref[g], 0.0).astype(BF16)
            vg = vn[c * CHUNK:(c + 1) * CHUNK, g * GROUP_W_B:(g + 1) * GROUP_W_B]
            groups.append(_dot(wsg, vg) + bst_ref[:, g:g + 1])
        chunks.append(jnp.concatenate(groups, axis=1))
    s_b = jnp.concatenate(chunks, axis=0)
    yield
    t = seg(C_U, C_VB) * s_b * _silu(seg(C_ZB, C_GA))
    yield
    mb_out[0, rs, :] = jax.nn.sigmoid(seg(C_GB, C_END)) * _dot(t.astype(BF16), wbrb_ref[...])


def _full(shape):
    nd = len(shape)
    return pl.BlockSpec(shape, lambda *_: (0,) * nd)


def _p_proj_call(x, shift, scale, norm_g, w_all, cos_t, sin_t, qg, kg, pek, pev, wck, wcv, vng, vnb,
                 w_s, bs_t, w_br_b):
    b, s, _ = x.shape
    tm = PROMPT_ROWS
    row = lambda w: pl.BlockSpec((1, tm, w), lambda bi, i: (bi, i, 0))
    tok_minor = pl.BlockSpec((1, KV_WIDTH, tm), lambda bi, i: (bi, 0, i))
    per_batch = pl.BlockSpec((1, 1, D_MODEL), lambda bi, i: (bi, 0, 0))
    cmp_spec = pl.BlockSpec((1, tm // L_CMP, KV_WIDTH), lambda bi, i: (bi, i, 0))
    tab = pl.BlockSpec((HEAD_DIM // 2, tm), lambda bi, i: (0, i))
    in_specs = [row(D_MODEL), per_batch, per_batch, _full(norm_g.shape), _full(w_all.shape), tab, tab,
                _full(qg.shape), _full(kg.shape), _full(pek.shape), _full(pev.shape), _full(wck.shape),
                _full(wcv.shape), _full(vng.shape), _full(vnb.shape), _full(w_s.shape), _full(bs_t.shape),
                _full(w_br_b.shape)]
    qt_spec = pl.BlockSpec((1, WIDTH_A, tm), lambda bi, i: (bi, 0, i))
    group_rows = pl.BlockSpec((1, N_KV, tm, KV_WIDTH), lambda bi, i: (bi, 0, i, 0))
    out_specs = [qt_spec] + [tok_minor] * 6 + [group_rows, row(KV_WIDTH), cmp_spec, cmp_spec,
                                                tok_minor, row(WIDTH_A), row(D_MODEL), row(D_MODEL)]
    tm_shape = jax.ShapeDtypeStruct((b, KV_WIDTH, s), F32)
    rows_shape = jax.ShapeDtypeStruct((b, s, KV_WIDTH), BF16)
    cmp_shape = jax.ShapeDtypeStruct((b, s // L_CMP, KV_WIDTH), F32)
    assert PROMPT_SUB_ROWS % SEL_CHUNK == 0
    out_shape = [jax.ShapeDtypeStruct((b, WIDTH_A, s), BF16)] + [tm_shape] * 6 + [
        jax.ShapeDtypeStruct((b, N_KV, s, KV_WIDTH), BF16), rows_shape, cmp_shape, cmp_shape,
        jax.ShapeDtypeStruct((b, LANES, s), F32), jax.ShapeDtypeStruct((b, s, WIDTH_A), F32),
        jax.ShapeDtypeStruct((b, s, D_MODEL), F32), jax.ShapeDtypeStruct((b, s, D_MODEL), F32)]
    return pl.pallas_call(
        _p_proj_kernel,
        grid=(b, s // tm),
        in_specs=in_specs,
        out_specs=out_specs,
        out_shape=out_shape,
        compiler_params=pltpu.CompilerParams(dimension_semantics=("arbitrary", "arbitrary"),
                                             vmem_limit_bytes=VMEM_LIMIT),
        name="p_proj",
    )(x, shift, scale, norm_g, w_all, cos_t, sin_t, qg, kg, pek, pev, wck, wcv, vng, vnb, w_s, bs_t, w_br_b)


def _select_blocks(imp_c, qblk, n_blocks):
    ratio = L_SLC // L_CMP
    assert ratio == 2
    lane = _iota(imp_c.shape, 1)
    imp = imp_c + pltpu.roll(imp_c, LANES - 1, 1)
    blk = lane // ratio
    forced = jnp.where((blk == 0) | (blk == qblk), 1.0, 0.0)
    score = jnp.where(blk <= qblk, imp + FORCE_BONUS * forced, NEG)
    rank = jnp.zeros(imp_c.shape, F32)
    for j in range(n_blocks):
        vj = score[:, ratio * j:ratio * j + 1]
        earlier = jnp.where(lane > ratio * j, 1.0, 0.0)
        rank = rank + jnp.where(vj > score, 1.0, jnp.where(vj == score, earlier, 0.0))
    cand = (lane % ratio == 0) & (lane < ratio * n_blocks)
    return jnp.where(cand & (rank < float(min(N_SEL, n_blocks))), 1.0, 0.0)


def _softmax_rows(s, valid):
    sm = jnp.where(valid, s, NEG)
    e = jnp.exp(sm - jnp.max(sm, axis=-1, keepdims=True))
    return e, 1.0 / jnp.sum(e, axis=-1, keepdims=True)


def _merge_and_project(o_a, sza, sga, mb, x, gate, wbra_ref, wout_ref):
    a = _dot((o_a * sza).astype(BF16), wbra_ref[...])
    m = sga * a + mb
    return x + gate * _dot(m.astype(BF16), wout_ref[...])


def _select_blocks_t(imp, qblk):
    n_blocks = imp.shape[0]
    blk = _iota(imp.shape, 0)
    forced = jnp.where((blk == 0) | (blk == qblk), 1.0, 0.0)
    score = jnp.where(blk <= qblk, imp + FORCE_BONUS * forced, NEG)
    rank = jnp.zeros(imp.shape, F32)
    for j in range(n_blocks):
        vj = score[j:j + 1, :]
        earlier = jnp.where(blk > j, 1.0, 0.0)
        rank = rank + jnp.where(vj > score, 1.0, jnp.where(vj == score, earlier, 0.0))
    return jnp.where(rank < float(min(N_SEL, n_blocks)), 1.0, 0.0)


def _softmax_cols(s):
    e = jnp.exp2(s - jnp.max(s, axis=0, keepdims=True))
    return e, 1.0 / jnp.sum(e, axis=0, keepdims=True)


def _p_attn_kernel(qt_ref, ks_ref, vst_ref, kw_ref, vwt_ref, kc_ref, vc_ref, gt_ref, sza_ref, sga_ref,
                   mb_ref, x_ref, gate_ref, wbra_ref, wout_ref, y_ref, qt_scr, pen_scr, m_scr, acc_scr, s_even, s_odd, sw_scr, mw_scr):
    i = pl.program_id(1)
    tq = Q_BLOCK
    n_cmp = kc_ref.shape[1]
    half = n_cmp // 2
    assert L_SLC == 2 * L_CMP and n_cmp <= LANES
    cols = GQA * tq
    qpos1 = i * tq + _iota((1, tq), 1)
    qpos = jnp.concatenate([qpos1] * GQA, axis=1)
    band = WINDOW + tq
    ws = pl.multiple_of(jnp.maximum(i * tq - WINDOW, 0), LANES)
    zeros_q = jnp.zeros((HEAD_DIM, cols), BF16)
    perm = lambda ref: jnp.concatenate([ref[0, pl.ds(0, half, stride=2), :], ref[0, pl.ds(1, half, stride=2), :],
                                        jnp.zeros((LANES - n_cmp, KV_WIDTH), F32)], axis=0)
    kc = perm(kc_ref).astype(BF16)
    vc_t = perm(vc_ref).T.astype(BF16)
    crow = _iota((LANES, cols), 0)
    cblk = 2 * (crow % half) + crow // half
    mc = ((cblk + 1) * L_CMP - 1 <= qpos) & (crow < n_cmp)
    gates_t = gt_ref[0]
    groups = [slice(g * HEAD_DIM, (g + 1) * HEAD_DIM) for g in range(N_KV)]

    for g in range(N_KV):
        qt_g = jnp.concatenate([qt_ref[0, h * HEAD_DIM:(h + 1) * HEAD_DIM, :]
                                for h in range(g * GQA, (g + 1) * GQA)], axis=1)
        qt_scr[g] = jnp.concatenate([qt_g, zeros_q] if g == 0 else [zeros_q, qt_g], axis=0)

    per_chunk = SEL_CHUNK // L_SLC
    assert band % SEL_CHUNK == 0
    w_chunks = [pl.ds(pl.multiple_of(ws + c * SEL_CHUNK, LANES), SEL_CHUNK) for c in range(band // SEL_CHUNK)]

    s_cmp = [_dot(kc, qt_scr[g]) for g in range(N_KV)]
    for c, kd in enumerate(w_chunks):
        k_rows = kw_ref[0, kd, :]
        for g in range(N_KV):
            sw_scr[g, c] = _dot(k_rows, qt_scr[g])

    o_c = []
    for g in range(N_KV):
        s_c = jnp.where(mc, s_cmp[g], NEG)
        e_c, r_c = _softmax_cols(s_c)
        p_c = jnp.where(mc, e_c * r_c, 0.0)
        o_c.append(_dot(vc_t[groups[g]], p_c.astype(BF16)))
        imp = p_c[:, 0:tq]
        for r in range(1, GQA):
            imp = imp + p_c[:, r * tq:(r + 1) * tq]
        sel = _select_blocks_t(imp[0:half] + imp[half:2 * half], qpos1 // L_SLC)
        pen = jnp.concatenate([jnp.where(sel > 0.5, 0.0, NEG)] * GQA, axis=1)
        pad_rows = jnp.zeros((PEN_ROWS - per_chunk, cols), F32)
        for c in range(half // per_chunk):
            pen_scr[g, c] = jnp.concatenate([pen[c * per_chunk:(c + 1) * per_chunk], pad_rows],
                                            axis=0).astype(BF16)

    def selected_scores(g, chunk):
        k_aug = ks_ref[0, g, pl.ds(pl.multiple_of(chunk * SEL_CHUNK, SEL_CHUNK), SEL_CHUNK), :]
        gap = jnp.zeros((HEAD_DIM - PEN_ROWS, cols), BF16)
        q_aug = ([qt_scr[0, 0:HEAD_DIM], pen_scr[0, chunk], gap] if g == 0 else
                 [pen_scr[1, chunk], gap, qt_scr[1, HEAD_DIM:2 * HEAD_DIM]])
        return _dot(k_aug, jnp.concatenate(q_aug, axis=0))

    for g in range(N_KV):
        s_even[g] = selected_scores(g, 0)

    for c in range(len(w_chunks)):
        kwpos = ws + c * SEL_CHUNK + _iota((SEL_CHUNK, tq), 0)
        bias_w = jnp.where((kwpos <= qpos1) & (kwpos > qpos1 - WINDOW), 0.0, NEG)
        bias_w = jnp.concatenate([bias_w] * GQA, axis=1)
        for g in range(N_KV):
            s_w = sw_scr[g, c] + bias_w
            sw_scr[g, c] = s_w
            m_c = jnp.max(s_w, axis=0, keepdims=True)
            mw_scr[g] = m_c if c == 0 else jnp.maximum(mw_scr[g], m_c)

    m_scr[...] = jnp.full(m_scr.shape, NEG, F32)
    acc_scr[...] = jnp.zeros(acc_scr.shape, F32)

    def with_ones(v_t):
        return jnp.concatenate([v_t.astype(BF16), jnp.ones((ONES_ROWS, v_t.shape[1]), BF16)], axis=0)

    def trip(kc_i, src, dst):
        off = pl.multiple_of(kc_i * SEL_CHUNK, SEL_CHUNK)
        for g in range(N_KV):
            if dst is not None:
                dst[g] = selected_scores(g, kc_i + 1)
                s = src[g]
            else:
                causal = jnp.where(off + _iota((SEL_CHUNK, tq), 0) <= qpos1, 0.0, NEG)
                s = src[g] + jnp.concatenate([causal] * GQA, axis=1)
            m_old = m_scr[g]
            m_new = jnp.maximum(m_old, jnp.max(s, axis=0, keepdims=True))
            p = jnp.exp2(s - m_new).astype(BF16)
            m_scr[g] = m_new
            acc_scr[g] = jnp.exp2(m_old - m_new) * acc_scr[g] + _dot(
                with_ones(vst_ref[0, groups[g], pl.ds(off, SEL_CHUNK)]), p)

    def by_parity(kc_i, dst_wanted):
        @pl.when(lax.rem(kc_i, 2) == 0)
        def _():
            trip(kc_i, s_even, s_odd if dst_wanted else None)

        @pl.when(lax.rem(kc_i, 2) == 1)
        def _():
            trip(kc_i, s_odd, s_even if dst_wanted else None)

    def body(kc_i, carry):
        by_parity(kc_i, True)
        return carry

    assert SEL_CHUNK == tq
    lax.fori_loop(0, i, body, 0)
    by_parity(i, False)

    acc_w = [None] * N_KV
    for c, kd in enumerate(w_chunks):
        for g in range(N_KV):
            pv = _dot(with_ones(vwt_ref[0, groups[g], kd]), jnp.exp2(sw_scr[g, c] - mw_scr[g]).astype(BF16))
            acc_w[g] = pv if c == 0 else acc_w[g] + pv
    pair = []
    for g in range(N_KV):
        o_w = acc_w[g][0:HEAD_DIM] * (1.0 / acc_w[g][HEAD_DIM:HEAD_DIM + 1])
        acc_s = acc_scr[g]
        o_s = acc_s[0:HEAD_DIM] * (1.0 / acc_s[HEAD_DIM:HEAD_DIM + 1])
        for r in range(GQA):
            h = g * GQA + r
            cs = slice(r * tq, (r + 1) * tq)
            pair.append(gates_t[3 * h:3 * h + 1] * o_c[g][:, cs] + gates_t[3 * h + 1:3 * h + 2] * o_s[:, cs]
                        + gates_t[3 * h + 2:3 * h + 3] * o_w[:, cs])
    per_lane = LANES // HEAD_DIM
    o_a = jnp.concatenate([jnp.concatenate(pair[j:j + per_lane], axis=0).T
                           for j in range(0, N_HEADS, per_lane)], axis=1)
    y_ref[0] = _merge_and_project(o_a, sza_ref[0], sga_ref[0], mb_ref[0], x_ref[0], gate_ref[0],
                                  wbra_ref, wout_ref)


def _p_attn_call(q_t, ks_rows, vs_t, kw_rows, vw_t, kc, vc, gates_t, sza, sga, mb, x, gate, w_br_a, w_out):
    b, s, _ = x.shape
    tq = Q_BLOCK
    row = lambda w: pl.BlockSpec((1, tq, w), lambda bi, i: (bi, i, 0))
    col = lambda a: pl.BlockSpec((1, a.shape[1], tq), lambda bi, i: (bi, 0, i))
    seq = lambda a: pl.BlockSpec((1,) + a.shape[1:], lambda bi, i: (bi,) + (0,) * (a.ndim - 1))
    in_specs = [col(q_t), seq(ks_rows), seq(vs_t), seq(kw_rows), seq(vw_t), seq(kc), seq(vc),
                col(gates_t), row(WIDTH_A), row(D_MODEL), row(D_MODEL), row(D_MODEL), seq(gate),
                _full(w_br_a.shape), _full(w_out.shape)]
    return pl.pallas_call(
        _p_attn_kernel,
        grid=(b, s // tq),
        in_specs=in_specs,
        out_specs=row(D_MODEL),
        out_shape=jax.ShapeDtypeStruct((b, s, D_MODEL), F32),
        scratch_shapes=[pltpu.VMEM((N_KV, KV_WIDTH, GQA * tq), BF16),
                        pltpu.VMEM((N_KV, s // SEL_CHUNK, PEN_ROWS, GQA * tq), BF16),
                        pltpu.VMEM((N_KV, 1, GQA * tq), F32),
                        pltpu.VMEM((N_KV, HEAD_DIM + ONES_ROWS, GQA * tq), F32),
                        pltpu.VMEM((N_KV, SEL_CHUNK, GQA * tq), F32),
                        pltpu.VMEM((N_KV, SEL_CHUNK, GQA * tq), F32),
                        pltpu.VMEM((N_KV, (WINDOW + tq) // SEL_CHUNK, SEL_CHUNK, GQA * tq), F32),
                        pltpu.VMEM((N_KV, 1, GQA * tq), F32)],
        compiler_params=pltpu.CompilerParams(dimension_semantics=("arbitrary", "arbitrary"),
                                             vmem_limit_bytes=VMEM_LIMIT),
        name="p_attn",
    )(q_t, ks_rows, vs_t, kw_rows, vw_t, kc, vc, gates_t, sza, sga, mb, x, gate, w_br_a, w_out)


def _s_proj_kernel(x_ref, shift_ref, scale_ref, ng_ref, w_ref, cos_ref, sin_ref, qg_ref, kg_ref,
                   vng_ref, vnb_ref, ws0_ref, bs0_ref, wbrb_ref,
                   q_out, k_out, v_out, kt_out, vt_out, gates_out, sza_out, sga_out, mb_out, vn_out):
    seg = _project(x_ref[...], shift_ref[...], scale_ref[...], ng_ref[...], w_ref)
    cos, sin = cos_ref[...], sin_ref[...]
    q_out[...] = _norm_rope(seg(C_Q, C_K), qg_ref[...], cos, sin)
    k = _norm_rope(seg(C_K, C_V), kg_ref[...], cos, sin)
    v = seg(C_V, C_G)
    k_out[...] = k
    v_out[...] = v
    for br in range(3):
        sl = slice(br * KV_WIDTH, (br + 1) * KV_WIDTH)
        kt_out[br] = k[:, sl].T
        vt_out[br] = v[:, sl].T
    gates_out[...] = jax.nn.sigmoid(seg(C_G, C_ZA))
    sza_out[...] = _silu(seg(C_ZA, C_U))
    sga_out[...] = jax.nn.sigmoid(seg(C_GA, C_GB))
    vn = _layer_norm(seg(C_VB, C_ZB), vng_ref[...], vnb_ref[...])
    vn_out[...] = vn
    s_b = ws0_ref[...] * vn + bs0_ref[...]
    t = seg(C_U, C_VB) * s_b * _silu(seg(C_ZB, C_GA))
    mb_out[...] = jax.nn.sigmoid(seg(C_GB, C_END)) * _dot(t.astype(BF16), wbrb_ref[...])


def _s_proj_call(x, shift, scale, norm_g, w_all, cos1, sin1, qg, kg, vng, vnb, ws0, bs0, w_br_b):
    n = x.shape[0]
    args = (x, shift, scale, norm_g, w_all, cos1, sin1, qg, kg, vng, vnb, ws0, bs0, w_br_b)
    sds = lambda *shape: jax.ShapeDtypeStruct(shape, F32)
    out_shape = [sds(n, WIDTH_A), sds(n, 3 * KV_WIDTH), sds(n, 3 * KV_WIDTH), sds(3, KV_WIDTH, n),
                 sds(3, KV_WIDTH, n), sds(n, LANES), sds(n, WIDTH_A), sds(n, D_MODEL), sds(n, D_MODEL),
                 sds(n, WIDTH_B)]
    return pl.pallas_call(
        _s_proj_kernel,
        grid=(1,),
        in_specs=[_full(a.shape) for a in args],
        out_specs=[_full(o.shape) for o in out_shape],
        out_shape=out_shape,
        compiler_params=pltpu.CompilerParams(vmem_limit_bytes=VMEM_LIMIT),
        name="s_proj",
    )(*args)


def _s_attn_kernel(pt_ref, q_ref, gates_ref, knew_ref, vnew_ref, knewt_ref, vnewt_ref, kwin_ref, vwin_ref,
                   pek_ref, pev_ref, wck_ref, wcv_ref, pool_ref, expand_ref,
                   kc_hbm, vc_hbm, ks_hbm, vs_hbm,
                   oa_ref, okw_ref, ovw_ref, buf, sem):
    t = pl.program_id(0)
    n_groups = pl.num_programs(0) - 1
    per_step = q_ref.shape[0]
    n_seqs = pt_ref.shape[0]
    n_pages = pt_ref.shape[1]
    past = n_pages * PAGE_SIZE
    caches = (kc_hbm, vc_hbm, ks_hbm, vs_hbm)

    def page_copies(group, slot_):
        return [pltpu.make_async_copy(hbm.at[pt_ref[jnp.minimum(group * per_step + j, n_seqs - 1), p]],
                                      buf.at[slot_, j, c, :, pl.ds(p * PAGE_SIZE, PAGE_SIZE)],
                                      sem.at[slot_, c])
                for j in range(per_step) for c, hbm in enumerate(caches) for p in range(n_pages)]

    @pl.when(t < n_groups)
    def _():
        for cp in page_copies(t, lax.rem(t, 2)):
            cp.start()

    @pl.when(t > 0)
    def _():
        group = t - 1
        slot = lax.rem(group, 2)
        for cp in page_copies(group, slot):
            cp.wait()
        chains = [_s_attn_one(group * per_step + j, j, buf.at[slot, j], q_ref, gates_ref, knew_ref, vnew_ref,
                              knewt_ref, vnewt_ref, kwin_ref, vwin_ref, pek_ref, pev_ref, wck_ref, wcv_ref,
                              pool_ref, expand_ref, oa_ref, okw_ref, ovw_ref, past) for j in range(per_step)]
        for _ in itertools.zip_longest(*chains):
            pass


def _s_attn_one(b, j, buf, q_ref, gates_ref, knew_ref, vnew_ref, knewt_ref, vnewt_ref, kwin_ref, vwin_ref,
                pek_ref, pev_ref, wck_ref, wcv_ref, pool_ref, expand_ref, oa_ref, okw_ref, ovw_ref, past):
    qpos = past

    lane = _iota((KV_WIDTH, knewt_ref.shape[2]), 1)
    col = lambda ref, br: jnp.sum(jnp.where(lane == b, ref[br], 0.0), axis=1, keepdims=True)
    knew = knew_ref[pl.ds(b, 1), :]
    vnew = vnew_ref[pl.ds(b, 1), :]

    hrow = _iota((N_HEADS, 1), 0)
    first_group = hrow < GQA
    by_group = lambda f: jnp.where(first_group, f(0), f(1))
    gl = lambda g: slice(g * HEAD_DIM, (g + 1) * HEAD_DIM)
    qb = q_ref[j].astype(BF16)

    n_cmp = -(-(past + 1) // L_SLC) * L_SLC // L_CMP
    n_slc = n_cmp * L_CMP // L_SLC

    def summaries(c, new_row, pe_ref, w_ref):
        chunk = pool_ref.shape[1]
        sums = []
        for t0 in range(0, past, chunk):
            hi, lo = _split_bf16(buf[c, :, t0:t0 + chunk])
            sums.append(_dot_nt(pool_ref[...], hi) + _dot_nt(pool_ref[...], lo))
        pe_sum = jnp.sum(pe_ref[...], axis=0, keepdims=True)
        pooled = (jnp.concatenate(sums, axis=0) + pe_sum) * (1.0 / L_CMP)
        r = _iota((SUBLANES, KV_WIDTH), 0)
        tail = jnp.where(r == 0, new_row + pe_sum, jnp.where(r == 1, pe_sum, 0.0)) * (1.0 / L_CMP)
        zeros = jnp.zeros((LANES - pooled.shape[0] - SUBLANES, KV_WIDTH), F32)
        return _dot(jnp.concatenate([pooled, tail, zeros], axis=0).astype(BF16), w_ref[...])

    kc = summaries(0, knew[:, 0:KV_WIDTH], pek_ref, wck_ref).astype(BF16)
    yield
    vc = summaries(1, vnew[:, 0:KV_WIDTH], pev_ref, wcv_ref).astype(BF16)
    yield
    s_c = by_group(lambda g: _dot_nt(qb, kc[:, gl(g)])) * SM_SCALE
    c = _iota(s_c.shape, 1)
    mc = ((c + 1) * L_CMP - 1 <= qpos) & (c < n_cmp)
    e_c, r_c = _softmax_rows(s_c, mc)
    p_c = jnp.where(mc, e_c * r_c, 0.0)
    o_c = by_group(lambda g: _dot(p_c.astype(BF16), vc[:, gl(g)]))
    imp = by_group(lambda g: jnp.sum(p_c[g * GQA:(g + 1) * GQA], axis=0, keepdims=True))
    imp = jnp.broadcast_to(imp, p_c.shape)
    yield
    sel = _select_blocks(imp, jnp.full((N_HEADS, 1), qpos // L_SLC, jnp.int32), n_slc)
    yield

    picked = _dot(sel.astype(BF16), expand_ref[...])
    s_s = by_group(lambda g: _dot(qb, buf[2, gl(g), :].astype(BF16))) * SM_SCALE
    kpos = _iota(s_s.shape, 1)
    s_s = jnp.where((picked > 0.5) & (kpos <= qpos), s_s, NEG)
    yield
    own =_iota((N_HEADS, KV_WIDTH), 1) // HEAD_DIM == hrow // GQA
    rounded = lambda a: a.astype(BF16).astype(F32)
    q_pair = jnp.concatenate([qb.astype(F32)] * N_KV, axis=1)
    s_new = jnp.sum(jnp.where(own, q_pair * rounded(knew[:, KV_WIDTH:2 * KV_WIDTH]), 0.0),
                    axis=1, keepdims=True) * SM_SCALE
    new_lane = (L_SLC // L_CMP) * (past // L_SLC)
    s_new = jnp.where((sel[:, new_lane:new_lane + 1] > 0.5) & (past <= qpos), s_new, NEG)
    m_s = jnp.maximum(jnp.max(s_s, axis=-1, keepdims=True), s_new)
    e_s, e_new = jnp.exp(s_s - m_s), jnp.exp(s_new - m_s)
    r_s = 1.0 / (jnp.sum(e_s, axis=-1, keepdims=True) + e_new)
    v_new = by_group(lambda g: rounded(vnew[:, KV_WIDTH + g * HEAD_DIM:KV_WIDTH + (g + 1) * HEAD_DIM]))
    o_s = (by_group(lambda g: _dot_nt(e_s.astype(BF16), buf[3, gl(g), :].astype(BF16)))
           + rounded(e_new) * v_new) * r_s
    yield

    wb = kwin_ref.shape[2]
    wlane = _iota((KV_WIDTH, wb), 1)
    kw = jnp.where(wlane == wb - 1, col(knewt_ref, 2), pltpu.roll(kwin_ref[j], wb - 1, 1))
    vw = jnp.where(wlane == wb - 1, col(vnewt_ref, 2), pltpu.roll(vwin_ref[j], wb - 1, 1))
    okw_ref[j] = kw
    ovw_ref[j] = vw
    yield
    s_w = by_group(lambda g: _dot(qb, kw[gl(g)].astype(BF16))) * SM_SCALE
    kwpos = past - wb + 1 + _iota(s_w.shape, 1)
    e_w, r_w = _softmax_rows(s_w, (kwpos <= qpos) & (kwpos > qpos - WINDOW) & (kwpos >= 0))
    o_w = by_group(lambda g: _dot_nt(e_w.astype(BF16), vw[gl(g)].astype(BF16))) * r_w

    gates = gates_ref[j]
    oa_ref[j] = gates[:, 0:1] * o_c + gates[:, 1:2] * o_s + gates[:, 2:3] * o_w


def _s_attn_call(page_table, q3, gates3, knew, vnew, knew_t, vnew_t, kwin_t, vwin_t, pek, pev, wck, wcv,
                 kc_pool, vc_pool, ks_pool, vs_pool):
    n, n_pages = page_table.shape
    past = n_pages * PAGE_SIZE
    wb = kwin_t.shape[2]
    tok = np.arange(SUMMARY_CHUNK)
    assert past % SUMMARY_CHUNK == 0
    pool = jnp.asarray((tok[None, :] // L_CMP == np.arange(SUMMARY_CHUNK // L_CMP)[:, None]), BF16)
    expand = jnp.asarray(np.arange(LANES)[:, None] == 2 * (np.arange(past)[None, :] // L_SLC), BF16)
    k = SAMPLE_SEQS_PER_STEP
    assert n % k == 0
    per_seq = lambda a: pl.BlockSpec((k,) + a.shape[1:],
                                     lambda t, pt: (jnp.maximum(t - 1, 0),) + (0,) * (a.ndim - 1))
    full = lambda a: pl.BlockSpec(a.shape, lambda t, pt: (0,) * a.ndim)
    hbm = pl.BlockSpec(memory_space=pl.ANY)
    resident = (knew, vnew, knew_t, vnew_t)
    consts = (pek, pev, wck, wcv, pool, expand)
    grid_spec = pltpu.PrefetchScalarGridSpec(
        num_scalar_prefetch=1,
        grid=(n // k + 1,),
        in_specs=[per_seq(q3), per_seq(gates3)] + [full(a) for a in resident]
                 + [per_seq(kwin_t), per_seq(vwin_t)] + [full(a) for a in consts] + [hbm] * 4,
        out_specs=[per_seq(q3), per_seq(kwin_t), per_seq(vwin_t)],
        scratch_shapes=[pltpu.VMEM((2, k, 4, KV_WIDTH, past), F32), pltpu.SemaphoreType.DMA((2, 4))],
    )
    return pl.pallas_call(
        _s_attn_kernel,
        grid_spec=grid_spec,
        out_shape=[jax.ShapeDtypeStruct(q3.shape, F32), jax.ShapeDtypeStruct(kwin_t.shape, F32),
                   jax.ShapeDtypeStruct(vwin_t.shape, F32)],
        compiler_params=pltpu.CompilerParams(dimension_semantics=("arbitrary",),
                                             vmem_limit_bytes=VMEM_LIMIT),
        name="s_attn",
    )(page_table, q3, gates3, knew, vnew, knew_t, vnew_t, kwin_t, vwin_t, pek, pev, wck, wcv, pool, expand,
      kc_pool, vc_pool, ks_pool, vs_pool)


def _s_out_kernel(oa_ref, sza_ref, sga_ref, mb_ref, x_ref, gate_ref, wbra_ref, wout_ref, y_ref):
    y_ref[...] = _merge_and_project(oa_ref[...], sza_ref[...], sga_ref[...], mb_ref[...], x_ref[...],
                                    gate_ref[...], wbra_ref, wout_ref)


def _s_out_call(o_a, sza, sga, mb, x, gate, w_br_a, w_out):
    args = (o_a, sza, sga, mb, x, gate, w_br_a, w_out)
    return pl.pallas_call(
        _s_out_kernel,
        grid=(1,),
        in_specs=[_full(a.shape) for a in args],
        out_specs=_full(x.shape),
        out_shape=jax.ShapeDtypeStruct(x.shape, F32),
        compiler_params=pltpu.CompilerParams(vmem_limit_bytes=VMEM_LIMIT),
        name="s_out",
    )(*args)


def _rope_angles(pos):
    half = HEAD_DIM // 2
    inv = ROPE_THETA ** (-jnp.arange(half, dtype=F32) * 2.0 / HEAD_DIM)
    return pos.astype(F32)[:, None] * inv[None, :]


def _rope_tables(pos):
    ang = _rope_angles(pos)
    cos, sin = jnp.cos(ang), jnp.sin(ang)
    cos_t = jnp.concatenate([cos, cos] * (LANES // HEAD_DIM), axis=1)
    sin_t = jnp.concatenate([-sin, sin] * (LANES // HEAD_DIM), axis=1)
    return cos_t, sin_t


def _token_minor(a):
    b, t = a.shape[:2]
    return jnp.transpose(a, (0, 2, 3, 1)).reshape(b, KV_WIDTH, t)


def _token_major(a_t):
    b, _, t = a_t.shape
    return jnp.transpose(a_t.reshape(b, N_KV, HEAD_DIM, t), (0, 3, 1, 2))


def kernel(x_prompt, x_sample, cache_k_cmp, cache_v_cmp, cache_k_slc, cache_v_slc, cache_k_win, cache_v_win, page_table, c_prompt, c_sample, w_ada, b_ada, norm_g, w_in, q_norm_g, k_norm_g, cmp_pos_k, cmp_pos_v, w_cmp_k, w_cmp_v, vnorm_g, vnorm_b, w_s, b_s, w_br_a, w_br_b, w_out):
    assert w_ada.shape[0] == 1, "single layer"
    b, s, _ = x_prompt.shape
    n = x_sample.shape[0]
    assert x_sample.shape[1] == 1
    n_pages = page_table.shape[1]
    past = n_pages * PAGE_SIZE

    w_t = w_in[0].T
    n_gate = 3 * N_HEADS
    w_all = jnp.concatenate([w_t[:C_G + n_gate], jnp.zeros((LANES - n_gate, D_MODEL), F32),
                             w_t[C_G + n_gate:]], axis=0).astype(BF16).T
    assert w_all.shape[1] == C_END
    eye = jnp.eye(N_KV, dtype=F32)
    wck = jnp.kron(eye, w_cmp_k[0]).astype(BF16)
    wcv = jnp.kron(eye, w_cmp_v[0]).astype(BF16)
    pek = jnp.tile(cmp_pos_k[0], (1, N_KV))
    pev = jnp.tile(cmp_pos_v[0], (1, N_KV))
    qg = jnp.tile(q_norm_g, (1, LANES // HEAD_DIM))
    kg = jnp.tile(k_norm_g, (1, LANES // HEAD_DIM))
    w_br_a_b, w_br_b_b, w_out_b = w_br_a[0].astype(BF16), w_br_b[0].astype(BF16), w_out[0].astype(BF16)

    mod = _ada_call(jnp.concatenate([c_prompt, c_sample], axis=0), w_ada[0], b_ada)
    shift, scale, gate = mod[:, :D_MODEL], mod[:, D_MODEL:2 * D_MODEL], mod[:, 2 * D_MODEL:]

    ang_p = _rope_angles(jnp.arange(s, dtype=jnp.int32)).T
    gain_cols = lambda g: jnp.broadcast_to(g[0][:, None], (HEAD_DIM, PROMPT_SUB_ROWS))
    (q_t, kc_t, ks_t, kw_t, vc_t, vs_t, vw_t, ks_rows, kw_rows, kcmp, vcmp, gates_t, sza, sga, mb) = _p_proj_call(
        x_prompt, shift[:b, None], scale[:b, None], norm_g, w_all, jnp.cos(ang_p), jnp.sin(ang_p),
        gain_cols(q_norm_g), gain_cols(k_norm_g), pek, pev, wck, wcv,
        vnorm_g, vnorm_b, w_s[0], b_s[0].T, w_br_b_b)
    y_prompt = _p_attn_call(q_t, ks_rows, vs_t, kw_rows, vw_t, kcmp, vcmp, gates_t, sza, sga, mb, x_prompt,
                            gate[:b, None], w_br_a_b, w_out_b)
    wb_p = min(WINDOW, s)
    p_states = [_token_major(a)[None] for a in (kc_t, vc_t, ks_t, vs_t, kw_t[:, :, s - wb_p:], vw_t[:, :, s - wb_p:])]

    xs = x_sample.reshape(n, D_MODEL)
    cos_s, sin_s = _rope_tables(jnp.full((1,), past, jnp.int32))
    ws0 = jnp.repeat(w_s[0, :, 0, 0], GROUP_W_B)[None]
    bs0 = jnp.repeat(b_s[0, :, 0], GROUP_W_B)[None]
    (q_s, k_s, v_s, kt_s, vt_s, gates_s, sza_s, sga_s, mb_s, vn_s) = _s_proj_call(
        xs, shift[b:], scale[b:], norm_g, w_all, cos_s, sin_s, qg, kg, vnorm_g, vnorm_b, ws0, bs0, w_br_b_b)
    pools = [_token_minor(c[0]) for c in (cache_k_cmp, cache_v_cmp, cache_k_slc, cache_v_slc)]
    o_a, kwin_new, vwin_new = _s_attn_call(
        page_table, q_s.reshape(n, N_HEADS, HEAD_DIM), gates_s[:, :3 * N_HEADS].reshape(n, N_HEADS, 3),
        k_s, v_s, kt_s, vt_s, _token_minor(cache_k_win[0]), _token_minor(cache_v_win[0]),
        pek, pev, wck, wcv, *pools)
    y_sample = _s_out_call(o_a.reshape(n, WIDTH_A), sza_s, sga_s, mb_s, xs, gate[b:], w_br_a_b, w_out_b)

    new_rows = lambda t, br: jnp.transpose(t[br].reshape(N_KV, HEAD_DIM, n), (2, 0, 1))[None, :, None]
    s_states = [new_rows(kt_s, 0), new_rows(vt_s, 0), new_rows(kt_s, 1), new_rows(vt_s, 1),
                _token_major(kwin_new)[None], _token_major(vwin_new)[None], vn_s[None, :, None]]
    return (y_prompt, y_sample.reshape(n, 1, D_MODEL), *p_states, *s_states)
```

```python
import itertools

import numpy as np
import jax
import jax.numpy as jnp
from jax import lax
from jax.experimental import pallas as pl
from jax.experimental.pallas import tpu as pltpu

F32 = jnp.float32
BF16 = jnp.bfloat16

D_MODEL = 1024
HEAD_DIM = 64
N_HEADS = 8
N_KV = 2
GQA = N_HEADS // N_KV
WIDTH_A = N_HEADS * HEAD_DIM
KV_WIDTH = N_KV * HEAD_DIM
L_CMP = 32
L_SLC = 64
N_SEL = 8
WINDOW = 512
Q_BLOCK = 256
FORCE_BONUS = 1.0e4
ROPE_THETA = 10000.0
CHUNK = 128
N_GROUPS_B = 4
WIDTH_B = 512
GROUP_W_B = WIDTH_B // N_GROUPS_B
PAGE_SIZE = 128
EPS = 1e-6
NEG = -1e30
SM_SCALE = HEAD_DIM ** -0.5
LOG2_E = 1.4426950408889634

LANES = 128
SUBLANES = 8
VMEM_LIMIT = 56 * 1024 * 1024

C_Q = 0
C_K = C_Q + WIDTH_A
C_V = C_K + 3 * KV_WIDTH
C_G = C_V + 3 * KV_WIDTH
C_ZA = C_G + LANES
C_U = C_ZA + WIDTH_A
C_VB = C_U + WIDTH_B
C_ZB = C_VB + WIDTH_B
C_GA = C_ZB + WIDTH_B
C_GB = C_GA + D_MODEL
C_END = C_GB + D_MODEL

PROMPT_ROWS = 512
PROMPT_SUB_ROWS = 256
SEL_CHUNK = 256
PEN_ROWS = 16
ONES_ROWS = 16
SUMMARY_CHUNK = 512
SAMPLE_SEQS_PER_STEP = 2


def _dot(a, b):
    return jnp.dot(a, b, preferred_element_type=F32)


def _dot_nt(a, b):
    return lax.dot_general(a, b, (((1,), (1,)), ((), ())), preferred_element_type=F32)


def _iota(shape, dim):
    return lax.broadcasted_iota(jnp.int32, shape, dim)


def _split_bf16(x):
    hi = x.astype(BF16)
    lo = (x - hi.astype(F32)).astype(BF16)
    return hi, lo


def _head_mean_sq(x):
    w = x.shape[1]
    ones_bd = jnp.where(_iota((LANES, LANES), 0) // HEAD_DIM == _iota((LANES, LANES), 1) // HEAD_DIM,
                        1.0, 0.0).astype(BF16)
    hi, lo = _split_bf16(x * x)
    cols = []
    for c in range(w // LANES):
        sl = slice(c * LANES, (c + 1) * LANES)
        cols.append(_dot(hi[:, sl], ones_bd) + _dot(lo[:, sl], ones_bd))
    return jnp.concatenate(cols, axis=1) * (1.0 / HEAD_DIM)


def _tile_lanes(t, width):
    return jnp.concatenate([t] * (width // t.shape[1]), axis=1)


def _norm_rope(x, g, cos, sin):
    w = x.shape[1]
    y = x * lax.rsqrt(_head_mean_sq(x) + EPS) * _tile_lanes(g, w)
    first_half = (_iota(y.shape, 1) % HEAD_DIM) < (HEAD_DIM // 2)
    rot = jnp.where(first_half, pltpu.roll(y, w - HEAD_DIM // 2, 1), pltpu.roll(y, HEAD_DIM // 2, 1))
    return y * _tile_lanes(cos, w) + rot * _tile_lanes(sin, w)


def _norm_rope_t(x_t, gain_t, cos_t, sin_t):
    half = HEAD_DIM // 2
    out = []
    for h in range(x_t.shape[0] // HEAD_DIM):
        x = x_t[h * HEAD_DIM:(h + 1) * HEAD_DIM]
        y = x * lax.rsqrt(jnp.mean(x * x, axis=0, keepdims=True) + EPS) * gain_t
        y1, y2 = y[:half], y[half:]
        out += [y1 * cos_t - y2 * sin_t, y2 * cos_t + y1 * sin_t]
    return jnp.concatenate(out, axis=0)


def _silu(z):
    return z * jax.nn.sigmoid(z)


def _project(x, shift, scale, norm_g, w_ref):
    ms = jnp.mean(x * x, axis=-1, keepdims=True)
    h = (x * lax.rsqrt(ms + EPS) * norm_g) * (1.0 + scale) + shift
    hb = h.astype(BF16)
    return lambda lo, hi: _dot(hb, w_ref[:, lo:hi])


def _layer_norm(v, g, b):
    mu = jnp.mean(v, axis=-1, keepdims=True)
    d = v - mu
    var = jnp.mean(d * d, axis=-1, keepdims=True)
    return d * lax.rsqrt(var + EPS) * g + b


def _ada_kernel(c_ref, w_ref, b_ref, o_ref):
    o_ref[...] = _dot(c_ref[...].astype(BF16), w_ref[...].astype(BF16)) + b_ref[...]


def _w_pack_kernel(wt_ref, o_ref):
    j = pl.program_id(0)
    rows = wt_ref[...]
    in_gate_block = j == C_G // LANES
    rows = jnp.where(in_gate_block & (_iota((LANES, 1), 0) >= 3 * N_HEADS), 0.0, rows)
    o_ref[...] = jnp.concatenate([rows[:, c * LANES:(c + 1) * LANES].T for c in range(D_MODEL // LANES)],
                                 axis=0).astype(BF16)


def _w_pack_call(w_t):
    d_in = w_t.shape[0]
    pad = LANES - 3 * N_HEADS
    assert d_in + pad == C_END and (C_G + LANES - pad) % SUBLANES == 0
    gate_block = C_G // LANES
    src_row = lambda j: (pl.multiple_of(jnp.where(j <= gate_block, j * LANES, j * LANES - pad), SUBLANES), 0)
    return pl.pallas_call(
        _w_pack_kernel,
        grid=(C_END // LANES,),
        in_specs=[pl.BlockSpec((pl.Element(LANES), pl.Element(D_MODEL)), src_row)],
        out_specs=pl.BlockSpec((D_MODEL, LANES), lambda j: (0, j)),
        out_shape=jax.ShapeDtypeStruct((D_MODEL, C_END), BF16),
        name="w_pack",
    )(w_t)


def _ada_call(c_all, w_ada, b_ada):
    rows = c_all.shape[0]
    n = w_ada.shape[1]
    return pl.pallas_call(
        _ada_kernel,
        grid=(n // D_MODEL,),
        in_specs=[pl.BlockSpec((rows, D_MODEL), lambda j: (0, 0)),
                  pl.BlockSpec((D_MODEL, D_MODEL), lambda j: (0, j)),
                  pl.BlockSpec((1, D_MODEL), lambda j: (0, j))],
        out_specs=pl.BlockSpec((rows, D_MODEL), lambda j: (0, j)),
        out_shape=jax.ShapeDtypeStruct((rows, n), F32),
        compiler_params=pltpu.CompilerParams(vmem_limit_bytes=VMEM_LIMIT),
        name="ada",
    )(c_all, w_ada, b_ada)


def _compress_rows(rows, pe, w_bd):
    t = rows.shape[0]
    pooled = jnp.sum(rows.reshape(t // L_CMP, L_CMP, KV_WIDTH) + pe[None], axis=1) * (1.0 / L_CMP)
    return _dot(pooled.astype(BF16), w_bd)


def _p_proj_kernel(x_ref, shift_ref, scale_ref, ng_ref, w_ref, cos_ref, sin_ref, qg_ref, kg_ref,
                   pek_ref, pev_ref, wck_ref, wcv_ref, vng_ref, vnb_ref, ws_ref, bst_ref, wbrb_ref,
                   *outs):
    tm = x_ref.shape[1]
    chains = [_p_proj_rows(r0, x_ref, shift_ref, scale_ref, ng_ref, w_ref, cos_ref, sin_ref, qg_ref, kg_ref,
                           pek_ref, pev_ref, wck_ref, wcv_ref, vng_ref, vnb_ref, ws_ref, bst_ref, wbrb_ref, *outs)
              for r0 in range(0, tm, PROMPT_SUB_ROWS)]
    for _ in itertools.zip_longest(*chains):
        pass


def _p_proj_rows(r0, x_ref, shift_ref, scale_ref, ng_ref, w_ref, cos_ref, sin_ref, qg_ref, kg_ref,
                 pek_ref, pev_ref, wck_ref, wcv_ref, vng_ref, vnb_ref, ws_ref, bst_ref, wbrb_ref,
                 qt_out, kc_t, ks_t, kw_t, vc_t, vs_t, vw_t, ks_rows, kw_rows, kcmp_out, vcmp_out,
                 gates_t, sza_out, sga_out, mb_out):
    sub = PROMPT_SUB_ROWS
    rs = slice(r0, r0 + sub)
    cmp_rows = slice(r0 // L_CMP, (r0 + sub) // L_CMP)
    seg = _project(x_ref[0, rs, :], shift_ref[0], scale_ref[0], ng_ref[...], w_ref)
    cos, sin = cos_ref[:, rs], sin_ref[:, rs]
    to_token_minor = lambda a: jnp.concatenate(
        [a[:, c * LANES:(c + 1) * LANES].T for c in range(a.shape[1] // LANES)], axis=0)
    yield

    qt_out[0, :, rs] = (_norm_rope_t(to_token_minor(seg(C_Q, C_K)), qg_ref[...], cos, sin)
                        * (SM_SCALE * LOG2_E)).astype(BF16)
    yield
    k_t = _norm_rope_t(to_token_minor(seg(C_K, C_V)), kg_ref[...], cos, sin)
    yield
    v = seg(C_V, C_G)
    for br, (k_out, v_out) in enumerate(((kc_t, vc_t), (ks_t, vs_t), (kw_t, vw_t))):
        sl = slice(br * KV_WIDTH, (br + 1) * KV_WIDTH)
        k_out[0, :, rs] = k_t[sl]
        v_out[0, :, rs] = v[:, sl].T
    blk_id = jnp.where((_iota((HEAD_DIM, sub), 1) // L_SLC) % (SEL_CHUNK // L_SLC) == _iota((HEAD_DIM, sub), 0),
                       1.0, 0.0)
    ks_g = [k_t[KV_WIDTH + g * HEAD_DIM:KV_WIDTH + (g + 1) * HEAD_DIM] for g in range(N_KV)]
    ks_rows[0, 0, rs, :] = jnp.concatenate([ks_g[0], blk_id], axis=0).T.astype(BF16)
    ks_rows[0, 1, rs, :] = jnp.concatenate([blk_id, ks_g[1]], axis=0).T.astype(BF16)
    kw_rows[0, rs, :] = k_t[2 * KV_WIDTH:3 * KV_WIDTH].T.astype(BF16)
    yield
    kcmp_out[0, cmp_rows, :] = _compress_rows(k_t[0:KV_WIDTH].T, pek_ref[...], wck_ref[...])
    vcmp_out[0, cmp_rows, :] = _compress_rows(v[:, 0:KV_WIDTH], pev_ref[...], wcv_ref[...])
    yield

    gates_t[0, :, rs] = jax.nn.sigmoid(seg(C_G, C_ZA)).T
    sza_out[0, rs, :] = _silu(seg(C_ZA, C_U))
    yield
    sga_out[0, rs, :] = jax.nn.sigmoid(seg(C_GA, C_GB))
    yield

    vn = _layer_norm(seg(C_VB, C_ZB), vng_ref[...], vnb_ref[...]).astype(BF16)
    yield
    causal = _iota((CHUNK, CHUNK), 0) >= _iota((CHUNK, CHUNK), 1)
    chunks = []
    for c in range(sub // CHUNK):
        groups = []
        for g in range(N_GROUPS_B):
            wsg = jnp.where(causal, ws_ref[g], 0.0).astype(BF16)
            vg = vn[c * CHUNK:(c + 1) * CHUNK, g * GROUP_W_B:(g + 1) * GROUP_W_B]
            groups.append(_dot(wsg, vg) + bst_ref[:, g:g + 1])
        chunks.append(jnp.concatenate(groups, axis=1))
    s_b = jnp.concatenate(chunks, axis=0)
    yield
    t = seg(C_U, C_VB) * s_b * _silu(seg(C_ZB, C_GA))
    yield
    mb_out[0, rs, :] = jax.nn.sigmoid(seg(C_GB, C_END)) * _dot(t.astype(BF16), wbrb_ref[...])


def _full(shape):
    nd = len(shape)
    return pl.BlockSpec(shape, lambda *_: (0,) * nd)


def _p_proj_call(x, shift, scale, norm_g, w_all, cos_t, sin_t, qg, kg, pek, pev, wck, wcv, vng, vnb,
                 w_s, bs_t, w_br_b):
    b, s, _ = x.shape
    tm = PROMPT_ROWS
    row = lambda w: pl.BlockSpec((1, tm, w), lambda bi, i: (bi, i, 0))
    tok_minor = pl.BlockSpec((1, KV_WIDTH, tm), lambda bi, i: (bi, 0, i))
    per_batch = pl.BlockSpec((1, 1, D_MODEL), lambda bi, i: (bi, 0, 0))
    cmp_spec = pl.BlockSpec((1, tm // L_CMP, KV_WIDTH), lambda bi, i: (bi, i, 0))
    tab = pl.BlockSpec((HEAD_DIM // 2, tm), lambda bi, i: (0, i))
    in_specs = [row(D_MODEL), per_batch, per_batch, _full(norm_g.shape), _full(w_all.shape), tab, tab,
                _full(qg.shape), _full(kg.shape), _full(pek.shape), _full(pev.shape), _full(wck.shape),
                _full(wcv.shape), _full(vng.shape), _full(vnb.shape), _full(w_s.shape), _full(bs_t.shape),
                _full(w_br_b.shape)]
    qt_spec = pl.BlockSpec((1, WIDTH_A, tm), lambda bi, i: (bi, 0, i))
    group_rows = pl.BlockSpec((1, N_KV, tm, KV_WIDTH), lambda bi, i: (bi, 0, i, 0))
    out_specs = [qt_spec] + [tok_minor] * 6 + [group_rows, row(KV_WIDTH), cmp_spec, cmp_spec,
                                                tok_minor, row(WIDTH_A), row(D_MODEL), row(D_MODEL)]
    tm_shape = jax.ShapeDtypeStruct((b, KV_WIDTH, s), F32)
    rows_shape = jax.ShapeDtypeStruct((b, s, KV_WIDTH), BF16)
    cmp_shape = jax.ShapeDtypeStruct((b, s // L_CMP, KV_WIDTH), F32)
    assert PROMPT_SUB_ROWS % SEL_CHUNK == 0
    out_shape = [jax.ShapeDtypeStruct((b, WIDTH_A, s), BF16)] + [tm_shape] * 6 + [
        jax.ShapeDtypeStruct((b, N_KV, s, KV_WIDTH), BF16), rows_shape, cmp_shape, cmp_shape,
        jax.ShapeDtypeStruct((b, LANES, s), F32), jax.ShapeDtypeStruct((b, s, WIDTH_A), F32),
        jax.ShapeDtypeStruct((b, s, D_MODEL), F32), jax.ShapeDtypeStruct((b, s, D_MODEL), F32)]
    return pl.pallas_call(
        _p_proj_kernel,
        grid=(b, s // tm),
        in_specs=in_specs,
        out_specs=out_specs,
        out_shape=out_shape,
        compiler_params=pltpu.CompilerParams(dimension_semantics=("arbitrary", "arbitrary"),
                                             vmem_limit_bytes=VMEM_LIMIT),
        name="p_proj",
    )(x, shift, scale, norm_g, w_all, cos_t, sin_t, qg, kg, pek, pev, wck, wcv, vng, vnb, w_s, bs_t, w_br_b)


def _select_blocks(imp_c, qblk, n_blocks):
    ratio = L_SLC // L_CMP
    assert ratio == 2
    lane = _iota(imp_c.shape, 1)
    imp = imp_c + pltpu.roll(imp_c, LANES - 1, 1)
    blk = lane // ratio
    forced = jnp.where((blk == 0) | (blk == qblk), 1.0, 0.0)
    score = jnp.where(blk <= qblk, imp + FORCE_BONUS * forced, NEG)
    rank = jnp.zeros(imp_c.shape, F32)
    for j in range(n_blocks):
        vj = score[:, ratio * j:ratio * j + 1]
        earlier = jnp.where(lane > ratio * j, 1.0, 0.0)
        rank = rank + jnp.where(vj > score, 1.0, jnp.where(vj == score, earlier, 0.0))
    cand = (lane % ratio == 0) & (lane < ratio * n_blocks)
    return jnp.where(cand & (rank < float(min(N_SEL, n_blocks))), 1.0, 0.0)


def _softmax_rows(s, valid):
    sm = jnp.where(valid, s, NEG)
    e = jnp.exp(sm - jnp.max(sm, axis=-1, keepdims=True))
    return e, 1.0 / jnp.sum(e, axis=-1, keepdims=True)


def _merge_and_project(o_a, sza, sga, mb, x, gate, wbra_ref, wout_ref):
    a = _dot((o_a * sza).astype(BF16), wbra_ref[...])
    m = sga * a + mb
    return x + gate * _dot(m.astype(BF16), wout_ref[...])


def _select_blocks_t(imp, qblk):
    n_blocks = imp.shape[0]
    blk = _iota(imp.shape, 0)
    forced = jnp.where((blk == 0) | (blk == qblk), 1.0, 0.0)
    score = jnp.where(blk <= qblk, imp + FORCE_BONUS * forced, NEG)
    rank = jnp.zeros(imp.shape, F32)
    for j in range(n_blocks):
        vj = score[j:j + 1, :]
        earlier = jnp.where(blk > j, 1.0, 0.0)
        rank = rank + jnp.where(vj > score, 1.0, jnp.where(vj == score, earlier, 0.0))
    return jnp.where(rank < float(min(N_SEL, n_blocks)), 1.0, 0.0)


def _softmax_cols(s):
    e = jnp.exp2(s - jnp.max(s, axis=0, keepdims=True))
    return e, 1.0 / jnp.sum(e, axis=0, keepdims=True)


def _p_attn_kernel(qt_ref, ks_ref, vst_ref, kw_ref, vwt_ref, kc_ref, vc_ref, gt_ref, sza_ref, sga_ref,
                   mb_ref, x_ref, gate_ref, wbra_ref, wout_ref, y_ref, qt_scr, pen_scr, m_scr, acc_scr, s_even, s_odd, sw_scr, mw_scr, oa_scr):
    tq = Q_BLOCK
    step = pl.program_id(0)
    i = lax.rem(jnp.minimum(step, pl.num_programs(0) - 2), ks_ref.shape[2] // tq)

    @pl.when(step == 0)
    def _():
        oa_scr[...] = jnp.zeros(oa_scr.shape, F32)
    n_cmp = kc_ref.shape[1]
    half = n_cmp // 2
    assert L_SLC == 2 * L_CMP and n_cmp <= LANES
    cols = GQA * tq
    qpos1 = i * tq + _iota((1, tq), 1)
    qpos = jnp.concatenate([qpos1] * GQA, axis=1)
    band = WINDOW + tq
    ws = pl.multiple_of(jnp.maximum(i * tq - WINDOW, 0), LANES)
    zeros_q = jnp.zeros((HEAD_DIM, cols), BF16)
    perm = lambda ref: jnp.concatenate([ref[0, pl.ds(0, half, stride=2), :], ref[0, pl.ds(1, half, stride=2), :],
                                        jnp.zeros((LANES - n_cmp, KV_WIDTH), F32)], axis=0)
    kc = perm(kc_ref).astype(BF16)
    vc_t = perm(vc_ref).T.astype(BF16)
    crow = _iota((LANES, cols), 0)
    cblk = 2 * (crow % half) + crow // half
    mc = ((cblk + 1) * L_CMP - 1 <= qpos) & (crow < n_cmp)
    gates_t = gt_ref[0]
    groups = [slice(g * HEAD_DIM, (g + 1) * HEAD_DIM) for g in range(N_KV)]

    for g in range(N_KV):
        qt_g = jnp.concatenate([qt_ref[0, h * HEAD_DIM:(h + 1) * HEAD_DIM, :]
                                for h in range(g * GQA, (g + 1) * GQA)], axis=1)
        qt_scr[g] = jnp.concatenate([qt_g, zeros_q] if g == 0 else [zeros_q, qt_g], axis=0)

    per_chunk = SEL_CHUNK // L_SLC
    assert band % SEL_CHUNK == 0
    w_chunks = [pl.ds(pl.multiple_of(ws + c * SEL_CHUNK, LANES), SEL_CHUNK) for c in range(band // SEL_CHUNK)]

    s_cmp = [_dot(kc, qt_scr[g]) for g in range(N_KV)]
    for c, kd in enumerate(w_chunks):
        k_rows = kw_ref[0, kd, :]
        for g in range(N_KV):
            sw_scr[g, c] = _dot(k_rows, qt_scr[g])

    o_c = []
    for g in range(N_KV):
        s_c = jnp.where(mc, s_cmp[g], NEG)
        e_c, r_c = _softmax_cols(s_c)
        p_c = jnp.where(mc, e_c * r_c, 0.0)
        o_c.append(_dot(vc_t[groups[g]], p_c.astype(BF16)))
        imp = p_c[:, 0:tq]
        for r in range(1, GQA):
            imp = imp + p_c[:, r * tq:(r + 1) * tq]
        sel = _select_blocks_t(imp[0:half] + imp[half:2 * half], qpos1 // L_SLC)
        pen = jnp.concatenate([jnp.where(sel > 0.5, 0.0, NEG)] * GQA, axis=1)
        pad_rows = jnp.zeros((PEN_ROWS - per_chunk, cols), F32)
        for c in range(half // per_chunk):
            pen_scr[g, c] = jnp.concatenate([pen[c * per_chunk:(c + 1) * per_chunk], pad_rows],
                                            axis=0).astype(BF16)

    def selected_scores(g, chunk):
        k_aug = ks_ref[0, g, pl.ds(pl.multiple_of(chunk * SEL_CHUNK, SEL_CHUNK), SEL_CHUNK), :]
        gap = jnp.zeros((HEAD_DIM - PEN_ROWS, cols), BF16)
        q_aug = ([qt_scr[0, 0:HEAD_DIM], pen_scr[0, chunk], gap] if g == 0 else
                 [pen_scr[1, chunk], gap, qt_scr[1, HEAD_DIM:2 * HEAD_DIM]])
        return _dot(k_aug, jnp.concatenate(q_aug, axis=0))

    for g in range(N_KV):
        s_even[g] = selected_scores(g, 0)

    for c in range(len(w_chunks)):
        kwpos = ws + c * SEL_CHUNK + _iota((SEL_CHUNK, tq), 0)
        bias_w = jnp.where((kwpos <= qpos1) & (kwpos > qpos1 - WINDOW), 0.0, NEG)
        bias_w = jnp.concatenate([bias_w] * GQA, axis=1)
        for g in range(N_KV):
            s_w = sw_scr[g, c] + bias_w
            sw_scr[g, c] = s_w
            m_c = jnp.max(s_w, axis=0, keepdims=True)
            mw_scr[g] = m_c if c == 0 else jnp.maximum(mw_scr[g], m_c)

    m_scr[...] = jnp.full(m_scr.shape, NEG, F32)
    acc_scr[...] = jnp.zeros(acc_scr.shape, F32)

    def with_ones(v_t):
        return jnp.concatenate([v_t.astype(BF16), jnp.ones((ONES_ROWS, v_t.shape[1]), BF16)], axis=0)

    def trip(kc_i, src, dst):
        off = pl.multiple_of(kc_i * SEL_CHUNK, SEL_CHUNK)
        for g in range(N_KV):
            if dst is not None:
                dst[g] = selected_scores(g, kc_i + 1)
                s = src[g]
            else:
                causal = jnp.where(off + _iota((SEL_CHUNK, tq), 0) <= qpos1, 0.0, NEG)
                s = src[g] + jnp.concatenate([causal] * GQA, axis=1)
            m_old = m_scr[g]
            m_new = jnp.maximum(m_old, jnp.max(s, axis=0, keepdims=True))
            p = jnp.exp2(s - m_new).astype(BF16)
            m_scr[g] = m_new
            acc_scr[g] = jnp.exp2(m_old - m_new) * acc_scr[g] + _dot(
                with_ones(vst_ref[0, groups[g], pl.ds(off, SEL_CHUNK)]), p)

    def by_parity(kc_i, dst_wanted):
        @pl.when(lax.rem(kc_i, 2) == 0)
        def _():
            trip(kc_i, s_even, s_odd if dst_wanted else None)

        @pl.when(lax.rem(kc_i, 2) == 1)
        def _():
            trip(kc_i, s_odd, s_even if dst_wanted else None)

    def body(kc_i, carry):
        by_parity(kc_i, True)
        return carry

    assert SEL_CHUNK == tq
    lax.fori_loop(0, i, body, 0)
    by_parity(i, False)

    acc_w = [None] * N_KV

    def window_pass2():
        for c, kd in enumerate(w_chunks):
            for g in range(N_KV):
                pv = _dot(with_ones(vwt_ref[0, groups[g], kd]),
                          jnp.exp2(sw_scr[g, c] - mw_scr[g]).astype(BF16))
                acc_w[g] = pv if c == 0 else acc_w[g] + pv
                yield

    def previous_block_output():
        n_blk = 2 * LANES
        t = (oa_scr[...] * sza_ref[0]).astype(BF16)
        a = []
        for n0 in range(0, D_MODEL, n_blk):
            a.append(_dot(t, wbra_ref[:, n0:n0 + n_blk]))
            yield
        m = (sga_ref[0] * jnp.concatenate(a, axis=1) + mb_ref[0]).astype(BF16)
        for n0 in range(0, D_MODEL, n_blk):
            y_ref[0, :, n0:n0 + n_blk] = (x_ref[0, :, n0:n0 + n_blk]
                                          + gate_ref[0][:, n0:n0 + n_blk] * _dot(m, wout_ref[:, n0:n0 + n_blk]))
            yield

    for _ in itertools.zip_longest(window_pass2(), previous_block_output()):
        pass
    pair = []
    for g in range(N_KV):
        o_w = acc_w[g][0:HEAD_DIM] * (1.0 / acc_w[g][HEAD_DIM:HEAD_DIM + 1])
        acc_s = acc_scr[g]
        o_s = acc_s[0:HEAD_DIM] * (1.0 / acc_s[HEAD_DIM:HEAD_DIM + 1])
        for r in range(GQA):
            h = g * GQA + r
            cs = slice(r * tq, (r + 1) * tq)
            pair.append(gates_t[3 * h:3 * h + 1] * o_c[g][:, cs] + gates_t[3 * h + 1:3 * h + 2] * o_s[:, cs]
                        + gates_t[3 * h + 2:3 * h + 3] * o_w[:, cs])
    per_lane = LANES // HEAD_DIM
    oa_scr[...] = jnp.concatenate([jnp.concatenate(pair[j:j + per_lane], axis=0).T
                                   for j in range(0, N_HEADS, per_lane)], axis=1)


def _p_attn_call(q_t, ks_rows, vs_t, kw_rows, vw_t, kc, vc, gates_t, sza, sga, mb, x, gate, w_br_a, w_out):
    b, s, _ = x.shape
    tq = Q_BLOCK
    n_i = s // tq
    n_blocks = b * n_i
    att = lambda j: jnp.minimum(j, n_blocks - 1)
    out = lambda j: jnp.maximum(j - 1, 0)
    row = lambda w: pl.BlockSpec((1, tq, w), lambda j: (out(j) // n_i, out(j) % n_i, 0))
    col = lambda a: pl.BlockSpec((1, a.shape[1], tq), lambda j: (att(j) // n_i, 0, att(j) % n_i))
    seq = lambda a: pl.BlockSpec((1,) + a.shape[1:], lambda j: (att(j) // n_i,) + (0,) * (a.ndim - 1))
    out_seq = lambda a: pl.BlockSpec((1,) + a.shape[1:], lambda j: (out(j) // n_i,) + (0,) * (a.ndim - 1))
    in_specs = [col(q_t), seq(ks_rows), seq(vs_t), seq(kw_rows), seq(vw_t), seq(kc), seq(vc),
                col(gates_t), row(WIDTH_A), row(D_MODEL), row(D_MODEL), row(D_MODEL), out_seq(gate),
                _full(w_br_a.shape), _full(w_out.shape)]
    return pl.pallas_call(
        _p_attn_kernel,
        grid=(n_blocks + 1,),
        in_specs=in_specs,
        out_specs=row(D_MODEL),
        out_shape=jax.ShapeDtypeStruct((b, s, D_MODEL), F32),
        scratch_shapes=[pltpu.VMEM((N_KV, KV_WIDTH, GQA * tq), BF16),
                        pltpu.VMEM((N_KV, s // SEL_CHUNK, PEN_ROWS, GQA * tq), BF16),
                        pltpu.VMEM((N_KV, 1, GQA * tq), F32),
                        pltpu.VMEM((N_KV, HEAD_DIM + ONES_ROWS, GQA * tq), F32),
                        pltpu.VMEM((N_KV, SEL_CHUNK, GQA * tq), F32),
                        pltpu.VMEM((N_KV, SEL_CHUNK, GQA * tq), F32),
                        pltpu.VMEM((N_KV, (WINDOW + tq) // SEL_CHUNK, SEL_CHUNK, GQA * tq), F32),
                        pltpu.VMEM((N_KV, 1, GQA * tq), F32),
                        pltpu.VMEM((tq, WIDTH_A), F32)],
        compiler_params=pltpu.CompilerParams(dimension_semantics=("arbitrary",),
                                             vmem_limit_bytes=VMEM_LIMIT),
        name="p_attn",
    )(q_t, ks_rows, vs_t, kw_rows, vw_t, kc, vc, gates_t, sza, sga, mb, x, gate, w_br_a, w_out)


def _s_proj_kernel(x_ref, shift_ref, scale_ref, ng_ref, w_ref, cos_ref, sin_ref, qg_ref, kg_ref,
                   vng_ref, vnb_ref, ws0_ref, bs0_ref, wbrb_ref,
                   q_out, k_out, v_out, kt_out, vt_out, gates_out, sza_out, sga_out, mb_out, vn_out):
    seg = _project(x_ref[...], shift_ref[...], scale_ref[...], ng_ref[...], w_ref)
    cos, sin = cos_ref[...], sin_ref[...]
    q_out[...] = _norm_rope(seg(C_Q, C_K), qg_ref[...], cos, sin)
    k = _norm_rope(seg(C_K, C_V), kg_ref[...], cos, sin)
    v = seg(C_V, C_G)
    k_out[...] = k
    v_out[...] = v
    for br in range(3):
        sl = slice(br * KV_WIDTH, (br + 1) * KV_WIDTH)
        kt_out[br] = k[:, sl].T
        vt_out[br] = v[:, sl].T
    gates_out[...] = jax.nn.sigmoid(seg(C_G, C_ZA))
    sza_out[...] = _silu(seg(C_ZA, C_U))
    sga_out[...] = jax.nn.sigmoid(seg(C_GA, C_GB))
    vn = _layer_norm(seg(C_VB, C_ZB), vng_ref[...], vnb_ref[...])
    vn_out[...] = vn
    s_b = ws0_ref[...] * vn + bs0_ref[...]
    t = seg(C_U, C_VB) * s_b * _silu(seg(C_ZB, C_GA))
    mb_out[...] = jax.nn.sigmoid(seg(C_GB, C_END)) * _dot(t.astype(BF16), wbrb_ref[...])


def _s_proj_call(x, shift, scale, norm_g, w_all, cos1, sin1, qg, kg, vng, vnb, ws0, bs0, w_br_b):
    n = x.shape[0]
    args = (x, shift, scale, norm_g, w_all, cos1, sin1, qg, kg, vng, vnb, ws0, bs0, w_br_b)
    sds = lambda *shape: jax.ShapeDtypeStruct(shape, F32)
    out_shape = [sds(n, WIDTH_A), sds(n, 3 * KV_WIDTH), sds(n, 3 * KV_WIDTH), sds(3, KV_WIDTH, n),
                 sds(3, KV_WIDTH, n), sds(n, LANES), sds(n, WIDTH_A), sds(n, D_MODEL), sds(n, D_MODEL),
                 sds(n, WIDTH_B)]
    return pl.pallas_call(
        _s_proj_kernel,
        grid=(1,),
        in_specs=[_full(a.shape) for a in args],
        out_specs=[_full(o.shape) for o in out_shape],
        out_shape=out_shape,
        compiler_params=pltpu.CompilerParams(vmem_limit_bytes=VMEM_LIMIT),
        name="s_proj",
    )(*args)


def _s_attn_kernel(pt_ref, q_ref, gates_ref, knew_ref, vnew_ref, knewt_ref, vnewt_ref, kwin_ref, vwin_ref,
                   pek_ref, pev_ref, wck_ref, wcv_ref, pool_ref, expand_ref,
                   kc_hbm, vc_hbm, ks_hbm, vs_hbm,
                   oa_ref, okw_ref, ovw_ref, buf, sem):
    t = pl.program_id(0)
    n_groups = pl.num_programs(0) - 1
    per_step = q_ref.shape[0]
    n_seqs = pt_ref.shape[0]
    n_pages = pt_ref.shape[1]
    past = n_pages * PAGE_SIZE
    caches = (kc_hbm, vc_hbm, ks_hbm, vs_hbm)

    def page_copies(group, slot_):
        return [pltpu.make_async_copy(hbm.at[pt_ref[jnp.minimum(group * per_step + j, n_seqs - 1), p]],
                                      buf.at[slot_, j, c, :, pl.ds(p * PAGE_SIZE, PAGE_SIZE)],
                                      sem.at[slot_, c])
                for j in range(per_step) for c, hbm in enumerate(caches) for p in range(n_pages)]

    @pl.when(t < n_groups)
    def _():
        for cp in page_copies(t, lax.rem(t, 2)):
            cp.start()

    @pl.when(t > 0)
    def _():
        group = t - 1
        slot = lax.rem(group, 2)
        for cp in page_copies(group, slot):
            cp.wait()
        chains = [_s_attn_one(group * per_step + j, j, buf.at[slot, j], q_ref, gates_ref, knew_ref, vnew_ref,
                              knewt_ref, vnewt_ref, kwin_ref, vwin_ref, pek_ref, pev_ref, wck_ref, wcv_ref,
                              pool_ref, expand_ref, oa_ref, okw_ref, ovw_ref, past) for j in range(per_step)]
        for _ in itertools.zip_longest(*chains):
            pass


def _s_attn_one(b, j, buf, q_ref, gates_ref, knew_ref, vnew_ref, knewt_ref, vnewt_ref, kwin_ref, vwin_ref,
                pek_ref, pev_ref, wck_ref, wcv_ref, pool_ref, expand_ref, oa_ref, okw_ref, ovw_ref, past):
    qpos = past

    lane = _iota((KV_WIDTH, knewt_ref.shape[2]), 1)
    col = lambda ref, br: jnp.sum(jnp.where(lane == b, ref[br], 0.0), axis=1, keepdims=True)
    knew = knew_ref[pl.ds(b, 1), :]
    vnew = vnew_ref[pl.ds(b, 1), :]

    hrow = _iota((N_HEADS, 1), 0)
    first_group = hrow < GQA
    by_group = lambda f: jnp.where(first_group, f(0), f(1))
    gl = lambda g: slice(g * HEAD_DIM, (g + 1) * HEAD_DIM)
    qb = q_ref[j].astype(BF16)

    n_cmp = -(-(past + 1) // L_SLC) * L_SLC // L_CMP
    n_slc = n_cmp * L_CMP // L_SLC

    def summaries(c, new_row, pe_ref, w_ref):
        chunk = pool_ref.shape[1]
        sums = []
        for t0 in range(0, past, chunk):
            hi, lo = _split_bf16(buf[c, :, t0:t0 + chunk])
            sums.append(_dot_nt(pool_ref[...], hi) + _dot_nt(pool_ref[...], lo))
        pe_sum = jnp.sum(pe_ref[...], axis=0, keepdims=True)
        pooled = (jnp.concatenate(sums, axis=0) + pe_sum) * (1.0 / L_CMP)
        r = _iota((SUBLANES, KV_WIDTH), 0)
        tail = jnp.where(r == 0, new_row + pe_sum, jnp.where(r == 1, pe_sum, 0.0)) * (1.0 / L_CMP)
        zeros = jnp.zeros((LANES - pooled.shape[0] - SUBLANES, KV_WIDTH), F32)
        return _dot(jnp.concatenate([pooled, tail, zeros], axis=0).astype(BF16), w_ref[...])

    kc = summaries(0, knew[:, 0:KV_WIDTH], pek_ref, wck_ref).astype(BF16)
    yield
    vc = summaries(1, vnew[:, 0:KV_WIDTH], pev_ref, wcv_ref).astype(BF16)
    yield
    s_c = by_group(lambda g: _dot_nt(qb, kc[:, gl(g)])) * SM_SCALE
    c = _iota(s_c.shape, 1)
    mc = ((c + 1) * L_CMP - 1 <= qpos) & (c < n_cmp)
    e_c, r_c = _softmax_rows(s_c, mc)
    p_c = jnp.where(mc, e_c * r_c, 0.0)
    o_c = by_group(lambda g: _dot(p_c.astype(BF16), vc[:, gl(g)]))
    imp = by_group(lambda g: jnp.sum(p_c[g * GQA:(g + 1) * GQA], axis=0, keepdims=True))
    imp = jnp.broadcast_to(imp, p_c.shape)
    yield
    sel = _select_blocks(imp, jnp.full((N_HEADS, 1), qpos // L_SLC, jnp.int32), n_slc)
    yield

    picked = _dot(sel.astype(BF16), expand_ref[...])
    s_s = by_group(lambda g: _dot(qb, buf[2, gl(g), :].astype(BF16))) * SM_SCALE
    kpos = _iota(s_s.shape, 1)
    s_s = jnp.where((picked > 0.5) & (kpos <= qpos), s_s, NEG)
    yield
    own =_iota((N_HEADS, KV_WIDTH), 1) // HEAD_DIM == hrow // GQA
    rounded = lambda a: a.astype(BF16).astype(F32)
    q_pair = jnp.concatenate([qb.astype(F32)] * N_KV, axis=1)
    s_new = jnp.sum(jnp.where(own, q_pair * rounded(knew[:, KV_WIDTH:2 * KV_WIDTH]), 0.0),
                    axis=1, keepdims=True) * SM_SCALE
    new_lane = (L_SLC // L_CMP) * (past // L_SLC)
    s_new = jnp.where((sel[:, new_lane:new_lane + 1] > 0.5) & (past <= qpos), s_new, NEG)
    m_s = jnp.maximum(jnp.max(s_s, axis=-1, keepdims=True), s_new)
    e_s, e_new = jnp.exp(s_s - m_s), jnp.exp(s_new - m_s)
    r_s = 1.0 / (jnp.sum(e_s, axis=-1, keepdims=True) + e_new)
    v_new = by_group(lambda g: rounded(vnew[:, KV_WIDTH + g * HEAD_DIM:KV_WIDTH + (g + 1) * HEAD_DIM]))
    o_s = (by_group(lambda g: _dot_nt(e_s.astype(BF16), buf[3, gl(g), :].astype(BF16)))
           + rounded(e_new) * v_new) * r_s
    yield

    wb = kwin_ref.shape[2]
    wlane = _iota((KV_WIDTH, wb), 1)
    kw = jnp.where(wlane == wb - 1, col(knewt_ref, 2), pltpu.roll(kwin_ref[j], wb - 1, 1))
    vw = jnp.where(wlane == wb - 1, col(vnewt_ref, 2), pltpu.roll(vwin_ref[j], wb - 1, 1))
    okw_ref[j] = kw
    ovw_ref[j] = vw
    yield
    s_w = by_group(lambda g: _dot(qb, kw[gl(g)].astype(BF16))) * SM_SCALE
    kwpos = past - wb + 1 + _iota(s_w.shape, 1)
    e_w, r_w = _softmax_rows(s_w, (kwpos <= qpos) & (kwpos > qpos - WINDOW) & (kwpos >= 0))
    o_w = by_group(lambda g: _dot_nt(e_w.astype(BF16), vw[gl(g)].astype(BF16))) * r_w

    gates = gates_ref[j]
    oa_ref[j] = gates[:, 0:1] * o_c + gates[:, 1:2] * o_s + gates[:, 2:3] * o_w


def _s_attn_call(page_table, q3, gates3, knew, vnew, knew_t, vnew_t, kwin_t, vwin_t, pek, pev, wck, wcv,
                 kc_pool, vc_pool, ks_pool, vs_pool):
    n, n_pages = page_table.shape
    past = n_pages * PAGE_SIZE
    wb = kwin_t.shape[2]
    tok = np.arange(SUMMARY_CHUNK)
    assert past % SUMMARY_CHUNK == 0
    pool = jnp.asarray((tok[None, :] // L_CMP == np.arange(SUMMARY_CHUNK // L_CMP)[:, None]), BF16)
    expand = jnp.asarray(np.arange(LANES)[:, None] == 2 * (np.arange(past)[None, :] // L_SLC), BF16)
    k = SAMPLE_SEQS_PER_STEP
    assert n % k == 0
    per_seq = lambda a: pl.BlockSpec((k,) + a.shape[1:],
                                     lambda t, pt: (jnp.maximum(t - 1, 0),) + (0,) * (a.ndim - 1))
    full = lambda a: pl.BlockSpec(a.shape, lambda t, pt: (0,) * a.ndim)
    hbm = pl.BlockSpec(memory_space=pl.ANY)
    resident = (knew, vnew, knew_t, vnew_t)
    consts = (pek, pev, wck, wcv, pool, expand)
    grid_spec = pltpu.PrefetchScalarGridSpec(
        num_scalar_prefetch=1,
        grid=(n // k + 1,),
        in_specs=[per_seq(q3), per_seq(gates3)] + [full(a) for a in resident]
                 + [per_seq(kwin_t), per_seq(vwin_t)] + [full(a) for a in consts] + [hbm] * 4,
        out_specs=[per_seq(q3), per_seq(kwin_t), per_seq(vwin_t)],
        scratch_shapes=[pltpu.VMEM((2, k, 4, KV_WIDTH, past), F32), pltpu.SemaphoreType.DMA((2, 4))],
    )
    return pl.pallas_call(
        _s_attn_kernel,
        grid_spec=grid_spec,
        out_shape=[jax.ShapeDtypeStruct(q3.shape, F32), jax.ShapeDtypeStruct(kwin_t.shape, F32),
                   jax.ShapeDtypeStruct(vwin_t.shape, F32)],
        compiler_params=pltpu.CompilerParams(dimension_semantics=("arbitrary",),
                                             vmem_limit_bytes=VMEM_LIMIT),
        name="s_attn",
    )(page_table, q3, gates3, knew, vnew, knew_t, vnew_t, kwin_t, vwin_t, pek, pev, wck, wcv, pool, expand,
      kc_pool, vc_pool, ks_pool, vs_pool)


def _s_out_kernel(oa_ref, sza_ref, sga_ref, mb_ref, x_ref, gate_ref, wbra_ref, wout_ref, y_ref):
    y_ref[...] = _merge_and_project(oa_ref[...], sza_ref[...], sga_ref[...], mb_ref[...], x_ref[...],
                                    gate_ref[...], wbra_ref, wout_ref)


def _s_out_call(o_a, sza, sga, mb, x, gate, w_br_a, w_out):
    args = (o_a, sza, sga, mb, x, gate, w_br_a, w_out)
    return pl.pallas_call(
        _s_out_kernel,
        grid=(1,),
        in_specs=[_full(a.shape) for a in args],
        out_specs=_full(x.shape),
        out_shape=jax.ShapeDtypeStruct(x.shape, F32),
        compiler_params=pltpu.CompilerParams(vmem_limit_bytes=VMEM_LIMIT),
        name="s_out",
    )(*args)


def _rope_angles(pos):
    half = HEAD_DIM // 2
    inv = ROPE_THETA ** (-jnp.arange(half, dtype=F32) * 2.0 / HEAD_DIM)
    return pos.astype(F32)[:, None] * inv[None, :]


def _rope_tables(pos):
    ang = _rope_angles(pos)
    cos, sin = jnp.cos(ang), jnp.sin(ang)
    cos_t = jnp.concatenate([cos, cos] * (LANES // HEAD_DIM), axis=1)
    sin_t = jnp.concatenate([-sin, sin] * (LANES // HEAD_DIM), axis=1)
    return cos_t, sin_t


def _token_minor(a):
    b, t = a.shape[:2]
    return jnp.transpose(a, (0, 2, 3, 1)).reshape(b, KV_WIDTH, t)


def _token_major(a_t):
    b, _, t = a_t.shape
    return jnp.transpose(a_t.reshape(b, N_KV, HEAD_DIM, t), (0, 3, 1, 2))


def kernel(x_prompt, x_sample, cache_k_cmp, cache_v_cmp, cache_k_slc, cache_v_slc, cache_k_win, cache_v_win, page_table, c_prompt, c_sample, w_ada, b_ada, norm_g, w_in, q_norm_g, k_norm_g, cmp_pos_k, cmp_pos_v, w_cmp_k, w_cmp_v, vnorm_g, vnorm_b, w_s, b_s, w_br_a, w_br_b, w_out):
    assert w_ada.shape[0] == 1, "single layer"
    b, s, _ = x_prompt.shape
    n = x_sample.shape[0]
    assert x_sample.shape[1] == 1
    n_pages = page_table.shape[1]
    past = n_pages * PAGE_SIZE

    w_all = _w_pack_call(w_in[0].T)
    eye = jnp.eye(N_KV, dtype=F32)
    wck = jnp.kron(eye, w_cmp_k[0]).astype(BF16)
    wcv = jnp.kron(eye, w_cmp_v[0]).astype(BF16)
    pek = jnp.tile(cmp_pos_k[0], (1, N_KV))
    pev = jnp.tile(cmp_pos_v[0], (1, N_KV))
    qg = jnp.tile(q_norm_g, (1, LANES // HEAD_DIM))
    kg = jnp.tile(k_norm_g, (1, LANES // HEAD_DIM))
    w_br_a_b, w_br_b_b, w_out_b = w_br_a[0].astype(BF16), w_br_b[0].astype(BF16), w_out[0].astype(BF16)

    mod = _ada_call(jnp.concatenate([c_prompt, c_sample], axis=0), w_ada[0], b_ada)
    shift, scale, gate = mod[:, :D_MODEL], mod[:, D_MODEL:2 * D_MODEL], mod[:, 2 * D_MODEL:]

    ang_p = _rope_angles(jnp.arange(s, dtype=jnp.int32)).T
    gain_cols = lambda g: jnp.broadcast_to(g[0][:, None], (HEAD_DIM, PROMPT_SUB_ROWS))
    (q_t, kc_t, ks_t, kw_t, vc_t, vs_t, vw_t, ks_rows, kw_rows, kcmp, vcmp, gates_t, sza, sga, mb) = _p_proj_call(
        x_prompt, shift[:b, None], scale[:b, None], norm_g, w_all, jnp.cos(ang_p), jnp.sin(ang_p),
        gain_cols(q_norm_g), gain_cols(k_norm_g), pek, pev, wck, wcv,
        vnorm_g, vnorm_b, w_s[0], b_s[0].T, w_br_b_b)
    y_prompt = _p_attn_call(q_t, ks_rows, vs_t, kw_rows, vw_t, kcmp, vcmp, gates_t, sza, sga, mb, x_prompt,
                            gate[:b, None], w_br_a_b, w_out_b)
    wb_p = min(WINDOW, s)
    p_states = [_token_major(a)[None] for a in (kc_t, vc_t, ks_t, vs_t, kw_t[:, :, s - wb_p:], vw_t[:, :, s - wb_p:])]

    xs = x_sample.reshape(n, D_MODEL)
    cos_s, sin_s = _rope_tables(jnp.full((1,), past, jnp.int32))
    ws0 = jnp.repeat(w_s[0, :, 0, 0], GROUP_W_B)[None]
    bs0 = jnp.repeat(b_s[0, :, 0], GROUP_W_B)[None]
    (q_s, k_s, v_s, kt_s, vt_s, gates_s, sza_s, sga_s, mb_s, vn_s) = _s_proj_call(
        xs, shift[b:], scale[b:], norm_g, w_all, cos_s, sin_s, qg, kg, vnorm_g, vnorm_b, ws0, bs0, w_br_b_b)
    pools = [_token_minor(c[0]) for c in (cache_k_cmp, cache_v_cmp, cache_k_slc, cache_v_slc)]
    o_a, kwin_new, vwin_new = _s_attn_call(
        page_table, q_s.reshape(n, N_HEADS, HEAD_DIM), gates_s[:, :3 * N_HEADS].reshape(n, N_HEADS, 3),
        k_s, v_s, kt_s, vt_s, _token_minor(cache_k_win[0]), _token_minor(cache_v_win[0]),
        pek, pev, wck, wcv, *pools)
    y_sample = _s_out_call(o_a.reshape(n, WIDTH_A), sza_s, sga_s, mb_s, xs, gate[b:], w_br_a_b, w_out_b)

    new_rows = lambda t, br: jnp.transpose(t[br].reshape(N_KV, HEAD_DIM, n), (2, 0, 1))[None, :, None]
    s_states = [new_rows(kt_s, 0), new_rows(vt_s, 0), new_rows(kt_s, 1), new_rows(vt_s, 1),
                _token_major(kwin_new)[None], _token_major(vwin_new)[None], vn_s[None, :, None]]
    return (y_prompt, y_sample.reshape(n, 1, D_MODEL), *p_states, *s_states)
```

```python
import itertools

import numpy as np
import jax
import jax.numpy as jnp
from jax import lax
from jax.experimental import pallas as pl
from jax.experimental.pallas import tpu as pltpu

F32 = jnp.float32
BF16 = jnp.bfloat16

D_MODEL = 1024
HEAD_DIM = 64
N_HEADS = 8
N_KV = 2
GQA = N_HEADS // N_KV
WIDTH_A = N_HEADS * HEAD_DIM
KV_WIDTH = N_KV * HEAD_DIM
L_CMP = 32
L_SLC = 64
N_SEL = 8
WINDOW = 512
Q_BLOCK = 256
FORCE_BONUS = 1.0e4
ROPE_THETA = 10000.0
CHUNK = 128
N_GROUPS_B = 4
WIDTH_B = 512
GROUP_W_B = WIDTH_B // N_GROUPS_B
PAGE_SIZE = 128
EPS = 1e-6
NEG = -1e30
SM_SCALE = HEAD_DIM ** -0.5
LOG2_E = 1.4426950408889634

LANES = 128
SUBLANES = 8
VMEM_LIMIT = 56 * 1024 * 1024

C_Q = 0
C_K = C_Q + WIDTH_A
C_V = C_K + 3 * KV_WIDTH
C_G = C_V + 3 * KV_WIDTH
C_ZA = C_G + LANES
C_U = C_ZA + WIDTH_A
C_VB = C_U + WIDTH_B
C_ZB = C_VB + WIDTH_B
C_GA = C_ZB + WIDTH_B
C_GB = C_GA + D_MODEL
C_END = C_GB + D_MODEL

PROMPT_ROWS = 512
PROMPT_SUB_ROWS = 256
SEL_CHUNK = 256
PEN_ROWS = 16
ONES_ROWS = 16
SUMMARY_CHUNK = 512
SAMPLE_SEQS_PER_STEP = 4


def _dot(a, b):
    return jnp.dot(a, b, preferred_element_type=F32)


def _dot_nt(a, b):
    return lax.dot_general(a, b, (((1,), (1,)), ((), ())), preferred_element_type=F32)


def _iota(shape, dim):
    return lax.broadcasted_iota(jnp.int32, shape, dim)


def _split_bf16(x):
    hi = x.astype(BF16)
    lo = (x - hi.astype(F32)).astype(BF16)
    return hi, lo


def _head_mean_sq(x):
    w = x.shape[1]
    ones_bd = jnp.where(_iota((LANES, LANES), 0) // HEAD_DIM == _iota((LANES, LANES), 1) // HEAD_DIM,
                        1.0, 0.0).astype(BF16)
    hi, lo = _split_bf16(x * x)
    cols = []
    for c in range(w // LANES):
        sl = slice(c * LANES, (c + 1) * LANES)
        cols.append(_dot(hi[:, sl], ones_bd) + _dot(lo[:, sl], ones_bd))
    return jnp.concatenate(cols, axis=1) * (1.0 / HEAD_DIM)


def _tile_lanes(t, width):
    return jnp.concatenate([t] * (width // t.shape[1]), axis=1)


def _norm_rope(x, g, cos, sin):
    w = x.shape[1]
    y = x * lax.rsqrt(_head_mean_sq(x) + EPS) * _tile_lanes(g, w)
    first_half = (_iota(y.shape, 1) % HEAD_DIM) < (HEAD_DIM // 2)
    rot = jnp.where(first_half, pltpu.roll(y, w - HEAD_DIM // 2, 1), pltpu.roll(y, HEAD_DIM // 2, 1))
    return y * _tile_lanes(cos, w) + rot * _tile_lanes(sin, w)


def _norm_rope_t(x_t, gain_t, cos_t, sin_t):
    half = HEAD_DIM // 2
    out = []
    for h in range(x_t.shape[0] // HEAD_DIM):
        x = x_t[h * HEAD_DIM:(h + 1) * HEAD_DIM]
        y = x * lax.rsqrt(jnp.mean(x * x, axis=0, keepdims=True) + EPS) * gain_t
        y1, y2 = y[:half], y[half:]
        out += [y1 * cos_t - y2 * sin_t, y2 * cos_t + y1 * sin_t]
    return jnp.concatenate(out, axis=0)


def _silu(z):
    return z * jax.nn.sigmoid(z)


def _project(x, shift, scale, norm_g, w_ref):
    ms = jnp.mean(x * x, axis=-1, keepdims=True)
    h = (x * lax.rsqrt(ms + EPS) * norm_g) * (1.0 + scale) + shift
    hb = h.astype(BF16)
    return lambda lo, hi: _dot(hb, w_ref[:, lo:hi])


def _layer_norm(v, g, b):
    mu = jnp.mean(v, axis=-1, keepdims=True)
    d = v - mu
    var = jnp.mean(d * d, axis=-1, keepdims=True)
    return d * lax.rsqrt(var + EPS) * g + b


def _ada_kernel(c_ref, w_ref, b_ref, o_ref):
    o_ref[...] = _dot(c_ref[...].astype(BF16), w_ref[...].astype(BF16)) + b_ref[...]


W_PACK_COLS = 512
W_GAP_LO = C_G + 3 * N_HEADS
W_GAP_HI = C_G + LANES
W_PACK_BLOCKS = -(-C_END // W_PACK_COLS)


def _w_pack_kernel(wt_ref, o_ref):
    j = pl.program_id(0)
    w = W_PACK_COLS
    gap_block, last = W_GAP_LO // w, W_PACK_BLOCKS - 1
    d_in = C_END - (W_GAP_HI - W_GAP_LO)

    def emit(rows):
        for s in range(w // LANES):
            blk = rows[s * LANES:(s + 1) * LANES]
            o_ref[:, s * LANES:(s + 1) * LANES] = jnp.concatenate(
                [blk[:, c * LANES:(c + 1) * LANES].T for c in range(D_MODEL // LANES)], axis=0).astype(BF16)

    @pl.when((j != gap_block) & (j != last))
    def _():
        emit(wt_ref[...])

    @pl.when(j == gap_block)
    def _():
        k0, k1 = W_GAP_LO - gap_block * w, W_GAP_HI - gap_block * w
        emit(jnp.concatenate([wt_ref[0:k0, :], jnp.zeros((k1 - k0, D_MODEL), F32), wt_ref[k0:k0 + w - k1, :]],
                             axis=0))

    @pl.when(j == last)
    def _():
        shift = (last * w - (W_GAP_HI - W_GAP_LO)) - (d_in - w)
        n_valid = C_END - last * w
        emit(jnp.concatenate([wt_ref[shift:shift + n_valid, :], jnp.zeros((w - n_valid, D_MODEL), F32)], axis=0))


def _w_pack_call(w_t):
    d_in = w_t.shape[0]
    w = W_PACK_COLS
    pad = W_GAP_HI - W_GAP_LO
    gap_block = W_GAP_LO // w
    assert d_in + pad == C_END and W_GAP_HI <= (gap_block + 1) * w and gap_block < W_PACK_BLOCKS - 1
    assert W_GAP_LO % SUBLANES == 0 and pad % SUBLANES == 0
    src_row = lambda j: (pl.multiple_of(
        jnp.minimum(jnp.where(j <= gap_block, j * w, j * w - pad), d_in - w), SUBLANES), 0)
    return pl.pallas_call(
        _w_pack_kernel,
        grid=(W_PACK_BLOCKS,),
        in_specs=[pl.BlockSpec((pl.Element(w), pl.Element(D_MODEL)), src_row)],
        out_specs=pl.BlockSpec((D_MODEL, w), lambda j: (0, j)),
        out_shape=jax.ShapeDtypeStruct((D_MODEL, W_PACK_BLOCKS * w), BF16),
        compiler_params=pltpu.CompilerParams(vmem_limit_bytes=VMEM_LIMIT),
        name="w_pack",
    )(w_t)


def _ada_call(c_all, w_ada, b_ada):
    rows = c_all.shape[0]
    n = w_ada.shape[1]
    return pl.pallas_call(
        _ada_kernel,
        grid=(n // D_MODEL,),
        in_specs=[pl.BlockSpec((rows, D_MODEL), lambda j: (0, 0)),
                  pl.BlockSpec((D_MODEL, D_MODEL), lambda j: (0, j)),
                  pl.BlockSpec((1, D_MODEL), lambda j: (0, j))],
        out_specs=pl.BlockSpec((rows, D_MODEL), lambda j: (0, j)),
        out_shape=jax.ShapeDtypeStruct((rows, n), F32),
        compiler_params=pltpu.CompilerParams(vmem_limit_bytes=VMEM_LIMIT),
        name="ada",
    )(c_all, w_ada, b_ada)


def _compress_rows(rows, pe, w_bd):
    t = rows.shape[0]
    pooled = jnp.sum(rows.reshape(t // L_CMP, L_CMP, KV_WIDTH) + pe[None], axis=1) * (1.0 / L_CMP)
    return _dot(pooled.astype(BF16), w_bd)


def _p_proj_kernel(x_ref, shift_ref, scale_ref, ng_ref, w_ref, cos_ref, sin_ref, qg_ref, kg_ref,
                   pek_ref, pev_ref, wck_ref, wcv_ref, vng_ref, vnb_ref, ws_ref, bst_ref, wbrb_ref,
                   *outs):
    tm = x_ref.shape[1]
    chains = [_p_proj_rows(r0, x_ref, shift_ref, scale_ref, ng_ref, w_ref, cos_ref, sin_ref, qg_ref, kg_ref,
                           pek_ref, pev_ref, wck_ref, wcv_ref, vng_ref, vnb_ref, ws_ref, bst_ref, wbrb_ref, *outs)
              for r0 in range(0, tm, PROMPT_SUB_ROWS)]
    for _ in itertools.zip_longest(*chains):
        pass


def _p_proj_rows(r0, x_ref, shift_ref, scale_ref, ng_ref, w_ref, cos_ref, sin_ref, qg_ref, kg_ref,
                 pek_ref, pev_ref, wck_ref, wcv_ref, vng_ref, vnb_ref, ws_ref, bst_ref, wbrb_ref,
                 qt_out, kc_t, ks_t, kw_t, vc_t, vs_t, vw_t, ks_rows, kw_rows, kcmp_out, vcmp_out,
                 gates_t, sza_out, sga_out, mb_out):
    sub = PROMPT_SUB_ROWS
    rs = slice(r0, r0 + sub)
    cmp_rows = slice(r0 // L_CMP, (r0 + sub) // L_CMP)
    seg = _project(x_ref[0, rs, :], shift_ref[0], scale_ref[0], ng_ref[...], w_ref)
    cos, sin = cos_ref[:, rs], sin_ref[:, rs]
    to_token_minor = lambda a: jnp.concatenate(
        [a[:, c * LANES:(c + 1) * LANES].T for c in range(a.shape[1] // LANES)], axis=0)
    yield

    qt_out[0, :, rs] = (_norm_rope_t(to_token_minor(seg(C_Q, C_K)), qg_ref[...], cos, sin)
                        * (SM_SCALE * LOG2_E)).astype(BF16)
    yield
    k_t = _norm_rope_t(to_token_minor(seg(C_K, C_V)), kg_ref[...], cos, sin)
    yield
    v = seg(C_V, C_G)
    for br, (k_out, v_out) in enumerate(((kc_t, vc_t), (ks_t, vs_t), (kw_t, vw_t))):
        sl = slice(br * KV_WIDTH, (br + 1) * KV_WIDTH)
        k_out[0, :, rs] = k_t[sl]
        v_out[0, :, rs] = v[:, sl].T
    blk_id = jnp.where((_iota((HEAD_DIM, sub), 1) // L_SLC) % (SEL_CHUNK // L_SLC) == _iota((HEAD_DIM, sub), 0),
                       1.0, 0.0)
    ks_g = [k_t[KV_WIDTH + g * HEAD_DIM:KV_WIDTH + (g + 1) * HEAD_DIM] for g in range(N_KV)]
    ks_rows[0, 0, rs, :] = jnp.concatenate([ks_g[0], blk_id], axis=0).T.astype(BF16)
    ks_rows[0, 1, rs, :] = jnp.concatenate([blk_id, ks_g[1]], axis=0).T.astype(BF16)
    kw_rows[0, rs, :] = k_t[2 * KV_WIDTH:3 * KV_WIDTH].T.astype(BF16)
    yield
    kcmp_out[0, cmp_rows, :] = _compress_rows(k_t[0:KV_WIDTH].T, pek_ref[...], wck_ref[...])
    vcmp_out[0, cmp_rows, :] = _compress_rows(v[:, 0:KV_WIDTH], pev_ref[...], wcv_ref[...])
    yield

    gates_t[0, :, rs] = jax.nn.sigmoid(seg(C_G, C_ZA)).T
    sza_out[0, rs, :] = _silu(seg(C_ZA, C_U))
    yield
    sga_out[0, rs, :] = jax.nn.sigmoid(seg(C_GA, C_GB))
    yield

    vn = _layer_norm(seg(C_VB, C_ZB), vng_ref[...], vnb_ref[...]).astype(BF16)
    yield
    causal = _iota((CHUNK, CHUNK), 0) >= _iota((CHUNK, CHUNK), 1)
    chunks = []
    for c in range(sub // CHUNK):
        groups = []
        for g in range(N_GROUPS_B):
            wsg = jnp.where(causal, ws_ref[g], 0.0).astype(BF16)
            vg = vn[c * CHUNK:(c + 1) * CHUNK, g * GROUP_W_B:(g + 1) * GROUP_W_B]
            groups.append(_dot(wsg, vg) + bst_ref[:, g:g + 1])
        chunks.append(jnp.concatenate(groups, axis=1))
    s_b = jnp.concatenate(chunks, axis=0)
    yield
    t = seg(C_U, C_VB) * s_b * _silu(seg(C_ZB, C_GA))
    yield
    mb_out[0, rs, :] = jax.nn.sigmoid(seg(C_GB, C_END)) * _dot(t.astype(BF16), wbrb_ref[...])


def _full(shape):
    nd = len(shape)
    return pl.BlockSpec(shape, lambda *_: (0,) * nd)


def _p_proj_call(x, shift, scale, norm_g, w_all, cos_t, sin_t, qg, kg, pek, pev, wck, wcv, vng, vnb,
                 w_s, bs_t, w_br_b):
    b, s, _ = x.shape
    tm = PROMPT_ROWS
    row = lambda w: pl.BlockSpec((1, tm, w), lambda bi, i: (bi, i, 0))
    tok_minor = pl.BlockSpec((1, KV_WIDTH, tm), lambda bi, i: (bi, 0, i))
    per_batch = pl.BlockSpec((1, 1, D_MODEL), lambda bi, i: (bi, 0, 0))
    cmp_spec = pl.BlockSpec((1, tm // L_CMP, KV_WIDTH), lambda bi, i: (bi, i, 0))
    tab = pl.BlockSpec((HEAD_DIM // 2, tm), lambda bi, i: (0, i))
    in_specs = [row(D_MODEL), per_batch, per_batch, _full(norm_g.shape), _full(w_all.shape), tab, tab,
                _full(qg.shape), _full(kg.shape), _full(pek.shape), _full(pev.shape), _full(wck.shape),
                _full(wcv.shape), _full(vng.shape), _full(vnb.shape), _full(w_s.shape), _full(bs_t.shape),
                _full(w_br_b.shape)]
    qt_spec = pl.BlockSpec((1, WIDTH_A, tm), lambda bi, i: (bi, 0, i))
    group_rows = pl.BlockSpec((1, N_KV, tm, KV_WIDTH), lambda bi, i: (bi, 0, i, 0))
    out_specs = [qt_spec] + [tok_minor] * 6 + [group_rows, row(KV_WIDTH), cmp_spec, cmp_spec,
                                                tok_minor, row(WIDTH_A), row(D_MODEL), row(D_MODEL)]
    tm_shape = jax.ShapeDtypeStruct((b, KV_WIDTH, s), F32)
    rows_shape = jax.ShapeDtypeStruct((b, s, KV_WIDTH), BF16)
    cmp_shape = jax.ShapeDtypeStruct((b, s // L_CMP, KV_WIDTH), F32)
    assert PROMPT_SUB_ROWS % SEL_CHUNK == 0
    out_shape = [jax.ShapeDtypeStruct((b, WIDTH_A, s), BF16)] + [tm_shape] * 6 + [
        jax.ShapeDtypeStruct((b, N_KV, s, KV_WIDTH), BF16), rows_shape, cmp_shape, cmp_shape,
        jax.ShapeDtypeStruct((b, LANES, s), F32), jax.ShapeDtypeStruct((b, s, WIDTH_A), F32),
        jax.ShapeDtypeStruct((b, s, D_MODEL), F32), jax.ShapeDtypeStruct((b, s, D_MODEL), F32)]
    return pl.pallas_call(
        _p_proj_kernel,
        grid=(b, s // tm),
        in_specs=in_specs,
        out_specs=out_specs,
        out_shape=out_shape,
        compiler_params=pltpu.CompilerParams(dimension_semantics=("arbitrary", "arbitrary"),
                                             vmem_limit_bytes=VMEM_LIMIT),
        name="p_proj",
    )(x, shift, scale, norm_g, w_all, cos_t, sin_t, qg, kg, pek, pev, wck, wcv, vng, vnb, w_s, bs_t, w_br_b)


def _select_blocks(imp_c, qblk, n_blocks):
    ratio = L_SLC // L_CMP
    assert ratio == 2
    lane = _iota(imp_c.shape, 1)
    imp = imp_c + pltpu.roll(imp_c, LANES - 1, 1)
    blk = lane // ratio
    forced = jnp.where((blk == 0) | (blk == qblk), 1.0, 0.0)
    score = jnp.where(blk <= qblk, imp + FORCE_BONUS * forced, NEG)
    beats = []
    for j in range(n_blocks):
        vj = score[:, ratio * j:ratio * j + 1]
        earlier = jnp.where(lane > ratio * j, 1.0, 0.0)
        beats.append(jnp.where(vj > score, 1.0, jnp.where(vj == score, earlier, 0.0)))
    while len(beats) > 1:
        beats = [a + b for a, b in zip(beats[0::2], beats[1::2])] + ([beats[-1]] if len(beats) % 2 else [])
    rank = beats[0]
    cand = (lane % ratio == 0) & (lane < ratio * n_blocks)
    return jnp.where(cand & (rank < float(min(N_SEL, n_blocks))), 1.0, 0.0)


def _softmax_rows(s, valid):
    sm = jnp.where(valid, s, NEG)
    e = jnp.exp(sm - jnp.max(sm, axis=-1, keepdims=True))
    return e, 1.0 / jnp.sum(e, axis=-1, keepdims=True)


def _merge_and_project(o_a, sza, sga, mb, x, gate, wbra_ref, wout_ref):
    a = _dot((o_a * sza).astype(BF16), wbra_ref[...])
    m = sga * a + mb
    return x + gate * _dot(m.astype(BF16), wout_ref[...])


def _select_blocks_t(imp, qblk):
    n_blocks = imp.shape[0]
    blk = _iota(imp.shape, 0)
    forced = jnp.where((blk == 0) | (blk == qblk), 1.0, 0.0)
    score = jnp.where(blk <= qblk, imp + FORCE_BONUS * forced, NEG)
    rank = jnp.zeros(imp.shape, F32)
    for j in range(n_blocks):
        vj = score[j:j + 1, :]
        earlier = jnp.where(blk > j, 1.0, 0.0)
        rank = rank + jnp.where(vj > score, 1.0, jnp.where(vj == score, earlier, 0.0))
    return jnp.where(rank < float(min(N_SEL, n_blocks)), 1.0, 0.0)


def _softmax_cols(s):
    e = jnp.exp2(s - jnp.max(s, axis=0, keepdims=True))
    return e, 1.0 / jnp.sum(e, axis=0, keepdims=True)


def _p_attn_kernel(qt_ref, ks_ref, vst_ref, kw_ref, vwt_ref, kc_ref, vc_ref, gt_ref, sza_ref, sga_ref,
                   mb_ref, x_ref, gate_ref, wbra_ref, wout_ref, y_ref, qt_scr, pen_scr, m_scr, acc_scr, s_even, s_odd, sw_scr, mw_scr, oa_scr):
    tq = Q_BLOCK
    step = pl.program_id(0)
    i = lax.rem(jnp.minimum(step, pl.num_programs(0) - 2), ks_ref.shape[2] // tq)

    @pl.when(step == 0)
    def _():
        oa_scr[...] = jnp.zeros(oa_scr.shape, F32)
    n_cmp = kc_ref.shape[1]
    half = n_cmp // 2
    assert L_SLC == 2 * L_CMP and n_cmp <= LANES
    cols = GQA * tq
    qpos1 = i * tq + _iota((1, tq), 1)
    qpos = jnp.concatenate([qpos1] * GQA, axis=1)
    band = WINDOW + tq
    ws = pl.multiple_of(jnp.maximum(i * tq - WINDOW, 0), LANES)
    zeros_q = jnp.zeros((HEAD_DIM, cols), BF16)
    perm = lambda ref: jnp.concatenate([ref[0, pl.ds(0, half, stride=2), :], ref[0, pl.ds(1, half, stride=2), :],
                                        jnp.zeros((LANES - n_cmp, KV_WIDTH), F32)], axis=0)
    kc = perm(kc_ref).astype(BF16)
    vc_t = perm(vc_ref).T.astype(BF16)
    crow = _iota((LANES, cols), 0)
    cblk = 2 * (crow % half) + crow // half
    mc = ((cblk + 1) * L_CMP - 1 <= qpos) & (crow < n_cmp)
    gates_t = gt_ref[0]
    groups = [slice(g * HEAD_DIM, (g + 1) * HEAD_DIM) for g in range(N_KV)]

    for g in range(N_KV):
        qt_g = jnp.concatenate([qt_ref[0, h * HEAD_DIM:(h + 1) * HEAD_DIM, :]
                                for h in range(g * GQA, (g + 1) * GQA)], axis=1)
        qt_scr[g] = jnp.concatenate([qt_g, zeros_q] if g == 0 else [zeros_q, qt_g], axis=0)

    per_chunk = SEL_CHUNK // L_SLC
    assert band % SEL_CHUNK == 0
    w_chunks = [pl.ds(pl.multiple_of(ws + c * SEL_CHUNK, LANES), SEL_CHUNK) for c in range(band // SEL_CHUNK)]

    s_cmp = [_dot(kc, qt_scr[g]) for g in range(N_KV)]
    for c, kd in enumerate(w_chunks):
        k_rows = kw_ref[0, kd, :]
        for g in range(N_KV):
            sw_scr[g, c] = _dot(k_rows, qt_scr[g])

    o_c = []
    for g in range(N_KV):
        s_c = jnp.where(mc, s_cmp[g], NEG)
        e_c, r_c = _softmax_cols(s_c)
        p_c = jnp.where(mc, e_c * r_c, 0.0)
        o_c.append(_dot(vc_t[groups[g]], p_c.astype(BF16)))
        imp = p_c[:, 0:tq]
        for r in range(1, GQA):
            imp = imp + p_c[:, r * tq:(r + 1) * tq]
        sel = _select_blocks_t(imp[0:half] + imp[half:2 * half], qpos1 // L_SLC)
        pen = jnp.concatenate([jnp.where(sel > 0.5, 0.0, NEG)] * GQA, axis=1)
        pad_rows = jnp.zeros((PEN_ROWS - per_chunk, cols), F32)
        for c in range(half // per_chunk):
            pen_scr[g, c] = jnp.concatenate([pen[c * per_chunk:(c + 1) * per_chunk], pad_rows],
                                            axis=0).astype(BF16)

    def selected_scores(g, chunk):
        k_aug = ks_ref[0, g, pl.ds(pl.multiple_of(chunk * SEL_CHUNK, SEL_CHUNK), SEL_CHUNK), :]
        gap = jnp.zeros((HEAD_DIM - PEN_ROWS, cols), BF16)
        q_aug = ([qt_scr[0, 0:HEAD_DIM], pen_scr[0, chunk], gap] if g == 0 else
                 [pen_scr[1, chunk], gap, qt_scr[1, HEAD_DIM:2 * HEAD_DIM]])
        return _dot(k_aug, jnp.concatenate(q_aug, axis=0))

    for g in range(N_KV):
        s_even[g] = selected_scores(g, 0)

    for c in range(len(w_chunks)):
        kwpos = ws + c * SEL_CHUNK + _iota((SEL_CHUNK, tq), 0)
        bias_w = jnp.where((kwpos <= qpos1) & (kwpos > qpos1 - WINDOW), 0.0, NEG)
        bias_w = jnp.concatenate([bias_w] * GQA, axis=1)
        for g in range(N_KV):
            s_w = sw_scr[g, c] + bias_w
            sw_scr[g, c] = s_w
            m_c = jnp.max(s_w, axis=0, keepdims=True)
            mw_scr[g] = m_c if c == 0 else jnp.maximum(mw_scr[g], m_c)

    m_scr[...] = jnp.full(m_scr.shape, NEG, F32)
    acc_scr[...] = jnp.zeros(acc_scr.shape, F32)

    def with_ones(v_t):
        return jnp.concatenate([v_t.astype(BF16), jnp.ones((ONES_ROWS, v_t.shape[1]), BF16)], axis=0)

    def trip(kc_i, src, dst):
        off = pl.multiple_of(kc_i * SEL_CHUNK, SEL_CHUNK)
        for g in range(N_KV):
            if dst is not None:
                dst[g] = selected_scores(g, kc_i + 1)
                s = src[g]
            else:
                causal = jnp.where(off + _iota((SEL_CHUNK, tq), 0) <= qpos1, 0.0, NEG)
                s = src[g] + jnp.concatenate([causal] * GQA, axis=1)
            m_old = m_scr[g]
            m_new = jnp.maximum(m_old, jnp.max(s, axis=0, keepdims=True))
            p = jnp.exp2(s - m_new).astype(BF16)
            m_scr[g] = m_new
            acc_scr[g] = jnp.exp2(m_old - m_new) * acc_scr[g] + _dot(
                with_ones(vst_ref[0, groups[g], pl.ds(off, SEL_CHUNK)]), p)

    def by_parity(kc_i, dst_wanted):
        @pl.when(lax.rem(kc_i, 2) == 0)
        def _():
            trip(kc_i, s_even, s_odd if dst_wanted else None)

        @pl.when(lax.rem(kc_i, 2) == 1)
        def _():
            trip(kc_i, s_odd, s_even if dst_wanted else None)

    def body(kc_i, carry):
        by_parity(kc_i, True)
        return carry

    assert SEL_CHUNK == tq
    lax.fori_loop(0, i, body, 0)
    by_parity(i, False)

    acc_w = [None] * N_KV

    def window_pass2():
        for c, kd in enumerate(w_chunks):
            for g in range(N_KV):
                pv = _dot(with_ones(vwt_ref[0, groups[g], kd]),
                          jnp.exp2(sw_scr[g, c] - mw_scr[g]).astype(BF16))
                acc_w[g] = pv if c == 0 else acc_w[g] + pv
                yield

    def previous_block_output():
        n_blk = 2 * LANES
        t = (oa_scr[...] * sza_ref[0]).astype(BF16)
        a = []
        for n0 in range(0, D_MODEL, n_blk):
            a.append(_dot(t, wbra_ref[:, n0:n0 + n_blk]))
            yield
        m = (sga_ref[0] * jnp.concatenate(a, axis=1) + mb_ref[0]).astype(BF16)
        for n0 in range(0, D_MODEL, n_blk):
            y_ref[0, :, n0:n0 + n_blk] = (x_ref[0, :, n0:n0 + n_blk]
                                          + gate_ref[0][:, n0:n0 + n_blk] * _dot(m, wout_ref[:, n0:n0 + n_blk]))
            yield

    for _ in itertools.zip_longest(window_pass2(), previous_block_output()):
        pass
    pair = []
    for g in range(N_KV):
        o_w = acc_w[g][0:HEAD_DIM] * (1.0 / acc_w[g][HEAD_DIM:HEAD_DIM + 1])
        acc_s = acc_scr[g]
        o_s = acc_s[0:HEAD_DIM] * (1.0 / acc_s[HEAD_DIM:HEAD_DIM + 1])
        for r in range(GQA):
            h = g * GQA + r
            cs = slice(r * tq, (r + 1) * tq)
            pair.append(gates_t[3 * h:3 * h + 1] * o_c[g][:, cs] + gates_t[3 * h + 1:3 * h + 2] * o_s[:, cs]
                        + gates_t[3 * h + 2:3 * h + 3] * o_w[:, cs])
    per_lane = LANES // HEAD_DIM
    oa_scr[...] = jnp.concatenate([jnp.concatenate(pair[j:j + per_lane], axis=0).T
                                   for j in range(0, N_HEADS, per_lane)], axis=1)


def _p_attn_call(q_t, ks_rows, vs_t, kw_rows, vw_t, kc, vc, gates_t, sza, sga, mb, x, gate, w_br_a, w_out):
    b, s, _ = x.shape
    tq = Q_BLOCK
    n_i = s // tq
    n_blocks = b * n_i
    att = lambda j: jnp.minimum(j, n_blocks - 1)
    out = lambda j: jnp.maximum(j - 1, 0)
    row = lambda w: pl.BlockSpec((1, tq, w), lambda j: (out(j) // n_i, out(j) % n_i, 0))
    col = lambda a: pl.BlockSpec((1, a.shape[1], tq), lambda j: (att(j) // n_i, 0, att(j) % n_i))
    seq = lambda a: pl.BlockSpec((1,) + a.shape[1:], lambda j: (att(j) // n_i,) + (0,) * (a.ndim - 1))
    out_seq = lambda a: pl.BlockSpec((1,) + a.shape[1:], lambda j: (out(j) // n_i,) + (0,) * (a.ndim - 1))
    in_specs = [col(q_t), seq(ks_rows), seq(vs_t), seq(kw_rows), seq(vw_t), seq(kc), seq(vc),
                col(gates_t), row(WIDTH_A), row(D_MODEL), row(D_MODEL), row(D_MODEL), out_seq(gate),
                _full(w_br_a.shape), _full(w_out.shape)]
    return pl.pallas_call(
        _p_attn_kernel,
        grid=(n_blocks + 1,),
        in_specs=in_specs,
        out_specs=row(D_MODEL),
        out_shape=jax.ShapeDtypeStruct((b, s, D_MODEL), F32),
        scratch_shapes=[pltpu.VMEM((N_KV, KV_WIDTH, GQA * tq), BF16),
                        pltpu.VMEM((N_KV, s // SEL_CHUNK, PEN_ROWS, GQA * tq), BF16),
                        pltpu.VMEM((N_KV, 1, GQA * tq), F32),
                        pltpu.VMEM((N_KV, HEAD_DIM + ONES_ROWS, GQA * tq), F32),
                        pltpu.VMEM((N_KV, SEL_CHUNK, GQA * tq), F32),
                        pltpu.VMEM((N_KV, SEL_CHUNK, GQA * tq), F32),
                        pltpu.VMEM((N_KV, (WINDOW + tq) // SEL_CHUNK, SEL_CHUNK, GQA * tq), F32),
                        pltpu.VMEM((N_KV, 1, GQA * tq), F32),
                        pltpu.VMEM((tq, WIDTH_A), F32)],
        compiler_params=pltpu.CompilerParams(dimension_semantics=("arbitrary",),
                                             vmem_limit_bytes=VMEM_LIMIT),
        name="p_attn",
    )(q_t, ks_rows, vs_t, kw_rows, vw_t, kc, vc, gates_t, sza, sga, mb, x, gate, w_br_a, w_out)


def _s_proj_kernel(x_ref, shift_ref, scale_ref, ng_ref, w_ref, cos_ref, sin_ref, qg_ref, kg_ref,
                   vng_ref, vnb_ref, ws0_ref, bs0_ref, wbrb_ref,
                   q_out, k_out, v_out, kt_out, vt_out, gates_out, sza_out, sga_out, mb_out, vn_out):
    seg = _project(x_ref[...], shift_ref[...], scale_ref[...], ng_ref[...], w_ref)
    cos, sin = cos_ref[...], sin_ref[...]
    q_out[...] = _norm_rope(seg(C_Q, C_K), qg_ref[...], cos, sin)
    k = _norm_rope(seg(C_K, C_V), kg_ref[...], cos, sin)
    v = seg(C_V, C_G)
    k_out[...] = k
    v_out[...] = v
    for br in range(3):
        sl = slice(br * KV_WIDTH, (br + 1) * KV_WIDTH)
        kt_out[br] = k[:, sl].T
        vt_out[br] = v[:, sl].T
    gates_out[...] = jax.nn.sigmoid(seg(C_G, C_ZA))
    sza_out[...] = _silu(seg(C_ZA, C_U))
    sga_out[...] = jax.nn.sigmoid(seg(C_GA, C_GB))
    vn = _layer_norm(seg(C_VB, C_ZB), vng_ref[...], vnb_ref[...])
    vn_out[...] = vn
    s_b = ws0_ref[...] * vn + bs0_ref[...]
    t = seg(C_U, C_VB) * s_b * _silu(seg(C_ZB, C_GA))
    mb_out[...] = jax.nn.sigmoid(seg(C_GB, C_END)) * _dot(t.astype(BF16), wbrb_ref[...])


def _s_proj_call(x, shift, scale, norm_g, w_all, cos1, sin1, qg, kg, vng, vnb, ws0, bs0, w_br_b):
    n = x.shape[0]
    args = (x, shift, scale, norm_g, w_all, cos1, sin1, qg, kg, vng, vnb, ws0, bs0, w_br_b)
    sds = lambda *shape: jax.ShapeDtypeStruct(shape, F32)
    out_shape = [sds(n, WIDTH_A), sds(n, 3 * KV_WIDTH), sds(n, 3 * KV_WIDTH), sds(3, KV_WIDTH, n),
                 sds(3, KV_WIDTH, n), sds(n, LANES), sds(n, WIDTH_A), sds(n, D_MODEL), sds(n, D_MODEL),
                 sds(n, WIDTH_B)]
    return pl.pallas_call(
        _s_proj_kernel,
        grid=(1,),
        in_specs=[_full(a.shape) for a in args],
        out_specs=[_full(o.shape) for o in out_shape],
        out_shape=out_shape,
        compiler_params=pltpu.CompilerParams(vmem_limit_bytes=VMEM_LIMIT),
        name="s_proj",
    )(*args)


def _s_attn_kernel(pt_ref, q_ref, gates_ref, knew_ref, vnew_ref, knewt_ref, vnewt_ref, kwin_ref, vwin_ref,
                   pek_ref, pev_ref, wck_ref, wcv_ref, pool_ref, expand_ref,
                   kc_hbm, vc_hbm, ks_hbm, vs_hbm,
                   oa_ref, okw_ref, ovw_ref, buf, sem):
    t = pl.program_id(0)
    n_groups = pl.num_programs(0) - 1
    per_step = q_ref.shape[0]
    n_seqs = pt_ref.shape[0]
    n_pages = pt_ref.shape[1]
    past = n_pages * PAGE_SIZE
    caches = (kc_hbm, vc_hbm, ks_hbm, vs_hbm)

    def page_copies(group, slot_):
        return [pltpu.make_async_copy(hbm.at[pt_ref[jnp.minimum(group * per_step + j, n_seqs - 1), p]],
                                      buf.at[slot_, j, c, :, pl.ds(p * PAGE_SIZE, PAGE_SIZE)],
                                      sem.at[slot_, c])
                for j in range(per_step) for c, hbm in enumerate(caches) for p in range(n_pages)]

    @pl.when(t < n_groups)
    def _():
        for cp in page_copies(t, lax.rem(t, 2)):
            cp.start()

    @pl.when(t > 0)
    def _():
        group = t - 1
        slot = lax.rem(group, 2)
        for cp in page_copies(group, slot):
            cp.wait()
        chains = [_s_attn_one(group * per_step + j, j, buf.at[slot, j], q_ref, gates_ref, knew_ref, vnew_ref,
                              knewt_ref, vnewt_ref, kwin_ref, vwin_ref, pek_ref, pev_ref, wck_ref, wcv_ref,
                              pool_ref, expand_ref, oa_ref, okw_ref, ovw_ref, past) for j in range(per_step)]
        for _ in itertools.zip_longest(*chains):
            pass


def _s_attn_one(b, j, buf, q_ref, gates_ref, knew_ref, vnew_ref, knewt_ref, vnewt_ref, kwin_ref, vwin_ref,
                pek_ref, pev_ref, wck_ref, wcv_ref, pool_ref, expand_ref, oa_ref, okw_ref, ovw_ref, past):
    qpos = past

    lane = _iota((KV_WIDTH, knewt_ref.shape[2]), 1)
    col = lambda ref, br: jnp.sum(jnp.where(lane == b, ref[br], 0.0), axis=1, keepdims=True)
    knew = knew_ref[pl.ds(b, 1), :]
    vnew = vnew_ref[pl.ds(b, 1), :]

    hrow = _iota((N_HEADS, 1), 0)
    first_group = hrow < GQA
    by_group = lambda f: jnp.where(first_group, f(0), f(1))
    gl = lambda g: slice(g * HEAD_DIM, (g + 1) * HEAD_DIM)
    qb = q_ref[j].astype(BF16)

    n_cmp = -(-(past + 1) // L_SLC) * L_SLC // L_CMP
    n_slc = n_cmp * L_CMP // L_SLC

    def summaries(c, new_row, pe_ref, w_ref):
        chunk = pool_ref.shape[1]
        sums = []
        for t0 in range(0, past, chunk):
            hi, lo = _split_bf16(buf[c, :, t0:t0 + chunk])
            sums.append(_dot_nt(pool_ref[...], hi) + _dot_nt(pool_ref[...], lo))
        pe_sum = jnp.sum(pe_ref[...], axis=0, keepdims=True)
        pooled = (jnp.concatenate(sums, axis=0) + pe_sum) * (1.0 / L_CMP)
        r = _iota((SUBLANES, KV_WIDTH), 0)
        tail = jnp.where(r == 0, new_row + pe_sum, jnp.where(r == 1, pe_sum, 0.0)) * (1.0 / L_CMP)
        zeros = jnp.zeros((LANES - pooled.shape[0] - SUBLANES, KV_WIDTH), F32)
        return _dot(jnp.concatenate([pooled, tail, zeros], axis=0).astype(BF16), w_ref[...])

    kc = summaries(0, knew[:, 0:KV_WIDTH], pek_ref, wck_ref).astype(BF16)
    yield
    vc = summaries(1, vnew[:, 0:KV_WIDTH], pev_ref, wcv_ref).astype(BF16)
    yield
    s_c = by_group(lambda g: _dot_nt(qb, kc[:, gl(g)])) * SM_SCALE
    c = _iota(s_c.shape, 1)
    mc = ((c + 1) * L_CMP - 1 <= qpos) & (c < n_cmp)
    e_c, r_c = _softmax_rows(s_c, mc)
    p_c = jnp.where(mc, e_c * r_c, 0.0)
    o_c = by_group(lambda g: _dot(p_c.astype(BF16), vc[:, gl(g)]))
    imp = by_group(lambda g: jnp.sum(p_c[g * GQA:(g + 1) * GQA], axis=0, keepdims=True))
    imp = jnp.broadcast_to(imp, p_c.shape)
    yield
    sel = _select_blocks(imp, jnp.full((N_HEADS, 1), qpos // L_SLC, jnp.int32), n_slc)
    yield

    picked = _dot(sel.astype(BF16), expand_ref[...])
    s_s = by_group(lambda g: _dot(qb, buf[2, gl(g), :].astype(BF16))) * SM_SCALE
    kpos = _iota(s_s.shape, 1)
    s_s = jnp.where((picked > 0.5) & (kpos <= qpos), s_s, NEG)
    yield
    own =_iota((N_HEADS, KV_WIDTH), 1) // HEAD_DIM == hrow // GQA
    rounded = lambda a: a.astype(BF16).astype(F32)
    q_pair = jnp.concatenate([qb.astype(F32)] * N_KV, axis=1)
    s_new = jnp.sum(jnp.where(own, q_pair * rounded(knew[:, KV_WIDTH:2 * KV_WIDTH]), 0.0),
                    axis=1, keepdims=True) * SM_SCALE
    new_lane = (L_SLC // L_CMP) * (past // L_SLC)
    s_new = jnp.where((sel[:, new_lane:new_lane + 1] > 0.5) & (past <= qpos), s_new, NEG)
    m_s = jnp.maximum(jnp.max(s_s, axis=-1, keepdims=True), s_new)
    e_s, e_new = jnp.exp(s_s - m_s), jnp.exp(s_new - m_s)
    r_s = 1.0 / (jnp.sum(e_s, axis=-1, keepdims=True) + e_new)
    v_new = by_group(lambda g: rounded(vnew[:, KV_WIDTH + g * HEAD_DIM:KV_WIDTH + (g + 1) * HEAD_DIM]))
    o_s = (by_group(lambda g: _dot_nt(e_s.astype(BF16), buf[3, gl(g), :].astype(BF16)))
           + rounded(e_new) * v_new) * r_s
    yield

    wb = kwin_ref.shape[2]
    wlane = _iota((KV_WIDTH, wb), 1)
    kw = jnp.where(wlane == wb - 1, col(knewt_ref, 2), pltpu.roll(kwin_ref[j], wb - 1, 1))
    vw = jnp.where(wlane == wb - 1, col(vnewt_ref, 2), pltpu.roll(vwin_ref[j], wb - 1, 1))
    okw_ref[j] = kw
    ovw_ref[j] = vw
    yield
    s_w = by_group(lambda g: _dot(qb, kw[gl(g)].astype(BF16))) * SM_SCALE
    kwpos = past - wb + 1 + _iota(s_w.shape, 1)
    e_w, r_w = _softmax_rows(s_w, (kwpos <= qpos) & (kwpos > qpos - WINDOW) & (kwpos >= 0))
    o_w = by_group(lambda g: _dot_nt(e_w.astype(BF16), vw[gl(g)].astype(BF16))) * r_w

    gates = gates_ref[j]
    oa_ref[j] = gates[:, 0:1] * o_c + gates[:, 1:2] * o_s + gates[:, 2:3] * o_w


def _s_attn_call(page_table, q3, gates3, knew, vnew, knew_t, vnew_t, kwin_t, vwin_t, pek, pev, wck, wcv,
                 kc_pool, vc_pool, ks_pool, vs_pool):
    n, n_pages = page_table.shape
    past = n_pages * PAGE_SIZE
    wb = kwin_t.shape[2]
    tok = np.arange(SUMMARY_CHUNK)
    assert past % SUMMARY_CHUNK == 0
    pool = jnp.asarray((tok[None, :] // L_CMP == np.arange(SUMMARY_CHUNK // L_CMP)[:, None]), BF16)
    expand = jnp.asarray(np.arange(LANES)[:, None] == 2 * (np.arange(past)[None, :] // L_SLC), BF16)
    k = SAMPLE_SEQS_PER_STEP
    assert n % k == 0
    per_seq = lambda a: pl.BlockSpec((k,) + a.shape[1:],
                                     lambda t, pt: (jnp.maximum(t - 1, 0),) + (0,) * (a.ndim - 1))
    full = lambda a: pl.BlockSpec(a.shape, lambda t, pt: (0,) * a.ndim)
    hbm = pl.BlockSpec(memory_space=pl.ANY)
    resident = (knew, vnew, knew_t, vnew_t)
    consts = (pek, pev, wck, wcv, pool, expand)
    grid_spec = pltpu.PrefetchScalarGridSpec(
        num_scalar_prefetch=1,
        grid=(n // k + 1,),
        in_specs=[per_seq(q3), per_seq(gates3)] + [full(a) for a in resident]
                 + [per_seq(kwin_t), per_seq(vwin_t)] + [full(a) for a in consts] + [hbm] * 4,
        out_specs=[per_seq(q3), per_seq(kwin_t), per_seq(vwin_t)],
        scratch_shapes=[pltpu.VMEM((2, k, 4, KV_WIDTH, past), F32), pltpu.SemaphoreType.DMA((2, 4))],
    )
    return pl.pallas_call(
        _s_attn_kernel,
        grid_spec=grid_spec,
        out_shape=[jax.ShapeDtypeStruct(q3.shape, F32), jax.ShapeDtypeStruct(kwin_t.shape, F32),
                   jax.ShapeDtypeStruct(vwin_t.shape, F32)],
        compiler_params=pltpu.CompilerParams(dimension_semantics=("arbitrary",),
                                             vmem_limit_bytes=VMEM_LIMIT),
        name="s_attn",
    )(page_table, q3, gates3, knew, vnew, knew_t, vnew_t, kwin_t, vwin_t, pek, pev, wck, wcv, pool, expand,
      kc_pool, vc_pool, ks_pool, vs_pool)


def _s_out_kernel(oa_ref, sza_ref, sga_ref, mb_ref, x_ref, gate_ref, wbra_ref, wout_ref, y_ref):
    y_ref[...] = _merge_and_project(oa_ref[...], sza_ref[...], sga_ref[...], mb_ref[...], x_ref[...],
                                    gate_ref[...], wbra_ref, wout_ref)


def _s_out_call(o_a, sza, sga, mb, x, gate, w_br_a, w_out):
    args = (o_a, sza, sga, mb, x, gate, w_br_a, w_out)
    return pl.pallas_call(
        _s_out_kernel,
        grid=(1,),
        in_specs=[_full(a.shape) for a in args],
        out_specs=_full(x.shape),
        out_shape=jax.ShapeDtypeStruct(x.shape, F32),
        compiler_params=pltpu.CompilerParams(vmem_limit_bytes=VMEM_LIMIT),
        name="s_out",
    )(*args)


def _rope_angles(pos):
    half = HEAD_DIM // 2
    inv = ROPE_THETA ** (-jnp.arange(half, dtype=F32) * 2.0 / HEAD_DIM)
    return pos.astype(F32)[:, None] * inv[None, :]


def _rope_tables(pos):
    ang = _rope_angles(pos)
    cos, sin = jnp.cos(ang), jnp.sin(ang)
    cos_t = jnp.concatenate([cos, cos] * (LANES // HEAD_DIM), axis=1)
    sin_t = jnp.concatenate([-sin, sin] * (LANES // HEAD_DIM), axis=1)
    return cos_t, sin_t


def _token_minor(a):
    b, t = a.shape[:2]
    return jnp.transpose(a, (0, 2, 3, 1)).reshape(b, KV_WIDTH, t)


def _token_major(a_t):
    b, _, t = a_t.shape
    return jnp.transpose(a_t.reshape(b, N_KV, HEAD_DIM, t), (0, 3, 1, 2))


def kernel(x_prompt, x_sample, cache_k_cmp, cache_v_cmp, cache_k_slc, cache_v_slc, cache_k_win, cache_v_win, page_table, c_prompt, c_sample, w_ada, b_ada, norm_g, w_in, q_norm_g, k_norm_g, cmp_pos_k, cmp_pos_v, w_cmp_k, w_cmp_v, vnorm_g, vnorm_b, w_s, b_s, w_br_a, w_br_b, w_out):
    assert w_ada.shape[0] == 1, "single layer"
    b, s, _ = x_prompt.shape
    n = x_sample.shape[0]
    assert x_sample.shape[1] == 1
    n_pages = page_table.shape[1]
    past = n_pages * PAGE_SIZE

    w_all = _w_pack_call(w_in[0].T)
    eye = jnp.eye(N_KV, dtype=F32)
    wck = jnp.kron(eye, w_cmp_k[0]).astype(BF16)
    wcv = jnp.kron(eye, w_cmp_v[0]).astype(BF16)
    pek = jnp.tile(cmp_pos_k[0], (1, N_KV))
    pev = jnp.tile(cmp_pos_v[0], (1, N_KV))
    qg = jnp.tile(q_norm_g, (1, LANES // HEAD_DIM))
    kg = jnp.tile(k_norm_g, (1, LANES // HEAD_DIM))
    w_br_a_b, w_br_b_b, w_out_b = w_br_a[0].astype(BF16), w_br_b[0].astype(BF16), w_out[0].astype(BF16)

    mod = _ada_call(jnp.concatenate([c_prompt, c_sample], axis=0), w_ada[0], b_ada)
    shift, scale, gate = mod[:, :D_MODEL], mod[:, D_MODEL:2 * D_MODEL], mod[:, 2 * D_MODEL:]

    ang_p = _rope_angles(jnp.arange(s, dtype=jnp.int32)).T
    gain_cols = lambda g: jnp.broadcast_to(g[0][:, None], (HEAD_DIM, PROMPT_SUB_ROWS))
    (q_t, kc_t, ks_t, kw_t, vc_t, vs_t, vw_t, ks_rows, kw_rows, kcmp, vcmp, gates_t, sza, sga, mb) = _p_proj_call(
        x_prompt, shift[:b, None], scale[:b, None], norm_g, w_all, jnp.cos(ang_p), jnp.sin(ang_p),
        gain_cols(q_norm_g), gain_cols(k_norm_g), pek, pev, wck, wcv,
        vnorm_g, vnorm_b, w_s[0], b_s[0].T, w_br_b_b)
    y_prompt = _p_attn_call(q_t, ks_rows, vs_t, kw_rows, vw_t, kcmp, vcmp, gates_t, sza, sga, mb, x_prompt,
                            gate[:b, None], w_br_a_b, w_out_b)
    wb_p = min(WINDOW, s)
    p_states = [_token_major(a)[None] for a in (kc_t, vc_t, ks_t, vs_t, kw_t[:, :, s - wb_p:], vw_t[:, :, s - wb_p:])]

    xs = x_sample.reshape(n, D_MODEL)
    cos_s, sin_s = _rope_tables(jnp.full((1,), past, jnp.int32))
    ws0 = jnp.repeat(w_s[0, :, 0, 0], GROUP_W_B)[None]
    bs0 = jnp.repeat(b_s[0, :, 0], GROUP_W_B)[None]
    (q_s, k_s, v_s, kt_s, vt_s, gates_s, sza_s, sga_s, mb_s, vn_s) = _s_proj_call(
        xs, shift[b:], scale[b:], norm_g, w_all, cos_s, sin_s, qg, kg, vnorm_g, vnorm_b, ws0, bs0, w_br_b_b)
    pools = [_token_minor(c[0]) for c in (cache_k_cmp, cache_v_cmp, cache_k_slc, cache_v_slc)]
    o_a, kwin_new, vwin_new = _s_attn_call(
        page_table, q_s.reshape(n, N_HEADS, HEAD_DIM), gates_s[:, :3 * N_HEADS].reshape(n, N_HEADS, 3),
        k_s, v_s, kt_s, vt_s, _token_minor(cache_k_win[0]), _token_minor(cache_v_win[0]),
        pek, pev, wck, wcv, *pools)
    y_sample = _s_out_call(o_a.reshape(n, WIDTH_A), sza_s, sga_s, mb_s, xs, gate[b:], w_br_a_b, w_out_b)

    new_rows = lambda t, br: jnp.transpose(t[br].reshape(N_KV, HEAD_DIM, n), (2, 0, 1))[None, :, None]
    s_states = [new_rows(kt_s, 0), new_rows(vt_s, 0), new_rows(kt_s, 1), new_rows(vt_s, 1),
                _token_major(kwin_new)[None], _token_major(vwin_new)[None], vn_s[None, :, None]]
    return (y_prompt, y_sample.reshape(n, 1, D_MODEL), *p_states, *s_states)
```

```python
import itertools

import numpy as np
import jax
import jax.numpy as jnp
from jax import lax
from jax.experimental import pallas as pl
from jax.experimental.pallas import tpu as pltpu

F32 = jnp.float32
BF16 = jnp.bfloat16

D_MODEL = 1024
HEAD_DIM = 64
N_HEADS = 8
N_KV = 2
GQA = N_HEADS // N_KV
WIDTH_A = N_HEADS * HEAD_DIM
KV_WIDTH = N_KV * HEAD_DIM
L_CMP = 32
L_SLC = 64
N_SEL = 8
WINDOW = 512
Q_BLOCK = 256
FORCE_BONUS = 1.0e4
ROPE_THETA = 10000.0
CHUNK = 128
N_GROUPS_B = 4
WIDTH_B = 512
GROUP_W_B = WIDTH_B // N_GROUPS_B
PAGE_SIZE = 128
EPS = 1e-6
NEG = -1e30
SM_SCALE = HEAD_DIM ** -0.5
LOG2_E = 1.4426950408889634

LANES = 128
SUBLANES = 8
VMEM_LIMIT = 56 * 1024 * 1024

C_Q = 0
C_K = C_Q + WIDTH_A
C_V = C_K + 3 * KV_WIDTH
C_G = C_V + 3 * KV_WIDTH
C_ZA = C_G + LANES
C_U = C_ZA + WIDTH_A
C_VB = C_U + WIDTH_B
C_ZB = C_VB + WIDTH_B
C_GA = C_ZB + WIDTH_B
C_GB = C_GA + D_MODEL
C_END = C_GB + D_MODEL

PROMPT_ROWS = 512
PROMPT_SUB_ROWS = 256
SEL_CHUNK = 256
PEN_ROWS = 16
ONES_ROWS = 16
SUMMARY_CHUNK = 512
SAMPLE_SEQS_PER_STEP = 4


def _dot(a, b):
    return jnp.dot(a, b, preferred_element_type=F32)


def _dot_nt(a, b):
    return lax.dot_general(a, b, (((1,), (1,)), ((), ())), preferred_element_type=F32)


def _iota(shape, dim):
    return lax.broadcasted_iota(jnp.int32, shape, dim)


def _split_bf16(x):
    hi = x.astype(BF16)
    lo = (x - hi.astype(F32)).astype(BF16)
    return hi, lo


def _head_mean_sq(x):
    w = x.shape[1]
    ones_bd = jnp.where(_iota((LANES, LANES), 0) // HEAD_DIM == _iota((LANES, LANES), 1) // HEAD_DIM,
                        1.0, 0.0).astype(BF16)
    hi, lo = _split_bf16(x * x)
    cols = []
    for c in range(w // LANES):
        sl = slice(c * LANES, (c + 1) * LANES)
        cols.append(_dot(hi[:, sl], ones_bd) + _dot(lo[:, sl], ones_bd))
    return jnp.concatenate(cols, axis=1) * (1.0 / HEAD_DIM)


def _tile_lanes(t, width):
    return jnp.concatenate([t] * (width // t.shape[1]), axis=1)


def _norm_rope(x, g, cos, sin):
    w = x.shape[1]
    y = x * lax.rsqrt(_head_mean_sq(x) + EPS) * _tile_lanes(g, w)
    first_half = (_iota(y.shape, 1) % HEAD_DIM) < (HEAD_DIM // 2)
    rot = jnp.where(first_half, pltpu.roll(y, w - HEAD_DIM // 2, 1), pltpu.roll(y, HEAD_DIM // 2, 1))
    return y * _tile_lanes(cos, w) + rot * _tile_lanes(sin, w)


def _norm_rope_t(x_t, gain_t, cos_t, sin_t):
    half = HEAD_DIM // 2
    out = []
    for h in range(x_t.shape[0] // HEAD_DIM):
        x = x_t[h * HEAD_DIM:(h + 1) * HEAD_DIM]
        y = x * lax.rsqrt(jnp.mean(x * x, axis=0, keepdims=True) + EPS) * gain_t
        y1, y2 = y[:half], y[half:]
        out += [y1 * cos_t - y2 * sin_t, y2 * cos_t + y1 * sin_t]
    return jnp.concatenate(out, axis=0)


def _silu(z):
    return z * jax.nn.sigmoid(z)


def _project(x, shift, scale, norm_g, w_ref):
    ms = jnp.mean(x * x, axis=-1, keepdims=True)
    h = (x * lax.rsqrt(ms + EPS) * norm_g) * (1.0 + scale) + shift
    hb = h.astype(BF16)
    return lambda lo, hi: _dot(hb, w_ref[:, lo:hi])


def _layer_norm(v, g, b):
    mu = jnp.mean(v, axis=-1, keepdims=True)
    d = v - mu
    var = jnp.mean(d * d, axis=-1, keepdims=True)
    return d * lax.rsqrt(var + EPS) * g + b


def _ada_kernel(c_ref, w_ref, b_ref, o_ref):
    o_ref[...] = _dot(c_ref[...].astype(BF16), w_ref[...].astype(BF16)) + b_ref[...]


W_PACK_COLS = 512
W_GAP_LO = C_G + 3 * N_HEADS
W_GAP_HI = C_G + LANES
W_PACK_BLOCKS = -(-C_END // W_PACK_COLS)


def _w_pack_kernel(wt_ref, o_ref):
    j = pl.program_id(0)
    w = W_PACK_COLS
    gap_block, last = W_GAP_LO // w, W_PACK_BLOCKS - 1
    d_in = C_END - (W_GAP_HI - W_GAP_LO)

    def emit(rows):
        for s in range(w // LANES):
            blk = rows[s * LANES:(s + 1) * LANES]
            o_ref[:, s * LANES:(s + 1) * LANES] = jnp.concatenate(
                [blk[:, c * LANES:(c + 1) * LANES].T for c in range(D_MODEL // LANES)], axis=0).astype(BF16)

    @pl.when((j != gap_block) & (j != last))
    def _():
        emit(wt_ref[...])

    @pl.when(j == gap_block)
    def _():
        k0, k1 = W_GAP_LO - gap_block * w, W_GAP_HI - gap_block * w
        emit(jnp.concatenate([wt_ref[0:k0, :], jnp.zeros((k1 - k0, D_MODEL), F32), wt_ref[k0:k0 + w - k1, :]],
                             axis=0))

    @pl.when(j == last)
    def _():
        shift = (last * w - (W_GAP_HI - W_GAP_LO)) - (d_in - w)
        n_valid = C_END - last * w
        emit(jnp.concatenate([wt_ref[shift:shift + n_valid, :], jnp.zeros((w - n_valid, D_MODEL), F32)], axis=0))


def _w_pack_call(w_t):
    d_in = w_t.shape[0]
    w = W_PACK_COLS
    pad = W_GAP_HI - W_GAP_LO
    gap_block = W_GAP_LO // w
    assert d_in + pad == C_END and W_GAP_HI <= (gap_block + 1) * w and gap_block < W_PACK_BLOCKS - 1
    assert W_GAP_LO % SUBLANES == 0 and pad % SUBLANES == 0
    src_row = lambda j: (pl.multiple_of(
        jnp.minimum(jnp.where(j <= gap_block, j * w, j * w - pad), d_in - w), SUBLANES), 0)
    return pl.pallas_call(
        _w_pack_kernel,
        grid=(W_PACK_BLOCKS,),
        in_specs=[pl.BlockSpec((pl.Element(w), pl.Element(D_MODEL)), src_row)],
        out_specs=pl.BlockSpec((D_MODEL, w), lambda j: (0, j)),
        out_shape=jax.ShapeDtypeStruct((D_MODEL, W_PACK_BLOCKS * w), BF16),
        compiler_params=pltpu.CompilerParams(vmem_limit_bytes=VMEM_LIMIT),
        name="w_pack",
    )(w_t)


def _ada_call(c_all, w_ada, b_ada):
    rows = c_all.shape[0]
    n = w_ada.shape[1]
    return pl.pallas_call(
        _ada_kernel,
        grid=(n // D_MODEL,),
        in_specs=[pl.BlockSpec((rows, D_MODEL), lambda j: (0, 0)),
                  pl.BlockSpec((D_MODEL, D_MODEL), lambda j: (0, j)),
                  pl.BlockSpec((1, D_MODEL), lambda j: (0, j))],
        out_specs=pl.BlockSpec((rows, D_MODEL), lambda j: (0, j)),
        out_shape=jax.ShapeDtypeStruct((rows, n), F32),
        compiler_params=pltpu.CompilerParams(vmem_limit_bytes=VMEM_LIMIT),
        name="ada",
    )(c_all, w_ada, b_ada)


def _compress_rows(rows, pe, w_bd):
    t = rows.shape[0]
    pooled = jnp.sum(rows.reshape(t // L_CMP, L_CMP, KV_WIDTH) + pe[None], axis=1) * (1.0 / L_CMP)
    return _dot(pooled.astype(BF16), w_bd)


def _p_proj_kernel(x_ref, shift_ref, scale_ref, ng_ref, w_ref, cos_ref, sin_ref, qg_ref, kg_ref,
                   pek_ref, pev_ref, wck_ref, wcv_ref, vng_ref, vnb_ref, ws_ref, bst_ref, wbrb_ref,
                   *outs):
    tm = x_ref.shape[1]
    chains = [_p_proj_rows(r0, x_ref, shift_ref, scale_ref, ng_ref, w_ref, cos_ref, sin_ref, qg_ref, kg_ref,
                           pek_ref, pev_ref, wck_ref, wcv_ref, vng_ref, vnb_ref, ws_ref, bst_ref, wbrb_ref, *outs)
              for r0 in range(0, tm, PROMPT_SUB_ROWS)]
    for _ in itertools.zip_longest(*chains):
        pass


def _p_proj_rows(r0, x_ref, shift_ref, scale_ref, ng_ref, w_ref, cos_ref, sin_ref, qg_ref, kg_ref,
                 pek_ref, pev_ref, wck_ref, wcv_ref, vng_ref, vnb_ref, ws_ref, bst_ref, wbrb_ref,
                 qt_out, kc_t, ks_t, vc_t, vs_t, vw_t, ks_rows, kw_rows, kcmp_out, vcmp_out,
                 gates_t, sza_out, sga_out, mb_out, kw_last, vw_last):
    sub = PROMPT_SUB_ROWS
    rs = slice(r0, r0 + sub)
    cmp_rows = slice(r0 // L_CMP, (r0 + sub) // L_CMP)
    seg = _project(x_ref[0, rs, :], shift_ref[0], scale_ref[0], ng_ref[...], w_ref)
    cos, sin = cos_ref[:, rs], sin_ref[:, rs]
    to_token_minor = lambda a: jnp.concatenate(
        [a[:, c * LANES:(c + 1) * LANES].T for c in range(a.shape[1] // LANES)], axis=0)
    yield

    qt_out[0, :, rs] = (_norm_rope_t(to_token_minor(seg(C_Q, C_K)), qg_ref[...], cos, sin)
                        * (SM_SCALE * LOG2_E)).astype(BF16)
    yield
    k_t = _norm_rope_t(to_token_minor(seg(C_K, C_V)), kg_ref[...], cos, sin)
    yield
    v = seg(C_V, C_G)
    for br, (k_out, v_out) in enumerate(((kc_t, vc_t), (ks_t, vs_t), (kw_last, vw_t))):
        sl = slice(br * KV_WIDTH, (br + 1) * KV_WIDTH)
        k_out[0, :, rs] = k_t[sl]
        v_t = v[:, sl].T
        v_out[0, :, rs] = v_t
    vw_last[0, :, rs] = v_t
    blk_id = jnp.where((_iota((HEAD_DIM, sub), 1) // L_SLC) % (SEL_CHUNK // L_SLC) == _iota((HEAD_DIM, sub), 0),
                       1.0, 0.0)
    ks_g = [k_t[KV_WIDTH + g * HEAD_DIM:KV_WIDTH + (g + 1) * HEAD_DIM] for g in range(N_KV)]
    ks_rows[0, 0, rs, :] = jnp.concatenate([ks_g[0], blk_id], axis=0).T.astype(BF16)
    ks_rows[0, 1, rs, :] = jnp.concatenate([blk_id, ks_g[1]], axis=0).T.astype(BF16)
    kw_rows[0, rs, :] = k_t[2 * KV_WIDTH:3 * KV_WIDTH].T.astype(BF16)
    yield
    kcmp_out[0, cmp_rows, :] = _compress_rows(k_t[0:KV_WIDTH].T, pek_ref[...], wck_ref[...])
    vcmp_out[0, cmp_rows, :] = _compress_rows(v[:, 0:KV_WIDTH], pev_ref[...], wcv_ref[...])
    yield

    gates_t[0, :, rs] = jax.nn.sigmoid(seg(C_G, C_ZA)).T
    sza_out[0, rs, :] = _silu(seg(C_ZA, C_U))
    yield
    sga_out[0, rs, :] = jax.nn.sigmoid(seg(C_GA, C_GB))
    yield

    vn = _layer_norm(seg(C_VB, C_ZB), vng_ref[...], vnb_ref[...]).astype(BF16)
    yield
    causal = _iota((CHUNK, CHUNK), 0) >= _iota((CHUNK, CHUNK), 1)
    chunks = []
    for c in range(sub // CHUNK):
        groups = []
        for g in range(N_GROUPS_B):
            wsg = jnp.where(causal, ws_ref[g], 0.0).astype(BF16)
            vg = vn[c * CHUNK:(c + 1) * CHUNK, g * GROUP_W_B:(g + 1) * GROUP_W_B]
            groups.append(_dot(wsg, vg) + bst_ref[:, g:g + 1])
        chunks.append(jnp.concatenate(groups, axis=1))
    s_b = jnp.concatenate(chunks, axis=0)
    yield
    t = seg(C_U, C_VB) * s_b * _silu(seg(C_ZB, C_GA))
    yield
    mb_out[0, rs, :] = jax.nn.sigmoid(seg(C_GB, C_END)) * _dot(t.astype(BF16), wbrb_ref[...])


def _full(shape):
    nd = len(shape)
    return pl.BlockSpec(shape, lambda *_: (0,) * nd)


def _p_proj_call(x, shift, scale, norm_g, w_all, cos_t, sin_t, qg, kg, pek, pev, wck, wcv, vng, vnb,
                 w_s, bs_t, w_br_b):
    b, s, _ = x.shape
    tm = PROMPT_ROWS
    row = lambda w: pl.BlockSpec((1, tm, w), lambda bi, i: (bi, i, 0))
    tok_minor = pl.BlockSpec((1, KV_WIDTH, tm), lambda bi, i: (bi, 0, i))
    per_batch = pl.BlockSpec((1, 1, D_MODEL), lambda bi, i: (bi, 0, 0))
    cmp_spec = pl.BlockSpec((1, tm // L_CMP, KV_WIDTH), lambda bi, i: (bi, i, 0))
    tab = pl.BlockSpec((HEAD_DIM // 2, tm), lambda bi, i: (0, i))
    in_specs = [row(D_MODEL), per_batch, per_batch, _full(norm_g.shape), _full(w_all.shape), tab, tab,
                _full(qg.shape), _full(kg.shape), _full(pek.shape), _full(pev.shape), _full(wck.shape),
                _full(wcv.shape), _full(vng.shape), _full(vnb.shape), _full(w_s.shape), _full(bs_t.shape),
                _full(w_br_b.shape)]
    qt_spec = pl.BlockSpec((1, WIDTH_A, tm), lambda bi, i: (bi, 0, i))
    group_rows = pl.BlockSpec((1, N_KV, tm, KV_WIDTH), lambda bi, i: (bi, 0, i, 0))
    assert min(WINDOW, s) == tm
    last_tile = pl.BlockSpec((1, KV_WIDTH, tm), lambda bi, i: (bi, 0, 0))
    out_specs = [qt_spec] + [tok_minor] * 5 + [group_rows, row(KV_WIDTH), cmp_spec, cmp_spec,
                                                tok_minor, row(WIDTH_A), row(D_MODEL), row(D_MODEL),
                                                last_tile, last_tile]
    tm_shape = jax.ShapeDtypeStruct((b, KV_WIDTH, s), F32)
    rows_shape = jax.ShapeDtypeStruct((b, s, KV_WIDTH), BF16)
    cmp_shape = jax.ShapeDtypeStruct((b, s // L_CMP, KV_WIDTH), F32)
    assert PROMPT_SUB_ROWS % SEL_CHUNK == 0
    out_shape = [jax.ShapeDtypeStruct((b, WIDTH_A, s), BF16)] + [tm_shape] * 5 + [
        jax.ShapeDtypeStruct((b, N_KV, s, KV_WIDTH), BF16), rows_shape, cmp_shape, cmp_shape,
        jax.ShapeDtypeStruct((b, LANES, s), F32), jax.ShapeDtypeStruct((b, s, WIDTH_A), F32),
        jax.ShapeDtypeStruct((b, s, D_MODEL), F32), jax.ShapeDtypeStruct((b, s, D_MODEL), F32),
        jax.ShapeDtypeStruct((b, KV_WIDTH, tm), F32), jax.ShapeDtypeStruct((b, KV_WIDTH, tm), F32)]
    return pl.pallas_call(
        _p_proj_kernel,
        grid=(b, s // tm),
        in_specs=in_specs,
        out_specs=out_specs,
        out_shape=out_shape,
        compiler_params=pltpu.CompilerParams(dimension_semantics=("arbitrary", "arbitrary"),
                                             vmem_limit_bytes=VMEM_LIMIT),
        name="p_proj",
    )(x, shift, scale, norm_g, w_all, cos_t, sin_t, qg, kg, pek, pev, wck, wcv, vng, vnb, w_s, bs_t, w_br_b)


def _select_blocks(imp_c, qblk, n_blocks):
    ratio = L_SLC // L_CMP
    assert ratio == 2
    lane = _iota(imp_c.shape, 1)
    imp = imp_c + pltpu.roll(imp_c, LANES - 1, 1)
    blk = lane // ratio
    forced = jnp.where((blk == 0) | (blk == qblk), 1.0, 0.0)
    score = jnp.where(blk <= qblk, imp + FORCE_BONUS * forced, NEG)
    beats = []
    for j in range(n_blocks):
        vj = score[:, ratio * j:ratio * j + 1]
        earlier = jnp.where(lane > ratio * j, 1.0, 0.0)
        beats.append(jnp.where(vj > score, 1.0, jnp.where(vj == score, earlier, 0.0)))
    while len(beats) > 1:
        beats = [a + b for a, b in zip(beats[0::2], beats[1::2])] + ([beats[-1]] if len(beats) % 2 else [])
    rank = beats[0]
    cand = (lane % ratio == 0) & (lane < ratio * n_blocks)
    return jnp.where(cand & (rank < float(min(N_SEL, n_blocks))), 1.0, 0.0)


def _softmax_rows(s, valid):
    sm = jnp.where(valid, s, NEG)
    e = jnp.exp(sm - jnp.max(sm, axis=-1, keepdims=True))
    return e, 1.0 / jnp.sum(e, axis=-1, keepdims=True)


def _merge_and_project(o_a, sza, sga, mb, x, gate, wbra_ref, wout_ref):
    a = _dot((o_a * sza).astype(BF16), wbra_ref[...])
    m = sga * a + mb
    return x + gate * _dot(m.astype(BF16), wout_ref[...])


def _select_blocks_t(imp, qblk):
    n_blocks = imp.shape[0]
    blk = _iota(imp.shape, 0)
    forced = jnp.where((blk == 0) | (blk == qblk), 1.0, 0.0)
    score = jnp.where(blk <= qblk, imp + FORCE_BONUS * forced, NEG)
    rank = jnp.zeros(imp.shape, F32)
    for j in range(n_blocks):
        vj = score[j:j + 1, :]
        earlier = jnp.where(blk > j, 1.0, 0.0)
        rank = rank + jnp.where(vj > score, 1.0, jnp.where(vj == score, earlier, 0.0))
    return jnp.where(rank < float(min(N_SEL, n_blocks)), 1.0, 0.0)


def _softmax_cols(s):
    e = jnp.exp2(s - jnp.max(s, axis=0, keepdims=True))
    return e, 1.0 / jnp.sum(e, axis=0, keepdims=True)


def _p_attn_kernel(qt_ref, ks_ref, vst_ref, kw_ref, vwt_ref, kc_ref, vc_ref, gt_ref, sza_ref, sga_ref,
                   mb_ref, x_ref, gate_ref, wbra_ref, wout_ref, y_ref, qt_scr, pen_scr, m_scr, acc_scr, s_even, s_odd, sw_scr, mw_scr, oa_scr, mprev_scr):
    tq = Q_BLOCK
    step = pl.program_id(0)
    i = lax.rem(jnp.minimum(step, pl.num_programs(0) - 2), ks_ref.shape[2] // tq)

    @pl.when(step == 0)
    def _():
        oa_scr[...] = jnp.zeros(oa_scr.shape, F32)
    n_cmp = kc_ref.shape[1]
    half = n_cmp // 2
    assert L_SLC == 2 * L_CMP and n_cmp <= LANES
    cols = GQA * tq
    qpos1 = i * tq + _iota((1, tq), 1)
    qpos = jnp.concatenate([qpos1] * GQA, axis=1)
    band = WINDOW + tq
    ws = pl.multiple_of(jnp.maximum(i * tq - WINDOW, 0), LANES)
    zeros_q = jnp.zeros((HEAD_DIM, cols), BF16)
    perm = lambda ref: jnp.concatenate([ref[0, pl.ds(0, half, stride=2), :], ref[0, pl.ds(1, half, stride=2), :],
                                        jnp.zeros((LANES - n_cmp, KV_WIDTH), F32)], axis=0)
    kc = perm(kc_ref).astype(BF16)
    vc_t = perm(vc_ref).T.astype(BF16)
    crow = _iota((LANES, cols), 0)
    cblk = 2 * (crow % half) + crow // half
    mc = ((cblk + 1) * L_CMP - 1 <= qpos) & (crow < n_cmp)
    gates_t = gt_ref[0]
    groups = [slice(g * HEAD_DIM, (g + 1) * HEAD_DIM) for g in range(N_KV)]

    for g in range(N_KV):
        qt_g = jnp.concatenate([qt_ref[0, h * HEAD_DIM:(h + 1) * HEAD_DIM, :]
                                for h in range(g * GQA, (g + 1) * GQA)], axis=1)
        qt_scr[g] = jnp.concatenate([qt_g, zeros_q] if g == 0 else [zeros_q, qt_g], axis=0)

    per_chunk = SEL_CHUNK // L_SLC
    assert band % SEL_CHUNK == 0
    w_chunks = [pl.ds(pl.multiple_of(ws + c * SEL_CHUNK, LANES), SEL_CHUNK) for c in range(band // SEL_CHUNK)]

    s_cmp = [_dot(kc, qt_scr[g]) for g in range(N_KV)]
    for c, kd in enumerate(w_chunks):
        k_rows = kw_ref[0, kd, :]
        for g in range(N_KV):
            sw_scr[g, c] = _dot(k_rows, qt_scr[g])

    def selected_scores(g, chunk):
        k_aug = ks_ref[0, g, pl.ds(pl.multiple_of(chunk * SEL_CHUNK, SEL_CHUNK), SEL_CHUNK), :]
        gap = jnp.zeros((HEAD_DIM - PEN_ROWS, cols), BF16)
        q_aug = ([qt_scr[0, 0:HEAD_DIM], pen_scr[0, chunk], gap] if g == 0 else
                 [pen_scr[1, chunk], gap, qt_scr[1, HEAD_DIM:2 * HEAD_DIM]])
        return _dot(k_aug, jnp.concatenate(q_aug, axis=0))

    o_c = []
    for g in range(N_KV):
        s_c = jnp.where(mc, s_cmp[g], NEG)
        e_c, r_c = _softmax_cols(s_c)
        p_c = jnp.where(mc, e_c * r_c, 0.0)
        o_c.append(_dot(vc_t[groups[g]], p_c.astype(BF16)))
        imp = p_c[:, 0:tq]
        for r in range(1, GQA):
            imp = imp + p_c[:, r * tq:(r + 1) * tq]
        sel = _select_blocks_t(imp[0:half] + imp[half:2 * half], qpos1 // L_SLC)
        pen = jnp.concatenate([jnp.where(sel > 0.5, 0.0, NEG)] * GQA, axis=1)
        pad_rows = jnp.zeros((PEN_ROWS - per_chunk, cols), F32)
        for c in range(half // per_chunk):
            pen_scr[g, c] = jnp.concatenate([pen[c * per_chunk:(c + 1) * per_chunk], pad_rows],
                                            axis=0).astype(BF16)
        s_even[g] = selected_scores(g, 0)

    for c in range(len(w_chunks)):
        kwpos = ws + c * SEL_CHUNK + _iota((SEL_CHUNK, tq), 0)
        bias_w = jnp.where((kwpos <= qpos1) & (kwpos > qpos1 - WINDOW), 0.0, NEG)
        bias_w = jnp.concatenate([bias_w] * GQA, axis=1)
        for g in range(N_KV):
            s_w = sw_scr[g, c] + bias_w
            sw_scr[g, c] = s_w
            m_c = jnp.max(s_w, axis=0, keepdims=True)
            mw_scr[g] = m_c if c == 0 else jnp.maximum(mw_scr[g], m_c)

    m_scr[...] = jnp.full(m_scr.shape, NEG, F32)
    acc_scr[...] = jnp.zeros(acc_scr.shape, F32)

    def with_ones(v_t):
        return jnp.concatenate([v_t.astype(BF16), jnp.ones((ONES_ROWS, v_t.shape[1]), BF16)], axis=0)

    def trip(kc_i, src, dst):
        off = pl.multiple_of(kc_i * SEL_CHUNK, SEL_CHUNK)
        if dst is None:
            t_prev = (oa_scr[...] * sza_ref[0]).astype(BF16)
            a_prev = []
        for g in range(N_KV):
            if dst is not None:
                dst[g] = selected_scores(g, kc_i + 1)
                s = src[g]
            else:
                half_n = D_MODEL // N_KV
                a_prev.append(_dot(t_prev, wbra_ref[:, g * half_n:(g + 1) * half_n]))
                causal = jnp.where(off + _iota((SEL_CHUNK, tq), 0) <= qpos1, 0.0, NEG)
                s = src[g] + jnp.concatenate([causal] * GQA, axis=1)
            m_old = m_scr[g]
            m_new = jnp.maximum(m_old, jnp.max(s, axis=0, keepdims=True))
            p = jnp.exp2(s - m_new).astype(BF16)
            m_scr[g] = m_new
            acc_scr[g] = jnp.exp2(m_old - m_new) * acc_scr[g] + _dot(
                with_ones(vst_ref[0, groups[g], pl.ds(off, SEL_CHUNK)]), p)
        if dst is None:
            mprev_scr[...] = (sga_ref[0] * jnp.concatenate(a_prev, axis=1) + mb_ref[0]).astype(BF16)

    def by_parity(kc_i, dst_wanted):
        @pl.when(lax.rem(kc_i, 2) == 0)
        def _():
            trip(kc_i, s_even, s_odd if dst_wanted else None)

        @pl.when(lax.rem(kc_i, 2) == 1)
        def _():
            trip(kc_i, s_odd, s_even if dst_wanted else None)

    def body(kc_i, carry):
        by_parity(kc_i, True)
        return carry

    assert SEL_CHUNK == tq
    lax.fori_loop(0, i, body, 0)
    by_parity(i, False)

    acc_w = [None] * N_KV

    def window_pass2():
        for c, kd in enumerate(w_chunks):
            for g in range(N_KV):
                pv = _dot(with_ones(vwt_ref[0, groups[g], kd]),
                          jnp.exp2(sw_scr[g, c] - mw_scr[g]).astype(BF16))
                acc_w[g] = pv if c == 0 else acc_w[g] + pv
                yield

    def previous_block_output():
        n_blk = 2 * LANES
        m = mprev_scr[...]
        for n0 in range(0, D_MODEL, n_blk):
            y_ref[0, :, n0:n0 + n_blk] = (x_ref[0, :, n0:n0 + n_blk]
                                          + gate_ref[0][:, n0:n0 + n_blk] * _dot(m, wout_ref[:, n0:n0 + n_blk]))
            yield

    for _ in itertools.zip_longest(window_pass2(), previous_block_output()):
        pass
    pair = []
    for g in range(N_KV):
        o_w = acc_w[g][0:HEAD_DIM] * (1.0 / acc_w[g][HEAD_DIM:HEAD_DIM + 1])
        acc_s = acc_scr[g]
        o_s = acc_s[0:HEAD_DIM] * (1.0 / acc_s[HEAD_DIM:HEAD_DIM + 1])
        for r in range(GQA):
            h = g * GQA + r
            cs = slice(r * tq, (r + 1) * tq)
            pair.append(gates_t[3 * h:3 * h + 1] * o_c[g][:, cs] + gates_t[3 * h + 1:3 * h + 2] * o_s[:, cs]
                        + gates_t[3 * h + 2:3 * h + 3] * o_w[:, cs])
    per_lane = LANES // HEAD_DIM
    oa_scr[...] = jnp.concatenate([jnp.concatenate(pair[j:j + per_lane], axis=0).T
                                   for j in range(0, N_HEADS, per_lane)], axis=1)


def _p_attn_call(q_t, ks_rows, vs_t, kw_rows, vw_t, kc, vc, gates_t, sza, sga, mb, x, gate, w_br_a, w_out):
    b, s, _ = x.shape
    tq = Q_BLOCK
    n_i = s // tq
    n_blocks = b * n_i
    att = lambda j: jnp.minimum(j, n_blocks - 1)
    out = lambda j: jnp.maximum(j - 1, 0)
    row = lambda w: pl.BlockSpec((1, tq, w), lambda j: (out(j) // n_i, out(j) % n_i, 0))
    col = lambda a: pl.BlockSpec((1, a.shape[1], tq), lambda j: (att(j) // n_i, 0, att(j) % n_i))
    seq = lambda a: pl.BlockSpec((1,) + a.shape[1:], lambda j: (att(j) // n_i,) + (0,) * (a.ndim - 1))
    out_seq = lambda a: pl.BlockSpec((1,) + a.shape[1:], lambda j: (out(j) // n_i,) + (0,) * (a.ndim - 1))
    in_specs = [col(q_t), seq(ks_rows), seq(vs_t), seq(kw_rows), seq(vw_t), seq(kc), seq(vc),
                col(gates_t), row(WIDTH_A), row(D_MODEL), row(D_MODEL), row(D_MODEL), out_seq(gate),
                _full(w_br_a.shape), _full(w_out.shape)]
    return pl.pallas_call(
        _p_attn_kernel,
        grid=(n_blocks + 1,),
        in_specs=in_specs,
        out_specs=row(D_MODEL),
        out_shape=jax.ShapeDtypeStruct((b, s, D_MODEL), F32),
        scratch_shapes=[pltpu.VMEM((N_KV, KV_WIDTH, GQA * tq), BF16),
                        pltpu.VMEM((N_KV, s // SEL_CHUNK, PEN_ROWS, GQA * tq), BF16),
                        pltpu.VMEM((N_KV, 1, GQA * tq), F32),
                        pltpu.VMEM((N_KV, HEAD_DIM + ONES_ROWS, GQA * tq), F32),
                        pltpu.VMEM((N_KV, SEL_CHUNK, GQA * tq), F32),
                        pltpu.VMEM((N_KV, SEL_CHUNK, GQA * tq), F32),
                        pltpu.VMEM((N_KV, (WINDOW + tq) // SEL_CHUNK, SEL_CHUNK, GQA * tq), F32),
                        pltpu.VMEM((N_KV, 1, GQA * tq), F32),
                        pltpu.VMEM((tq, WIDTH_A), F32),
                        pltpu.VMEM((tq, D_MODEL), BF16)],
        compiler_params=pltpu.CompilerParams(dimension_semantics=("arbitrary",),
                                             vmem_limit_bytes=VMEM_LIMIT),
        name="p_attn",
    )(q_t, ks_rows, vs_t, kw_rows, vw_t, kc, vc, gates_t, sza, sga, mb, x, gate, w_br_a, w_out)


def _s_proj_kernel(x_ref, shift_ref, scale_ref, ng_ref, w_ref, cos_ref, sin_ref, qg_ref, kg_ref,
                   vng_ref, vnb_ref, ws0_ref, bs0_ref, wbrb_ref,
                   q_out, k_out, v_out, kt_out, vt_out, gates_out, sza_out, sga_out, mb_out, vn_out):
    seg = _project(x_ref[...], shift_ref[...], scale_ref[...], ng_ref[...], w_ref)
    cos, sin = cos_ref[...], sin_ref[...]
    q_out[...] = _norm_rope(seg(C_Q, C_K), qg_ref[...], cos, sin)
    k = _norm_rope(seg(C_K, C_V), kg_ref[...], cos, sin)
    v = seg(C_V, C_G)
    k_out[...] = k
    v_out[...] = v
    for br in range(3):
        sl = slice(br * KV_WIDTH, (br + 1) * KV_WIDTH)
        kt_out[br] = k[:, sl].T
        vt_out[br] = v[:, sl].T
    gates_out[...] = jax.nn.sigmoid(seg(C_G, C_ZA))
    sza_out[...] = _silu(seg(C_ZA, C_U))
    sga_out[...] = jax.nn.sigmoid(seg(C_GA, C_GB))
    vn = _layer_norm(seg(C_VB, C_ZB), vng_ref[...], vnb_ref[...])
    vn_out[...] = vn
    s_b = ws0_ref[...] * vn + bs0_ref[...]
    t = seg(C_U, C_VB) * s_b * _silu(seg(C_ZB, C_GA))
    mb_out[...] = jax.nn.sigmoid(seg(C_GB, C_END)) * _dot(t.astype(BF16), wbrb_ref[...])


def _s_proj_call(x, shift, scale, norm_g, w_all, cos1, sin1, qg, kg, vng, vnb, ws0, bs0, w_br_b):
    n = x.shape[0]
    args = (x, shift, scale, norm_g, w_all, cos1, sin1, qg, kg, vng, vnb, ws0, bs0, w_br_b)
    sds = lambda *shape: jax.ShapeDtypeStruct(shape, F32)
    out_shape = [sds(n, WIDTH_A), sds(n, 3 * KV_WIDTH), sds(n, 3 * KV_WIDTH), sds(3, KV_WIDTH, n),
                 sds(3, KV_WIDTH, n), sds(n, LANES), sds(n, WIDTH_A), sds(n, D_MODEL), sds(n, D_MODEL),
                 sds(n, WIDTH_B)]
    return pl.pallas_call(
        _s_proj_kernel,
        grid=(1,),
        in_specs=[_full(a.shape) for a in args],
        out_specs=[_full(o.shape) for o in out_shape],
        out_shape=out_shape,
        compiler_params=pltpu.CompilerParams(vmem_limit_bytes=VMEM_LIMIT),
        name="s_proj",
    )(*args)


def _s_attn_kernel(pt_ref, q_ref, gates_ref, knew_ref, vnew_ref, knewt_ref, vnewt_ref, kwin_ref, vwin_ref,
                   pek_ref, pev_ref, wck_ref, wcv_ref, pool_ref, expand_ref,
                   kc_hbm, vc_hbm, ks_hbm, vs_hbm,
                   oa_ref, okw_ref, ovw_ref, buf, sem):
    t = pl.program_id(0)
    n_groups = pl.num_programs(0) - 1
    per_step = q_ref.shape[0]
    n_seqs = pt_ref.shape[0]
    n_pages = pt_ref.shape[1]
    past = n_pages * PAGE_SIZE
    caches = (kc_hbm, vc_hbm, ks_hbm, vs_hbm)

    def page_copies(group, slot_):
        return [pltpu.make_async_copy(hbm.at[pt_ref[jnp.minimum(group * per_step + j, n_seqs - 1), p]],
                                      buf.at[slot_, j, c, :, pl.ds(p * PAGE_SIZE, PAGE_SIZE)],
                                      sem.at[slot_, c])
                for j in range(per_step) for c, hbm in enumerate(caches) for p in range(n_pages)]

    @pl.when(t < n_groups)
    def _():
        for cp in page_copies(t, lax.rem(t, 2)):
            cp.start()

    @pl.when(t > 0)
    def _():
        group = t - 1
        slot = lax.rem(group, 2)
        for cp in page_copies(group, slot):
            cp.wait()
        chains = [_s_attn_one(group * per_step + j, j, buf.at[slot, j], q_ref, gates_ref, knew_ref, vnew_ref,
                              knewt_ref, vnewt_ref, kwin_ref, vwin_ref, pek_ref, pev_ref, wck_ref, wcv_ref,
                              pool_ref, expand_ref, oa_ref, okw_ref, ovw_ref, past) for j in range(per_step)]
        for _ in itertools.zip_longest(*chains):
            pass


def _s_attn_one(b, j, buf, q_ref, gates_ref, knew_ref, vnew_ref, knewt_ref, vnewt_ref, kwin_ref, vwin_ref,
                pek_ref, pev_ref, wck_ref, wcv_ref, pool_ref, expand_ref, oa_ref, okw_ref, ovw_ref, past):
    qpos = past

    lane = _iota((KV_WIDTH, knewt_ref.shape[2]), 1)
    col = lambda ref, br: jnp.sum(jnp.where(lane == b, ref[br], 0.0), axis=1, keepdims=True)
    knew = knew_ref[pl.ds(b, 1), :]
    vnew = vnew_ref[pl.ds(b, 1), :]

    hrow = _iota((N_HEADS, 1), 0)
    first_group = hrow < GQA
    by_group = lambda f: jnp.where(first_group, f(0), f(1))
    gl = lambda g: slice(g * HEAD_DIM, (g + 1) * HEAD_DIM)
    qb = q_ref[j].astype(BF16)

    n_cmp = -(-(past + 1) // L_SLC) * L_SLC // L_CMP
    n_slc = n_cmp * L_CMP // L_SLC

    def summaries(c, new_row, pe_ref, w_ref):
        chunk = pool_ref.shape[1]
        sums = []
        for t0 in range(0, past, chunk):
            hi, lo = _split_bf16(buf[c, :, t0:t0 + chunk])
            sums.append(_dot_nt(pool_ref[...], hi) + _dot_nt(pool_ref[...], lo))
        pe_sum = jnp.sum(pe_ref[...], axis=0, keepdims=True)
        pooled = (jnp.concatenate(sums, axis=0) + pe_sum) * (1.0 / L_CMP)
        r = _iota((SUBLANES, KV_WIDTH), 0)
        tail = jnp.where(r == 0, new_row + pe_sum, jnp.where(r == 1, pe_sum, 0.0)) * (1.0 / L_CMP)
        zeros = jnp.zeros((LANES - pooled.shape[0] - SUBLANES, KV_WIDTH), F32)
        return _dot(jnp.concatenate([pooled, tail, zeros], axis=0).astype(BF16), w_ref[...])

    kc = summaries(0, knew[:, 0:KV_WIDTH], pek_ref, wck_ref).astype(BF16)
    yield
    vc = summaries(1, vnew[:, 0:KV_WIDTH], pev_ref, wcv_ref).astype(BF16)
    yield
    s_c = by_group(lambda g: _dot_nt(qb, kc[:, gl(g)])) * SM_SCALE
    c = _iota(s_c.shape, 1)
    mc = ((c + 1) * L_CMP - 1 <= qpos) & (c < n_cmp)
    e_c, r_c = _softmax_rows(s_c, mc)
    p_c = jnp.where(mc, e_c * r_c, 0.0)
    o_c = by_group(lambda g: _dot(p_c.astype(BF16), vc[:, gl(g)]))
    imp = by_group(lambda g: jnp.sum(p_c[g * GQA:(g + 1) * GQA], axis=0, keepdims=True))
    imp = jnp.broadcast_to(imp, p_c.shape)
    yield
    sel = _select_blocks(imp, jnp.full((N_HEADS, 1), qpos // L_SLC, jnp.int32), n_slc)
    yield

    picked = _dot(sel.astype(BF16), expand_ref[...])
    s_s = by_group(lambda g: _dot(qb, buf[2, gl(g), :].astype(BF16))) * SM_SCALE
    kpos = _iota(s_s.shape, 1)
    s_s = jnp.where((picked > 0.5) & (kpos <= qpos), s_s, NEG)
    yield
    own =_iota((N_HEADS, KV_WIDTH), 1) // HEAD_DIM == hrow // GQA
    rounded = lambda a: a.astype(BF16).astype(F32)
    q_pair = jnp.concatenate([qb.astype(F32)] * N_KV, axis=1)
    s_new = jnp.sum(jnp.where(own, q_pair * rounded(knew[:, KV_WIDTH:2 * KV_WIDTH]), 0.0),
                    axis=1, keepdims=True) * SM_SCALE
    new_lane = (L_SLC // L_CMP) * (past // L_SLC)
    s_new = jnp.where((sel[:, new_lane:new_lane + 1] > 0.5) & (past <= qpos), s_new, NEG)
    m_s = jnp.maximum(jnp.max(s_s, axis=-1, keepdims=True), s_new)
    e_s, e_new = jnp.exp(s_s - m_s), jnp.exp(s_new - m_s)
    r_s = 1.0 / (jnp.sum(e_s, axis=-1, keepdims=True) + e_new)
    v_new = by_group(lambda g: rounded(vnew[:, KV_WIDTH + g * HEAD_DIM:KV_WIDTH + (g + 1) * HEAD_DIM]))
    o_s = (by_group(lambda g: _dot_nt(e_s.astype(BF16), buf[3, gl(g), :].astype(BF16)))
           + rounded(e_new) * v_new) * r_s
    yield

    wb = kwin_ref.shape[2]
    wlane = _iota((KV_WIDTH, wb), 1)
    kw = jnp.where(wlane == wb - 1, col(knewt_ref, 2), pltpu.roll(kwin_ref[j], wb - 1, 1))
    vw = jnp.where(wlane == wb - 1, col(vnewt_ref, 2), pltpu.roll(vwin_ref[j], wb - 1, 1))
    okw_ref[j] = kw
    ovw_ref[j] = vw
    yield
    s_w = by_group(lambda g: _dot(qb, kw[gl(g)].astype(BF16))) * SM_SCALE
    kwpos = past - wb + 1 + _iota(s_w.shape, 1)
    e_w, r_w = _softmax_rows(s_w, (kwpos <= qpos) & (kwpos > qpos - WINDOW) & (kwpos >= 0))
    o_w = by_group(lambda g: _dot_nt(e_w.astype(BF16), vw[gl(g)].astype(BF16))) * r_w

    gates = gates_ref[j]
    oa_ref[j] = gates[:, 0:1] * o_c + gates[:, 1:2] * o_s + gates[:, 2:3] * o_w


def _s_attn_call(page_table, q3, gates3, knew, vnew, knew_t, vnew_t, kwin_t, vwin_t, pek, pev, wck, wcv,
                 kc_pool, vc_pool, ks_pool, vs_pool):
    n, n_pages = page_table.shape
    past = n_pages * PAGE_SIZE
    wb = kwin_t.shape[2]
    tok = np.arange(SUMMARY_CHUNK)
    assert past % SUMMARY_CHUNK == 0
    pool = jnp.asarray((tok[None, :] // L_CMP == np.arange(SUMMARY_CHUNK // L_CMP)[:, None]), BF16)
    expand = jnp.asarray(np.arange(LANES)[:, None] == 2 * (np.arange(past)[None, :] // L_SLC), BF16)
    k = SAMPLE_SEQS_PER_STEP
    assert n % k == 0
    per_seq = lambda a: pl.BlockSpec((k,) + a.shape[1:],
                                     lambda t, pt: (jnp.maximum(t - 1, 0),) + (0,) * (a.ndim - 1))
    full = lambda a: pl.BlockSpec(a.shape, lambda t, pt: (0,) * a.ndim)
    hbm = pl.BlockSpec(memory_space=pl.ANY)
    resident = (knew, vnew, knew_t, vnew_t)
    consts = (pek, pev, wck, wcv, pool, expand)
    grid_spec = pltpu.PrefetchScalarGridSpec(
        num_scalar_prefetch=1,
        grid=(n // k + 1,),
        in_specs=[per_seq(q3), per_seq(gates3)] + [full(a) for a in resident]
                 + [per_seq(kwin_t), per_seq(vwin_t)] + [full(a) for a in consts] + [hbm] * 4,
        out_specs=[per_seq(q3), per_seq(kwin_t), per_seq(vwin_t)],
        scratch_shapes=[pltpu.VMEM((2, k, 4, KV_WIDTH, past), F32), pltpu.SemaphoreType.DMA((2, 4))],
    )
    return pl.pallas_call(
        _s_attn_kernel,
        grid_spec=grid_spec,
        out_shape=[jax.ShapeDtypeStruct(q3.shape, F32), jax.ShapeDtypeStruct(kwin_t.shape, F32),
                   jax.ShapeDtypeStruct(vwin_t.shape, F32)],
        compiler_params=pltpu.CompilerParams(dimension_semantics=("arbitrary",),
                                             vmem_limit_bytes=VMEM_LIMIT),
        name="s_attn",
    )(page_table, q3, gates3, knew, vnew, knew_t, vnew_t, kwin_t, vwin_t, pek, pev, wck, wcv, pool, expand,
      kc_pool, vc_pool, ks_pool, vs_pool)


def _s_out_kernel(oa_ref, sza_ref, sga_ref, mb_ref, x_ref, gate_ref, wbra_ref, wout_ref, y_ref):
    y_ref[...] = _merge_and_project(oa_ref[...], sza_ref[...], sga_ref[...], mb_ref[...], x_ref[...],
                                    gate_ref[...], wbra_ref, wout_ref)


def _s_out_call(o_a, sza, sga, mb, x, gate, w_br_a, w_out):
    args = (o_a, sza, sga, mb, x, gate, w_br_a, w_out)
    return pl.pallas_call(
        _s_out_kernel,
        grid=(1,),
        in_specs=[_full(a.shape) for a in args],
        out_specs=_full(x.shape),
        out_shape=jax.ShapeDtypeStruct(x.shape, F32),
        compiler_params=pltpu.CompilerParams(vmem_limit_bytes=VMEM_LIMIT),
        name="s_out",
    )(*args)


def _rope_angles(pos):
    half = HEAD_DIM // 2
    inv = ROPE_THETA ** (-jnp.arange(half, dtype=F32) * 2.0 / HEAD_DIM)
    return pos.astype(F32)[:, None] * inv[None, :]


def _rope_tables(pos):
    ang = _rope_angles(pos)
    cos, sin = jnp.cos(ang), jnp.sin(ang)
    cos_t = jnp.concatenate([cos, cos] * (LANES // HEAD_DIM), axis=1)
    sin_t = jnp.concatenate([-sin, sin] * (LANES // HEAD_DIM), axis=1)
    return cos_t, sin_t


def _token_minor(a):
    b, t = a.shape[:2]
    return jnp.transpose(a, (0, 2, 3, 1)).reshape(b, KV_WIDTH, t)


def _token_major(a_t):
    b, _, t = a_t.shape
    return jnp.transpose(a_t.reshape(b, N_KV, HEAD_DIM, t), (0, 3, 1, 2))


def kernel(x_prompt, x_sample, cache_k_cmp, cache_v_cmp, cache_k_slc, cache_v_slc, cache_k_win, cache_v_win, page_table, c_prompt, c_sample, w_ada, b_ada, norm_g, w_in, q_norm_g, k_norm_g, cmp_pos_k, cmp_pos_v, w_cmp_k, w_cmp_v, vnorm_g, vnorm_b, w_s, b_s, w_br_a, w_br_b, w_out):
    assert w_ada.shape[0] == 1, "single layer"
    b, s, _ = x_prompt.shape
    n = x_sample.shape[0]
    assert x_sample.shape[1] == 1
    n_pages = page_table.shape[1]
    past = n_pages * PAGE_SIZE

    w_all = _w_pack_call(w_in[0].T)
    eye = jnp.eye(N_KV, dtype=F32)
    wck = jnp.kron(eye, w_cmp_k[0]).astype(BF16)
    wcv = jnp.kron(eye, w_cmp_v[0]).astype(BF16)
    pek = jnp.tile(cmp_pos_k[0], (1, N_KV))
    pev = jnp.tile(cmp_pos_v[0], (1, N_KV))
    qg = jnp.tile(q_norm_g, (1, LANES // HEAD_DIM))
    kg = jnp.tile(k_norm_g, (1, LANES // HEAD_DIM))
    w_br_a_b, w_br_b_b, w_out_b = w_br_a[0].astype(BF16), w_br_b[0].astype(BF16), w_out[0].astype(BF16)

    mod = _ada_call(jnp.concatenate([c_prompt, c_sample], axis=0), w_ada[0], b_ada)
    shift, scale, gate = mod[:, :D_MODEL], mod[:, D_MODEL:2 * D_MODEL], mod[:, 2 * D_MODEL:]

    ang_p = _rope_angles(jnp.arange(s, dtype=jnp.int32)).T
    gain_cols = lambda g: jnp.broadcast_to(g[0][:, None], (HEAD_DIM, PROMPT_SUB_ROWS))
    (q_t, kc_t, ks_t, vc_t, vs_t, vw_t, ks_rows, kw_rows, kcmp, vcmp, gates_t, sza, sga, mb,
     kw_last, vw_last) = _p_proj_call(
        x_prompt, shift[:b, None], scale[:b, None], norm_g, w_all, jnp.cos(ang_p), jnp.sin(ang_p),
        gain_cols(q_norm_g), gain_cols(k_norm_g), pek, pev, wck, wcv,
        vnorm_g, vnorm_b, w_s[0], b_s[0].T, w_br_b_b)
    y_prompt = _p_attn_call(q_t, ks_rows, vs_t, kw_rows, vw_t, kcmp, vcmp, gates_t, sza, sga, mb, x_prompt,
                            gate[:b, None], w_br_a_b, w_out_b)
    p_states = [_token_major(a)[None] for a in (kc_t, vc_t, ks_t, vs_t, kw_last, vw_last)]

    xs = x_sample.reshape(n, D_MODEL)
    cos_s, sin_s = _rope_tables(jnp.full((1,), past, jnp.int32))
    ws0 = jnp.repeat(w_s[0, :, 0, 0], GROUP_W_B)[None]
    bs0 = jnp.repeat(b_s[0, :, 0], GROUP_W_B)[None]
    (q_s, k_s, v_s, kt_s, vt_s, gates_s, sza_s, sga_s, mb_s, vn_s) = _s_proj_call(
        xs, shift[b:], scale[b:], norm_g, w_all, cos_s, sin_s, qg, kg, vnorm_g, vnorm_b, ws0, bs0, w_br_b_b)
    pools = [_token_minor(c[0]) for c in (cache_k_cmp, cache_v_cmp, cache_k_slc, cache_v_slc)]
    o_a, kwin_new, vwin_new = _s_attn_call(
        page_table, q_s.reshape(n, N_HEADS, HEAD_DIM), gates_s[:, :3 * N_HEADS].reshape(n, N_HEADS, 3),
        k_s, v_s, kt_s, vt_s, _token_minor(cache_k_win[0]), _token_minor(cache_v_win[0]),
        pek, pev, wck, wcv, *pools)
    y_sample = _s_out_call(o_a.reshape(n, WIDTH_A), sza_s, sga_s, mb_s, xs, gate[b:], w_br_a_b, w_out_b)

    new_rows = lambda t, br: jnp.transpose(t[br].reshape(N_KV, HEAD_DIM, n), (2, 0, 1))[None, :, None]
    s_states = [new_rows(kt_s, 0), new_rows(vt_s, 0), new_rows(kt_s, 1), new_rows(vt_s, 1),
                _token_major(kwin_new)[None], _token_major(vwin_new)[None], vn_s[None, :, None]]
    return (y_prompt, y_sample.reshape(n, 1, D_MODEL), *p_states, *s_states)
```

```python
import itertools

import numpy as np
import jax
import jax.numpy as jnp
from jax import lax
from jax.experimental import pallas as pl
from jax.experimental.pallas import tpu as pltpu

F32 = jnp.float32
BF16 = jnp.bfloat16

D_MODEL = 1024
HEAD_DIM = 64
N_HEADS = 8
N_KV = 2
GQA = N_HEADS // N_KV
WIDTH_A = N_HEADS * HEAD_DIM
KV_WIDTH = N_KV * HEAD_DIM
L_CMP = 32
L_SLC = 64
N_SEL = 8
WINDOW = 512
Q_BLOCK = 256
FORCE_BONUS = 1.0e4
ROPE_THETA = 10000.0
CHUNK = 128
N_GROUPS_B = 4
WIDTH_B = 512
GROUP_W_B = WIDTH_B // N_GROUPS_B
PAGE_SIZE = 128
EPS = 1e-6
NEG = -1e30
SM_SCALE = HEAD_DIM ** -0.5
LOG2_E = 1.4426950408889634

LANES = 128
SUBLANES = 8
VMEM_LIMIT = 56 * 1024 * 1024

C_Q = 0
C_K = C_Q + WIDTH_A
C_V = C_K + 3 * KV_WIDTH
C_G = C_V + 3 * KV_WIDTH
C_ZA = C_G + LANES
C_U = C_ZA + WIDTH_A
C_VB = C_U + WIDTH_B
C_ZB = C_VB + WIDTH_B
C_GA = C_ZB + WIDTH_B
C_GB = C_GA + D_MODEL
C_END = C_GB + D_MODEL

PROMPT_ROWS = 512
PROMPT_SUB_ROWS = 256
SEL_CHUNK = 256
PEN_ROWS = 16
ONES_ROWS = 16
SUMMARY_CHUNK = 512
SAMPLE_SEQS_PER_STEP = 4


def _dot(a, b):
    return jnp.dot(a, b, preferred_element_type=F32)


def _dot_nt(a, b):
    return lax.dot_general(a, b, (((1,), (1,)), ((), ())), preferred_element_type=F32)


def _iota(shape, dim):
    return lax.broadcasted_iota(jnp.int32, shape, dim)


def _split_bf16(x):
    hi = x.astype(BF16)
    lo = (x - hi.astype(F32)).astype(BF16)
    return hi, lo


def _head_mean_sq(x):
    w = x.shape[1]
    ones_bd = jnp.where(_iota((LANES, LANES), 0) // HEAD_DIM == _iota((LANES, LANES), 1) // HEAD_DIM,
                        1.0, 0.0).astype(BF16)
    hi, lo = _split_bf16(x * x)
    cols = []
    for c in range(w // LANES):
        sl = slice(c * LANES, (c + 1) * LANES)
        cols.append(_dot(hi[:, sl], ones_bd) + _dot(lo[:, sl], ones_bd))
    return jnp.concatenate(cols, axis=1) * (1.0 / HEAD_DIM)


def _tile_lanes(t, width):
    return jnp.concatenate([t] * (width // t.shape[1]), axis=1)


def _norm_rope(x, g, cos, sin):
    w = x.shape[1]
    y = x * lax.rsqrt(_head_mean_sq(x) + EPS) * _tile_lanes(g, w)
    first_half = (_iota(y.shape, 1) % HEAD_DIM) < (HEAD_DIM // 2)
    rot = jnp.where(first_half, pltpu.roll(y, w - HEAD_DIM // 2, 1), pltpu.roll(y, HEAD_DIM // 2, 1))
    return y * _tile_lanes(cos, w) + rot * _tile_lanes(sin, w)


def _norm_rope_t(x_t, gain_t, cos_t, sin_t):
    half = HEAD_DIM // 2
    out = []
    for h in range(x_t.shape[0] // HEAD_DIM):
        x = x_t[h * HEAD_DIM:(h + 1) * HEAD_DIM]
        y = x * lax.rsqrt(jnp.mean(x * x, axis=0, keepdims=True) + EPS) * gain_t
        y1, y2 = y[:half], y[half:]
        out += [y1 * cos_t - y2 * sin_t, y2 * cos_t + y1 * sin_t]
    return jnp.concatenate(out, axis=0)


def _silu(z):
    return z * jax.nn.sigmoid(z)


def _project(x, shift, scale, norm_g, w_ref):
    ms = jnp.mean(x * x, axis=-1, keepdims=True)
    h = (x * lax.rsqrt(ms + EPS) * norm_g) * (1.0 + scale) + shift
    hb = h.astype(BF16)
    return lambda lo, hi: _dot(hb, w_ref[:, lo:hi])


def _layer_norm(v, g, b):
    mu = jnp.mean(v, axis=-1, keepdims=True)
    d = v - mu
    var = jnp.mean(d * d, axis=-1, keepdims=True)
    return d * lax.rsqrt(var + EPS) * g + b


def _ada_kernel(c_ref, w_ref, b_ref, o_ref):
    o_ref[...] = _dot(c_ref[...].astype(BF16), w_ref[...].astype(BF16)) + b_ref[...]


W_PACK_COLS = 512
W_GAP_LO = C_G + 3 * N_HEADS
W_GAP_HI = C_G + LANES
W_PACK_BLOCKS = -(-C_END // W_PACK_COLS)


def _w_pack_kernel(wt_ref, o_ref):
    j = pl.program_id(0)
    w = W_PACK_COLS
    gap_block, last = W_GAP_LO // w, W_PACK_BLOCKS - 1
    d_in = C_END - (W_GAP_HI - W_GAP_LO)

    def emit(rows):
        for s in range(w // LANES):
            blk = rows[s * LANES:(s + 1) * LANES]
            o_ref[:, s * LANES:(s + 1) * LANES] = jnp.concatenate(
                [blk[:, c * LANES:(c + 1) * LANES].T for c in range(D_MODEL // LANES)], axis=0).astype(BF16)

    @pl.when((j != gap_block) & (j != last))
    def _():
        emit(wt_ref[...])

    @pl.when(j == gap_block)
    def _():
        k0, k1 = W_GAP_LO - gap_block * w, W_GAP_HI - gap_block * w
        emit(jnp.concatenate([wt_ref[0:k0, :], jnp.zeros((k1 - k0, D_MODEL), F32), wt_ref[k0:k0 + w - k1, :]],
                             axis=0))

    @pl.when(j == last)
    def _():
        shift = (last * w - (W_GAP_HI - W_GAP_LO)) - (d_in - w)
        n_valid = C_END - last * w
        emit(jnp.concatenate([wt_ref[shift:shift + n_valid, :], jnp.zeros((w - n_valid, D_MODEL), F32)], axis=0))


def _w_pack_call(w_t):
    d_in = w_t.shape[0]
    w = W_PACK_COLS
    pad = W_GAP_HI - W_GAP_LO
    gap_block = W_GAP_LO // w
    assert d_in + pad == C_END and W_GAP_HI <= (gap_block + 1) * w and gap_block < W_PACK_BLOCKS - 1
    assert W_GAP_LO % SUBLANES == 0 and pad % SUBLANES == 0
    src_row = lambda j: (pl.multiple_of(
        jnp.minimum(jnp.where(j <= gap_block, j * w, j * w - pad), d_in - w), SUBLANES), 0)
    return pl.pallas_call(
        _w_pack_kernel,
        grid=(W_PACK_BLOCKS,),
        in_specs=[pl.BlockSpec((pl.Element(w), pl.Element(D_MODEL)), src_row)],
        out_specs=pl.BlockSpec((D_MODEL, w), lambda j: (0, j)),
        out_shape=jax.ShapeDtypeStruct((D_MODEL, W_PACK_BLOCKS * w), BF16),
        compiler_params=pltpu.CompilerParams(vmem_limit_bytes=VMEM_LIMIT),
        name="w_pack",
    )(w_t)


def _ada_call(c_all, w_ada, b_ada):
    rows = c_all.shape[0]
    n = w_ada.shape[1]
    return pl.pallas_call(
        _ada_kernel,
        grid=(n // D_MODEL,),
        in_specs=[pl.BlockSpec((rows, D_MODEL), lambda j: (0, 0)),
                  pl.BlockSpec((D_MODEL, D_MODEL), lambda j: (0, j)),
                  pl.BlockSpec((1, D_MODEL), lambda j: (0, j))],
        out_specs=pl.BlockSpec((rows, D_MODEL), lambda j: (0, j)),
        out_shape=jax.ShapeDtypeStruct((rows, n), F32),
        compiler_params=pltpu.CompilerParams(vmem_limit_bytes=VMEM_LIMIT),
        name="ada",
    )(c_all, w_ada, b_ada)


def _compress_rows(rows, pe, w_bd):
    t = rows.shape[0]
    pooled = jnp.sum(rows.reshape(t // L_CMP, L_CMP, KV_WIDTH) + pe[None], axis=1) * (1.0 / L_CMP)
    return _dot(pooled.astype(BF16), w_bd)


def _p_proj_kernel(x_ref, shift_ref, scale_ref, ng_ref, w_ref, cos_ref, sin_ref, qg_ref, kg_ref,
                   pek_ref, pev_ref, wck_ref, wcv_ref, vng_ref, vnb_ref, ws_ref, bst_ref, wbrb_ref,
                   *outs):
    tm = x_ref.shape[1]
    chains = [_p_proj_rows(r0, x_ref, shift_ref, scale_ref, ng_ref, w_ref, cos_ref, sin_ref, qg_ref, kg_ref,
                           pek_ref, pev_ref, wck_ref, wcv_ref, vng_ref, vnb_ref, ws_ref, bst_ref, wbrb_ref, *outs)
              for r0 in range(0, tm, PROMPT_SUB_ROWS)]
    for _ in itertools.zip_longest(*chains):
        pass


def _p_proj_rows(r0, x_ref, shift_ref, scale_ref, ng_ref, w_ref, cos_ref, sin_ref, qg_ref, kg_ref,
                 pek_ref, pev_ref, wck_ref, wcv_ref, vng_ref, vnb_ref, ws_ref, bst_ref, wbrb_ref,
                 qt_out, kc_t, ks_t, vc_t, vs_t, vw_t, ks_rows, kw_rows, kcmp_out, vcmp_out,
                 gates_t, sza_out, sga_out, mb_out, kw_last, vw_last):
    sub = PROMPT_SUB_ROWS
    rs = slice(r0, r0 + sub)
    cmp_rows = slice(r0 // L_CMP, (r0 + sub) // L_CMP)
    seg = _project(x_ref[0, rs, :], shift_ref[0], scale_ref[0], ng_ref[...], w_ref)
    cos, sin = cos_ref[:, rs], sin_ref[:, rs]
    to_token_minor = lambda a: jnp.concatenate(
        [a[:, c * LANES:(c + 1) * LANES].T for c in range(a.shape[1] // LANES)], axis=0)
    yield

    qt_out[0, :, rs] = (_norm_rope_t(to_token_minor(seg(C_Q, C_K)), qg_ref[...], cos, sin)
                        * (SM_SCALE * LOG2_E)).astype(BF16)
    yield
    k_t = _norm_rope_t(to_token_minor(seg(C_K, C_V)), kg_ref[...], cos, sin)
    yield
    v = seg(C_V, C_G)
    for br, (k_out, v_out) in enumerate(((kc_t, vc_t), (ks_t, vs_t), (kw_last, vw_t))):
        sl = slice(br * KV_WIDTH, (br + 1) * KV_WIDTH)
        k_out[0, :, rs] = k_t[sl]
        v_t = v[:, sl].T
        v_out[0, :, rs] = v_t
    vw_last[0, :, rs] = v_t
    blk_id = jnp.where((_iota((HEAD_DIM, sub), 1) // L_SLC) % (SEL_CHUNK // L_SLC) == _iota((HEAD_DIM, sub), 0),
                       1.0, 0.0)
    ks_g = [k_t[KV_WIDTH + g * HEAD_DIM:KV_WIDTH + (g + 1) * HEAD_DIM] for g in range(N_KV)]
    ks_rows[0, 0, rs, :] = jnp.concatenate([ks_g[0], blk_id], axis=0).T.astype(BF16)
    ks_rows[0, 1, rs, :] = jnp.concatenate([blk_id, ks_g[1]], axis=0).T.astype(BF16)
    kw_rows[0, rs, :] = k_t[2 * KV_WIDTH:3 * KV_WIDTH].T.astype(BF16)
    yield
    kcmp_out[0, cmp_rows, :] = _compress_rows(k_t[0:KV_WIDTH].T, pek_ref[...], wck_ref[...])
    vcmp_out[0, cmp_rows, :] = _compress_rows(v[:, 0:KV_WIDTH], pev_ref[...], wcv_ref[...])
    yield

    gates_t[0, :, rs] = jax.nn.sigmoid(seg(C_G, C_ZA)).T
    sza_out[0, rs, :] = _silu(seg(C_ZA, C_U))
    yield
    sga_out[0, rs, :] = jax.nn.sigmoid(seg(C_GA, C_GB))
    yield

    vn = _layer_norm(seg(C_VB, C_ZB), vng_ref[...], vnb_ref[...]).astype(BF16)
    yield
    causal = _iota((CHUNK, CHUNK), 0) >= _iota((CHUNK, CHUNK), 1)
    chunks = []
    for c in range(sub // CHUNK):
        groups = []
        for g in range(N_GROUPS_B):
            wsg = jnp.where(causal, ws_ref[g], 0.0).astype(BF16)
            vg = vn[c * CHUNK:(c + 1) * CHUNK, g * GROUP_W_B:(g + 1) * GROUP_W_B]
            groups.append(_dot(wsg, vg) + bst_ref[:, g:g + 1])
        chunks.append(jnp.concatenate(groups, axis=1))
    s_b = jnp.concatenate(chunks, axis=0)
    yield
    t = seg(C_U, C_VB) * s_b * _silu(seg(C_ZB, C_GA))
    yield
    mb_out[0, rs, :] = jax.nn.sigmoid(seg(C_GB, C_END)) * _dot(t.astype(BF16), wbrb_ref[...])


def _full(shape):
    nd = len(shape)
    return pl.BlockSpec(shape, lambda *_: (0,) * nd)


def _p_proj_call(x, shift, scale, norm_g, w_all, cos_t, sin_t, qg, kg, pek, pev, wck, wcv, vng, vnb,
                 w_s, bs_t, w_br_b):
    b, s, _ = x.shape
    tm = PROMPT_ROWS
    row = lambda w: pl.BlockSpec((1, tm, w), lambda bi, i: (bi, i, 0))
    tok_minor = pl.BlockSpec((1, KV_WIDTH, tm), lambda bi, i: (bi, 0, i))
    per_batch = pl.BlockSpec((1, 1, D_MODEL), lambda bi, i: (bi, 0, 0))
    cmp_spec = pl.BlockSpec((1, tm // L_CMP, KV_WIDTH), lambda bi, i: (bi, i, 0))
    tab = pl.BlockSpec((HEAD_DIM // 2, tm), lambda bi, i: (0, i))
    in_specs = [row(D_MODEL), per_batch, per_batch, _full(norm_g.shape), _full(w_all.shape), tab, tab,
                _full(qg.shape), _full(kg.shape), _full(pek.shape), _full(pev.shape), _full(wck.shape),
                _full(wcv.shape), _full(vng.shape), _full(vnb.shape), _full(w_s.shape), _full(bs_t.shape),
                _full(w_br_b.shape)]
    qt_spec = pl.BlockSpec((1, WIDTH_A, tm), lambda bi, i: (bi, 0, i))
    group_rows = pl.BlockSpec((1, N_KV, tm, KV_WIDTH), lambda bi, i: (bi, 0, i, 0))
    assert min(WINDOW, s) == tm
    last_tile = pl.BlockSpec((1, KV_WIDTH, tm), lambda bi, i: (bi, 0, 0))
    out_specs = [qt_spec] + [tok_minor] * 5 + [group_rows, row(KV_WIDTH), cmp_spec, cmp_spec,
                                                tok_minor, row(WIDTH_A), row(D_MODEL), row(D_MODEL),
                                                last_tile, last_tile]
    tm_shape = jax.ShapeDtypeStruct((b, KV_WIDTH, s), F32)
    rows_shape = jax.ShapeDtypeStruct((b, s, KV_WIDTH), BF16)
    cmp_shape = jax.ShapeDtypeStruct((b, s // L_CMP, KV_WIDTH), F32)
    assert PROMPT_SUB_ROWS % SEL_CHUNK == 0
    out_shape = [jax.ShapeDtypeStruct((b, WIDTH_A, s), BF16)] + [tm_shape] * 5 + [
        jax.ShapeDtypeStruct((b, N_KV, s, KV_WIDTH), BF16), rows_shape, cmp_shape, cmp_shape,
        jax.ShapeDtypeStruct((b, LANES, s), F32), jax.ShapeDtypeStruct((b, s, WIDTH_A), F32),
        jax.ShapeDtypeStruct((b, s, D_MODEL), F32), jax.ShapeDtypeStruct((b, s, D_MODEL), F32),
        jax.ShapeDtypeStruct((b, KV_WIDTH, tm), F32), jax.ShapeDtypeStruct((b, KV_WIDTH, tm), F32)]
    return pl.pallas_call(
        _p_proj_kernel,
        grid=(b, s // tm),
        in_specs=in_specs,
        out_specs=out_specs,
        out_shape=out_shape,
        compiler_params=pltpu.CompilerParams(dimension_semantics=("arbitrary", "arbitrary"),
                                             vmem_limit_bytes=VMEM_LIMIT),
        name="p_proj",
    )(x, shift, scale, norm_g, w_all, cos_t, sin_t, qg, kg, pek, pev, wck, wcv, vng, vnb, w_s, bs_t, w_br_b)


def _select_blocks(imp_c, qblk, n_blocks):
    ratio = L_SLC // L_CMP
    assert ratio == 2
    lane = _iota(imp_c.shape, 1)
    imp = imp_c + pltpu.roll(imp_c, LANES - 1, 1)
    blk = lane // ratio
    forced = jnp.where((blk == 0) | (blk == qblk), 1.0, 0.0)
    score = jnp.where(blk <= qblk, imp + FORCE_BONUS * forced, NEG)
    beats = []
    for j in range(n_blocks):
        vj = score[:, ratio * j:ratio * j + 1]
        earlier = jnp.where(lane > ratio * j, 1.0, 0.0)
        beats.append(jnp.where(vj > score, 1.0, jnp.where(vj == score, earlier, 0.0)))
    while len(beats) > 1:
        beats = [a + b for a, b in zip(beats[0::2], beats[1::2])] + ([beats[-1]] if len(beats) % 2 else [])
    rank = beats[0]
    cand = (lane % ratio == 0) & (lane < ratio * n_blocks)
    return jnp.where(cand & (rank < float(min(N_SEL, n_blocks))), 1.0, 0.0)


def _softmax_rows(s, valid):
    sm = jnp.where(valid, s, NEG)
    e = jnp.exp(sm - jnp.max(sm, axis=-1, keepdims=True))
    return e, 1.0 / jnp.sum(e, axis=-1, keepdims=True)


def _merge_and_project(o_a, sza, sga, mb, x, gate, wbra_ref, wout_ref):
    a = _dot((o_a * sza).astype(BF16), wbra_ref[...])
    m = sga * a + mb
    return x + gate * _dot(m.astype(BF16), wout_ref[...])


def _select_blocks_t(imp, qblk):
    n_blocks = imp.shape[0]
    blk = _iota(imp.shape, 0)
    forced = jnp.where((blk == 0) | (blk == qblk), 1.0, 0.0)
    score = jnp.where(blk <= qblk, imp + FORCE_BONUS * forced, NEG)
    rank = jnp.zeros(imp.shape, F32)
    for j in range(n_blocks):
        vj = score[j:j + 1, :]
        earlier = jnp.where(blk > j, 1.0, 0.0)
        rank = rank + jnp.where(vj > score, 1.0, jnp.where(vj == score, earlier, 0.0))
    return jnp.where(rank < float(min(N_SEL, n_blocks)), 1.0, 0.0)


def _softmax_cols(s):
    e = jnp.exp2(s - jnp.max(s, axis=0, keepdims=True))
    return e, 1.0 / jnp.sum(e, axis=0, keepdims=True)


def _p_attn_kernel(qt_ref, ks_ref, vst_ref, kw_ref, vwt_ref, kc_ref, vc_ref, gt_ref, sza_ref, sga_ref,
                   mb_ref, x_ref, gate_ref, wbra_ref, wout_ref, y_ref,
                   qt_scr, pen_scr, m_scr, acc_scr, s_even, s_odd, sw_scr, mw_scr, oa_scr, mprev_scr):
    tq = Q_BLOCK
    step = pl.program_id(0)
    i = lax.rem(jnp.minimum(step, pl.num_programs(0) - 2), ks_ref.shape[2] // tq)

    @pl.when(step == 0)
    def _():
        oa_scr[...] = jnp.zeros(oa_scr.shape, F32)
    n_cmp = kc_ref.shape[1]
    half = n_cmp // 2
    assert L_SLC == 2 * L_CMP and n_cmp <= LANES
    cols = GQA * tq
    qpos1 = i * tq + _iota((1, tq), 1)
    qpos = jnp.concatenate([qpos1] * GQA, axis=1)
    band = WINDOW + tq
    ws = pl.multiple_of(jnp.maximum(i * tq - WINDOW, 0), LANES)
    zeros_q = jnp.zeros((HEAD_DIM, cols), BF16)
    perm = lambda ref: jnp.concatenate([ref[0, pl.ds(0, half, stride=2), :], ref[0, pl.ds(1, half, stride=2), :],
                                        jnp.zeros((LANES - n_cmp, KV_WIDTH), F32)], axis=0)
    kc = perm(kc_ref).astype(BF16)
    vc_t = perm(vc_ref).T.astype(BF16)
    crow = _iota((LANES, cols), 0)
    cblk = 2 * (crow % half) + crow // half
    mc = ((cblk + 1) * L_CMP - 1 <= qpos) & (crow < n_cmp)
    gates_t = gt_ref[0]
    groups = [slice(g * HEAD_DIM, (g + 1) * HEAD_DIM) for g in range(N_KV)]

    for g in range(N_KV):
        qt_g = jnp.concatenate([qt_ref[0, h * HEAD_DIM:(h + 1) * HEAD_DIM, :]
                                for h in range(g * GQA, (g + 1) * GQA)], axis=1)
        qt_scr[g] = jnp.concatenate([qt_g, zeros_q] if g == 0 else [zeros_q, qt_g], axis=0)

    per_chunk = SEL_CHUNK // L_SLC
    assert band % SEL_CHUNK == 0
    w_chunks = [pl.ds(pl.multiple_of(ws + c * SEL_CHUNK, LANES), SEL_CHUNK) for c in range(band // SEL_CHUNK)]

    s_cmp = [_dot(kc, qt_scr[g]) for g in range(N_KV)]
    for c, kd in enumerate(w_chunks):
        k_rows = kw_ref[0, kd, :]
        for g in range(N_KV):
            sw_scr[g, c] = _dot(k_rows, qt_scr[g])

    def selected_scores(g, chunk):
        k_aug = ks_ref[0, g, pl.ds(pl.multiple_of(chunk * SEL_CHUNK, SEL_CHUNK), SEL_CHUNK), :]
        gap = jnp.zeros((HEAD_DIM - PEN_ROWS, cols), BF16)
        q_aug = ([qt_scr[0, 0:HEAD_DIM], pen_scr[0, chunk], gap] if g == 0 else
                 [pen_scr[1, chunk], gap, qt_scr[1, HEAD_DIM:2 * HEAD_DIM]])
        return _dot(k_aug, jnp.concatenate(q_aug, axis=0))

    o_c = []
    for g in range(N_KV):
        s_c = jnp.where(mc, s_cmp[g], NEG)
        e_c, r_c = _softmax_cols(s_c)
        p_c = jnp.where(mc, e_c * r_c, 0.0)
        o_c.append(_dot(vc_t[groups[g]], p_c.astype(BF16)))
        imp = p_c[:, 0:tq]
        for r in range(1, GQA):
            imp = imp + p_c[:, r * tq:(r + 1) * tq]
        sel = _select_blocks_t(imp[0:half] + imp[half:2 * half], qpos1 // L_SLC)
        pen = jnp.concatenate([jnp.where(sel > 0.5, 0.0, NEG)] * GQA, axis=1)
        pad_rows = jnp.zeros((PEN_ROWS - per_chunk, cols), F32)
        for c in range(half // per_chunk):
            pen_scr[g, c] = jnp.concatenate([pen[c * per_chunk:(c + 1) * per_chunk], pad_rows],
                                            axis=0).astype(BF16)
        s_even[g] = selected_scores(g, 0)

    for c in range(len(w_chunks)):
        kwpos = ws + c * SEL_CHUNK + _iota((SEL_CHUNK, tq), 0)
        bias_w = jnp.where((kwpos <= qpos1) & (kwpos > qpos1 - WINDOW), 0.0, NEG)
        bias_w = jnp.concatenate([bias_w] * GQA, axis=1)
        for g in range(N_KV):
            s_w = sw_scr[g, c] + bias_w
            sw_scr[g, c] = s_w
            m_c = jnp.max(s_w, axis=0, keepdims=True)
            mw_scr[g] = m_c if c == 0 else jnp.maximum(mw_scr[g], m_c)

    m_scr[...] = jnp.full(m_scr.shape, NEG, F32)
    acc_scr[...] = jnp.zeros(acc_scr.shape, F32)

    def with_ones(v_t):
        return jnp.concatenate([v_t.astype(BF16), jnp.ones((ONES_ROWS, v_t.shape[1]), BF16)], axis=0)

    def trip(kc_i, src, dst):
        off = pl.multiple_of(kc_i * SEL_CHUNK, SEL_CHUNK)
        if dst is None:
            t_prev = (oa_scr[...] * sza_ref[0]).astype(BF16)
            a_prev = []
        for g in range(N_KV):
            if dst is not None:
                dst[g] = selected_scores(g, kc_i + 1)
                s = src[g]
            else:
                half_n = D_MODEL // N_KV
                a_prev.append(_dot(t_prev, wbra_ref[:, g * half_n:(g + 1) * half_n]))
                causal = jnp.where(off + _iota((SEL_CHUNK, tq), 0) <= qpos1, 0.0, NEG)
                s = src[g] + jnp.concatenate([causal] * GQA, axis=1)
            m_old = m_scr[g]
            m_new = jnp.maximum(m_old, jnp.max(s, axis=0, keepdims=True))
            p = jnp.exp2(s - m_new).astype(BF16)
            m_scr[g] = m_new
            acc_scr[g] = jnp.exp2(m_old - m_new) * acc_scr[g] + _dot(
                with_ones(vst_ref[0, groups[g], pl.ds(off, SEL_CHUNK)]), p)
        if dst is None:
            mprev_scr[...] = (sga_ref[0] * jnp.concatenate(a_prev, axis=1) + mb_ref[0]).astype(BF16)

    def by_parity(kc_i, dst_wanted):
        @pl.when(lax.rem(kc_i, 2) == 0)
        def _():
            trip(kc_i, s_even, s_odd if dst_wanted else None)

        @pl.when(lax.rem(kc_i, 2) == 1)
        def _():
            trip(kc_i, s_odd, s_even if dst_wanted else None)

    def body(kc_i, carry):
        by_parity(kc_i, True)
        return carry

    assert SEL_CHUNK == tq
    lax.fori_loop(0, i, body, 0)
    by_parity(i, False)

    acc_w = [None] * N_KV

    def window_pass2():
        for c, kd in enumerate(w_chunks):
            for g in range(N_KV):
                pv = _dot(with_ones(vwt_ref[0, groups[g], kd]),
                          jnp.exp2(sw_scr[g, c] - mw_scr[g]).astype(BF16))
                acc_w[g] = pv if c == 0 else acc_w[g] + pv
                yield

    def previous_block_output():
        n_blk = 2 * LANES
        m = mprev_scr[...]
        for n0 in range(0, D_MODEL, n_blk):
            y_ref[0, :, n0:n0 + n_blk] = (x_ref[0, :, n0:n0 + n_blk]
                                          + gate_ref[0][:, n0:n0 + n_blk] * _dot(m, wout_ref[:, n0:n0 + n_blk]))
            yield

    for _ in itertools.zip_longest(window_pass2(), previous_block_output()):
        pass
    pair = []
    for g in range(N_KV):
        o_w = acc_w[g][0:HEAD_DIM] * (1.0 / acc_w[g][HEAD_DIM:HEAD_DIM + 1])
        acc_s = acc_scr[g]
        o_s = acc_s[0:HEAD_DIM] * (1.0 / acc_s[HEAD_DIM:HEAD_DIM + 1])
        for r in range(GQA):
            h = g * GQA + r
            cs = slice(r * tq, (r + 1) * tq)
            pair.append(gates_t[3 * h:3 * h + 1] * o_c[g][:, cs] + gates_t[3 * h + 1:3 * h + 2] * o_s[:, cs]
                        + gates_t[3 * h + 2:3 * h + 3] * o_w[:, cs])
    per_lane = LANES // HEAD_DIM
    oa_scr[...] = jnp.concatenate([jnp.concatenate(pair[j:j + per_lane], axis=0).T
                                   for j in range(0, N_HEADS, per_lane)], axis=1)


def _p_attn_call(q_t, ks_rows, vs_t, kw_rows, vw_t, kc, vc, gates_t, sza, sga, mb, x, gate, w_br_a, w_out):
    b, s, _ = x.shape
    tq = Q_BLOCK
    n_i = s // tq
    n_blocks = b * n_i
    att = lambda j: jnp.minimum(j, n_blocks - 1)
    out = lambda j: jnp.maximum(j - 1, 0)
    row = lambda w: pl.BlockSpec((1, tq, w), lambda j: (out(j) // n_i, out(j) % n_i, 0))
    col = lambda a: pl.BlockSpec((1, a.shape[1], tq), lambda j: (att(j) // n_i, 0, att(j) % n_i))
    seq = lambda a: pl.BlockSpec((1,) + a.shape[1:], lambda j: (att(j) // n_i,) + (0,) * (a.ndim - 1))
    out_seq = lambda a: pl.BlockSpec((1,) + a.shape[1:], lambda j: (out(j) // n_i,) + (0,) * (a.ndim - 1))
    in_specs = [col(q_t), seq(ks_rows), seq(vs_t), seq(kw_rows), seq(vw_t), seq(kc), seq(vc),
                col(gates_t), row(WIDTH_A), row(D_MODEL), row(D_MODEL), row(D_MODEL), out_seq(gate),
                _full(w_br_a.shape), _full(w_out.shape)]
    return pl.pallas_call(
        _p_attn_kernel,
        grid=(n_blocks + 1,),
        in_specs=in_specs,
        out_specs=row(D_MODEL),
        out_shape=jax.ShapeDtypeStruct((b, s, D_MODEL), F32),
        scratch_shapes=[pltpu.VMEM((N_KV, KV_WIDTH, GQA * tq), BF16),
                        pltpu.VMEM((N_KV, s // SEL_CHUNK, PEN_ROWS, GQA * tq), BF16),
                        pltpu.VMEM((N_KV, 1, GQA * tq), F32),
                        pltpu.VMEM((N_KV, HEAD_DIM + ONES_ROWS, GQA * tq), F32),
                        pltpu.VMEM((N_KV, SEL_CHUNK, GQA * tq), F32),
                        pltpu.VMEM((N_KV, SEL_CHUNK, GQA * tq), F32),
                        pltpu.VMEM((N_KV, (WINDOW + tq) // SEL_CHUNK, SEL_CHUNK, GQA * tq), F32),
                        pltpu.VMEM((N_KV, 1, GQA * tq), F32),
                        pltpu.VMEM((tq, WIDTH_A), F32),
                        pltpu.VMEM((tq, D_MODEL), BF16)],
        compiler_params=pltpu.CompilerParams(dimension_semantics=("arbitrary",),
                                             vmem_limit_bytes=VMEM_LIMIT),
        name="p_attn",
    )(q_t, ks_rows, vs_t, kw_rows, vw_t, kc, vc, gates_t, sza, sga, mb, x, gate, w_br_a, w_out)


def _s_proj_kernel(x_ref, shift_ref, scale_ref, ng_ref, w_ref, cos_ref, sin_ref, qg_ref, kg_ref,
                   vng_ref, vnb_ref, ws0_ref, bs0_ref, wbrb_ref,
                   q_out, k_out, v_out, kt_out, vt_out, gates_out, sza_out, sga_out, mb_out, vn_out):
    seg = _project(x_ref[...], shift_ref[...], scale_ref[...], ng_ref[...], w_ref)
    cos, sin = cos_ref[...], sin_ref[...]
    q_out[...] = _norm_rope(seg(C_Q, C_K), qg_ref[...], cos, sin)
    k = _norm_rope(seg(C_K, C_V), kg_ref[...], cos, sin)
    v = seg(C_V, C_G)
    k_out[...] = k
    v_out[...] = v
    for br in range(3):
        sl = slice(br * KV_WIDTH, (br + 1) * KV_WIDTH)
        kt_out[br] = k[:, sl].T
        vt_out[br] = v[:, sl].T
    gates_out[...] = jax.nn.sigmoid(seg(C_G, C_ZA))
    sza_out[...] = _silu(seg(C_ZA, C_U))
    sga_out[...] = jax.nn.sigmoid(seg(C_GA, C_GB))
    vn = _layer_norm(seg(C_VB, C_ZB), vng_ref[...], vnb_ref[...])
    vn_out[...] = vn
    s_b = ws0_ref[...] * vn + bs0_ref[...]
    t = seg(C_U, C_VB) * s_b * _silu(seg(C_ZB, C_GA))
    mb_out[...] = jax.nn.sigmoid(seg(C_GB, C_END)) * _dot(t.astype(BF16), wbrb_ref[...])


def _s_proj_call(x, shift, scale, norm_g, w_all, cos1, sin1, qg, kg, vng, vnb, ws0, bs0, w_br_b):
    n = x.shape[0]
    args = (x, shift, scale, norm_g, w_all, cos1, sin1, qg, kg, vng, vnb, ws0, bs0, w_br_b)
    sds = lambda *shape: jax.ShapeDtypeStruct(shape, F32)
    out_shape = [sds(n, WIDTH_A), sds(n, 3 * KV_WIDTH), sds(n, 3 * KV_WIDTH), sds(3, KV_WIDTH, n),
                 sds(3, KV_WIDTH, n), sds(n, LANES), sds(n, WIDTH_A), sds(n, D_MODEL), sds(n, D_MODEL),
                 sds(n, WIDTH_B)]
    return pl.pallas_call(
        _s_proj_kernel,
        grid=(1,),
        in_specs=[_full(a.shape) for a in args],
        out_specs=[_full(o.shape) for o in out_shape],
        out_shape=out_shape,
        compiler_params=pltpu.CompilerParams(vmem_limit_bytes=VMEM_LIMIT),
        name="s_proj",
    )(*args)


def _s_attn_kernel(pt_ref, q_ref, gates_ref, knew_ref, vnew_ref, knewt_ref, vnewt_ref, kwin_ref, vwin_ref,
                   pek_ref, pev_ref, wck_ref, wcv_ref, pool_ref,
                   kc_hbm, vc_hbm, ks_hbm, vs_hbm,
                   oa_ref, okw_ref, ovw_ref, buf, sem):
    t = pl.program_id(0)
    n_groups = pl.num_programs(0) - 1
    per_step = q_ref.shape[0]
    n_seqs = pt_ref.shape[0]
    n_pages = pt_ref.shape[1]
    past = n_pages * PAGE_SIZE
    caches = (kc_hbm, vc_hbm, ks_hbm, vs_hbm)

    def page_copies(group, slot_):
        return [pltpu.make_async_copy(hbm.at[pt_ref[jnp.minimum(group * per_step + j, n_seqs - 1), p]],
                                      buf.at[slot_, j, c, :, pl.ds(p * PAGE_SIZE, PAGE_SIZE)],
                                      sem.at[slot_, c])
                for j in range(per_step) for c, hbm in enumerate(caches) for p in range(n_pages)]

    @pl.when(t < n_groups)
    def _():
        for cp in page_copies(t, lax.rem(t, 2)):
            cp.start()

    @pl.when(t > 0)
    def _():
        group = t - 1
        slot = lax.rem(group, 2)
        for cp in page_copies(group, slot):
            cp.wait()
        chains = [_s_attn_one(group * per_step + j, j, buf.at[slot, j], q_ref, gates_ref, knew_ref, vnew_ref,
                              knewt_ref, vnewt_ref, kwin_ref, vwin_ref, pek_ref, pev_ref, wck_ref, wcv_ref,
                              pool_ref, oa_ref, okw_ref, ovw_ref, past) for j in range(per_step)]
        for _ in itertools.zip_longest(*chains):
            pass


def _s_attn_one(b, j, buf, q_ref, gates_ref, knew_ref, vnew_ref, knewt_ref, vnewt_ref, kwin_ref, vwin_ref,
                pek_ref, pev_ref, wck_ref, wcv_ref, pool_ref, oa_ref, okw_ref, ovw_ref, past):
    qpos = past

    lane = _iota((KV_WIDTH, knewt_ref.shape[2]), 1)
    col = lambda ref, br: jnp.sum(jnp.where(lane == b, ref[br], 0.0), axis=1, keepdims=True)
    knew = knew_ref[pl.ds(b, 1), :]
    vnew = vnew_ref[pl.ds(b, 1), :]

    hrow = _iota((N_HEADS, 1), 0)
    first_group = hrow < GQA
    by_group = lambda f: jnp.where(first_group, f(0), f(1))
    gl = lambda g: slice(g * HEAD_DIM, (g + 1) * HEAD_DIM)
    qb = q_ref[j].astype(BF16)

    n_cmp = -(-(past + 1) // L_SLC) * L_SLC // L_CMP
    n_slc = n_cmp * L_CMP // L_SLC

    def summaries(c, new_row, pe_ref, w_ref):
        chunk = pool_ref.shape[1]
        sums = []
        for t0 in range(0, past, chunk):
            hi, lo = _split_bf16(buf[c, :, t0:t0 + chunk])
            sums.append(_dot_nt(pool_ref[...], hi) + _dot_nt(pool_ref[...], lo))
        pe_sum = jnp.sum(pe_ref[...], axis=0, keepdims=True)
        pooled = (jnp.concatenate(sums, axis=0) + pe_sum) * (1.0 / L_CMP)
        r = _iota((SUBLANES, KV_WIDTH), 0)
        tail = jnp.where(r == 0, new_row + pe_sum, jnp.where(r == 1, pe_sum, 0.0)) * (1.0 / L_CMP)
        zeros = jnp.zeros((LANES - pooled.shape[0] - SUBLANES, KV_WIDTH), F32)
        return _dot(jnp.concatenate([pooled, tail, zeros], axis=0).astype(BF16), w_ref[...])

    kc = summaries(0, knew[:, 0:KV_WIDTH], pek_ref, wck_ref).astype(BF16)
    yield
    vc = summaries(1, vnew[:, 0:KV_WIDTH], pev_ref, wcv_ref).astype(BF16)
    yield
    s_c = by_group(lambda g: _dot_nt(qb, kc[:, gl(g)])) * SM_SCALE
    c = _iota(s_c.shape, 1)
    mc = ((c + 1) * L_CMP - 1 <= qpos) & (c < n_cmp)
    e_c, r_c = _softmax_rows(s_c, mc)
    p_c = jnp.where(mc, e_c * r_c, 0.0)
    o_c = by_group(lambda g: _dot(p_c.astype(BF16), vc[:, gl(g)]))
    imp = by_group(lambda g: jnp.sum(p_c[g * GQA:(g + 1) * GQA], axis=0, keepdims=True))
    imp = jnp.broadcast_to(imp, p_c.shape)
    yield
    sel = _select_blocks(imp, jnp.full((N_HEADS, 1), qpos // L_SLC, jnp.int32), n_slc)
    yield

    ratio = L_SLC // L_CMP
    first_blk = _iota((N_HEADS, LANES), 1) < L_SLC
    blk_col = lambda j: sel[:, ratio * j:ratio * j + 1]
    picked = jnp.concatenate([jnp.where(first_blk, blk_col(2 * c), blk_col(2 * c + 1))
                              for c in range(past // LANES)], axis=1)
    s_s = by_group(lambda g: _dot(qb, buf[2, gl(g), :].astype(BF16))) * SM_SCALE
    kpos = _iota(s_s.shape, 1)
    s_s = jnp.where((picked > 0.5) & (kpos <= qpos), s_s, NEG)
    yield
    own =_iota((N_HEADS, KV_WIDTH), 1) // HEAD_DIM == hrow // GQA
    rounded = lambda a: a.astype(BF16).astype(F32)
    q_pair = jnp.concatenate([qb.astype(F32)] * N_KV, axis=1)
    s_new = jnp.sum(jnp.where(own, q_pair * rounded(knew[:, KV_WIDTH:2 * KV_WIDTH]), 0.0),
                    axis=1, keepdims=True) * SM_SCALE
    new_lane = (L_SLC // L_CMP) * (past // L_SLC)
    s_new = jnp.where((sel[:, new_lane:new_lane + 1] > 0.5) & (past <= qpos), s_new, NEG)
    m_s = jnp.maximum(jnp.max(s_s, axis=-1, keepdims=True), s_new)
    e_s, e_new = jnp.exp(s_s - m_s), jnp.exp(s_new - m_s)
    r_s = 1.0 / (jnp.sum(e_s, axis=-1, keepdims=True) + e_new)
    v_new = by_group(lambda g: rounded(vnew[:, KV_WIDTH + g * HEAD_DIM:KV_WIDTH + (g + 1) * HEAD_DIM]))
    o_s = (by_group(lambda g: _dot_nt(e_s.astype(BF16), buf[3, gl(g), :].astype(BF16)))
           + rounded(e_new) * v_new) * r_s
    yield

    wb = kwin_ref.shape[2]
    wlane = _iota((KV_WIDTH, wb), 1)
    kw = jnp.where(wlane == wb - 1, col(knewt_ref, 2), pltpu.roll(kwin_ref[j], wb - 1, 1))
    vw = jnp.where(wlane == wb - 1, col(vnewt_ref, 2), pltpu.roll(vwin_ref[j], wb - 1, 1))
    okw_ref[j] = kw
    ovw_ref[j] = vw
    yield
    s_w = by_group(lambda g: _dot(qb, kw[gl(g)].astype(BF16))) * SM_SCALE
    kwpos = past - wb + 1 + _iota(s_w.shape, 1)
    e_w, r_w = _softmax_rows(s_w, (kwpos <= qpos) & (kwpos > qpos - WINDOW) & (kwpos >= 0))
    o_w = by_group(lambda g: _dot_nt(e_w.astype(BF16), vw[gl(g)].astype(BF16))) * r_w

    gates = gates_ref[j]
    oa_ref[j] = gates[:, 0:1] * o_c + gates[:, 1:2] * o_s + gates[:, 2:3] * o_w


def _s_attn_call(page_table, q3, gates3, knew, vnew, knew_t, vnew_t, kwin_t, vwin_t, pek, pev, wck, wcv,
                 kc_pool, vc_pool, ks_pool, vs_pool):
    n, n_pages = page_table.shape
    past = n_pages * PAGE_SIZE
    wb = kwin_t.shape[2]
    tok = np.arange(SUMMARY_CHUNK)
    assert past % SUMMARY_CHUNK == 0
    pool = jnp.asarray((tok[None, :] // L_CMP == np.arange(SUMMARY_CHUNK // L_CMP)[:, None]), BF16)
    k = SAMPLE_SEQS_PER_STEP
    assert n % k == 0
    per_seq = lambda a: pl.BlockSpec((k,) + a.shape[1:],
                                     lambda t, pt: (jnp.maximum(t - 1, 0),) + (0,) * (a.ndim - 1))
    full = lambda a: pl.BlockSpec(a.shape, lambda t, pt: (0,) * a.ndim)
    hbm = pl.BlockSpec(memory_space=pl.ANY)
    resident = (knew, vnew, knew_t, vnew_t)
    consts = (pek, pev, wck, wcv, pool)
    grid_spec = pltpu.PrefetchScalarGridSpec(
        num_scalar_prefetch=1,
        grid=(n // k + 1,),
        in_specs=[per_seq(q3), per_seq(gates3)] + [full(a) for a in resident]
                 + [per_seq(kwin_t), per_seq(vwin_t)] + [full(a) for a in consts] + [hbm] * 4,
        out_specs=[per_seq(q3), per_seq(kwin_t), per_seq(vwin_t)],
        scratch_shapes=[pltpu.VMEM((2, k, 4, KV_WIDTH, past), F32), pltpu.SemaphoreType.DMA((2, 4))],
    )
    return pl.pallas_call(
        _s_attn_kernel,
        grid_spec=grid_spec,
        out_shape=[jax.ShapeDtypeStruct(q3.shape, F32), jax.ShapeDtypeStruct(kwin_t.shape, F32),
                   jax.ShapeDtypeStruct(vwin_t.shape, F32)],
        compiler_params=pltpu.CompilerParams(dimension_semantics=("arbitrary",),
                                             vmem_limit_bytes=VMEM_LIMIT),
        name="s_attn",
    )(page_table, q3, gates3, knew, vnew, knew_t, vnew_t, kwin_t, vwin_t, pek, pev, wck, wcv, pool,
      kc_pool, vc_pool, ks_pool, vs_pool)


def _s_out_kernel(oa_ref, sza_ref, sga_ref, mb_ref, x_ref, gate_ref, wbra_ref, wout_ref, y_ref):
    y_ref[...] = _merge_and_project(oa_ref[...], sza_ref[...], sga_ref[...], mb_ref[...], x_ref[...],
                                    gate_ref[...], wbra_ref, wout_ref)


def _s_out_call(o_a, sza, sga, mb, x, gate, w_br_a, w_out):
    args = (o_a, sza, sga, mb, x, gate, w_br_a, w_out)
    return pl.pallas_call(
        _s_out_kernel,
        grid=(1,),
        in_specs=[_full(a.shape) for a in args],
        out_specs=_full(x.shape),
        out_shape=jax.ShapeDtypeStruct(x.shape, F32),
        compiler_params=pltpu.CompilerParams(vmem_limit_bytes=VMEM_LIMIT),
        name="s_out",
    )(*args)


def _rope_angles(pos):
    half = HEAD_DIM // 2
    inv = ROPE_THETA ** (-jnp.arange(half, dtype=F32) * 2.0 / HEAD_DIM)
    return pos.astype(F32)[:, None] * inv[None, :]


def _rope_tables(pos):
    ang = _rope_angles(pos)
    cos, sin = jnp.cos(ang), jnp.sin(ang)
    cos_t = jnp.concatenate([cos, cos] * (LANES // HEAD_DIM), axis=1)
    sin_t = jnp.concatenate([-sin, sin] * (LANES // HEAD_DIM), axis=1)
    return cos_t, sin_t


def _token_minor(a):
    b, t = a.shape[:2]
    return jnp.transpose(a, (0, 2, 3, 1)).reshape(b, KV_WIDTH, t)


def _token_major(a_t):
    b, _, t = a_t.shape
    return jnp.transpose(a_t.reshape(b, N_KV, HEAD_DIM, t), (0, 3, 1, 2))


def kernel(x_prompt, x_sample, cache_k_cmp, cache_v_cmp, cache_k_slc, cache_v_slc, cache_k_win, cache_v_win, page_table, c_prompt, c_sample, w_ada, b_ada, norm_g, w_in, q_norm_g, k_norm_g, cmp_pos_k, cmp_pos_v, w_cmp_k, w_cmp_v, vnorm_g, vnorm_b, w_s, b_s, w_br_a, w_br_b, w_out):
    assert w_ada.shape[0] == 1, "single layer"
    b, s, _ = x_prompt.shape
    n = x_sample.shape[0]
    assert x_sample.shape[1] == 1
    n_pages = page_table.shape[1]
    past = n_pages * PAGE_SIZE

    w_all = _w_pack_call(w_in[0].T)
    eye = jnp.eye(N_KV, dtype=F32)
    wck = jnp.kron(eye, w_cmp_k[0]).astype(BF16)
    wcv = jnp.kron(eye, w_cmp_v[0]).astype(BF16)
    pek = jnp.tile(cmp_pos_k[0], (1, N_KV))
    pev = jnp.tile(cmp_pos_v[0], (1, N_KV))
    qg = jnp.tile(q_norm_g, (1, LANES // HEAD_DIM))
    kg = jnp.tile(k_norm_g, (1, LANES // HEAD_DIM))
    w_br_a_b, w_br_b_b, w_out_b = w_br_a[0].astype(BF16), w_br_b[0].astype(BF16), w_out[0].astype(BF16)

    mod = _ada_call(jnp.concatenate([c_prompt, c_sample], axis=0), w_ada[0], b_ada)
    shift, scale, gate = mod[:, :D_MODEL], mod[:, D_MODEL:2 * D_MODEL], mod[:, 2 * D_MODEL:]

    ang_p = _rope_angles(jnp.arange(s, dtype=jnp.int32)).T
    gain_cols = lambda g: jnp.broadcast_to(g[0][:, None], (HEAD_DIM, PROMPT_SUB_ROWS))
    (q_t, kc_t, ks_t, vc_t, vs_t, vw_t, ks_rows, kw_rows, kcmp, vcmp, gates_t, sza, sga, mb,
     kw_last, vw_last) = _p_proj_call(
        x_prompt, shift[:b, None], scale[:b, None], norm_g, w_all, jnp.cos(ang_p), jnp.sin(ang_p),
        gain_cols(q_norm_g), gain_cols(k_norm_g), pek, pev, wck, wcv,
        vnorm_g, vnorm_b, w_s[0], b_s[0].T, w_br_b_b)
    y_prompt = _p_attn_call(q_t, ks_rows, vs_t, kw_rows, vw_t, kcmp, vcmp, gates_t, sza, sga, mb, x_prompt,
                            gate[:b, None], w_br_a_b, w_out_b)
    p_states = [_token_major(a)[None] for a in (kc_t, vc_t, ks_t, vs_t, kw_last, vw_last)]

    xs = x_sample.reshape(n, D_MODEL)
    cos_s, sin_s = _rope_tables(jnp.full((1,), past, jnp.int32))
    ws0 = jnp.repeat(w_s[0, :, 0, 0], GROUP_W_B)[None]
    bs0 = jnp.repeat(b_s[0, :, 0], GROUP_W_B)[None]
    (q_s, k_s, v_s, kt_s, vt_s, gates_s, sza_s, sga_s, mb_s, vn_s) = _s_proj_call(
        xs, shift[b:], scale[b:], norm_g, w_all, cos_s, sin_s, qg, kg, vnorm_g, vnorm_b, ws0, bs0, w_br_b_b)
    pools = [_token_minor(c[0]) for c in (cache_k_cmp, cache_v_cmp, cache_k_slc, cache_v_slc)]
    o_a, kwin_new, vwin_new = _s_attn_call(
        page_table, q_s.reshape(n, N_HEADS, HEAD_DIM), gates_s[:, :3 * N_HEADS].reshape(n, N_HEADS, 3),
        k_s, v_s, kt_s, vt_s, _token_minor(cache_k_win[0]), _token_minor(cache_v_win[0]),
        pek, pev, wck, wcv, *pools)
    y_sample = _s_out_call(o_a.reshape(n, WIDTH_A), sza_s, sga_s, mb_s, xs, gate[b:], w_br_a_b, w_out_b)

    new_rows = lambda t, br: jnp.transpose(t[br].reshape(N_KV, HEAD_DIM, n), (2, 0, 1))[None, :, None]
    s_states = [new_rows(kt_s, 0), new_rows(vt_s, 0), new_rows(kt_s, 1), new_rows(vt_s, 1),
                _token_major(kwin_new)[None], _token_major(vwin_new)[None], vn_s[None, :, None]]
    return (y_prompt, y_sample.reshape(n, 1, D_MODEL), *p_states, *s_states)
```

```python
import itertools

import numpy as np
import jax
import jax.numpy as jnp
from jax import lax
from jax.experimental import pallas as pl
from jax.experimental.pallas import tpu as pltpu

F32 = jnp.float32
BF16 = jnp.bfloat16

D_MODEL = 1024
HEAD_DIM = 64
N_HEADS = 8
N_KV = 2
GQA = N_HEADS // N_KV
WIDTH_A = N_HEADS * HEAD_DIM
KV_WIDTH = N_KV * HEAD_DIM
L_CMP = 32
L_SLC = 64
N_SEL = 8
WINDOW = 512
Q_BLOCK = 256
FORCE_BONUS = 1.0e4
ROPE_THETA = 10000.0
CHUNK = 128
N_GROUPS_B = 4
WIDTH_B = 512
GROUP_W_B = WIDTH_B // N_GROUPS_B
PAGE_SIZE = 128
EPS = 1e-6
NEG = -1e30
SM_SCALE = HEAD_DIM ** -0.5
LOG2_E = 1.4426950408889634

LANES = 128
SUBLANES = 8
VMEM_LIMIT = 56 * 1024 * 1024

C_Q = 0
C_K = C_Q + WIDTH_A
C_V = C_K + 3 * KV_WIDTH
C_G = C_V + 3 * KV_WIDTH
C_ZA = C_G + LANES
C_U = C_ZA + WIDTH_A
C_VB = C_U + WIDTH_B
C_ZB = C_VB + WIDTH_B
C_GA = C_ZB + WIDTH_B
C_GB = C_GA + D_MODEL
C_END = C_GB + D_MODEL

PROMPT_ROWS = 512
PROMPT_SUB_ROWS = 256
SEL_CHUNK = 256
PEN_ROWS = 16
ONES_ROWS = 16
SUMMARY_CHUNK = 512
SAMPLE_SEQS_PER_STEP = 4


def _dot(a, b):
    return jnp.dot(a, b, preferred_element_type=F32)


def _dot_nt(a, b):
    return lax.dot_general(a, b, (((1,), (1,)), ((), ())), preferred_element_type=F32)


def _iota(shape, dim):
    return lax.broadcasted_iota(jnp.int32, shape, dim)


def _split_bf16(x):
    hi = x.astype(BF16)
    lo = (x - hi.astype(F32)).astype(BF16)
    return hi, lo


def _head_mean_sq(x):
    w = x.shape[1]
    ones_bd = jnp.where(_iota((LANES, LANES), 0) // HEAD_DIM == _iota((LANES, LANES), 1) // HEAD_DIM,
                        1.0, 0.0).astype(BF16)
    hi, lo = _split_bf16(x * x)
    cols = []
    for c in range(w // LANES):
        sl = slice(c * LANES, (c + 1) * LANES)
        cols.append(_dot(hi[:, sl], ones_bd) + _dot(lo[:, sl], ones_bd))
    return jnp.concatenate(cols, axis=1) * (1.0 / HEAD_DIM)


def _tile_lanes(t, width):
    return jnp.concatenate([t] * (width // t.shape[1]), axis=1)


def _norm_rope(x, g, cos, sin):
    w = x.shape[1]
    y = x * lax.rsqrt(_head_mean_sq(x) + EPS) * _tile_lanes(g, w)
    first_half = (_iota(y.shape, 1) % HEAD_DIM) < (HEAD_DIM // 2)
    rot = jnp.where(first_half, pltpu.roll(y, w - HEAD_DIM // 2, 1), pltpu.roll(y, HEAD_DIM // 2, 1))
    return y * _tile_lanes(cos, w) + rot * _tile_lanes(sin, w)


def _norm_rope_t(x_t, gain_t, cos_t, sin_t):
    half = HEAD_DIM // 2
    out = []
    for h in range(x_t.shape[0] // HEAD_DIM):
        x = x_t[h * HEAD_DIM:(h + 1) * HEAD_DIM]
        y = x * lax.rsqrt(jnp.mean(x * x, axis=0, keepdims=True) + EPS) * gain_t
        y1, y2 = y[:half], y[half:]
        out += [y1 * cos_t - y2 * sin_t, y2 * cos_t + y1 * sin_t]
    return jnp.concatenate(out, axis=0)


def _silu(z):
    return z * jax.nn.sigmoid(z)


def _project(x, shift, scale, norm_g, w_ref):
    ms = jnp.mean(x * x, axis=-1, keepdims=True)
    h = (x * lax.rsqrt(ms + EPS) * norm_g) * (1.0 + scale) + shift
    hb = h.astype(BF16)
    return lambda lo, hi: _dot(hb, w_ref[:, lo:hi])


def _layer_norm(v, g, b):
    mu = jnp.mean(v, axis=-1, keepdims=True)
    d = v - mu
    var = jnp.mean(d * d, axis=-1, keepdims=True)
    return d * lax.rsqrt(var + EPS) * g + b


def _ada_kernel(c_ref, w_ref, b_ref, o_ref):
    o_ref[...] = _dot(c_ref[...].astype(BF16), w_ref[...].astype(BF16)) + b_ref[...]


W_PACK_COLS = 512
W_GAP_LO = C_G + 3 * N_HEADS
W_GAP_HI = C_G + LANES
W_PACK_BLOCKS = -(-C_END // W_PACK_COLS)


def _w_pack_kernel(wt_ref, o_ref):
    j = pl.program_id(0)
    w = W_PACK_COLS
    gap_block, last = W_GAP_LO // w, W_PACK_BLOCKS - 1
    d_in = C_END - (W_GAP_HI - W_GAP_LO)

    def emit(rows):
        for s in range(w // LANES):
            blk = rows[s * LANES:(s + 1) * LANES]
            o_ref[:, s * LANES:(s + 1) * LANES] = jnp.concatenate(
                [blk[:, c * LANES:(c + 1) * LANES].T for c in range(D_MODEL // LANES)], axis=0).astype(BF16)

    @pl.when((j != gap_block) & (j != last))
    def _():
        emit(wt_ref[...])

    @pl.when(j == gap_block)
    def _():
        k0, k1 = W_GAP_LO - gap_block * w, W_GAP_HI - gap_block * w
        emit(jnp.concatenate([wt_ref[0:k0, :], jnp.zeros((k1 - k0, D_MODEL), F32), wt_ref[k0:k0 + w - k1, :]],
                             axis=0))

    @pl.when(j == last)
    def _():
        shift = (last * w - (W_GAP_HI - W_GAP_LO)) - (d_in - w)
        n_valid = C_END - last * w
        emit(jnp.concatenate([wt_ref[shift:shift + n_valid, :], jnp.zeros((w - n_valid, D_MODEL), F32)], axis=0))


def _w_pack_call(w_t):
    d_in = w_t.shape[0]
    w = W_PACK_COLS
    pad = W_GAP_HI - W_GAP_LO
    gap_block = W_GAP_LO // w
    assert d_in + pad == C_END and W_GAP_HI <= (gap_block + 1) * w and gap_block < W_PACK_BLOCKS - 1
    assert W_GAP_LO % SUBLANES == 0 and pad % SUBLANES == 0
    src_row = lambda j: (pl.multiple_of(
        jnp.minimum(jnp.where(j <= gap_block, j * w, j * w - pad), d_in - w), SUBLANES), 0)
    return pl.pallas_call(
        _w_pack_kernel,
        grid=(W_PACK_BLOCKS,),
        in_specs=[pl.BlockSpec((pl.Element(w), pl.Element(D_MODEL)), src_row)],
        out_specs=pl.BlockSpec((D_MODEL, w), lambda j: (0, j)),
        out_shape=jax.ShapeDtypeStruct((D_MODEL, W_PACK_BLOCKS * w), BF16),
        compiler_params=pltpu.CompilerParams(vmem_limit_bytes=VMEM_LIMIT),
        name="w_pack",
    )(w_t)


def _ada_call(c_all, w_ada, b_ada):
    rows = c_all.shape[0]
    n = w_ada.shape[1]
    return pl.pallas_call(
        _ada_kernel,
        grid=(n // D_MODEL,),
        in_specs=[pl.BlockSpec((rows, D_MODEL), lambda j: (0, 0)),
                  pl.BlockSpec((D_MODEL, D_MODEL), lambda j: (0, j)),
                  pl.BlockSpec((1, D_MODEL), lambda j: (0, j))],
        out_specs=pl.BlockSpec((rows, D_MODEL), lambda j: (0, j)),
        out_shape=jax.ShapeDtypeStruct((rows, n), F32),
        compiler_params=pltpu.CompilerParams(vmem_limit_bytes=VMEM_LIMIT),
        name="ada",
    )(c_all, w_ada, b_ada)


def _compress_rows(rows, pe, w_bd):
    t = rows.shape[0]
    pooled = jnp.sum(rows.reshape(t // L_CMP, L_CMP, KV_WIDTH) + pe[None], axis=1) * (1.0 / L_CMP)
    return _dot(pooled.astype(BF16), w_bd)


def _p_proj_kernel(x_ref, shift_ref, scale_ref, ng_ref, w_ref, cos_ref, sin_ref, qg_ref, kg_ref,
                   pek_ref, pev_ref, wck_ref, wcv_ref, vng_ref, vnb_ref, ws_ref, bst_ref, wbrb_ref,
                   *outs):
    tm = x_ref.shape[1]
    chains = [_p_proj_rows(r0, x_ref, shift_ref, scale_ref, ng_ref, w_ref, cos_ref, sin_ref, qg_ref, kg_ref,
                           pek_ref, pev_ref, wck_ref, wcv_ref, vng_ref, vnb_ref, ws_ref, bst_ref, wbrb_ref, *outs)
              for r0 in range(0, tm, PROMPT_SUB_ROWS)]
    for _ in itertools.zip_longest(*chains):
        pass


def _p_proj_rows(r0, x_ref, shift_ref, scale_ref, ng_ref, w_ref, cos_ref, sin_ref, qg_ref, kg_ref,
                 pek_ref, pev_ref, wck_ref, wcv_ref, vng_ref, vnb_ref, ws_ref, bst_ref, wbrb_ref,
                 qt_out, kc_t, ks_t, vc_t, vs_t, vw_t, ks_rows, kw_rows, kcmp_out, vcmp_out,
                 gates_t, sza_out, sga_out, mb_out, kw_last, vw_last):
    sub = PROMPT_SUB_ROWS
    rs = slice(r0, r0 + sub)
    cmp_rows = slice(r0 // L_CMP, (r0 + sub) // L_CMP)
    seg = _project(x_ref[0, rs, :], shift_ref[0], scale_ref[0], ng_ref[...], w_ref)
    cos, sin = cos_ref[:, rs], sin_ref[:, rs]
    to_token_minor = lambda a: jnp.concatenate(
        [a[:, c * LANES:(c + 1) * LANES].T for c in range(a.shape[1] // LANES)], axis=0)
    yield

    qt_out[0, :, rs] = (_norm_rope_t(to_token_minor(seg(C_Q, C_K)), qg_ref[...], cos, sin)
                        * (SM_SCALE * LOG2_E)).astype(BF16)
    yield
    k_t = _norm_rope_t(to_token_minor(seg(C_K, C_V)), kg_ref[...], cos, sin)
    yield
    v = seg(C_V, C_G)
    for br, (k_out, v_out) in enumerate(((kc_t, vc_t), (ks_t, vs_t), (kw_last, vw_t))):
        sl = slice(br * KV_WIDTH, (br + 1) * KV_WIDTH)
        k_out[0, :, rs] = k_t[sl]
        v_t = v[:, sl].T
        v_out[0, :, rs] = v_t
    vw_last[0, :, rs] = v_t
    blk_id = jnp.where((_iota((HEAD_DIM, sub), 1) // L_SLC) % (SEL_CHUNK // L_SLC) == _iota((HEAD_DIM, sub), 0),
                       1.0, 0.0)
    ks_g = [k_t[KV_WIDTH + g * HEAD_DIM:KV_WIDTH + (g + 1) * HEAD_DIM] for g in range(N_KV)]
    ks_rows[0, 0, rs, :] = jnp.concatenate([ks_g[0], blk_id], axis=0).T.astype(BF16)
    ks_rows[0, 1, rs, :] = jnp.concatenate([blk_id, ks_g[1]], axis=0).T.astype(BF16)
    kw_rows[0, rs, :] = k_t[2 * KV_WIDTH:3 * KV_WIDTH].T.astype(BF16)
    yield
    kcmp_out[0, cmp_rows, :] = _compress_rows(k_t[0:KV_WIDTH].T, pek_ref[...], wck_ref[...])
    vcmp_out[0, cmp_rows, :] = _compress_rows(v[:, 0:KV_WIDTH], pev_ref[...], wcv_ref[...])
    yield

    gates_t[0, :, rs] = jax.nn.sigmoid(seg(C_G, C_ZA)).T
    sza_out[0, rs, :] = _silu(seg(C_ZA, C_U))
    yield
    sga_out[0, rs, :] = jax.nn.sigmoid(seg(C_GA, C_GB))
    yield

    vn = _layer_norm(seg(C_VB, C_ZB), vng_ref[...], vnb_ref[...]).astype(BF16)
    yield
    causal = _iota((CHUNK, CHUNK), 0) >= _iota((CHUNK, CHUNK), 1)
    chunks = []
    for c in range(sub // CHUNK):
        groups = []
        for g in range(N_GROUPS_B):
            wsg = jnp.where(causal, ws_ref[g], 0.0).astype(BF16)
            vg = vn[c * CHUNK:(c + 1) * CHUNK, g * GROUP_W_B:(g + 1) * GROUP_W_B]
            groups.append(_dot(wsg, vg) + bst_ref[:, g:g + 1])
        chunks.append(jnp.concatenate(groups, axis=1))
    s_b = jnp.concatenate(chunks, axis=0)
    yield
    t = seg(C_U, C_VB) * s_b * _silu(seg(C_ZB, C_GA))
    yield
    mb_out[0, rs, :] = jax.nn.sigmoid(seg(C_GB, C_END)) * _dot(t.astype(BF16), wbrb_ref[...])


def _full(shape):
    nd = len(shape)
    return pl.BlockSpec(shape, lambda *_: (0,) * nd)


def _p_proj_call(x, shift, scale, norm_g, w_all, cos_t, sin_t, qg, kg, pek, pev, wck, wcv, vng, vnb,
                 w_s, bs_t, w_br_b):
    b, s, _ = x.shape
    tm = PROMPT_ROWS
    row = lambda w: pl.BlockSpec((1, tm, w), lambda bi, i: (bi, i, 0))
    tok_minor = pl.BlockSpec((1, KV_WIDTH, tm), lambda bi, i: (bi, 0, i))
    per_batch = pl.BlockSpec((1, 1, D_MODEL), lambda bi, i: (bi, 0, 0))
    cmp_spec = pl.BlockSpec((1, tm // L_CMP, KV_WIDTH), lambda bi, i: (bi, i, 0))
    tab = pl.BlockSpec((HEAD_DIM // 2, tm), lambda bi, i: (0, i))
    in_specs = [row(D_MODEL), per_batch, per_batch, _full(norm_g.shape), _full(w_all.shape), tab, tab,
                _full(qg.shape), _full(kg.shape), _full(pek.shape), _full(pev.shape), _full(wck.shape),
                _full(wcv.shape), _full(vng.shape), _full(vnb.shape), _full(w_s.shape), _full(bs_t.shape),
                _full(w_br_b.shape)]
    qt_spec = pl.BlockSpec((1, WIDTH_A, tm), lambda bi, i: (bi, 0, i))
    group_rows = pl.BlockSpec((1, N_KV, tm, KV_WIDTH), lambda bi, i: (bi, 0, i, 0))
    assert min(WINDOW, s) == tm
    last_tile = pl.BlockSpec((1, KV_WIDTH, tm), lambda bi, i: (bi, 0, 0))
    out_specs = [qt_spec] + [tok_minor] * 5 + [group_rows, row(KV_WIDTH), cmp_spec, cmp_spec,
                                                tok_minor, row(WIDTH_A), row(D_MODEL), row(D_MODEL),
                                                last_tile, last_tile]
    tm_shape = jax.ShapeDtypeStruct((b, KV_WIDTH, s), F32)
    rows_shape = jax.ShapeDtypeStruct((b, s, KV_WIDTH), BF16)
    cmp_shape = jax.ShapeDtypeStruct((b, s // L_CMP, KV_WIDTH), F32)
    assert PROMPT_SUB_ROWS % SEL_CHUNK == 0
    out_shape = [jax.ShapeDtypeStruct((b, WIDTH_A, s), BF16)] + [tm_shape] * 5 + [
        jax.ShapeDtypeStruct((b, N_KV, s, KV_WIDTH), BF16), rows_shape, cmp_shape, cmp_shape,
        jax.ShapeDtypeStruct((b, LANES, s), F32), jax.ShapeDtypeStruct((b, s, WIDTH_A), F32),
        jax.ShapeDtypeStruct((b, s, D_MODEL), F32), jax.ShapeDtypeStruct((b, s, D_MODEL), F32),
        jax.ShapeDtypeStruct((b, KV_WIDTH, tm), F32), jax.ShapeDtypeStruct((b, KV_WIDTH, tm), F32)]
    return pl.pallas_call(
        _p_proj_kernel,
        grid=(b, s // tm),
        in_specs=in_specs,
        out_specs=out_specs,
        out_shape=out_shape,
        compiler_params=pltpu.CompilerParams(dimension_semantics=("arbitrary", "arbitrary"),
                                             vmem_limit_bytes=VMEM_LIMIT),
        name="p_proj",
    )(x, shift, scale, norm_g, w_all, cos_t, sin_t, qg, kg, pek, pev, wck, wcv, vng, vnb, w_s, bs_t, w_br_b)


def _select_blocks(imp_c, qblk, n_blocks):
    ratio = L_SLC // L_CMP
    assert ratio == 2
    lane = _iota(imp_c.shape, 1)
    imp = imp_c + pltpu.roll(imp_c, LANES - 1, 1)
    blk = lane // ratio
    forced = jnp.where((blk == 0) | (blk == qblk), 1.0, 0.0)
    score = jnp.where(blk <= qblk, imp + FORCE_BONUS * forced, NEG)
    beats = []
    for j in range(n_blocks):
        vj = score[:, ratio * j:ratio * j + 1]
        earlier = jnp.where(lane > ratio * j, 1.0, 0.0)
        beats.append(jnp.where(vj > score, 1.0, jnp.where(vj == score, earlier, 0.0)))
    while len(beats) > 1:
        beats = [a + b for a, b in zip(beats[0::2], beats[1::2])] + ([beats[-1]] if len(beats) % 2 else [])
    rank = beats[0]
    cand = (lane % ratio == 0) & (lane < ratio * n_blocks)
    return jnp.where(cand & (rank < float(min(N_SEL, n_blocks))), 1.0, 0.0)


def _softmax_rows(s, valid):
    sm = jnp.where(valid, s, NEG)
    e = jnp.exp(sm - jnp.max(sm, axis=-1, keepdims=True))
    return e, 1.0 / jnp.sum(e, axis=-1, keepdims=True)


def _merge_and_project(o_a, sza, sga, mb, x, gate, wbra_ref, wout_ref):
    a = _dot((o_a * sza).astype(BF16), wbra_ref[...])
    m = sga * a + mb
    return x + gate * _dot(m.astype(BF16), wout_ref[...])


def _select_blocks_t(imp, qblk):
    n_blocks = imp.shape[0]
    blk = _iota(imp.shape, 0)
    forced = jnp.where((blk == 0) | (blk == qblk), 1.0, 0.0)
    score = jnp.where(blk <= qblk, imp + FORCE_BONUS * forced, NEG)
    rank = jnp.zeros(imp.shape, F32)
    for j in range(n_blocks):
        vj = score[j:j + 1, :]
        earlier = jnp.where(blk > j, 1.0, 0.0)
        rank = rank + jnp.where(vj > score, 1.0, jnp.where(vj == score, earlier, 0.0))
    return jnp.where(rank < float(min(N_SEL, n_blocks)), 1.0, 0.0)


def _softmax_cols(s):
    e = jnp.exp2(s - jnp.max(s, axis=0, keepdims=True))
    return e, 1.0 / jnp.sum(e, axis=0, keepdims=True)


def _p_attn_kernel(qt_ref, ks_ref, vst_ref, kw_ref, vwt_ref, kc_ref, vc_ref, gt_ref, sza_ref, sga_ref,
                   mb_ref, x_ref, gate_ref, wbra_ref, wout_ref, y_ref,
                   qt_scr, pen_scr, m_scr, acc_scr, s_even, s_odd, sw_scr, mw_scr, oa_scr, mprev_scr):
    tq = Q_BLOCK
    step = pl.program_id(0)
    i = lax.rem(jnp.minimum(step, pl.num_programs(0) - 2), ks_ref.shape[2] // tq)

    @pl.when(step == 0)
    def _():
        oa_scr[...] = jnp.zeros(oa_scr.shape, F32)
    n_cmp = kc_ref.shape[1]
    half = n_cmp // 2
    assert L_SLC == 2 * L_CMP and n_cmp <= LANES
    cols = GQA * tq
    qpos1 = i * tq + _iota((1, tq), 1)
    qpos = jnp.concatenate([qpos1] * GQA, axis=1)
    band = WINDOW + tq
    ws = pl.multiple_of(jnp.maximum(i * tq - WINDOW, 0), LANES)
    zeros_q = jnp.zeros((HEAD_DIM, cols), BF16)
    perm = lambda ref: jnp.concatenate([ref[0, pl.ds(0, half, stride=2), :], ref[0, pl.ds(1, half, stride=2), :],
                                        jnp.zeros((LANES - n_cmp, KV_WIDTH), F32)], axis=0)
    kc = perm(kc_ref).astype(BF16)
    vc_t = perm(vc_ref).T.astype(BF16)
    crow = _iota((LANES, cols), 0)
    cblk = 2 * (crow % half) + crow // half
    mc = ((cblk + 1) * L_CMP - 1 <= qpos) & (crow < n_cmp)
    gates_t = gt_ref[0]
    groups = [slice(g * HEAD_DIM, (g + 1) * HEAD_DIM) for g in range(N_KV)]

    for g in range(N_KV):
        qt_g = jnp.concatenate([qt_ref[0, h * HEAD_DIM:(h + 1) * HEAD_DIM, :]
                                for h in range(g * GQA, (g + 1) * GQA)], axis=1)
        qt_scr[g] = jnp.concatenate([qt_g, zeros_q] if g == 0 else [zeros_q, qt_g], axis=0)

    per_chunk = SEL_CHUNK // L_SLC
    assert band % SEL_CHUNK == 0
    w_chunks = [pl.ds(pl.multiple_of(ws + c * SEL_CHUNK, LANES), SEL_CHUNK) for c in range(band // SEL_CHUNK)]

    s_cmp = [_dot(kc, qt_scr[g]) for g in range(N_KV)]
    for c, kd in enumerate(w_chunks):
        k_rows = kw_ref[0, kd, :]
        for g in range(N_KV):
            sw_scr[g, c] = _dot(k_rows, qt_scr[g])

    def selected_scores(g, chunk):
        k_aug = ks_ref[0, g, pl.ds(pl.multiple_of(chunk * SEL_CHUNK, SEL_CHUNK), SEL_CHUNK), :]
        gap = jnp.zeros((HEAD_DIM - PEN_ROWS, cols), BF16)
        q_aug = ([qt_scr[0, 0:HEAD_DIM], pen_scr[0, chunk], gap] if g == 0 else
                 [pen_scr[1, chunk], gap, qt_scr[1, HEAD_DIM:2 * HEAD_DIM]])
        return _dot(k_aug, jnp.concatenate(q_aug, axis=0))

    o_c = []
    for g in range(N_KV):
        s_c = jnp.where(mc, s_cmp[g], NEG)
        e_c, r_c = _softmax_cols(s_c)
        p_c = jnp.where(mc, e_c * r_c, 0.0)
        o_c.append(_dot(vc_t[groups[g]], p_c.astype(BF16)))
        imp = p_c[:, 0:tq]
        for r in range(1, GQA):
            imp = imp + p_c[:, r * tq:(r + 1) * tq]
        sel = _select_blocks_t(imp[0:half] + imp[half:2 * half], qpos1 // L_SLC)
        pen = jnp.concatenate([jnp.where(sel > 0.5, 0.0, NEG)] * GQA, axis=1)
        pad_rows = jnp.zeros((PEN_ROWS - per_chunk, cols), F32)
        for c in range(half // per_chunk):
            pen_scr[g, c] = jnp.concatenate([pen[c * per_chunk:(c + 1) * per_chunk], pad_rows],
                                            axis=0).astype(BF16)
        s_even[g] = selected_scores(g, 0)

    for c in range(len(w_chunks)):
        kwpos = ws + c * SEL_CHUNK + _iota((SEL_CHUNK, tq), 0)
        bias_w = jnp.where((kwpos <= qpos1) & (kwpos > qpos1 - WINDOW), 0.0, NEG)
        bias_w = jnp.concatenate([bias_w] * GQA, axis=1)
        for g in range(N_KV):
            s_w = sw_scr[g, c] + bias_w
            sw_scr[g, c] = s_w
            m_c = jnp.max(s_w, axis=0, keepdims=True)
            mw_scr[g] = m_c if c == 0 else jnp.maximum(mw_scr[g], m_c)

    m_scr[...] = jnp.full(m_scr.shape, NEG, F32)
    acc_scr[...] = jnp.zeros(acc_scr.shape, F32)

    def with_ones(v_t):
        return jnp.concatenate([v_t.astype(BF16), jnp.ones((ONES_ROWS, v_t.shape[1]), BF16)], axis=0)

    def trip(kc_i, src, dst):
        off = pl.multiple_of(kc_i * SEL_CHUNK, SEL_CHUNK)
        if dst is None:
            t_prev = (oa_scr[...] * sza_ref[0]).astype(BF16)
            a_prev = []
        for g in range(N_KV):
            if dst is not None:
                dst[g] = selected_scores(g, kc_i + 1)
                s = src[g]
            else:
                half_n = D_MODEL // N_KV
                a_prev.append(_dot(t_prev, wbra_ref[:, g * half_n:(g + 1) * half_n]))
                causal = jnp.where(off + _iota((SEL_CHUNK, tq), 0) <= qpos1, 0.0, NEG)
                s = src[g] + jnp.concatenate([causal] * GQA, axis=1)
            m_old = m_scr[g]
            m_new = jnp.maximum(m_old, jnp.max(s, axis=0, keepdims=True))
            p = jnp.exp2(s - m_new).astype(BF16)
            m_scr[g] = m_new
            acc_scr[g] = jnp.exp2(m_old - m_new) * acc_scr[g] + _dot(
                with_ones(vst_ref[0, groups[g], pl.ds(off, SEL_CHUNK)]), p)
        if dst is None:
            mprev_scr[...] = (sga_ref[0] * jnp.concatenate(a_prev, axis=1) + mb_ref[0]).astype(BF16)

    def by_parity(kc_i, dst_wanted):
        @pl.when(lax.rem(kc_i, 2) == 0)
        def _():
            trip(kc_i, s_even, s_odd if dst_wanted else None)

        @pl.when(lax.rem(kc_i, 2) == 1)
        def _():
            trip(kc_i, s_odd, s_even if dst_wanted else None)

    def body(kc_i, carry):
        by_parity(kc_i, True)
        return carry

    assert SEL_CHUNK == tq
    lax.fori_loop(0, i, body, 0)
    by_parity(i, False)

    acc_w = [None] * N_KV

    def window_pass2():
        for c, kd in enumerate(w_chunks):
            for g in range(N_KV):
                pv = _dot(with_ones(vwt_ref[0, groups[g], kd]),
                          jnp.exp2(sw_scr[g, c] - mw_scr[g]).astype(BF16))
                acc_w[g] = pv if c == 0 else acc_w[g] + pv
                yield

    def previous_block_output():
        n_blk = 2 * LANES
        m = mprev_scr[...]
        for n0 in range(0, D_MODEL, n_blk):
            y_ref[0, :, n0:n0 + n_blk] = (x_ref[0, :, n0:n0 + n_blk]
                                          + gate_ref[0][:, n0:n0 + n_blk] * _dot(m, wout_ref[:, n0:n0 + n_blk]))
            yield

    for _ in itertools.zip_longest(window_pass2(), previous_block_output()):
        pass
    pair = []
    for g in range(N_KV):
        o_w = acc_w[g][0:HEAD_DIM] * (1.0 / acc_w[g][HEAD_DIM:HEAD_DIM + 1])
        acc_s = acc_scr[g]
        o_s = acc_s[0:HEAD_DIM] * (1.0 / acc_s[HEAD_DIM:HEAD_DIM + 1])
        for r in range(GQA):
            h = g * GQA + r
            cs = slice(r * tq, (r + 1) * tq)
            pair.append(gates_t[3 * h:3 * h + 1] * o_c[g][:, cs] + gates_t[3 * h + 1:3 * h + 2] * o_s[:, cs]
                        + gates_t[3 * h + 2:3 * h + 3] * o_w[:, cs])
    per_lane = LANES // HEAD_DIM
    oa_scr[...] = jnp.concatenate([jnp.concatenate(pair[j:j + per_lane], axis=0).T
                                   for j in range(0, N_HEADS, per_lane)], axis=1)


def _p_attn_call(q_t, ks_rows, vs_t, kw_rows, vw_t, kc, vc, gates_t, sza, sga, mb, x, gate, w_br_a, w_out):
    b, s, _ = x.shape
    tq = Q_BLOCK
    n_i = s // tq
    n_blocks = b * n_i
    att = lambda j: jnp.minimum(j, n_blocks - 1)
    out = lambda j: jnp.maximum(j - 1, 0)
    row = lambda w: pl.BlockSpec((1, tq, w), lambda j: (out(j) // n_i, out(j) % n_i, 0))
    col = lambda a: pl.BlockSpec((1, a.shape[1], tq), lambda j: (att(j) // n_i, 0, att(j) % n_i))
    seq = lambda a: pl.BlockSpec((1,) + a.shape[1:], lambda j: (att(j) // n_i,) + (0,) * (a.ndim - 1))
    out_seq = lambda a: pl.BlockSpec((1,) + a.shape[1:], lambda j: (out(j) // n_i,) + (0,) * (a.ndim - 1))
    in_specs = [col(q_t), seq(ks_rows), seq(vs_t), seq(kw_rows), seq(vw_t), seq(kc), seq(vc),
                col(gates_t), row(WIDTH_A), row(D_MODEL), row(D_MODEL), row(D_MODEL), out_seq(gate),
                _full(w_br_a.shape), _full(w_out.shape)]
    return pl.pallas_call(
        _p_attn_kernel,
        grid=(n_blocks + 1,),
        in_specs=in_specs,
        out_specs=row(D_MODEL),
        out_shape=jax.ShapeDtypeStruct((b, s, D_MODEL), F32),
        scratch_shapes=[pltpu.VMEM((N_KV, KV_WIDTH, GQA * tq), BF16),
                        pltpu.VMEM((N_KV, s // SEL_CHUNK, PEN_ROWS, GQA * tq), BF16),
                        pltpu.VMEM((N_KV, 1, GQA * tq), F32),
                        pltpu.VMEM((N_KV, HEAD_DIM + ONES_ROWS, GQA * tq), F32),
                        pltpu.VMEM((N_KV, SEL_CHUNK, GQA * tq), F32),
                        pltpu.VMEM((N_KV, SEL_CHUNK, GQA * tq), F32),
                        pltpu.VMEM((N_KV, (WINDOW + tq) // SEL_CHUNK, SEL_CHUNK, GQA * tq), F32),
                        pltpu.VMEM((N_KV, 1, GQA * tq), F32),
                        pltpu.VMEM((tq, WIDTH_A), F32),
                        pltpu.VMEM((tq, D_MODEL), BF16)],
        compiler_params=pltpu.CompilerParams(dimension_semantics=("arbitrary",),
                                             vmem_limit_bytes=VMEM_LIMIT),
        name="p_attn",
    )(q_t, ks_rows, vs_t, kw_rows, vw_t, kc, vc, gates_t, sza, sga, mb, x, gate, w_br_a, w_out)


def _s_proj_kernel(x_ref, shift_ref, scale_ref, ng_ref, w_ref, cos_ref, sin_ref, qg_ref, kg_ref,
                   vng_ref, vnb_ref, ws0_ref, bs0_ref, wbrb_ref,
                   q_out, k_out, v_out, kt_out, vt_out, gates_out, sza_out, sga_out, mb_out, vn_out):
    seg = _project(x_ref[...], shift_ref[...], scale_ref[...], ng_ref[...], w_ref)
    cos, sin = cos_ref[...], sin_ref[...]
    q_out[...] = _norm_rope(seg(C_Q, C_K), qg_ref[...], cos, sin)
    k = _norm_rope(seg(C_K, C_V), kg_ref[...], cos, sin)
    v = seg(C_V, C_G)
    k_out[...] = k
    v_out[...] = v
    for br in range(3):
        sl = slice(br * KV_WIDTH, (br + 1) * KV_WIDTH)
        kt_out[br] = k[:, sl].T
        vt_out[br] = v[:, sl].T
    gates_out[...] = jax.nn.sigmoid(seg(C_G, C_ZA))
    sza_out[...] = _silu(seg(C_ZA, C_U))
    sga_out[...] = jax.nn.sigmoid(seg(C_GA, C_GB))
    vn = _layer_norm(seg(C_VB, C_ZB), vng_ref[...], vnb_ref[...])
    vn_out[...] = vn
    s_b = ws0_ref[...] * vn + bs0_ref[...]
    t = seg(C_U, C_VB) * s_b * _silu(seg(C_ZB, C_GA))
    mb_out[...] = jax.nn.sigmoid(seg(C_GB, C_END)) * _dot(t.astype(BF16), wbrb_ref[...])


def _s_proj_call(x, shift, scale, norm_g, w_all, cos1, sin1, qg, kg, vng, vnb, ws0, bs0, w_br_b):
    n = x.shape[0]
    args = (x, shift, scale, norm_g, w_all, cos1, sin1, qg, kg, vng, vnb, ws0, bs0, w_br_b)
    sds = lambda *shape: jax.ShapeDtypeStruct(shape, F32)
    out_shape = [sds(n, WIDTH_A), sds(n, 3 * KV_WIDTH), sds(n, 3 * KV_WIDTH), sds(3, KV_WIDTH, n),
                 sds(3, KV_WIDTH, n), sds(n, LANES), sds(n, WIDTH_A), sds(n, D_MODEL), sds(n, D_MODEL),
                 sds(n, WIDTH_B)]
    return pl.pallas_call(
        _s_proj_kernel,
        grid=(1,),
        in_specs=[_full(a.shape) for a in args],
        out_specs=[_full(o.shape) for o in out_shape],
        out_shape=out_shape,
        compiler_params=pltpu.CompilerParams(vmem_limit_bytes=VMEM_LIMIT),
        name="s_proj",
    )(*args)


def _s_attn_kernel(pt_ref, q_ref, gates_ref, knew_ref, vnew_ref, knewt_ref, vnewt_ref, kwin_ref, vwin_ref,
                   pek_ref, pev_ref, wck_ref, wcv_ref, pool_ref,
                   kc_hbm, vc_hbm, ks_hbm, vs_hbm,
                   oa_ref, okw_ref, ovw_ref, buf, sem):
    t = pl.program_id(0)
    n_groups = pl.num_programs(0) - 1
    per_step = q_ref.shape[0]
    n_seqs = pt_ref.shape[0]
    n_pages = pt_ref.shape[1]
    past = n_pages * PAGE_SIZE
    caches = (kc_hbm, vc_hbm, ks_hbm, vs_hbm)

    def page_copies(group, slot_):
        return [pltpu.make_async_copy(hbm.at[pt_ref[jnp.minimum(group * per_step + j, n_seqs - 1), p]],
                                      buf.at[slot_, j, c, :, pl.ds(p * PAGE_SIZE, PAGE_SIZE)],
                                      sem.at[slot_, c])
                for j in range(per_step) for c, hbm in enumerate(caches) for p in range(n_pages)]

    @pl.when(t < n_groups)
    def _():
        for cp in page_copies(t, lax.rem(t, 2)):
            cp.start()

    @pl.when(t > 0)
    def _():
        group = t - 1
        slot = lax.rem(group, 2)
        for cp in page_copies(group, slot):
            cp.wait()
        chains = [_s_attn_one(group * per_step + j, j, buf.at[slot, j], q_ref, gates_ref, knew_ref, vnew_ref,
                              knewt_ref, vnewt_ref, kwin_ref, vwin_ref, pek_ref, pev_ref, wck_ref, wcv_ref,
                              pool_ref, oa_ref, okw_ref, ovw_ref, past) for j in range(per_step)]
        for _ in itertools.zip_longest(*chains):
            pass


def _s_attn_one(b, j, buf, q_ref, gates_ref, knew_ref, vnew_ref, knewt_ref, vnewt_ref, kwin_ref, vwin_ref,
                pek_ref, pev_ref, wck_ref, wcv_ref, pool_ref, oa_ref, okw_ref, ovw_ref, past):
    qpos = past

    lane = _iota((KV_WIDTH, knewt_ref.shape[2]), 1)
    col = lambda ref, br: jnp.sum(jnp.where(lane == b, ref[br], 0.0), axis=1, keepdims=True)
    knew = knew_ref[pl.ds(b, 1), :]
    vnew = vnew_ref[pl.ds(b, 1), :]

    hrow = _iota((N_HEADS, 1), 0)
    first_group = hrow < GQA
    by_group = lambda f: jnp.where(first_group, f(0), f(1))
    gl = lambda g: slice(g * HEAD_DIM, (g + 1) * HEAD_DIM)
    qb = q_ref[j].astype(BF16)

    n_cmp = -(-(past + 1) // L_SLC) * L_SLC // L_CMP
    n_slc = n_cmp * L_CMP // L_SLC

    def summaries(c, new_row, pe_ref, w_ref, out):
        chunk = pool_ref.shape[1]
        sums = []
        for t0 in range(0, past, chunk):
            hi, lo = _split_bf16(buf[c, :, t0:t0 + chunk])
            sums.append(_dot_nt(pool_ref[...], hi) + _dot_nt(pool_ref[...], lo))
            yield
        pe_sum = jnp.sum(pe_ref[...], axis=0, keepdims=True)
        pooled = (jnp.concatenate(sums, axis=0) + pe_sum) * (1.0 / L_CMP)
        r = _iota((SUBLANES, KV_WIDTH), 0)
        tail = jnp.where(r == 0, new_row + pe_sum, jnp.where(r == 1, pe_sum, 0.0)) * (1.0 / L_CMP)
        zeros = jnp.zeros((LANES - pooled.shape[0] - SUBLANES, KV_WIDTH), F32)
        out.append(_dot(jnp.concatenate([pooled, tail, zeros], axis=0).astype(BF16), w_ref[...]).astype(BF16))

    summary = []
    yield from summaries(0, knew[:, 0:KV_WIDTH], pek_ref, wck_ref, summary)
    yield from summaries(1, vnew[:, 0:KV_WIDTH], pev_ref, wcv_ref, summary)
    kc, vc = summary
    s_c = by_group(lambda g: _dot_nt(qb, kc[:, gl(g)])) * SM_SCALE
    c = _iota(s_c.shape, 1)
    mc = ((c + 1) * L_CMP - 1 <= qpos) & (c < n_cmp)
    e_c, r_c = _softmax_rows(s_c, mc)
    p_c = jnp.where(mc, e_c * r_c, 0.0)
    o_c = by_group(lambda g: _dot(p_c.astype(BF16), vc[:, gl(g)]))
    imp = by_group(lambda g: jnp.sum(p_c[g * GQA:(g + 1) * GQA], axis=0, keepdims=True))
    imp = jnp.broadcast_to(imp, p_c.shape)
    yield
    sel = _select_blocks(imp, jnp.full((N_HEADS, 1), qpos // L_SLC, jnp.int32), n_slc)
    yield

    ratio = L_SLC // L_CMP
    first_blk = _iota((N_HEADS, LANES), 1) < L_SLC
    blk_col = lambda j: sel[:, ratio * j:ratio * j + 1]
    picked = jnp.concatenate([jnp.where(first_blk, blk_col(2 * c), blk_col(2 * c + 1))
                              for c in range(past // LANES)], axis=1)
    s_s = by_group(lambda g: _dot(qb, buf[2, gl(g), :].astype(BF16))) * SM_SCALE
    kpos = _iota(s_s.shape, 1)
    s_s = jnp.where((picked > 0.5) & (kpos <= qpos), s_s, NEG)
    yield
    own =_iota((N_HEADS, KV_WIDTH), 1) // HEAD_DIM == hrow // GQA
    rounded = lambda a: a.astype(BF16).astype(F32)
    q_pair = jnp.concatenate([qb.astype(F32)] * N_KV, axis=1)
    s_new = jnp.sum(jnp.where(own, q_pair * rounded(knew[:, KV_WIDTH:2 * KV_WIDTH]), 0.0),
                    axis=1, keepdims=True) * SM_SCALE
    new_lane = (L_SLC // L_CMP) * (past // L_SLC)
    s_new = jnp.where((sel[:, new_lane:new_lane + 1] > 0.5) & (past <= qpos), s_new, NEG)
    m_s = jnp.maximum(jnp.max(s_s, axis=-1, keepdims=True), s_new)
    e_s, e_new = jnp.exp(s_s - m_s), jnp.exp(s_new - m_s)
    r_s = 1.0 / (jnp.sum(e_s, axis=-1, keepdims=True) + e_new)
    v_new = by_group(lambda g: rounded(vnew[:, KV_WIDTH + g * HEAD_DIM:KV_WIDTH + (g + 1) * HEAD_DIM]))
    o_s = (by_group(lambda g: _dot_nt(e_s.astype(BF16), buf[3, gl(g), :].astype(BF16)))
           + rounded(e_new) * v_new) * r_s
    yield

    wb = kwin_ref.shape[2]
    wlane = _iota((KV_WIDTH, wb), 1)
    kw = jnp.where(wlane == wb - 1, col(knewt_ref, 2), pltpu.roll(kwin_ref[j], wb - 1, 1))
    vw = jnp.where(wlane == wb - 1, col(vnewt_ref, 2), pltpu.roll(vwin_ref[j], wb - 1, 1))
    okw_ref[j] = kw
    ovw_ref[j] = vw
    yield
    s_w = by_group(lambda g: _dot(qb, kw[gl(g)].astype(BF16))) * SM_SCALE
    kwpos = past - wb + 1 + _iota(s_w.shape, 1)
    e_w, r_w = _softmax_rows(s_w, (kwpos <= qpos) & (kwpos > qpos - WINDOW) & (kwpos >= 0))
    o_w = by_group(lambda g: _dot_nt(e_w.astype(BF16), vw[gl(g)].astype(BF16))) * r_w

    gates = gates_ref[j]
    oa_ref[j] = gates[:, 0:1] * o_c + gates[:, 1:2] * o_s + gates[:, 2:3] * o_w


def _s_attn_call(page_table, q3, gates3, knew, vnew, knew_t, vnew_t, kwin_t, vwin_t, pek, pev, wck, wcv,
                 kc_pool, vc_pool, ks_pool, vs_pool):
    n, n_pages = page_table.shape
    past = n_pages * PAGE_SIZE
    wb = kwin_t.shape[2]
    tok = np.arange(SUMMARY_CHUNK)
    assert past % SUMMARY_CHUNK == 0
    pool = jnp.asarray((tok[None, :] // L_CMP == np.arange(SUMMARY_CHUNK // L_CMP)[:, None]), BF16)
    k = SAMPLE_SEQS_PER_STEP
    assert n % k == 0
    per_seq = lambda a: pl.BlockSpec((k,) + a.shape[1:],
                                     lambda t, pt: (jnp.maximum(t - 1, 0),) + (0,) * (a.ndim - 1))
    full = lambda a: pl.BlockSpec(a.shape, lambda t, pt: (0,) * a.ndim)
    hbm = pl.BlockSpec(memory_space=pl.ANY)
    resident = (knew, vnew, knew_t, vnew_t)
    consts = (pek, pev, wck, wcv, pool)
    grid_spec = pltpu.PrefetchScalarGridSpec(
        num_scalar_prefetch=1,
        grid=(n // k + 1,),
        in_specs=[per_seq(q3), per_seq(gates3)] + [full(a) for a in resident]
                 + [per_seq(kwin_t), per_seq(vwin_t)] + [full(a) for a in consts] + [hbm] * 4,
        out_specs=[per_seq(q3), per_seq(kwin_t), per_seq(vwin_t)],
        scratch_shapes=[pltpu.VMEM((2, k, 4, KV_WIDTH, past), F32), pltpu.SemaphoreType.DMA((2, 4))],
    )
    return pl.pallas_call(
        _s_attn_kernel,
        grid_spec=grid_spec,
        out_shape=[jax.ShapeDtypeStruct(q3.shape, F32), jax.ShapeDtypeStruct(kwin_t.shape, F32),
                   jax.ShapeDtypeStruct(vwin_t.shape, F32)],
        compiler_params=pltpu.CompilerParams(dimension_semantics=("arbitrary",),
                                             vmem_limit_bytes=VMEM_LIMIT),
        name="s_attn",
    )(page_table, q3, gates3, knew, vnew, knew_t, vnew_t, kwin_t, vwin_t, pek, pev, wck, wcv, pool,
      kc_pool, vc_pool, ks_pool, vs_pool)


def _s_out_kernel(oa_ref, sza_ref, sga_ref, mb_ref, x_ref, gate_ref, wbra_ref, wout_ref, y_ref):
    y_ref[...] = _merge_and_project(oa_ref[...], sza_ref[...], sga_ref[...], mb_ref[...], x_ref[...],
                                    gate_ref[...], wbra_ref, wout_ref)


def _s_out_call(o_a, sza, sga, mb, x, gate, w_br_a, w_out):
    args = (o_a, sza, sga, mb, x, gate, w_br_a, w_out)
    return pl.pallas_call(
        _s_out_kernel,
        grid=(1,),
        in_specs=[_full(a.shape) for a in args],
        out_specs=_full(x.shape),
        out_shape=jax.ShapeDtypeStruct(x.shape, F32),
        compiler_params=pltpu.CompilerParams(vmem_limit_bytes=VMEM_LIMIT),
        name="s_out",
    )(*args)


def _rope_angles(pos):
    half = HEAD_DIM // 2
    inv = ROPE_THETA ** (-jnp.arange(half, dtype=F32) * 2.0 / HEAD_DIM)
    return pos.astype(F32)[:, None] * inv[None, :]


def _rope_tables(pos):
    ang = _rope_angles(pos)
    cos, sin = jnp.cos(ang), jnp.sin(ang)
    cos_t = jnp.concatenate([cos, cos] * (LANES // HEAD_DIM), axis=1)
    sin_t = jnp.concatenate([-sin, sin] * (LANES // HEAD_DIM), axis=1)
    return cos_t, sin_t


def _token_minor(a):
    b, t = a.shape[:2]
    return jnp.transpose(a, (0, 2, 3, 1)).reshape(b, KV_WIDTH, t)


def _token_major(a_t):
    b, _, t = a_t.shape
    return jnp.transpose(a_t.reshape(b, N_KV, HEAD_DIM, t), (0, 3, 1, 2))


def kernel(x_prompt, x_sample, cache_k_cmp, cache_v_cmp, cache_k_slc, cache_v_slc, cache_k_win, cache_v_win, page_table, c_prompt, c_sample, w_ada, b_ada, norm_g, w_in, q_norm_g, k_norm_g, cmp_pos_k, cmp_pos_v, w_cmp_k, w_cmp_v, vnorm_g, vnorm_b, w_s, b_s, w_br_a, w_br_b, w_out):
    assert w_ada.shape[0] == 1, "single layer"
    b, s, _ = x_prompt.shape
    n = x_sample.shape[0]
    assert x_sample.shape[1] == 1
    n_pages = page_table.shape[1]
    past = n_pages * PAGE_SIZE

    w_all = _w_pack_call(w_in[0].T)
    eye = jnp.eye(N_KV, dtype=F32)
    wck = jnp.kron(eye, w_cmp_k[0]).astype(BF16)
    wcv = jnp.kron(eye, w_cmp_v[0]).astype(BF16)
    pek = jnp.tile(cmp_pos_k[0], (1, N_KV))
    pev = jnp.tile(cmp_pos_v[0], (1, N_KV))
    qg = jnp.tile(q_norm_g, (1, LANES // HEAD_DIM))
    kg = jnp.tile(k_norm_g, (1, LANES // HEAD_DIM))
    w_br_a_b, w_br_b_b, w_out_b = w_br_a[0].astype(BF16), w_br_b[0].astype(BF16), w_out[0].astype(BF16)

    mod = _ada_call(jnp.concatenate([c_prompt, c_sample], axis=0), w_ada[0], b_ada)
    shift, scale, gate = mod[:, :D_MODEL], mod[:, D_MODEL:2 * D_MODEL], mod[:, 2 * D_MODEL:]

    ang_p = _rope_angles(jnp.arange(s, dtype=jnp.int32)).T
    gain_cols = lambda g: jnp.broadcast_to(g[0][:, None], (HEAD_DIM, PROMPT_SUB_ROWS))
    (q_t, kc_t, ks_t, vc_t, vs_t, vw_t, ks_rows, kw_rows, kcmp, vcmp, gates_t, sza, sga, mb,
     kw_last, vw_last) = _p_proj_call(
        x_prompt, shift[:b, None], scale[:b, None], norm_g, w_all, jnp.cos(ang_p), jnp.sin(ang_p),
        gain_cols(q_norm_g), gain_cols(k_norm_g), pek, pev, wck, wcv,
        vnorm_g, vnorm_b, w_s[0], b_s[0].T, w_br_b_b)
    y_prompt = _p_attn_call(q_t, ks_rows, vs_t, kw_rows, vw_t, kcmp, vcmp, gates_t, sza, sga, mb, x_prompt,
                            gate[:b, None], w_br_a_b, w_out_b)
    p_states = [_token_major(a)[None] for a in (kc_t, vc_t, ks_t, vs_t, kw_last, vw_last)]

    xs = x_sample.reshape(n, D_MODEL)
    cos_s, sin_s = _rope_tables(jnp.full((1,), past, jnp.int32))
    ws0 = jnp.repeat(w_s[0, :, 0, 0], GROUP_W_B)[None]
    bs0 = jnp.repeat(b_s[0, :, 0], GROUP_W_B)[None]
    (q_s, k_s, v_s, kt_s, vt_s, gates_s, sza_s, sga_s, mb_s, vn_s) = _s_proj_call(
        xs, shift[b:], scale[b:], norm_g, w_all, cos_s, sin_s, qg, kg, vnorm_g, vnorm_b, ws0, bs0, w_br_b_b)
    pools = [_token_minor(c[0]) for c in (cache_k_cmp, cache_v_cmp, cache_k_slc, cache_v_slc)]
    o_a, kwin_new, vwin_new = _s_attn_call(
        page_table, q_s.reshape(n, N_HEADS, HEAD_DIM), gates_s[:, :3 * N_HEADS].reshape(n, N_HEADS, 3),
        k_s, v_s, kt_s, vt_s, _token_minor(cache_k_win[0]), _token_minor(cache_v_win[0]),
        pek, pev, wck, wcv, *pools)
    y_sample = _s_out_call(o_a.reshape(n, WIDTH_A), sza_s, sga_s, mb_s, xs, gate[b:], w_br_a_b, w_out_b)

    new_rows = lambda t, br: jnp.transpose(t[br].reshape(N_KV, HEAD_DIM, n), (2, 0, 1))[None, :, None]
    s_states = [new_rows(kt_s, 0), new_rows(vt_s, 0), new_rows(kt_s, 1), new_rows(vt_s, 1),
                _token_major(kwin_new)[None], _token_major(vwin_new)[None], vn_s[None, :, None]]
    return (y_prompt, y_sample.reshape(n, 1, D_MODEL), *p_states, *s_states)
```

```python
import itertools

import numpy as np
import jax
import jax.numpy as jnp
from jax import lax
from jax.experimental import pallas as pl
from jax.experimental.pallas import tpu as pltpu

F32 = jnp.float32
BF16 = jnp.bfloat16

D_MODEL = 1024
HEAD_DIM = 64
N_HEADS = 8
N_KV = 2
GQA = N_HEADS // N_KV
WIDTH_A = N_HEADS * HEAD_DIM
KV_WIDTH = N_KV * HEAD_DIM
L_CMP = 32
L_SLC = 64
N_SEL = 8
WINDOW = 512
Q_BLOCK = 256
FORCE_BONUS = 1.0e4
ROPE_THETA = 10000.0
CHUNK = 128
N_GROUPS_B = 4
WIDTH_B = 512
GROUP_W_B = WIDTH_B // N_GROUPS_B
PAGE_SIZE = 128
EPS = 1e-6
NEG = -1e30
SM_SCALE = HEAD_DIM ** -0.5
LOG2_E = 1.4426950408889634

LANES = 128
SUBLANES = 8
VMEM_LIMIT = 56 * 1024 * 1024

C_Q = 0
C_K = C_Q + WIDTH_A
C_V = C_K + 3 * KV_WIDTH
C_G = C_V + 3 * KV_WIDTH
C_ZA = C_G + LANES
C_U = C_ZA + WIDTH_A
C_VB = C_U + WIDTH_B
C_ZB = C_VB + WIDTH_B
C_GA = C_ZB + WIDTH_B
C_GB = C_GA + D_MODEL
C_END = C_GB + D_MODEL

PROMPT_ROWS = 512
PROMPT_SUB_ROWS = 256
SEL_CHUNK = 256
PEN_ROWS = 16
ONES_ROWS = 16
SUMMARY_CHUNK = 512
SAMPLE_SEQS_PER_STEP = 4


def _dot(a, b):
    return jnp.dot(a, b, preferred_element_type=F32)


def _dot_nt(a, b):
    return lax.dot_general(a, b, (((1,), (1,)), ((), ())), preferred_element_type=F32)


def _iota(shape, dim):
    return lax.broadcasted_iota(jnp.int32, shape, dim)


def _split_bf16(x):
    hi = x.astype(BF16)
    lo = (x - hi.astype(F32)).astype(BF16)
    return hi, lo


def _head_mean_sq(x):
    w = x.shape[1]
    ones_bd = jnp.where(_iota((LANES, LANES), 0) // HEAD_DIM == _iota((LANES, LANES), 1) // HEAD_DIM,
                        1.0, 0.0).astype(BF16)
    hi, lo = _split_bf16(x * x)
    cols = []
    for c in range(w // LANES):
        sl = slice(c * LANES, (c + 1) * LANES)
        cols.append(_dot(hi[:, sl], ones_bd) + _dot(lo[:, sl], ones_bd))
    return jnp.concatenate(cols, axis=1) * (1.0 / HEAD_DIM)


def _tile_lanes(t, width):
    return jnp.concatenate([t] * (width // t.shape[1]), axis=1)


def _norm_rope(x, g, cos, sin):
    w = x.shape[1]
    y = x * lax.rsqrt(_head_mean_sq(x) + EPS) * _tile_lanes(g, w)
    first_half = (_iota(y.shape, 1) % HEAD_DIM) < (HEAD_DIM // 2)
    rot = jnp.where(first_half, pltpu.roll(y, w - HEAD_DIM // 2, 1), pltpu.roll(y, HEAD_DIM // 2, 1))
    return y * _tile_lanes(cos, w) + rot * _tile_lanes(sin, w)


def _norm_rope_t(x_t, gain_t, cos_t, sin_t):
    half = HEAD_DIM // 2
    out = []
    for h in range(x_t.shape[0] // HEAD_DIM):
        x = x_t[h * HEAD_DIM:(h + 1) * HEAD_DIM]
        y = x * lax.rsqrt(jnp.mean(x * x, axis=0, keepdims=True) + EPS) * gain_t
        y1, y2 = y[:half], y[half:]
        out += [y1 * cos_t - y2 * sin_t, y2 * cos_t + y1 * sin_t]
    return jnp.concatenate(out, axis=0)


def _silu(z):
    return z * jax.nn.sigmoid(z)


def _project(x, shift, scale, norm_g, w_ref):
    ms = jnp.mean(x * x, axis=-1, keepdims=True)
    h = (x * lax.rsqrt(ms + EPS) * norm_g) * (1.0 + scale) + shift
    hb = h.astype(BF16)
    return lambda lo, hi: _dot(hb, w_ref[:, lo:hi])


def _layer_norm(v, g, b):
    mu = jnp.mean(v, axis=-1, keepdims=True)
    d = v - mu
    var = jnp.mean(d * d, axis=-1, keepdims=True)
    return d * lax.rsqrt(var + EPS) * g + b


def _ada_kernel(c_ref, w_ref, b_ref, o_ref):
    o_ref[...] = _dot(c_ref[...].astype(BF16), w_ref[...].astype(BF16)) + b_ref[...]


W_PACK_COLS = 512
W_GAP_LO = C_G + 3 * N_HEADS
W_GAP_HI = C_G + LANES
W_PACK_BLOCKS = -(-C_END // W_PACK_COLS)


def _w_pack_kernel(wt_ref, o_ref):
    j = pl.program_id(0)
    w = W_PACK_COLS
    gap_block, last = W_GAP_LO // w, W_PACK_BLOCKS - 1
    d_in = C_END - (W_GAP_HI - W_GAP_LO)

    def emit(rows):
        for s in range(w // LANES):
            blk = rows[s * LANES:(s + 1) * LANES]
            o_ref[:, s * LANES:(s + 1) * LANES] = jnp.concatenate(
                [blk[:, c * LANES:(c + 1) * LANES].T for c in range(D_MODEL // LANES)], axis=0).astype(BF16)

    @pl.when((j != gap_block) & (j != last))
    def _():
        emit(wt_ref[...])

    @pl.when(j == gap_block)
    def _():
        k0, k1 = W_GAP_LO - gap_block * w, W_GAP_HI - gap_block * w
        emit(jnp.concatenate([wt_ref[0:k0, :], jnp.zeros((k1 - k0, D_MODEL), F32), wt_ref[k0:k0 + w - k1, :]],
                             axis=0))

    @pl.when(j == last)
    def _():
        shift = (last * w - (W_GAP_HI - W_GAP_LO)) - (d_in - w)
        n_valid = C_END - last * w
        emit(jnp.concatenate([wt_ref[shift:shift + n_valid, :], jnp.zeros((w - n_valid, D_MODEL), F32)], axis=0))


def _w_pack_call(w_t):
    d_in = w_t.shape[0]
    w = W_PACK_COLS
    pad = W_GAP_HI - W_GAP_LO
    gap_block = W_GAP_LO // w
    assert d_in + pad == C_END and W_GAP_HI <= (gap_block + 1) * w and gap_block < W_PACK_BLOCKS - 1
    assert W_GAP_LO % SUBLANES == 0 and pad % SUBLANES == 0
    src_row = lambda j: (pl.multiple_of(
        jnp.minimum(jnp.where(j <= gap_block, j * w, j * w - pad), d_in - w), SUBLANES), 0)
    return pl.pallas_call(
        _w_pack_kernel,
        grid=(W_PACK_BLOCKS,),
        in_specs=[pl.BlockSpec((pl.Element(w), pl.Element(D_MODEL)), src_row)],
        out_specs=pl.BlockSpec((D_MODEL, w), lambda j: (0, j)),
        out_shape=jax.ShapeDtypeStruct((D_MODEL, W_PACK_BLOCKS * w), BF16),
        compiler_params=pltpu.CompilerParams(vmem_limit_bytes=VMEM_LIMIT),
        name="w_pack",
    )(w_t)


def _ada_call(c_all, w_ada, b_ada):
    rows = c_all.shape[0]
    n = w_ada.shape[1]
    return pl.pallas_call(
        _ada_kernel,
        grid=(n // D_MODEL,),
        in_specs=[pl.BlockSpec((rows, D_MODEL), lambda j: (0, 0)),
                  pl.BlockSpec((D_MODEL, D_MODEL), lambda j: (0, j)),
                  pl.BlockSpec((1, D_MODEL), lambda j: (0, j))],
        out_specs=pl.BlockSpec((rows, D_MODEL), lambda j: (0, j)),
        out_shape=jax.ShapeDtypeStruct((rows, n), F32),
        compiler_params=pltpu.CompilerParams(vmem_limit_bytes=VMEM_LIMIT),
        name="ada",
    )(c_all, w_ada, b_ada)


def _compress_rows(rows, pe, w_bd):
    t = rows.shape[0]
    pooled = jnp.sum(rows.reshape(t // L_CMP, L_CMP, KV_WIDTH) + pe[None], axis=1) * (1.0 / L_CMP)
    return _dot(pooled.astype(BF16), w_bd)


def _p_proj_kernel(x_ref, shift_ref, scale_ref, ng_ref, w_ref, cos_ref, sin_ref, qg_ref, kg_ref,
                   pek_ref, pev_ref, wck_ref, wcv_ref, vng_ref, vnb_ref, ws_ref, bst_ref, wbrb_ref,
                   *outs):
    tm = x_ref.shape[1]
    chains = [_p_proj_rows(r0, x_ref, shift_ref, scale_ref, ng_ref, w_ref, cos_ref, sin_ref, qg_ref, kg_ref,
                           pek_ref, pev_ref, wck_ref, wcv_ref, vng_ref, vnb_ref, ws_ref, bst_ref, wbrb_ref, *outs)
              for r0 in range(0, tm, PROMPT_SUB_ROWS)]
    for _ in itertools.zip_longest(*chains):
        pass


def _p_proj_rows(r0, x_ref, shift_ref, scale_ref, ng_ref, w_ref, cos_ref, sin_ref, qg_ref, kg_ref,
                 pek_ref, pev_ref, wck_ref, wcv_ref, vng_ref, vnb_ref, ws_ref, bst_ref, wbrb_ref,
                 qt_out, kc_t, ks_t, vc_t, vs_t, vw_t, ks_rows, kw_rows, kcmp_out, vcmp_out,
                 gates_t, sza_out, sga_out, mb_out, kw_last, vw_last):
    sub = PROMPT_SUB_ROWS
    rs = slice(r0, r0 + sub)
    cmp_rows = slice(r0 // L_CMP, (r0 + sub) // L_CMP)
    seg = _project(x_ref[0, rs, :], shift_ref[0], scale_ref[0], ng_ref[...], w_ref)
    cos, sin = cos_ref[:, rs], sin_ref[:, rs]
    to_token_minor = lambda a: jnp.concatenate(
        [a[:, c * LANES:(c + 1) * LANES].T for c in range(a.shape[1] // LANES)], axis=0)
    yield

    qt_out[0, :, rs] = (_norm_rope_t(to_token_minor(seg(C_Q, C_K)), qg_ref[...], cos, sin)
                        * (SM_SCALE * LOG2_E)).astype(BF16)
    yield
    k_t = _norm_rope_t(to_token_minor(seg(C_K, C_V)), kg_ref[...], cos, sin)
    yield
    v = seg(C_V, C_G)
    for br, (k_out, v_out) in enumerate(((kc_t, vc_t), (ks_t, vs_t), (kw_last, vw_t))):
        sl = slice(br * KV_WIDTH, (br + 1) * KV_WIDTH)
        k_out[0, :, rs] = k_t[sl]
        v_t = v[:, sl].T
        v_out[0, :, rs] = v_t
    vw_last[0, :, rs] = v_t
    blk_id = jnp.where((_iota((HEAD_DIM, sub), 1) // L_SLC) % (SEL_CHUNK // L_SLC) == _iota((HEAD_DIM, sub), 0),
                       1.0, 0.0)
    ks_g = [k_t[KV_WIDTH + g * HEAD_DIM:KV_WIDTH + (g + 1) * HEAD_DIM] for g in range(N_KV)]
    ks_rows[0, 0, rs, :] = jnp.concatenate([ks_g[0], blk_id], axis=0).T.astype(BF16)
    ks_rows[0, 1, rs, :] = jnp.concatenate([blk_id, ks_g[1]], axis=0).T.astype(BF16)
    kw_rows[0, rs, :] = k_t[2 * KV_WIDTH:3 * KV_WIDTH].T.astype(BF16)
    yield
    kcmp_out[0, cmp_rows, :] = _compress_rows(k_t[0:KV_WIDTH].T, pek_ref[...], wck_ref[...])
    vcmp_out[0, cmp_rows, :] = _compress_rows(v[:, 0:KV_WIDTH], pev_ref[...], wcv_ref[...])
    yield

    gates_t[0, :, rs] = jax.nn.sigmoid(seg(C_G, C_ZA)).T
    sza_out[0, rs, :] = _silu(seg(C_ZA, C_U))
    yield
    sga_out[0, rs, :] = jax.nn.sigmoid(seg(C_GA, C_GB))
    yield

    vn = _layer_norm(seg(C_VB, C_ZB), vng_ref[...], vnb_ref[...]).astype(BF16)
    yield
    causal = _iota((CHUNK, CHUNK), 0) >= _iota((CHUNK, CHUNK), 1)
    chunks = []
    for c in range(sub // CHUNK):
        groups = []
        for g in range(N_GROUPS_B):
            wsg = jnp.where(causal, ws_ref[g], 0.0).astype(BF16)
            vg = vn[c * CHUNK:(c + 1) * CHUNK, g * GROUP_W_B:(g + 1) * GROUP_W_B]
            groups.append(_dot(wsg, vg) + bst_ref[:, g:g + 1])
        chunks.append(jnp.concatenate(groups, axis=1))
    s_b = jnp.concatenate(chunks, axis=0)
    yield
    t = seg(C_U, C_VB) * s_b * _silu(seg(C_ZB, C_GA))
    yield
    mb_out[0, rs, :] = jax.nn.sigmoid(seg(C_GB, C_END)) * _dot(t.astype(BF16), wbrb_ref[...])


def _full(shape):
    nd = len(shape)
    return pl.BlockSpec(shape, lambda *_: (0,) * nd)


def _p_proj_call(x, shift, scale, norm_g, w_all, cos_t, sin_t, qg, kg, pek, pev, wck, wcv, vng, vnb,
                 w_s, bs_t, w_br_b):
    b, s, _ = x.shape
    tm = PROMPT_ROWS
    row = lambda w: pl.BlockSpec((1, tm, w), lambda bi, i: (bi, i, 0))
    tok_minor = pl.BlockSpec((1, KV_WIDTH, tm), lambda bi, i: (bi, 0, i))
    per_batch = pl.BlockSpec((1, 1, D_MODEL), lambda bi, i: (bi, 0, 0))
    cmp_spec = pl.BlockSpec((1, tm // L_CMP, KV_WIDTH), lambda bi, i: (bi, i, 0))
    tab = pl.BlockSpec((HEAD_DIM // 2, tm), lambda bi, i: (0, i))
    in_specs = [row(D_MODEL), per_batch, per_batch, _full(norm_g.shape), _full(w_all.shape), tab, tab,
                _full(qg.shape), _full(kg.shape), _full(pek.shape), _full(pev.shape), _full(wck.shape),
                _full(wcv.shape), _full(vng.shape), _full(vnb.shape), _full(w_s.shape), _full(bs_t.shape),
                _full(w_br_b.shape)]
    qt_spec = pl.BlockSpec((1, WIDTH_A, tm), lambda bi, i: (bi, 0, i))
    group_rows = pl.BlockSpec((1, N_KV, tm, KV_WIDTH), lambda bi, i: (bi, 0, i, 0))
    assert min(WINDOW, s) == tm
    last_tile = pl.BlockSpec((1, KV_WIDTH, tm), lambda bi, i: (bi, 0, 0))
    out_specs = [qt_spec] + [tok_minor] * 5 + [group_rows, row(KV_WIDTH), cmp_spec, cmp_spec,
                                                tok_minor, row(WIDTH_A), row(D_MODEL), row(D_MODEL),
                                                last_tile, last_tile]
    tm_shape = jax.ShapeDtypeStruct((b, KV_WIDTH, s), F32)
    rows_shape = jax.ShapeDtypeStruct((b, s, KV_WIDTH), BF16)
    cmp_shape = jax.ShapeDtypeStruct((b, s // L_CMP, KV_WIDTH), F32)
    assert PROMPT_SUB_ROWS % SEL_CHUNK == 0
    out_shape = [jax.ShapeDtypeStruct((b, WIDTH_A, s), BF16)] + [tm_shape] * 5 + [
        jax.ShapeDtypeStruct((b, N_KV, s, KV_WIDTH), BF16), rows_shape, cmp_shape, cmp_shape,
        jax.ShapeDtypeStruct((b, LANES, s), F32), jax.ShapeDtypeStruct((b, s, WIDTH_A), F32),
        jax.ShapeDtypeStruct((b, s, D_MODEL), F32), jax.ShapeDtypeStruct((b, s, D_MODEL), F32),
        jax.ShapeDtypeStruct((b, KV_WIDTH, tm), F32), jax.ShapeDtypeStruct((b, KV_WIDTH, tm), F32)]
    return pl.pallas_call(
        _p_proj_kernel,
        grid=(b, s // tm),
        in_specs=in_specs,
        out_specs=out_specs,
        out_shape=out_shape,
        compiler_params=pltpu.CompilerParams(dimension_semantics=("arbitrary", "arbitrary"),
                                             vmem_limit_bytes=VMEM_LIMIT),
        name="p_proj",
    )(x, shift, scale, norm_g, w_all, cos_t, sin_t, qg, kg, pek, pev, wck, wcv, vng, vnb, w_s, bs_t, w_br_b)


def _select_blocks(imp_c, qblk, n_blocks):
    ratio = L_SLC // L_CMP
    assert ratio == 2
    lane = _iota(imp_c.shape, 1)
    imp = imp_c + pltpu.roll(imp_c, LANES - 1, 1)
    blk = lane // ratio
    forced = jnp.where((blk == 0) | (blk == qblk), 1.0, 0.0)
    score = jnp.where(blk <= qblk, imp + FORCE_BONUS * forced, NEG)
    beats = []
    for j in range(n_blocks):
        vj = score[:, ratio * j:ratio * j + 1]
        earlier = jnp.where(lane > ratio * j, 1.0, 0.0)
        beats.append(jnp.where(vj > score, 1.0, jnp.where(vj == score, earlier, 0.0)))
    while len(beats) > 1:
        beats = [a + b for a, b in zip(beats[0::2], beats[1::2])] + ([beats[-1]] if len(beats) % 2 else [])
    rank = beats[0]
    cand = (lane % ratio == 0) & (lane < ratio * n_blocks)
    return jnp.where(cand & (rank < float(min(N_SEL, n_blocks))), 1.0, 0.0)


def _softmax_rows(s, valid):
    sm = jnp.where(valid, s, NEG)
    e = jnp.exp(sm - jnp.max(sm, axis=-1, keepdims=True))
    return e, 1.0 / jnp.sum(e, axis=-1, keepdims=True)


def _merge_and_project(o_a, sza, sga, mb, x, gate, wbra_ref, wout_ref):
    a = _dot((o_a * sza).astype(BF16), wbra_ref[...])
    m = sga * a + mb
    return x + gate * _dot(m.astype(BF16), wout_ref[...])


def _select_blocks_t(imp, qblk):
    n_blocks = imp.shape[0]
    blk = _iota(imp.shape, 0)
    forced = jnp.where((blk == 0) | (blk == qblk), 1.0, 0.0)
    score = jnp.where(blk <= qblk, imp + FORCE_BONUS * forced, NEG)
    rank = jnp.zeros(imp.shape, F32)
    for j in range(n_blocks):
        vj = score[j:j + 1, :]
        earlier = jnp.where(blk > j, 1.0, 0.0)
        rank = rank + jnp.where(vj > score, 1.0, jnp.where(vj == score, earlier, 0.0))
    return jnp.where(rank < float(min(N_SEL, n_blocks)), 1.0, 0.0)


def _softmax_cols(s):
    e = jnp.exp2(s - jnp.max(s, axis=0, keepdims=True))
    return e, 1.0 / jnp.sum(e, axis=0, keepdims=True)


def _p_attn_kernel(qt_ref, ks_ref, vst_ref, kw_ref, vwt_ref, kc_ref, vc_ref, gt_ref, sza_ref, sga_ref,
                   mb_ref, x_ref, gate_ref, wbra_ref, wout_ref, y_ref,
                   qt_scr, pen_scr, m_scr, acc_scr, s_even, s_odd, sw_scr, mw_scr, oa_scr, mprev_scr):
    tq = Q_BLOCK
    step = pl.program_id(0)
    i = lax.rem(jnp.minimum(step, pl.num_programs(0) - 2), ks_ref.shape[2] // tq)

    @pl.when(step == 0)
    def _():
        oa_scr[...] = jnp.zeros(oa_scr.shape, F32)
    n_cmp = kc_ref.shape[1]
    half = n_cmp // 2
    assert L_SLC == 2 * L_CMP and n_cmp <= LANES
    cols = GQA * tq
    qpos1 = i * tq + _iota((1, tq), 1)
    qpos = jnp.concatenate([qpos1] * GQA, axis=1)
    band = WINDOW + tq
    ws = pl.multiple_of(jnp.maximum(i * tq - WINDOW, 0), LANES)
    zeros_q = jnp.zeros((HEAD_DIM, cols), BF16)
    perm = lambda ref: jnp.concatenate([ref[0, pl.ds(0, half, stride=2), :], ref[0, pl.ds(1, half, stride=2), :],
                                        jnp.zeros((LANES - n_cmp, KV_WIDTH), F32)], axis=0)
    kc = perm(kc_ref).astype(BF16)
    vc_t = perm(vc_ref).T.astype(BF16)
    crow = _iota((LANES, cols), 0)
    cblk = 2 * (crow % half) + crow // half
    mc = ((cblk + 1) * L_CMP - 1 <= qpos) & (crow < n_cmp)
    gates_t = gt_ref[0]
    groups = [slice(g * HEAD_DIM, (g + 1) * HEAD_DIM) for g in range(N_KV)]

    for g in range(N_KV):
        qt_g = jnp.concatenate([qt_ref[0, h * HEAD_DIM:(h + 1) * HEAD_DIM, :]
                                for h in range(g * GQA, (g + 1) * GQA)], axis=1)
        qt_scr[g] = jnp.concatenate([qt_g, zeros_q] if g == 0 else [zeros_q, qt_g], axis=0)

    per_chunk = SEL_CHUNK // L_SLC
    assert band % SEL_CHUNK == 0
    w_chunks = [pl.ds(pl.multiple_of(ws + c * SEL_CHUNK, LANES), SEL_CHUNK) for c in range(band // SEL_CHUNK)]

    s_cmp = [_dot(kc, qt_scr[g]) for g in range(N_KV)]
    for c, kd in enumerate(w_chunks):
        k_rows = kw_ref[0, kd, :]
        for g in range(N_KV):
            sw_scr[g, c] = _dot(k_rows, qt_scr[g])

    def selected_scores(g, chunk):
        k_aug = ks_ref[0, g, pl.ds(pl.multiple_of(chunk * SEL_CHUNK, SEL_CHUNK), SEL_CHUNK), :]
        gap = jnp.zeros((HEAD_DIM - PEN_ROWS, cols), BF16)
        q_aug = ([qt_scr[0, 0:HEAD_DIM], pen_scr[0, chunk], gap] if g == 0 else
                 [pen_scr[1, chunk], gap, qt_scr[1, HEAD_DIM:2 * HEAD_DIM]])
        return _dot(k_aug, jnp.concatenate(q_aug, axis=0))

    o_c = []
    for g in range(N_KV):
        s_c = jnp.where(mc, s_cmp[g], NEG)
        e_c, r_c = _softmax_cols(s_c)
        p_c = jnp.where(mc, e_c * r_c, 0.0)
        o_c.append(_dot(vc_t[groups[g]], p_c.astype(BF16)))
        imp = p_c[:, 0:tq]
        for r in range(1, GQA):
            imp = imp + p_c[:, r * tq:(r + 1) * tq]
        sel = _select_blocks_t(imp[0:half] + imp[half:2 * half], qpos1 // L_SLC)
        pen = jnp.concatenate([jnp.where(sel > 0.5, 0.0, NEG)] * GQA, axis=1)
        pad_rows = jnp.zeros((PEN_ROWS - per_chunk, cols), F32)
        for c in range(half // per_chunk):
            pen_scr[g, c] = jnp.concatenate([pen[c * per_chunk:(c + 1) * per_chunk], pad_rows],
                                            axis=0).astype(BF16)
        s_even[g] = selected_scores(g, 0)

    for c in range(len(w_chunks)):
        kwpos = ws + c * SEL_CHUNK + _iota((SEL_CHUNK, tq), 0)
        bias_w = jnp.where((kwpos <= qpos1) & (kwpos > qpos1 - WINDOW), 0.0, NEG)
        bias_w = jnp.concatenate([bias_w] * GQA, axis=1)
        for g in range(N_KV):
            s_w = sw_scr[g, c] + bias_w
            sw_scr[g, c] = s_w
            m_c = jnp.max(s_w, axis=0, keepdims=True)
            mw_scr[g] = m_c if c == 0 else jnp.maximum(mw_scr[g], m_c)

    m_scr[...] = jnp.full(m_scr.shape, NEG, F32)
    acc_scr[...] = jnp.zeros(acc_scr.shape, F32)

    def with_ones(v_t):
        return jnp.concatenate([v_t.astype(BF16), jnp.ones((ONES_ROWS, v_t.shape[1]), BF16)], axis=0)

    def trip(kc_i, src, dst):
        off = pl.multiple_of(kc_i * SEL_CHUNK, SEL_CHUNK)
        if dst is None:
            t_prev = (oa_scr[...] * sza_ref[0]).astype(BF16)
            a_prev = []
        for g in range(N_KV):
            if dst is not None:
                dst[g] = selected_scores(g, kc_i + 1)
                s = src[g]
            else:
                half_n = D_MODEL // N_KV
                a_prev.append(_dot(t_prev, wbra_ref[:, g * half_n:(g + 1) * half_n]))
                causal = jnp.where(off + _iota((SEL_CHUNK, tq), 0) <= qpos1, 0.0, NEG)
                s = src[g] + jnp.concatenate([causal] * GQA, axis=1)
            m_old = m_scr[g]
            m_new = jnp.maximum(m_old, jnp.max(s, axis=0, keepdims=True))
            p = jnp.exp2(s - m_new).astype(BF16)
            m_scr[g] = m_new
            acc_scr[g] = jnp.exp2(m_old - m_new) * acc_scr[g] + _dot(
                with_ones(vst_ref[0, groups[g], pl.ds(off, SEL_CHUNK)]), p)
        if dst is None:
            mprev_scr[...] = (sga_ref[0] * jnp.concatenate(a_prev, axis=1) + mb_ref[0]).astype(BF16)

    def by_parity(kc_i, dst_wanted):
        @pl.when(lax.rem(kc_i, 2) == 0)
        def _():
            trip(kc_i, s_even, s_odd if dst_wanted else None)

        @pl.when(lax.rem(kc_i, 2) == 1)
        def _():
            trip(kc_i, s_odd, s_even if dst_wanted else None)

    def body(kc_i, carry):
        by_parity(kc_i, True)
        return carry

    assert SEL_CHUNK == tq
    lax.fori_loop(0, i, body, 0)
    by_parity(i, False)

    acc_w = [None] * N_KV

    def window_pass2():
        for c, kd in enumerate(w_chunks):
            for g in range(N_KV):
                pv = _dot(with_ones(vwt_ref[0, groups[g], kd]),
                          jnp.exp2(sw_scr[g, c] - mw_scr[g]).astype(BF16))
                acc_w[g] = pv if c == 0 else acc_w[g] + pv
                yield

    def previous_block_output():
        n_blk = 2 * LANES
        m = mprev_scr[...]
        for n0 in range(0, D_MODEL, n_blk):
            y_ref[0, :, n0:n0 + n_blk] = (x_ref[0, :, n0:n0 + n_blk]
                                          + gate_ref[0][:, n0:n0 + n_blk] * _dot(m, wout_ref[:, n0:n0 + n_blk]))
            yield

    for _ in itertools.zip_longest(window_pass2(), previous_block_output()):
        pass
    pair = []
    for g in range(N_KV):
        o_w = acc_w[g][0:HEAD_DIM] * (1.0 / acc_w[g][HEAD_DIM:HEAD_DIM + 1])
        acc_s = acc_scr[g]
        o_s = acc_s[0:HEAD_DIM] * (1.0 / acc_s[HEAD_DIM:HEAD_DIM + 1])
        for r in range(GQA):
            h = g * GQA + r
            cs = slice(r * tq, (r + 1) * tq)
            pair.append(gates_t[3 * h:3 * h + 1] * o_c[g][:, cs] + gates_t[3 * h + 1:3 * h + 2] * o_s[:, cs]
                        + gates_t[3 * h + 2:3 * h + 3] * o_w[:, cs])
    per_lane = LANES // HEAD_DIM
    oa_scr[...] = jnp.concatenate([jnp.concatenate(pair[j:j + per_lane], axis=0).T
                                   for j in range(0, N_HEADS, per_lane)], axis=1)


def _p_attn_call(q_t, ks_rows, vs_t, kw_rows, vw_t, kc, vc, gates_t, sza, sga, mb, x, gate, w_br_a, w_out):
    b, s, _ = x.shape
    tq = Q_BLOCK
    n_i = s // tq
    n_blocks = b * n_i
    att = lambda j: jnp.minimum(j, n_blocks - 1)
    out = lambda j: jnp.maximum(j - 1, 0)
    row = lambda w: pl.BlockSpec((1, tq, w), lambda j: (out(j) // n_i, out(j) % n_i, 0))
    col = lambda a: pl.BlockSpec((1, a.shape[1], tq), lambda j: (att(j) // n_i, 0, att(j) % n_i))
    seq = lambda a: pl.BlockSpec((1,) + a.shape[1:], lambda j: (att(j) // n_i,) + (0,) * (a.ndim - 1))
    out_seq = lambda a: pl.BlockSpec((1,) + a.shape[1:], lambda j: (out(j) // n_i,) + (0,) * (a.ndim - 1))
    in_specs = [col(q_t), seq(ks_rows), seq(vs_t), seq(kw_rows), seq(vw_t), seq(kc), seq(vc),
                col(gates_t), row(WIDTH_A), row(D_MODEL), row(D_MODEL), row(D_MODEL), out_seq(gate),
                _full(w_br_a.shape), _full(w_out.shape)]
    return pl.pallas_call(
        _p_attn_kernel,
        grid=(n_blocks + 1,),
        in_specs=in_specs,
        out_specs=row(D_MODEL),
        out_shape=jax.ShapeDtypeStruct((b, s, D_MODEL), F32),
        scratch_shapes=[pltpu.VMEM((N_KV, KV_WIDTH, GQA * tq), BF16),
                        pltpu.VMEM((N_KV, s // SEL_CHUNK, PEN_ROWS, GQA * tq), BF16),
                        pltpu.VMEM((N_KV, 1, GQA * tq), F32),
                        pltpu.VMEM((N_KV, HEAD_DIM + ONES_ROWS, GQA * tq), F32),
                        pltpu.VMEM((N_KV, SEL_CHUNK, GQA * tq), F32),
                        pltpu.VMEM((N_KV, SEL_CHUNK, GQA * tq), F32),
                        pltpu.VMEM((N_KV, (WINDOW + tq) // SEL_CHUNK, SEL_CHUNK, GQA * tq), F32),
                        pltpu.VMEM((N_KV, 1, GQA * tq), F32),
                        pltpu.VMEM((tq, WIDTH_A), F32),
                        pltpu.VMEM((tq, D_MODEL), BF16)],
        compiler_params=pltpu.CompilerParams(dimension_semantics=("arbitrary",),
                                             vmem_limit_bytes=VMEM_LIMIT),
        name="p_attn",
    )(q_t, ks_rows, vs_t, kw_rows, vw_t, kc, vc, gates_t, sza, sga, mb, x, gate, w_br_a, w_out)


def _s_proj_kernel(x_ref, shift_ref, scale_ref, ng_ref, w_ref, cos_ref, sin_ref, qg_ref, kg_ref,
                   vng_ref, vnb_ref, ws0_ref, bs0_ref, wbrb_ref,
                   q_out, k_out, v_out, kt_out, vt_out, gates_out, sza_out, sga_out, mb_out, vn_out):
    seg = _project(x_ref[...], shift_ref[...], scale_ref[...], ng_ref[...], w_ref)
    cos, sin = cos_ref[...], sin_ref[...]
    q_out[...] = _norm_rope(seg(C_Q, C_K), qg_ref[...], cos, sin)
    k = _norm_rope(seg(C_K, C_V), kg_ref[...], cos, sin)
    v = seg(C_V, C_G)
    k_out[...] = k
    v_out[...] = v
    for br in range(3):
        sl = slice(br * KV_WIDTH, (br + 1) * KV_WIDTH)
        kt_out[br] = k[:, sl].T
        vt_out[br] = v[:, sl].T
    gates_out[...] = jax.nn.sigmoid(seg(C_G, C_ZA))
    sza_out[...] = _silu(seg(C_ZA, C_U))
    sga_out[...] = jax.nn.sigmoid(seg(C_GA, C_GB))
    vn = _layer_norm(seg(C_VB, C_ZB), vng_ref[...], vnb_ref[...])
    vn_out[...] = vn
    s_b = ws0_ref[...] * vn + bs0_ref[...]
    t = seg(C_U, C_VB) * s_b * _silu(seg(C_ZB, C_GA))
    mb_out[...] = jax.nn.sigmoid(seg(C_GB, C_END)) * _dot(t.astype(BF16), wbrb_ref[...])


def _s_proj_call(x, shift, scale, norm_g, w_all, cos1, sin1, qg, kg, vng, vnb, ws0, bs0, w_br_b):
    n = x.shape[0]
    args = (x, shift, scale, norm_g, w_all, cos1, sin1, qg, kg, vng, vnb, ws0, bs0, w_br_b)
    sds = lambda *shape: jax.ShapeDtypeStruct(shape, F32)
    out_shape = [sds(n, WIDTH_A), sds(n, 3 * KV_WIDTH), sds(n, 3 * KV_WIDTH), sds(3, KV_WIDTH, n),
                 sds(3, KV_WIDTH, n), sds(n, LANES), sds(n, WIDTH_A), sds(n, D_MODEL), sds(n, D_MODEL),
                 sds(n, WIDTH_B)]
    return pl.pallas_call(
        _s_proj_kernel,
        grid=(1,),
        in_specs=[_full(a.shape) for a in args],
        out_specs=[_full(o.shape) for o in out_shape],
        out_shape=out_shape,
        compiler_params=pltpu.CompilerParams(vmem_limit_bytes=VMEM_LIMIT),
        name="s_proj",
    )(*args)


def _s_attn_kernel(pt_ref, q_ref, gates_ref, knew_ref, vnew_ref, knewt_ref, vnewt_ref, kwin_ref, vwin_ref,
                   pek_ref, pev_ref, wck_ref, wcv_ref, pool_ref,
                   kc_hbm, vc_hbm, ks_hbm, vs_hbm,
                   oa_ref, okw_ref, ovw_ref, buf, sem):
    t = pl.program_id(0)
    n_groups = pl.num_programs(0) - 1
    per_step = q_ref.shape[0]
    n_seqs = pt_ref.shape[0]
    n_pages = pt_ref.shape[1]
    past = n_pages * PAGE_SIZE
    caches = (kc_hbm, vc_hbm, ks_hbm, vs_hbm)

    def page_copies(group, slot_):
        return [pltpu.make_async_copy(hbm.at[pt_ref[jnp.minimum(group * per_step + j, n_seqs - 1), p]],
                                      buf.at[slot_, j, c, :, pl.ds(p * PAGE_SIZE, PAGE_SIZE)],
                                      sem.at[slot_, c])
                for j in range(per_step) for c, hbm in enumerate(caches) for p in range(n_pages)]

    @pl.when(t < n_groups)
    def _():
        for cp in page_copies(t, lax.rem(t, 2)):
            cp.start()

    @pl.when(t > 0)
    def _():
        group = t - 1
        slot = lax.rem(group, 2)
        for cp in page_copies(group, slot):
            cp.wait()
        chains = [_s_attn_one(group * per_step + j, j, buf.at[slot, j], q_ref, gates_ref, knew_ref, vnew_ref,
                              knewt_ref, vnewt_ref, kwin_ref, vwin_ref, pek_ref, pev_ref, wck_ref, wcv_ref,
                              pool_ref, oa_ref, okw_ref, ovw_ref, past) for j in range(per_step)]
        for _ in itertools.zip_longest(*chains):
            pass


def _s_attn_one(b, j, buf, q_ref, gates_ref, knew_ref, vnew_ref, knewt_ref, vnewt_ref, kwin_ref, vwin_ref,
                pek_ref, pev_ref, wck_ref, wcv_ref, pool_ref, oa_ref, okw_ref, ovw_ref, past):
    qpos = past

    lane = _iota((KV_WIDTH, knewt_ref.shape[2]), 1)
    col = lambda ref, br: jnp.sum(jnp.where(lane == b, ref[br], 0.0), axis=1, keepdims=True)
    knew = knew_ref[pl.ds(b, 1), :]
    vnew = vnew_ref[pl.ds(b, 1), :]

    hrow = _iota((N_HEADS, 1), 0)
    first_group = hrow < GQA
    by_group = lambda f: jnp.where(first_group, f(0), f(1))
    gl = lambda g: slice(g * HEAD_DIM, (g + 1) * HEAD_DIM)
    qb = q_ref[j].astype(BF16)

    n_cmp = -(-(past + 1) // L_SLC) * L_SLC // L_CMP
    n_slc = n_cmp * L_CMP // L_SLC

    def summaries(c, new_row, pe_ref, w_ref, out):
        chunk = pool_ref.shape[1]
        sums = []
        for t0 in range(0, past, chunk):
            hi, lo = _split_bf16(buf[c, :, t0:t0 + chunk])
            sums.append(_dot_nt(pool_ref[...], hi) + _dot_nt(pool_ref[...], lo))
            yield
        pe_sum = jnp.sum(pe_ref[...], axis=0, keepdims=True)
        pooled = (jnp.concatenate(sums, axis=0) + pe_sum) * (1.0 / L_CMP)
        r = _iota((SUBLANES, KV_WIDTH), 0)
        tail = jnp.where(r == 0, new_row + pe_sum, jnp.where(r == 1, pe_sum, 0.0)) * (1.0 / L_CMP)
        zeros = jnp.zeros((LANES - pooled.shape[0] - SUBLANES, KV_WIDTH), F32)
        out.append(_dot(jnp.concatenate([pooled, tail, zeros], axis=0).astype(BF16), w_ref[...]).astype(BF16))

    summary = []
    yield from summaries(0, knew[:, 0:KV_WIDTH], pek_ref, wck_ref, summary)
    yield from summaries(1, vnew[:, 0:KV_WIDTH], pev_ref, wcv_ref, summary)
    kc, vc = summary
    s_c = by_group(lambda g: _dot_nt(qb, kc[:, gl(g)])) * SM_SCALE
    yield
    c = _iota(s_c.shape, 1)
    mc = ((c + 1) * L_CMP - 1 <= qpos) & (c < n_cmp)
    e_c, r_c = _softmax_rows(s_c, mc)
    p_c = jnp.where(mc, e_c * r_c, 0.0)
    yield
    o_c = by_group(lambda g: _dot(p_c.astype(BF16), vc[:, gl(g)]))
    imp = by_group(lambda g: jnp.sum(p_c[g * GQA:(g + 1) * GQA], axis=0, keepdims=True))
    imp = jnp.broadcast_to(imp, p_c.shape)
    yield
    sel = _select_blocks(imp, jnp.full((N_HEADS, 1), qpos // L_SLC, jnp.int32), n_slc)
    yield

    ratio = L_SLC // L_CMP
    first_blk = _iota((N_HEADS, LANES), 1) < L_SLC
    blk_col = lambda j: sel[:, ratio * j:ratio * j + 1]
    picked = jnp.concatenate([jnp.where(first_blk, blk_col(2 * c), blk_col(2 * c + 1))
                              for c in range(past // LANES)], axis=1)
    yield
    s_s = by_group(lambda g: _dot(qb, buf[2, gl(g), :].astype(BF16))) * SM_SCALE
    yield
    kpos = _iota(s_s.shape, 1)
    s_s = jnp.where((picked > 0.5) & (kpos <= qpos), s_s, NEG)
    yield
    own =_iota((N_HEADS, KV_WIDTH), 1) // HEAD_DIM == hrow // GQA
    rounded = lambda a: a.astype(BF16).astype(F32)
    q_pair = jnp.concatenate([qb.astype(F32)] * N_KV, axis=1)
    s_new = jnp.sum(jnp.where(own, q_pair * rounded(knew[:, KV_WIDTH:2 * KV_WIDTH]), 0.0),
                    axis=1, keepdims=True) * SM_SCALE
    new_lane = (L_SLC // L_CMP) * (past // L_SLC)
    s_new = jnp.where((sel[:, new_lane:new_lane + 1] > 0.5) & (past <= qpos), s_new, NEG)
    m_s = jnp.maximum(jnp.max(s_s, axis=-1, keepdims=True), s_new)
    e_s, e_new = jnp.exp(s_s - m_s), jnp.exp(s_new - m_s)
    r_s = 1.0 / (jnp.sum(e_s, axis=-1, keepdims=True) + e_new)
    yield
    v_new = by_group(lambda g: rounded(vnew[:, KV_WIDTH + g * HEAD_DIM:KV_WIDTH + (g + 1) * HEAD_DIM]))
    o_s = (by_group(lambda g: _dot_nt(e_s.astype(BF16), buf[3, gl(g), :].astype(BF16)))
           + rounded(e_new) * v_new) * r_s
    yield

    wb = kwin_ref.shape[2]
    wlane = _iota((KV_WIDTH, wb), 1)
    kw = jnp.where(wlane == wb - 1, col(knewt_ref, 2), pltpu.roll(kwin_ref[j], wb - 1, 1))
    vw = jnp.where(wlane == wb - 1, col(vnewt_ref, 2), pltpu.roll(vwin_ref[j], wb - 1, 1))
    okw_ref[j] = kw
    ovw_ref[j] = vw
    yield
    s_w = by_group(lambda g: _dot(qb, kw[gl(g)].astype(BF16))) * SM_SCALE
    yield
    kwpos = past - wb + 1 + _iota(s_w.shape, 1)
    e_w, r_w = _softmax_rows(s_w, (kwpos <= qpos) & (kwpos > qpos - WINDOW) & (kwpos >= 0))
    yield
    o_w = by_group(lambda g: _dot_nt(e_w.astype(BF16), vw[gl(g)].astype(BF16))) * r_w

    gates = gates_ref[j]
    oa_ref[j] = gates[:, 0:1] * o_c + gates[:, 1:2] * o_s + gates[:, 2:3] * o_w


def _s_attn_call(page_table, q3, gates3, knew, vnew, knew_t, vnew_t, kwin_t, vwin_t, pek, pev, wck, wcv,
                 kc_pool, vc_pool, ks_pool, vs_pool):
    n, n_pages = page_table.shape
    past = n_pages * PAGE_SIZE
    wb = kwin_t.shape[2]
    tok = np.arange(SUMMARY_CHUNK)
    assert past % SUMMARY_CHUNK == 0
    pool = jnp.asarray((tok[None, :] // L_CMP == np.arange(SUMMARY_CHUNK // L_CMP)[:, None]), BF16)
    k = SAMPLE_SEQS_PER_STEP
    assert n % k == 0
    per_seq = lambda a: pl.BlockSpec((k,) + a.shape[1:],
                                     lambda t, pt: (jnp.maximum(t - 1, 0),) + (0,) * (a.ndim - 1))
    full = lambda a: pl.BlockSpec(a.shape, lambda t, pt: (0,) * a.ndim)
    hbm = pl.BlockSpec(memory_space=pl.ANY)
    resident = (knew, vnew, knew_t, vnew_t)
    consts = (pek, pev, wck, wcv, pool)
    grid_spec = pltpu.PrefetchScalarGridSpec(
        num_scalar_prefetch=1,
        grid=(n // k + 1,),
        in_specs=[per_seq(q3), per_seq(gates3)] + [full(a) for a in resident]
                 + [per_seq(kwin_t), per_seq(vwin_t)] + [full(a) for a in consts] + [hbm] * 4,
        out_specs=[per_seq(q3), per_seq(kwin_t), per_seq(vwin_t)],
        scratch_shapes=[pltpu.VMEM((2, k, 4, KV_WIDTH, past), F32), pltpu.SemaphoreType.DMA((2, 4))],
    )
    return pl.pallas_call(
        _s_attn_kernel,
        grid_spec=grid_spec,
        out_shape=[jax.ShapeDtypeStruct(q3.shape, F32), jax.ShapeDtypeStruct(kwin_t.shape, F32),
                   jax.ShapeDtypeStruct(vwin_t.shape, F32)],
        compiler_params=pltpu.CompilerParams(dimension_semantics=("arbitrary",),
                                             vmem_limit_bytes=VMEM_LIMIT),
        name="s_attn",
    )(page_table, q3, gates3, knew, vnew, knew_t, vnew_t, kwin_t, vwin_t, pek, pev, wck, wcv, pool,
      kc_pool, vc_pool, ks_pool, vs_pool)


def _s_out_kernel(oa_ref, sza_ref, sga_ref, mb_ref, x_ref, gate_ref, wbra_ref, wout_ref, y_ref):
    y_ref[...] = _merge_and_project(oa_ref[...], sza_ref[...], sga_ref[...], mb_ref[...], x_ref[...],
                                    gate_ref[...], wbra_ref, wout_ref)


def _s_out_call(o_a, sza, sga, mb, x, gate, w_br_a, w_out):
    args = (o_a, sza, sga, mb, x, gate, w_br_a, w_out)
    return pl.pallas_call(
        _s_out_kernel,
        grid=(1,),
        in_specs=[_full(a.shape) for a in args],
        out_specs=_full(x.shape),
        out_shape=jax.ShapeDtypeStruct(x.shape, F32),
        compiler_params=pltpu.CompilerParams(vmem_limit_bytes=VMEM_LIMIT),
        name="s_out",
    )(*args)


def _rope_angles(pos):
    half = HEAD_DIM // 2
    inv = ROPE_THETA ** (-jnp.arange(half, dtype=F32) * 2.0 / HEAD_DIM)
    return pos.astype(F32)[:, None] * inv[None, :]


def _rope_tables(pos):
    ang = _rope_angles(pos)
    cos, sin = jnp.cos(ang), jnp.sin(ang)
    cos_t = jnp.concatenate([cos, cos] * (LANES // HEAD_DIM), axis=1)
    sin_t = jnp.concatenate([-sin, sin] * (LANES // HEAD_DIM), axis=1)
    return cos_t, sin_t


def _token_minor(a):
    b, t = a.shape[:2]
    return jnp.transpose(a, (0, 2, 3, 1)).reshape(b, KV_WIDTH, t)


def _token_major(a_t):
    b, _, t = a_t.shape
    return jnp.transpose(a_t.reshape(b, N_KV, HEAD_DIM, t), (0, 3, 1, 2))


def kernel(x_prompt, x_sample, cache_k_cmp, cache_v_cmp, cache_k_slc, cache_v_slc, cache_k_win, cache_v_win, page_table, c_prompt, c_sample, w_ada, b_ada, norm_g, w_in, q_norm_g, k_norm_g, cmp_pos_k, cmp_pos_v, w_cmp_k, w_cmp_v, vnorm_g, vnorm_b, w_s, b_s, w_br_a, w_br_b, w_out):
    assert w_ada.shape[0] == 1, "single layer"
    b, s, _ = x_prompt.shape
    n = x_sample.shape[0]
    assert x_sample.shape[1] == 1
    n_pages = page_table.shape[1]
    past = n_pages * PAGE_SIZE

    w_all = _w_pack_call(w_in[0].T)
    eye = jnp.eye(N_KV, dtype=F32)
    wck = jnp.kron(eye, w_cmp_k[0]).astype(BF16)
    wcv = jnp.kron(eye, w_cmp_v[0]).astype(BF16)
    pek = jnp.tile(cmp_pos_k[0], (1, N_KV))
    pev = jnp.tile(cmp_pos_v[0], (1, N_KV))
    qg = jnp.tile(q_norm_g, (1, LANES // HEAD_DIM))
    kg = jnp.tile(k_norm_g, (1, LANES // HEAD_DIM))
    w_br_a_b, w_br_b_b, w_out_b = w_br_a[0].astype(BF16), w_br_b[0].astype(BF16), w_out[0].astype(BF16)

    mod = _ada_call(jnp.concatenate([c_prompt, c_sample], axis=0), w_ada[0], b_ada)
    shift, scale, gate = mod[:, :D_MODEL], mod[:, D_MODEL:2 * D_MODEL], mod[:, 2 * D_MODEL:]

    ang_p = _rope_angles(jnp.arange(s, dtype=jnp.int32)).T
    gain_cols = lambda g: jnp.broadcast_to(g[0][:, None], (HEAD_DIM, PROMPT_SUB_ROWS))
    (q_t, kc_t, ks_t, vc_t, vs_t, vw_t, ks_rows, kw_rows, kcmp, vcmp, gates_t, sza, sga, mb,
     kw_last, vw_last) = _p_proj_call(
        x_prompt, shift[:b, None], scale[:b, None], norm_g, w_all, jnp.cos(ang_p), jnp.sin(ang_p),
        gain_cols(q_norm_g), gain_cols(k_norm_g), pek, pev, wck, wcv,
        vnorm_g, vnorm_b, w_s[0], b_s[0].T, w_br_b_b)
    y_prompt = _p_attn_call(q_t, ks_rows, vs_t, kw_rows, vw_t, kcmp, vcmp, gates_t, sza, sga, mb, x_prompt,
                            gate[:b, None], w_br_a_b, w_out_b)
    p_states = [_token_major(a)[None] for a in (kc_t, vc_t, ks_t, vs_t, kw_last, vw_last)]

    xs = x_sample.reshape(n, D_MODEL)
    cos_s, sin_s = _rope_tables(jnp.full((1,), past, jnp.int32))
    ws0 = jnp.repeat(w_s[0, :, 0, 0], GROUP_W_B)[None]
    bs0 = jnp.repeat(b_s[0, :, 0], GROUP_W_B)[None]
    (q_s, k_s, v_s, kt_s, vt_s, gates_s, sza_s, sga_s, mb_s, vn_s) = _s_proj_call(
        xs, shift[b:], scale[b:], norm_g, w_all, cos_s, sin_s, qg, kg, vnorm_g, vnorm_b, ws0, bs0, w_br_b_b)
    pools = [_token_minor(c[0]) for c in (cache_k_cmp, cache_v_cmp, cache_k_slc, cache_v_slc)]
    o_a, kwin_new, vwin_new = _s_attn_call(
        page_table, q_s.reshape(n, N_HEADS, HEAD_DIM), gates_s[:, :3 * N_HEADS].reshape(n, N_HEADS, 3),
        k_s, v_s, kt_s, vt_s, _token_minor(cache_k_win[0]), _token_minor(cache_v_win[0]),
        pek, pev, wck, wcv, *pools)
    y_sample = _s_out_call(o_a.reshape(n, WIDTH_A), sza_s, sga_s, mb_s, xs, gate[b:], w_br_a_b, w_out_b)

    new_rows = lambda t, br: jnp.transpose(t[br].reshape(N_KV, HEAD_DIM, n), (2, 0, 1))[None, :, None]
    s_states = [new_rows(kt_s, 0), new_rows(vt_s, 0), new_rows(kt_s, 1), new_rows(vt_s, 1),
                _token_major(kwin_new)[None], _token_major(vwin_new)[None], vn_s[None, :, None]]
    return (y_prompt, y_sample.reshape(n, 1, D_MODEL), *p_states, *s_states)
```

```python
import itertools

import numpy as np
import jax
import jax.numpy as jnp
from jax import lax
from jax.experimental import pallas as pl
from jax.experimental.pallas import tpu as pltpu

F32 = jnp.float32
BF16 = jnp.bfloat16

D_MODEL = 1024
HEAD_DIM = 64
N_HEADS = 8
N_KV = 2
GQA = N_HEADS // N_KV
WIDTH_A = N_HEADS * HEAD_DIM
KV_WIDTH = N_KV * HEAD_DIM
L_CMP = 32
L_SLC = 64
N_SEL = 8
WINDOW = 512
Q_BLOCK = 256
FORCE_BONUS = 1.0e4
ROPE_THETA = 10000.0
CHUNK = 128
N_GROUPS_B = 4
WIDTH_B = 512
GROUP_W_B = WIDTH_B // N_GROUPS_B
PAGE_SIZE = 128
EPS = 1e-6
NEG = -1e30
SM_SCALE = HEAD_DIM ** -0.5
LOG2_E = 1.4426950408889634

LANES = 128
SUBLANES = 8
VMEM_LIMIT = 56 * 1024 * 1024

C_Q = 0
C_K = C_Q + WIDTH_A
C_V = C_K + 3 * KV_WIDTH
C_G = C_V + 3 * KV_WIDTH
C_ZA = C_G + LANES
C_U = C_ZA + WIDTH_A
C_VB = C_U + WIDTH_B
C_ZB = C_VB + WIDTH_B
C_GA = C_ZB + WIDTH_B
C_GB = C_GA + D_MODEL
C_END = C_GB + D_MODEL

PROMPT_ROWS = 512
PROMPT_SUB_ROWS = 256
SEL_CHUNK = 256
PEN_ROWS = 16
ONES_ROWS = 16
SUMMARY_CHUNK = 512
SAMPLE_SEQS_PER_STEP = 4


def _dot(a, b):
    return jnp.dot(a, b, preferred_element_type=F32)


def _dot_nt(a, b):
    return lax.dot_general(a, b, (((1,), (1,)), ((), ())), preferred_element_type=F32)


def _iota(shape, dim):
    return lax.broadcasted_iota(jnp.int32, shape, dim)


def _split_bf16(x):
    hi = x.astype(BF16)
    lo = (x - hi.astype(F32)).astype(BF16)
    return hi, lo


def _head_mean_sq(x):
    w = x.shape[1]
    ones_bd = jnp.where(_iota((LANES, LANES), 0) // HEAD_DIM == _iota((LANES, LANES), 1) // HEAD_DIM,
                        1.0, 0.0).astype(BF16)
    hi, lo = _split_bf16(x * x)
    cols = []
    for c in range(w // LANES):
        sl = slice(c * LANES, (c + 1) * LANES)
        cols.append(_dot(hi[:, sl], ones_bd) + _dot(lo[:, sl], ones_bd))
    return jnp.concatenate(cols, axis=1) * (1.0 / HEAD_DIM)


def _tile_lanes(t, width):
    return jnp.concatenate([t] * (width // t.shape[1]), axis=1)


def _norm_rope(x, g, cos, sin):
    w = x.shape[1]
    y = x * lax.rsqrt(_head_mean_sq(x) + EPS) * _tile_lanes(g, w)
    first_half = (_iota(y.shape, 1) % HEAD_DIM) < (HEAD_DIM // 2)
    rot = jnp.where(first_half, pltpu.roll(y, w - HEAD_DIM // 2, 1), pltpu.roll(y, HEAD_DIM // 2, 1))
    return y * _tile_lanes(cos, w) + rot * _tile_lanes(sin, w)


def _norm_rope_t(x_t, gain_t, cos_t, sin_t):
    half = HEAD_DIM // 2
    out = []
    for h in range(x_t.shape[0] // HEAD_DIM):
        x = x_t[h * HEAD_DIM:(h + 1) * HEAD_DIM]
        y = x * lax.rsqrt(jnp.mean(x * x, axis=0, keepdims=True) + EPS) * gain_t
        y1, y2 = y[:half], y[half:]
        out += [y1 * cos_t - y2 * sin_t, y2 * cos_t + y1 * sin_t]
    return jnp.concatenate(out, axis=0)


def _silu(z):
    return z * jax.nn.sigmoid(z)


def _project(x, shift, scale, norm_g, w_ref):
    ms = jnp.mean(x * x, axis=-1, keepdims=True)
    h = (x * lax.rsqrt(ms + EPS) * norm_g) * (1.0 + scale) + shift
    hb = h.astype(BF16)
    return lambda lo, hi: _dot(hb, w_ref[:, lo:hi])


def _layer_norm(v, g, b):
    mu = jnp.mean(v, axis=-1, keepdims=True)
    d = v - mu
    var = jnp.mean(d * d, axis=-1, keepdims=True)
    return d * lax.rsqrt(var + EPS) * g + b


def _ada_kernel(c_ref, w_ref, b_ref, o_ref):
    o_ref[...] = _dot(c_ref[...].astype(BF16), w_ref[...].astype(BF16)) + b_ref[...]


W_PACK_COLS = 512
W_GAP_LO = C_G + 3 * N_HEADS
W_GAP_HI = C_G + LANES
W_PACK_BLOCKS = -(-C_END // W_PACK_COLS)


def _w_pack_kernel(wt_ref, o_ref):
    j = pl.program_id(0)
    w = W_PACK_COLS
    gap_block, last = W_GAP_LO // w, W_PACK_BLOCKS - 1
    d_in = C_END - (W_GAP_HI - W_GAP_LO)

    def emit(rows):
        for s in range(w // LANES):
            blk = rows[s * LANES:(s + 1) * LANES]
            o_ref[:, s * LANES:(s + 1) * LANES] = jnp.concatenate(
                [blk[:, c * LANES:(c + 1) * LANES].T for c in range(D_MODEL // LANES)], axis=0).astype(BF16)

    @pl.when((j != gap_block) & (j != last))
    def _():
        emit(wt_ref[...])

    @pl.when(j == gap_block)
    def _():
        k0, k1 = W_GAP_LO - gap_block * w, W_GAP_HI - gap_block * w
        emit(jnp.concatenate([wt_ref[0:k0, :], jnp.zeros((k1 - k0, D_MODEL), F32), wt_ref[k0:k0 + w - k1, :]],
                             axis=0))

    @pl.when(j == last)
    def _():
        shift = (last * w - (W_GAP_HI - W_GAP_LO)) - (d_in - w)
        n_valid = C_END - last * w
        emit(jnp.concatenate([wt_ref[shift:shift + n_valid, :], jnp.zeros((w - n_valid, D_MODEL), F32)], axis=0))


def _w_pack_call(w_t):
    d_in = w_t.shape[0]
    w = W_PACK_COLS
    pad = W_GAP_HI - W_GAP_LO
    gap_block = W_GAP_LO // w
    assert d_in + pad == C_END and W_GAP_HI <= (gap_block + 1) * w and gap_block < W_PACK_BLOCKS - 1
    assert W_GAP_LO % SUBLANES == 0 and pad % SUBLANES == 0
    src_row = lambda j: (pl.multiple_of(
        jnp.minimum(jnp.where(j <= gap_block, j * w, j * w - pad), d_in - w), SUBLANES), 0)
    return pl.pallas_call(
        _w_pack_kernel,
        grid=(W_PACK_BLOCKS,),
        in_specs=[pl.BlockSpec((pl.Element(w), pl.Element(D_MODEL)), src_row)],
        out_specs=pl.BlockSpec((D_MODEL, w), lambda j: (0, j)),
        out_shape=jax.ShapeDtypeStruct((D_MODEL, W_PACK_BLOCKS * w), BF16),
        compiler_params=pltpu.CompilerParams(vmem_limit_bytes=VMEM_LIMIT),
        name="w_pack",
    )(w_t)


def _ada_call(c_all, w_ada, b_ada):
    rows = c_all.shape[0]
    n = w_ada.shape[1]
    return pl.pallas_call(
        _ada_kernel,
        grid=(n // D_MODEL,),
        in_specs=[pl.BlockSpec((rows, D_MODEL), lambda j: (0, 0)),
                  pl.BlockSpec((D_MODEL, D_MODEL), lambda j: (0, j)),
                  pl.BlockSpec((1, D_MODEL), lambda j: (0, j))],
        out_specs=pl.BlockSpec((rows, D_MODEL), lambda j: (0, j)),
        out_shape=jax.ShapeDtypeStruct((rows, n), F32),
        compiler_params=pltpu.CompilerParams(vmem_limit_bytes=VMEM_LIMIT),
        name="ada",
    )(c_all, w_ada, b_ada)


def _compress_rows(rows, pe, w_bd):
    t = rows.shape[0]
    pooled = jnp.sum(rows.reshape(t // L_CMP, L_CMP, KV_WIDTH) + pe[None], axis=1) * (1.0 / L_CMP)
    return _dot(pooled.astype(BF16), w_bd)


def _p_proj_kernel(x_ref, shift_ref, scale_ref, ng_ref, w_ref, cos_ref, sin_ref, qg_ref, kg_ref,
                   pek_ref, pev_ref, wck_ref, wcv_ref, vng_ref, vnb_ref, ws_ref, bst_ref, wbrb_ref,
                   *outs):
    tm = x_ref.shape[1]
    chains = [_p_proj_rows(r0, x_ref, shift_ref, scale_ref, ng_ref, w_ref, cos_ref, sin_ref, qg_ref, kg_ref,
                           pek_ref, pev_ref, wck_ref, wcv_ref, vng_ref, vnb_ref, ws_ref, bst_ref, wbrb_ref, *outs)
              for r0 in range(0, tm, PROMPT_SUB_ROWS)]
    for _ in itertools.zip_longest(*chains):
        pass


def _p_proj_rows(r0, x_ref, shift_ref, scale_ref, ng_ref, w_ref, cos_ref, sin_ref, qg_ref, kg_ref,
                 pek_ref, pev_ref, wck_ref, wcv_ref, vng_ref, vnb_ref, ws_ref, bst_ref, wbrb_ref,
                 qt_out, kc_t, ks_t, vc_t, vs_t, vw_t, ks_rows, kw_rows, kcmp_out, vcmp_out,
                 gates_t, sza_out, sga_out, mb_out, kw_last, vw_last):
    sub = PROMPT_SUB_ROWS
    rs = slice(r0, r0 + sub)
    cmp_rows = slice(r0 // L_CMP, (r0 + sub) // L_CMP)
    seg = _project(x_ref[0, rs, :], shift_ref[0], scale_ref[0], ng_ref[...], w_ref)
    cos, sin = cos_ref[:, rs], sin_ref[:, rs]
    to_token_minor = lambda a: jnp.concatenate(
        [a[:, c * LANES:(c + 1) * LANES].T for c in range(a.shape[1] // LANES)], axis=0)
    yield

    qt_out[0, :, rs] = (_norm_rope_t(to_token_minor(seg(C_Q, C_K)), qg_ref[...], cos, sin)
                        * (SM_SCALE * LOG2_E)).astype(BF16)
    yield
    k_t = _norm_rope_t(to_token_minor(seg(C_K, C_V)), kg_ref[...], cos, sin)
    yield
    v = seg(C_V, C_G)
    for br, (k_out, v_out) in enumerate(((kc_t, vc_t), (ks_t, vs_t), (kw_last, vw_t))):
        sl = slice(br * KV_WIDTH, (br + 1) * KV_WIDTH)
        k_out[0, :, rs] = k_t[sl]
        v_t = v[:, sl].T
        v_out[0, :, rs] = v_t
    vw_last[0, :, rs] = v_t
    blk_id = jnp.where((_iota((HEAD_DIM, sub), 1) // L_SLC) % (SEL_CHUNK // L_SLC) == _iota((HEAD_DIM, sub), 0),
                       1.0, 0.0)
    ks_g = [k_t[KV_WIDTH + g * HEAD_DIM:KV_WIDTH + (g + 1) * HEAD_DIM] for g in range(N_KV)]
    ks_rows[0, 0, rs, :] = jnp.concatenate([ks_g[0], blk_id], axis=0).T.astype(BF16)
    ks_rows[0, 1, rs, :] = jnp.concatenate([blk_id, ks_g[1]], axis=0).T.astype(BF16)
    kw_rows[0, rs, :] = k_t[2 * KV_WIDTH:3 * KV_WIDTH].T.astype(BF16)
    yield
    kcmp_out[0, cmp_rows, :] = _compress_rows(k_t[0:KV_WIDTH].T, pek_ref[...], wck_ref[...])
    vcmp_out[0, cmp_rows, :] = _compress_rows(v[:, 0:KV_WIDTH], pev_ref[...], wcv_ref[...])
    yield

    gates_t[0, :, rs] = jax.nn.sigmoid(seg(C_G, C_ZA)).T
    sza_out[0, rs, :] = _silu(seg(C_ZA, C_U))
    yield
    sga_out[0, rs, :] = jax.nn.sigmoid(seg(C_GA, C_GB))
    yield

    vn = _layer_norm(seg(C_VB, C_ZB), vng_ref[...], vnb_ref[...]).astype(BF16)
    yield
    causal = _iota((CHUNK, CHUNK), 0) >= _iota((CHUNK, CHUNK), 1)
    chunks = []
    for c in range(sub // CHUNK):
        groups = []
        for g in range(N_GROUPS_B):
            wsg = jnp.where(causal, ws_ref[g], 0.0).astype(BF16)
            vg = vn[c * CHUNK:(c + 1) * CHUNK, g * GROUP_W_B:(g + 1) * GROUP_W_B]
            groups.append(_dot(wsg, vg) + bst_ref[:, g:g + 1])
        chunks.append(jnp.concatenate(groups, axis=1))
    s_b = jnp.concatenate(chunks, axis=0)
    yield
    t = seg(C_U, C_VB) * s_b * _silu(seg(C_ZB, C_GA))
    yield
    mb_out[0, rs, :] = jax.nn.sigmoid(seg(C_GB, C_END)) * _dot(t.astype(BF16), wbrb_ref[...])


def _full(shape):
    nd = len(shape)
    return pl.BlockSpec(shape, lambda *_: (0,) * nd)


def _p_proj_call(x, shift, scale, norm_g, w_all, cos_t, sin_t, qg, kg, pek, pev, wck, wcv, vng, vnb,
                 w_s, bs_t, w_br_b):
    b, s, _ = x.shape
    tm = PROMPT_ROWS
    row = lambda w: pl.BlockSpec((1, tm, w), lambda bi, i: (bi, i, 0))
    tok_minor = pl.BlockSpec((1, KV_WIDTH, tm), lambda bi, i: (bi, 0, i))
    per_batch = pl.BlockSpec((1, 1, D_MODEL), lambda bi, i: (bi, 0, 0))
    cmp_spec = pl.BlockSpec((1, tm // L_CMP, KV_WIDTH), lambda bi, i: (bi, i, 0))
    tab = pl.BlockSpec((HEAD_DIM // 2, tm), lambda bi, i: (0, i))
    in_specs = [row(D_MODEL), per_batch, per_batch, _full(norm_g.shape), _full(w_all.shape), tab, tab,
                _full(qg.shape), _full(kg.shape), _full(pek.shape), _full(pev.shape), _full(wck.shape),
                _full(wcv.shape), _full(vng.shape), _full(vnb.shape), _full(w_s.shape), _full(bs_t.shape),
                _full(w_br_b.shape)]
    qt_spec = pl.BlockSpec((1, WIDTH_A, tm), lambda bi, i: (bi, 0, i))
    group_rows = pl.BlockSpec((1, N_KV, tm, KV_WIDTH), lambda bi, i: (bi, 0, i, 0))
    assert min(WINDOW, s) == tm
    last_tile = pl.BlockSpec((1, KV_WIDTH, tm), lambda bi, i: (bi, 0, 0))
    out_specs = [qt_spec] + [tok_minor] * 5 + [group_rows, row(KV_WIDTH), cmp_spec, cmp_spec,
                                                tok_minor, row(WIDTH_A), row(D_MODEL), row(D_MODEL),
                                                last_tile, last_tile]
    tm_shape = jax.ShapeDtypeStruct((b, KV_WIDTH, s), F32)
    rows_shape = jax.ShapeDtypeStruct((b, s, KV_WIDTH), BF16)
    cmp_shape = jax.ShapeDtypeStruct((b, s // L_CMP, KV_WIDTH), F32)
    assert PROMPT_SUB_ROWS % SEL_CHUNK == 0
    out_shape = [jax.ShapeDtypeStruct((b, WIDTH_A, s), BF16)] + [tm_shape] * 5 + [
        jax.ShapeDtypeStruct((b, N_KV, s, KV_WIDTH), BF16), rows_shape, cmp_shape, cmp_shape,
        jax.ShapeDtypeStruct((b, LANES, s), F32), jax.ShapeDtypeStruct((b, s, WIDTH_A), F32),
        jax.ShapeDtypeStruct((b, s, D_MODEL), F32), jax.ShapeDtypeStruct((b, s, D_MODEL), F32),
        jax.ShapeDtypeStruct((b, KV_WIDTH, tm), F32), jax.ShapeDtypeStruct((b, KV_WIDTH, tm), F32)]
    return pl.pallas_call(
        _p_proj_kernel,
        grid=(b, s // tm),
        in_specs=in_specs,
        out_specs=out_specs,
        out_shape=out_shape,
        compiler_params=pltpu.CompilerParams(dimension_semantics=("arbitrary", "arbitrary"),
                                             vmem_limit_bytes=VMEM_LIMIT),
        name="p_proj",
    )(x, shift, scale, norm_g, w_all, cos_t, sin_t, qg, kg, pek, pev, wck, wcv, vng, vnb, w_s, bs_t, w_br_b)


def _select_blocks(imp_c, qblk, n_blocks):
    ratio = L_SLC // L_CMP
    assert ratio == 2
    lane = _iota(imp_c.shape, 1)
    imp = imp_c + pltpu.roll(imp_c, LANES - 1, 1)
    blk = lane // ratio
    forced = jnp.where((blk == 0) | (blk == qblk), 1.0, 0.0)
    score = jnp.where(blk <= qblk, imp + FORCE_BONUS * forced, NEG)
    beats = []
    for j in range(n_blocks):
        vj = score[:, ratio * j:ratio * j + 1]
        earlier = jnp.where(lane > ratio * j, 1.0, 0.0)
        beats.append(jnp.where(vj > score, 1.0, jnp.where(vj == score, earlier, 0.0)))
    while len(beats) > 1:
        beats = [a + b for a, b in zip(beats[0::2], beats[1::2])] + ([beats[-1]] if len(beats) % 2 else [])
    rank = beats[0]
    cand = (lane % ratio == 0) & (lane < ratio * n_blocks)
    return jnp.where(cand & (rank < float(min(N_SEL, n_blocks))), 1.0, 0.0)


def _softmax_rows(s, valid):
    sm = jnp.where(valid, s, NEG)
    e = jnp.exp(sm - jnp.max(sm, axis=-1, keepdims=True))
    return e, 1.0 / jnp.sum(e, axis=-1, keepdims=True)


def _merge_and_project(o_a, sza, sga, mb, x, gate, wbra_ref, wout_ref):
    a = _dot((o_a * sza).astype(BF16), wbra_ref[...])
    m = sga * a + mb
    return x + gate * _dot(m.astype(BF16), wout_ref[...])


def _select_blocks_t(imp, qblk):
    n_blocks = imp.shape[0]
    blk = _iota(imp.shape, 0)
    forced = jnp.where((blk == 0) | (blk == qblk), 1.0, 0.0)
    score = jnp.where(blk <= qblk, imp + FORCE_BONUS * forced, NEG)
    rank = jnp.zeros(imp.shape, F32)
    for j in range(n_blocks):
        vj = score[j:j + 1, :]
        earlier = jnp.where(blk > j, 1.0, 0.0)
        rank = rank + jnp.where(vj > score, 1.0, jnp.where(vj == score, earlier, 0.0))
    return jnp.where(rank < float(min(N_SEL, n_blocks)), 1.0, 0.0)


def _softmax_cols(s):
    e = jnp.exp2(s - jnp.max(s, axis=0, keepdims=True))
    return e, 1.0 / jnp.sum(e, axis=0, keepdims=True)


def _p_attn_kernel(qt_ref, ks_ref, vst_ref, kw_ref, vwt_ref, kc_ref, vc_ref, gt_ref, sza_ref, sga_ref,
                   mb_ref, x_ref, gate_ref, wbra_ref, wout_ref, y_ref,
                   qt_scr, pen_scr, m_scr, acc_scr, s_even, s_odd, sw_scr, mw_scr, oa_scr, mprev_scr):
    tq = Q_BLOCK
    step = pl.program_id(0)
    i = lax.rem(jnp.minimum(step, pl.num_programs(0) - 2), ks_ref.shape[2] // tq)

    @pl.when(step == 0)
    def _():
        oa_scr[...] = jnp.zeros(oa_scr.shape, F32)
    n_cmp = kc_ref.shape[1]
    half = n_cmp // 2
    assert L_SLC == 2 * L_CMP and n_cmp <= LANES
    cols = GQA * tq
    qpos1 = i * tq + _iota((1, tq), 1)
    qpos = jnp.concatenate([qpos1] * GQA, axis=1)
    band = WINDOW + tq
    ws = pl.multiple_of(jnp.maximum(i * tq - WINDOW, 0), LANES)
    zeros_q = jnp.zeros((HEAD_DIM, cols), BF16)
    perm = lambda ref: jnp.concatenate([ref[0, pl.ds(0, half, stride=2), :], ref[0, pl.ds(1, half, stride=2), :],
                                        jnp.zeros((LANES - n_cmp, KV_WIDTH), F32)], axis=0)
    kc = perm(kc_ref).astype(BF16)
    vc_t = perm(vc_ref).T.astype(BF16)
    crow = _iota((LANES, cols), 0)
    cblk = 2 * (crow % half) + crow // half
    mc = ((cblk + 1) * L_CMP - 1 <= qpos) & (crow < n_cmp)
    gates_t = gt_ref[0]
    groups = [slice(g * HEAD_DIM, (g + 1) * HEAD_DIM) for g in range(N_KV)]

    for g in range(N_KV):
        qt_g = jnp.concatenate([qt_ref[0, h * HEAD_DIM:(h + 1) * HEAD_DIM, :]
                                for h in range(g * GQA, (g + 1) * GQA)], axis=1)
        qt_scr[g] = jnp.concatenate([qt_g, zeros_q] if g == 0 else [zeros_q, qt_g], axis=0)

    per_chunk = SEL_CHUNK // L_SLC
    assert band % SEL_CHUNK == 0
    w_chunks = [pl.ds(pl.multiple_of(ws + c * SEL_CHUNK, LANES), SEL_CHUNK) for c in range(band // SEL_CHUNK)]

    s_cmp = [_dot(kc, qt_scr[g]) for g in range(N_KV)]
    for c, kd in enumerate(w_chunks):
        k_rows = kw_ref[0, kd, :]
        for g in range(N_KV):
            sw_scr[g, c] = _dot(k_rows, qt_scr[g])

    def selected_scores(g, chunk):
        k_aug = ks_ref[0, g, pl.ds(pl.multiple_of(chunk * SEL_CHUNK, SEL_CHUNK), SEL_CHUNK), :]
        gap = jnp.zeros((HEAD_DIM - PEN_ROWS, cols), BF16)
        q_aug = ([qt_scr[0, 0:HEAD_DIM], pen_scr[0, chunk], gap] if g == 0 else
                 [pen_scr[1, chunk], gap, qt_scr[1, HEAD_DIM:2 * HEAD_DIM]])
        return _dot(k_aug, jnp.concatenate(q_aug, axis=0))

    o_c = []
    for g in range(N_KV):
        s_c = jnp.where(mc, s_cmp[g], NEG)
        e_c, r_c = _softmax_cols(s_c)
        p_c = jnp.where(mc, e_c * r_c, 0.0)
        o_c.append(_dot(vc_t[groups[g]], p_c.astype(BF16)))
        imp = p_c[:, 0:tq]
        for r in range(1, GQA):
            imp = imp + p_c[:, r * tq:(r + 1) * tq]
        sel = _select_blocks_t(imp[0:half] + imp[half:2 * half], qpos1 // L_SLC)
        pen = jnp.concatenate([jnp.where(sel > 0.5, 0.0, NEG)] * GQA, axis=1)
        pad_rows = jnp.zeros((PEN_ROWS - per_chunk, cols), F32)
        for c in range(half // per_chunk):
            pen_scr[g, c] = jnp.concatenate([pen[c * per_chunk:(c + 1) * per_chunk], pad_rows],
                                            axis=0).astype(BF16)
        s_even[g] = selected_scores(g, 0)

    for c in range(len(w_chunks)):
        kwpos = ws + c * SEL_CHUNK + _iota((SEL_CHUNK, tq), 0)
        bias_w = jnp.where((kwpos <= qpos1) & (kwpos > qpos1 - WINDOW), 0.0, NEG)
        bias_w = jnp.concatenate([bias_w] * GQA, axis=1)
        for g in range(N_KV):
            s_w = sw_scr[g, c] + bias_w
            sw_scr[g, c] = s_w
            m_c = jnp.max(s_w, axis=0, keepdims=True)
            mw_scr[g] = m_c if c == 0 else jnp.maximum(mw_scr[g], m_c)

    m_scr[...] = jnp.full(m_scr.shape, NEG, F32)
    acc_scr[...] = jnp.zeros(acc_scr.shape, F32)

    def with_ones(v_t):
        return jnp.concatenate([v_t.astype(BF16), jnp.ones((ONES_ROWS, v_t.shape[1]), BF16)], axis=0)

    def trip(kc_i, src, dst):
        off = pl.multiple_of(kc_i * SEL_CHUNK, SEL_CHUNK)
        if dst is None:
            t_prev = (oa_scr[...] * sza_ref[0]).astype(BF16)
            a_prev = []
        for g in range(N_KV):
            if dst is not None:
                dst[g] = selected_scores(g, kc_i + 1)
                s = src[g]
            else:
                half_n = D_MODEL // N_KV
                a_prev.append(_dot(t_prev, wbra_ref[:, g * half_n:(g + 1) * half_n]))
                causal = jnp.where(off + _iota((SEL_CHUNK, tq), 0) <= qpos1, 0.0, NEG)
                s = src[g] + jnp.concatenate([causal] * GQA, axis=1)
            m_old = m_scr[g]
            m_new = jnp.maximum(m_old, jnp.max(s, axis=0, keepdims=True))
            p = jnp.exp2(s - m_new).astype(BF16)
            m_scr[g] = m_new
            acc_scr[g] = jnp.exp2(m_old - m_new) * acc_scr[g] + _dot(
                with_ones(vst_ref[0, groups[g], pl.ds(off, SEL_CHUNK)]), p)
        if dst is None:
            mprev_scr[...] = (sga_ref[0] * jnp.concatenate(a_prev, axis=1) + mb_ref[0]).astype(BF16)

    def by_parity(kc_i, dst_wanted):
        @pl.when(lax.rem(kc_i, 2) == 0)
        def _():
            trip(kc_i, s_even, s_odd if dst_wanted else None)

        @pl.when(lax.rem(kc_i, 2) == 1)
        def _():
            trip(kc_i, s_odd, s_even if dst_wanted else None)

    def body(kc_i, carry):
        by_parity(kc_i, True)
        return carry

    assert SEL_CHUNK == tq
    lax.fori_loop(0, i, body, 0)
    by_parity(i, False)

    acc_w = [None] * N_KV

    def window_pass2():
        for c, kd in enumerate(w_chunks):
            for g in range(N_KV):
                pv = _dot(with_ones(vwt_ref[0, groups[g], kd]),
                          jnp.exp2(sw_scr[g, c] - mw_scr[g]).astype(BF16))
                acc_w[g] = pv if c == 0 else acc_w[g] + pv
                yield

    def previous_block_output():
        n_blk = 2 * LANES
        m = mprev_scr[...]
        for n0 in range(0, D_MODEL, n_blk):
            y_ref[0, :, n0:n0 + n_blk] = (x_ref[0, :, n0:n0 + n_blk]
                                          + gate_ref[0][:, n0:n0 + n_blk] * _dot(m, wout_ref[:, n0:n0 + n_blk]))
            yield

    for _ in itertools.zip_longest(window_pass2(), previous_block_output()):
        pass
    pair = []
    for g in range(N_KV):
        o_w = acc_w[g][0:HEAD_DIM] * (1.0 / acc_w[g][HEAD_DIM:HEAD_DIM + 1])
        acc_s = acc_scr[g]
        o_s = acc_s[0:HEAD_DIM] * (1.0 / acc_s[HEAD_DIM:HEAD_DIM + 1])
        for r in range(GQA):
            h = g * GQA + r
            cs = slice(r * tq, (r + 1) * tq)
            pair.append(gates_t[3 * h:3 * h + 1] * o_c[g][:, cs] + gates_t[3 * h + 1:3 * h + 2] * o_s[:, cs]
                        + gates_t[3 * h + 2:3 * h + 3] * o_w[:, cs])
    per_lane = LANES // HEAD_DIM
    oa_scr[...] = jnp.concatenate([jnp.concatenate(pair[j:j + per_lane], axis=0).T
                                   for j in range(0, N_HEADS, per_lane)], axis=1)


def _p_attn_call(q_t, ks_rows, vs_t, kw_rows, vw_t, kc, vc, gates_t, sza, sga, mb, x, gate, w_br_a, w_out):
    b, s, _ = x.shape
    tq = Q_BLOCK
    n_i = s // tq
    n_blocks = b * n_i
    att = lambda j: jnp.minimum(j, n_blocks - 1)
    out = lambda j: jnp.maximum(j - 1, 0)
    row = lambda w: pl.BlockSpec((1, tq, w), lambda j: (out(j) // n_i, out(j) % n_i, 0))
    col = lambda a: pl.BlockSpec((1, a.shape[1], tq), lambda j: (att(j) // n_i, 0, att(j) % n_i))
    seq = lambda a: pl.BlockSpec((1,) + a.shape[1:], lambda j: (att(j) // n_i,) + (0,) * (a.ndim - 1))
    out_seq = lambda a: pl.BlockSpec((1,) + a.shape[1:], lambda j: (out(j) // n_i,) + (0,) * (a.ndim - 1))
    in_specs = [col(q_t), seq(ks_rows), seq(vs_t), seq(kw_rows), seq(vw_t), seq(kc), seq(vc),
                col(gates_t), row(WIDTH_A), row(D_MODEL), row(D_MODEL), row(D_MODEL), out_seq(gate),
                _full(w_br_a.shape), _full(w_out.shape)]
    return pl.pallas_call(
        _p_attn_kernel,
        grid=(n_blocks + 1,),
        in_specs=in_specs,
        out_specs=row(D_MODEL),
        out_shape=jax.ShapeDtypeStruct((b, s, D_MODEL), F32),
        scratch_shapes=[pltpu.VMEM((N_KV, KV_WIDTH, GQA * tq), BF16),
                        pltpu.VMEM((N_KV, s // SEL_CHUNK, PEN_ROWS, GQA * tq), BF16),
                        pltpu.VMEM((N_KV, 1, GQA * tq), F32),
                        pltpu.VMEM((N_KV, HEAD_DIM + ONES_ROWS, GQA * tq), F32),
                        pltpu.VMEM((N_KV, SEL_CHUNK, GQA * tq), F32),
                        pltpu.VMEM((N_KV, SEL_CHUNK, GQA * tq), F32),
                        pltpu.VMEM((N_KV, (WINDOW + tq) // SEL_CHUNK, SEL_CHUNK, GQA * tq), F32),
                        pltpu.VMEM((N_KV, 1, GQA * tq), F32),
                        pltpu.VMEM((tq, WIDTH_A), F32),
                        pltpu.VMEM((tq, D_MODEL), BF16)],
        compiler_params=pltpu.CompilerParams(dimension_semantics=("arbitrary",),
                                             vmem_limit_bytes=VMEM_LIMIT),
        name="p_attn",
    )(q_t, ks_rows, vs_t, kw_rows, vw_t, kc, vc, gates_t, sza, sga, mb, x, gate, w_br_a, w_out)


def _s_proj_kernel(x_ref, shift_ref, scale_ref, ng_ref, w_ref, cos_ref, sin_ref, qg_ref, kg_ref,
                   vng_ref, vnb_ref, ws0_ref, bs0_ref, wbrb_ref,
                   q_out, k_out, v_out, kt_out, vt_out, gates_out, sza_out, sga_out, mb_out, vn_out):
    seg = _project(x_ref[...], shift_ref[...], scale_ref[...], ng_ref[...], w_ref)
    cos, sin = cos_ref[...], sin_ref[...]
    q_out[...] = _norm_rope(seg(C_Q, C_K), qg_ref[...], cos, sin)
    k = _norm_rope(seg(C_K, C_V), kg_ref[...], cos, sin)
    v = seg(C_V, C_G)
    k_out[...] = k
    v_out[...] = v
    for br in range(3):
        sl = slice(br * KV_WIDTH, (br + 1) * KV_WIDTH)
        kt_out[br] = k[:, sl].T
        vt_out[br] = v[:, sl].T
    gates_out[...] = jax.nn.sigmoid(seg(C_G, C_ZA))
    sza_out[...] = _silu(seg(C_ZA, C_U))
    sga_out[...] = jax.nn.sigmoid(seg(C_GA, C_GB))
    vn = _layer_norm(seg(C_VB, C_ZB), vng_ref[...], vnb_ref[...])
    vn_out[...] = vn
    s_b = ws0_ref[...] * vn + bs0_ref[...]
    t = seg(C_U, C_VB) * s_b * _silu(seg(C_ZB, C_GA))
    mb_out[...] = jax.nn.sigmoid(seg(C_GB, C_END)) * _dot(t.astype(BF16), wbrb_ref[...])


def _s_proj_call(x, shift, scale, norm_g, w_all, cos1, sin1, qg, kg, vng, vnb, ws0, bs0, w_br_b):
    n = x.shape[0]
    args = (x, shift, scale, norm_g, w_all, cos1, sin1, qg, kg, vng, vnb, ws0, bs0, w_br_b)
    sds = lambda *shape: jax.ShapeDtypeStruct(shape, F32)
    out_shape = [sds(n, WIDTH_A), sds(n, 3 * KV_WIDTH), sds(n, 3 * KV_WIDTH), sds(3, KV_WIDTH, n),
                 sds(3, KV_WIDTH, n), sds(n, LANES), sds(n, WIDTH_A), sds(n, D_MODEL), sds(n, D_MODEL),
                 sds(n, WIDTH_B)]
    return pl.pallas_call(
        _s_proj_kernel,
        grid=(1,),
        in_specs=[_full(a.shape) for a in args],
        out_specs=[_full(o.shape) for o in out_shape],
        out_shape=out_shape,
        compiler_params=pltpu.CompilerParams(vmem_limit_bytes=VMEM_LIMIT),
        name="s_proj",
    )(*args)


def _s_attn_kernel(pt_ref, q_ref, gates_ref, knew_ref, vnew_ref, knewt_ref, vnewt_ref, kwin_ref, vwin_ref,
                   pek_ref, pev_ref, wck_ref, wcv_ref, pool_ref,
                   kc_hbm, vc_hbm, ks_hbm, vs_hbm,
                   oa_ref, okw_ref, ovw_ref, buf, sem):
    t = pl.program_id(0)
    n_groups = pl.num_programs(0) - 1
    per_step = q_ref.shape[0]
    n_seqs = pt_ref.shape[0]
    n_pages = pt_ref.shape[1]
    past = n_pages * PAGE_SIZE
    caches = (kc_hbm, vc_hbm, ks_hbm, vs_hbm)

    def page_copies(group, slot_):
        return [pltpu.make_async_copy(hbm.at[pt_ref[jnp.minimum(group * per_step + j, n_seqs - 1), p]],
                                      buf.at[slot_, j, c, :, pl.ds(p * PAGE_SIZE, PAGE_SIZE)],
                                      sem.at[slot_, c])
                for j in range(per_step) for c, hbm in enumerate(caches) for p in range(n_pages)]

    @pl.when(t < n_groups)
    def _():
        for n_cp, cp in enumerate(page_copies(t, lax.rem(t, 2))):
            cp.start(priority=n_cp % 2)

    @pl.when(t > 0)
    def _():
        group = t - 1
        slot = lax.rem(group, 2)
        for cp in page_copies(group, slot):
            cp.wait()
        chains = [_s_attn_one(group * per_step + j, j, buf.at[slot, j], q_ref, gates_ref, knew_ref, vnew_ref,
                              knewt_ref, vnewt_ref, kwin_ref, vwin_ref, pek_ref, pev_ref, wck_ref, wcv_ref,
                              pool_ref, oa_ref, okw_ref, ovw_ref, past) for j in range(per_step)]
        for _ in itertools.zip_longest(*chains):
            pass


def _s_attn_one(b, j, buf, q_ref, gates_ref, knew_ref, vnew_ref, knewt_ref, vnewt_ref, kwin_ref, vwin_ref,
                pek_ref, pev_ref, wck_ref, wcv_ref, pool_ref, oa_ref, okw_ref, ovw_ref, past):
    qpos = past

    lane = _iota((KV_WIDTH, knewt_ref.shape[2]), 1)
    col = lambda ref, br: jnp.sum(jnp.where(lane == b, ref[br], 0.0), axis=1, keepdims=True)
    knew = knew_ref[pl.ds(b, 1), :]
    vnew = vnew_ref[pl.ds(b, 1), :]

    hrow = _iota((N_HEADS, 1), 0)
    first_group = hrow < GQA
    by_group = lambda f: jnp.where(first_group, f(0), f(1))
    gl = lambda g: slice(g * HEAD_DIM, (g + 1) * HEAD_DIM)
    qb = q_ref[j].astype(BF16)

    n_cmp = -(-(past + 1) // L_SLC) * L_SLC // L_CMP
    n_slc = n_cmp * L_CMP // L_SLC

    def summaries(c, new_row, pe_ref, w_ref, out):
        chunk = pool_ref.shape[1]
        sums = []
        for t0 in range(0, past, chunk):
            hi, lo = _split_bf16(buf[c, :, t0:t0 + chunk])
            sums.append(_dot_nt(pool_ref[...], hi) + _dot_nt(pool_ref[...], lo))
            yield
        pe_sum = jnp.sum(pe_ref[...], axis=0, keepdims=True)
        pooled = (jnp.concatenate(sums, axis=0) + pe_sum) * (1.0 / L_CMP)
        r = _iota((SUBLANES, KV_WIDTH), 0)
        tail = jnp.where(r == 0, new_row + pe_sum, jnp.where(r == 1, pe_sum, 0.0)) * (1.0 / L_CMP)
        zeros = jnp.zeros((LANES - pooled.shape[0] - SUBLANES, KV_WIDTH), F32)
        out.append(_dot(jnp.concatenate([pooled, tail, zeros], axis=0).astype(BF16), w_ref[...]).astype(BF16))

    summary = []
    yield from summaries(0, knew[:, 0:KV_WIDTH], pek_ref, wck_ref, summary)
    yield from summaries(1, vnew[:, 0:KV_WIDTH], pev_ref, wcv_ref, summary)
    kc, vc = summary
    s_c = by_group(lambda g: _dot_nt(qb, kc[:, gl(g)])) * SM_SCALE
    yield
    c = _iota(s_c.shape, 1)
    mc = ((c + 1) * L_CMP - 1 <= qpos) & (c < n_cmp)
    e_c, r_c = _softmax_rows(s_c, mc)
    p_c = jnp.where(mc, e_c * r_c, 0.0)
    yield
    o_c = by_group(lambda g: _dot(p_c.astype(BF16), vc[:, gl(g)]))
    imp = by_group(lambda g: jnp.sum(p_c[g * GQA:(g + 1) * GQA], axis=0, keepdims=True))
    imp = jnp.broadcast_to(imp, p_c.shape)
    yield
    sel = _select_blocks(imp, jnp.full((N_HEADS, 1), qpos // L_SLC, jnp.int32), n_slc)
    yield

    ratio = L_SLC // L_CMP
    first_blk = _iota((N_HEADS, LANES), 1) < L_SLC
    blk_col = lambda j: sel[:, ratio * j:ratio * j + 1]
    picked = jnp.concatenate([jnp.where(first_blk, blk_col(2 * c), blk_col(2 * c + 1))
                              for c in range(past // LANES)], axis=1)
    yield
    s_s = by_group(lambda g: _dot(qb, buf[2, gl(g), :].astype(BF16))) * SM_SCALE
    yield
    kpos = _iota(s_s.shape, 1)
    s_s = jnp.where((picked > 0.5) & (kpos <= qpos), s_s, NEG)
    yield
    own =_iota((N_HEADS, KV_WIDTH), 1) // HEAD_DIM == hrow // GQA
    rounded = lambda a: a.astype(BF16).astype(F32)
    q_pair = jnp.concatenate([qb.astype(F32)] * N_KV, axis=1)
    s_new = jnp.sum(jnp.where(own, q_pair * rounded(knew[:, KV_WIDTH:2 * KV_WIDTH]), 0.0),
                    axis=1, keepdims=True) * SM_SCALE
    new_lane = (L_SLC // L_CMP) * (past // L_SLC)
    s_new = jnp.where((sel[:, new_lane:new_lane + 1] > 0.5) & (past <= qpos), s_new, NEG)
    m_s = jnp.maximum(jnp.max(s_s, axis=-1, keepdims=True), s_new)
    e_s, e_new = jnp.exp(s_s - m_s), jnp.exp(s_new - m_s)
    r_s = 1.0 / (jnp.sum(e_s, axis=-1, keepdims=True) + e_new)
    yield
    v_new = by_group(lambda g: rounded(vnew[:, KV_WIDTH + g * HEAD_DIM:KV_WIDTH + (g + 1) * HEAD_DIM]))
    o_s = (by_group(lambda g: _dot_nt(e_s.astype(BF16), buf[3, gl(g), :].astype(BF16)))
           + rounded(e_new) * v_new) * r_s
    yield

    wb = kwin_ref.shape[2]
    wlane = _iota((KV_WIDTH, wb), 1)
    kw = jnp.where(wlane == wb - 1, col(knewt_ref, 2), pltpu.roll(kwin_ref[j], wb - 1, 1))
    vw = jnp.where(wlane == wb - 1, col(vnewt_ref, 2), pltpu.roll(vwin_ref[j], wb - 1, 1))
    okw_ref[j] = kw
    ovw_ref[j] = vw
    yield
    s_w = by_group(lambda g: _dot(qb, kw[gl(g)].astype(BF16))) * SM_SCALE
    yield
    kwpos = past - wb + 1 + _iota(s_w.shape, 1)
    e_w, r_w = _softmax_rows(s_w, (kwpos <= qpos) & (kwpos > qpos - WINDOW) & (kwpos >= 0))
    yield
    o_w = by_group(lambda g: _dot_nt(e_w.astype(BF16), vw[gl(g)].astype(BF16))) * r_w

    gates = gates_ref[j]
    oa_ref[j] = gates[:, 0:1] * o_c + gates[:, 1:2] * o_s + gates[:, 2:3] * o_w


def _s_attn_call(page_table, q3, gates3, knew, vnew, knew_t, vnew_t, kwin_t, vwin_t, pek, pev, wck, wcv,
                 kc_pool, vc_pool, ks_pool, vs_pool):
    n, n_pages = page_table.shape
    past = n_pages * PAGE_SIZE
    wb = kwin_t.shape[2]
    tok = np.arange(SUMMARY_CHUNK)
    assert past % SUMMARY_CHUNK == 0
    pool = jnp.asarray((tok[None, :] // L_CMP == np.arange(SUMMARY_CHUNK // L_CMP)[:, None]), BF16)
    k = SAMPLE_SEQS_PER_STEP
    assert n % k == 0
    per_seq = lambda a: pl.BlockSpec((k,) + a.shape[1:],
                                     lambda t, pt: (jnp.maximum(t - 1, 0),) + (0,) * (a.ndim - 1))
    full = lambda a: pl.BlockSpec(a.shape, lambda t, pt: (0,) * a.ndim)
    hbm = pl.BlockSpec(memory_space=pl.ANY)
    resident = (knew, vnew, knew_t, vnew_t)
    consts = (pek, pev, wck, wcv, pool)
    grid_spec = pltpu.PrefetchScalarGridSpec(
        num_scalar_prefetch=1,
        grid=(n // k + 1,),
        in_specs=[per_seq(q3), per_seq(gates3)] + [full(a) for a in resident]
                 + [per_seq(kwin_t), per_seq(vwin_t)] + [full(a) for a in consts] + [hbm] * 4,
        out_specs=[per_seq(q3), per_seq(kwin_t), per_seq(vwin_t)],
        scratch_shapes=[pltpu.VMEM((2, k, 4, KV_WIDTH, past), F32), pltpu.SemaphoreType.DMA((2, 4))],
    )
    return pl.pallas_call(
        _s_attn_kernel,
        grid_spec=grid_spec,
        out_shape=[jax.ShapeDtypeStruct(q3.shape, F32), jax.ShapeDtypeStruct(kwin_t.shape, F32),
                   jax.ShapeDtypeStruct(vwin_t.shape, F32)],
        compiler_params=pltpu.CompilerParams(dimension_semantics=("arbitrary",),
                                             vmem_limit_bytes=VMEM_LIMIT),
        name="s_attn",
    )(page_table, q3, gates3, knew, vnew, knew_t, vnew_t, kwin_t, vwin_t, pek, pev, wck, wcv, pool,
      kc_pool, vc_pool, ks_pool, vs_pool)


def _s_out_kernel(oa_ref, sza_ref, sga_ref, mb_ref, x_ref, gate_ref, wbra_ref, wout_ref, y_ref):
    y_ref[...] = _merge_and_project(oa_ref[...], sza_ref[...], sga_ref[...], mb_ref[...], x_ref[...],
                                    gate_ref[...], wbra_ref, wout_ref)


def _s_out_call(o_a, sza, sga, mb, x, gate, w_br_a, w_out):
    args = (o_a, sza, sga, mb, x, gate, w_br_a, w_out)
    return pl.pallas_call(
        _s_out_kernel,
        grid=(1,),
        in_specs=[_full(a.shape) for a in args],
        out_specs=_full(x.shape),
        out_shape=jax.ShapeDtypeStruct(x.shape, F32),
        compiler_params=pltpu.CompilerParams(vmem_limit_bytes=VMEM_LIMIT),
        name="s_out",
    )(*args)


def _rope_angles(pos):
    half = HEAD_DIM // 2
    inv = ROPE_THETA ** (-jnp.arange(half, dtype=F32) * 2.0 / HEAD_DIM)
    return pos.astype(F32)[:, None] * inv[None, :]


def _rope_tables(pos):
    ang = _rope_angles(pos)
    cos, sin = jnp.cos(ang), jnp.sin(ang)
    cos_t = jnp.concatenate([cos, cos] * (LANES // HEAD_DIM), axis=1)
    sin_t = jnp.concatenate([-sin, sin] * (LANES // HEAD_DIM), axis=1)
    return cos_t, sin_t


def _token_minor(a):
    b, t = a.shape[:2]
    return jnp.transpose(a, (0, 2, 3, 1)).reshape(b, KV_WIDTH, t)


def _token_major(a_t):
    b, _, t = a_t.shape
    return jnp.transpose(a_t.reshape(b, N_KV, HEAD_DIM, t), (0, 3, 1, 2))


def kernel(x_prompt, x_sample, cache_k_cmp, cache_v_cmp, cache_k_slc, cache_v_slc, cache_k_win, cache_v_win, page_table, c_prompt, c_sample, w_ada, b_ada, norm_g, w_in, q_norm_g, k_norm_g, cmp_pos_k, cmp_pos_v, w_cmp_k, w_cmp_v, vnorm_g, vnorm_b, w_s, b_s, w_br_a, w_br_b, w_out):
    assert w_ada.shape[0] == 1, "single layer"
    b, s, _ = x_prompt.shape
    n = x_sample.shape[0]
    assert x_sample.shape[1] == 1
    n_pages = page_table.shape[1]
    past = n_pages * PAGE_SIZE

    w_all = _w_pack_call(w_in[0].T)
    eye = jnp.eye(N_KV, dtype=F32)
    wck = jnp.kron(eye, w_cmp_k[0]).astype(BF16)
    wcv = jnp.kron(eye, w_cmp_v[0]).astype(BF16)
    pek = jnp.tile(cmp_pos_k[0], (1, N_KV))
    pev = jnp.tile(cmp_pos_v[0], (1, N_KV))
    qg = jnp.tile(q_norm_g, (1, LANES // HEAD_DIM))
    kg = jnp.tile(k_norm_g, (1, LANES // HEAD_DIM))
    w_br_a_b, w_br_b_b, w_out_b = w_br_a[0].astype(BF16), w_br_b[0].astype(BF16), w_out[0].astype(BF16)

    mod = _ada_call(jnp.concatenate([c_prompt, c_sample], axis=0), w_ada[0], b_ada)
    shift, scale, gate = mod[:, :D_MODEL], mod[:, D_MODEL:2 * D_MODEL], mod[:, 2 * D_MODEL:]

    ang_p = _rope_angles(jnp.arange(s, dtype=jnp.int32)).T
    gain_cols = lambda g: jnp.broadcast_to(g[0][:, None], (HEAD_DIM, PROMPT_SUB_ROWS))
    (q_t, kc_t, ks_t, vc_t, vs_t, vw_t, ks_rows, kw_rows, kcmp, vcmp, gates_t, sza, sga, mb,
     kw_last, vw_last) = _p_proj_call(
        x_prompt, shift[:b, None], scale[:b, None], norm_g, w_all, jnp.cos(ang_p), jnp.sin(ang_p),
        gain_cols(q_norm_g), gain_cols(k_norm_g), pek, pev, wck, wcv,
        vnorm_g, vnorm_b, w_s[0], b_s[0].T, w_br_b_b)
    y_prompt = _p_attn_call(q_t, ks_rows, vs_t, kw_rows, vw_t, kcmp, vcmp, gates_t, sza, sga, mb, x_prompt,
                            gate[:b, None], w_br_a_b, w_out_b)
    p_states = [_token_major(a)[None] for a in (kc_t, vc_t, ks_t, vs_t, kw_last, vw_last)]

    xs = x_sample.reshape(n, D_MODEL)
    cos_s, sin_s = _rope_tables(jnp.full((1,), past, jnp.int32))
    ws0 = jnp.repeat(w_s[0, :, 0, 0], GROUP_W_B)[None]
    bs0 = jnp.repeat(b_s[0, :, 0], GROUP_W_B)[None]
    (q_s, k_s, v_s, kt_s, vt_s, gates_s, sza_s, sga_s, mb_s, vn_s) = _s_proj_call(
        xs, shift[b:], scale[b:], norm_g, w_all, cos_s, sin_s, qg, kg, vnorm_g, vnorm_b, ws0, bs0, w_br_b_b)
    pools = [_token_minor(c[0]) for c in (cache_k_cmp, cache_v_cmp, cache_k_slc, cache_v_slc)]
    o_a, kwin_new, vwin_new = _s_attn_call(
        page_table, q_s.reshape(n, N_HEADS, HEAD_DIM), gates_s[:, :3 * N_HEADS].reshape(n, N_HEADS, 3),
        k_s, v_s, kt_s, vt_s, _token_minor(cache_k_win[0]), _token_minor(cache_v_win[0]),
        pek, pev, wck, wcv, *pools)
    y_sample = _s_out_call(o_a.reshape(n, WIDTH_A), sza_s, sga_s, mb_s, xs, gate[b:], w_br_a_b, w_out_b)

    new_rows = lambda t, br: jnp.transpose(t[br].reshape(N_KV, HEAD_DIM, n), (2, 0, 1))[None, :, None]
    s_states = [new_rows(kt_s, 0), new_rows(vt_s, 0), new_rows(kt_s, 1), new_rows(vt_s, 1),
                _token_major(kwin_new)[None], _token_major(vwin_new)[None], vn_s[None, :, None]]
    return (y_prompt, y_sample.reshape(n, 1, D_MODEL), *p_states, *s_states)
```
